```python
import jax
import jax.numpy as jnp
from jax import lax
import numpy as np

D_MODEL = 1024
BATCH = 32
SEQ = 256
DEPTH = 2
DEC_BATCH = 2
DEC_SEQ = 2048
PAST_LEN = 256

GRID_W = 64
CHUNK = 128
HEAD_DIM = 64
RET_HEADS = 4
GLA_HEADS = 4
RET_W = RET_HEADS * HEAD_DIM
GLA_W = GLA_HEADS * HEAD_DIM
S5_W = D_MODEL - RET_W - GLA_W
S5_CH = 16
S5_GROUPS = S5_W // S5_CH
S5_STATE = 64
GLA_RANK = 16
GLA_TAU = 16.0
IN_COLS = 4 * RET_W + 4 * GLA_W + GLA_RANK + S5_W
N_EXPERTS = 32
TOP_K = 4
D_FF = D_MODEL
SWIGLU_LIMIT = 7.0
SWIGLU_ALPHA = 1.702
ROPE_BASE = 10000.0
EPS = 1e-6

kernel_name = "hybrid_ret_gla_s5_moe_flow_step"


def _rmsnorm(x, g=None):
    xf = x.astype(jnp.float32)
    y = xf * lax.rsqrt(jnp.mean(xf * xf, axis=-1, keepdims=True) + EPS)
    if g is not None:
        y = y * g.astype(jnp.float32)
    return y.astype(x.dtype)


def _grid_rope(t):
    rows = t // GRID_W
    rr, cc = jnp.meshgrid(jnp.arange(rows), jnp.arange(GRID_W), indexing='ij')
    nf = HEAD_DIM // 4
    inv = ROPE_BASE ** (-jnp.arange(nf, dtype=jnp.float32) / nf)
    ang = jnp.concatenate([rr.reshape(-1, 1) * inv, cc.reshape(-1, 1) * inv], axis=-1)
    return jnp.cos(ang), jnp.sin(ang)


def _apply_rope(x, cos, sin):
    x1, x2 = x[..., :HEAD_DIM // 2], x[..., HEAD_DIM // 2:]
    return jnp.concatenate([x1 * cos - x2 * sin, x1 * sin + x2 * cos], axis=-1)


def _chunk_scan(q, k, v, log_a, s0):
    bsz, nh, t, _ = q.shape
    dv = v.shape[-1]
    n = t // CHUNK

    def chunks(a):
        return a.astype(jnp.float32).reshape(bsz, nh, n, CHUNK, a.shape[-1])

    q, k, v, la = chunks(q), chunks(k), chunks(v), chunks(log_a)
    b = jnp.cumsum(la, axis=3)
    b_last = b[:, :, :, -1:, :]
    causal = jnp.tril(jnp.ones((CHUNK, CHUNK), dtype=bool))
    q_dec = q * jnp.exp(b)
    if la.shape[-1] == 1:
        bs = b[..., 0]
        diff = bs[..., :, None] - bs[..., None, :]
        decay = jnp.exp(jnp.where(causal, diff, -jnp.inf))
        scores = jnp.einsum('bhnik,bhnjk->bhnij', q, k) * decay
    else:
        scores = jnp.einsum('bhnik,bhnjk->bhnij', q_dec, k * jnp.exp(-b))
        scores = jnp.where(causal, scores, 0.0)
    o_intra = jnp.einsum('bhnij,bhnjv->bhniv', scores, v)
    upd = jnp.einsum('bhnjk,bhnjv->bhnkv', k * jnp.exp(b_last - b), v)
    chunk_decay = jnp.exp(b_last[:, :, :, 0, :])

    def step(s, inp):
        qd, u, dec = inp
        o = jnp.einsum('bhck,bhkv->bhcv', qd, s)
        return dec[..., None] * s + u, o

    xs = (jnp.moveaxis(q_dec, 2, 0), jnp.moveaxis(upd, 2, 0), jnp.moveaxis(chunk_decay, 2, 0))
    s_t, o_inter = lax.scan(step, s0.astype(jnp.float32), xs)
    o = o_intra + jnp.moveaxis(o_inter, 0, 2)
    return o.reshape(bsz, nh, t, dv), s_t


def _bidir_scan(q, k, v, la_f, la_b, s0_f, s0_b):
    o_f, s_f = _chunk_scan(q, k, v, la_f, s0_f)
    fl = lambda a: jnp.flip(a, axis=2)
    o_b, s_b = _chunk_scan(fl(q), fl(k), fl(v), fl(la_b), s0_b)
    return o_f + fl(o_b), s_f, s_b


def _complex_affine_combine(e1, e2):
    a1r, a1i, b1r, b1i = e1
    a2r, a2i, b2r, b2i = e2
    return (a2r * a1r - a2i * a1i,
            a2r * a1i + a2i * a1r,
            a2r * b1r - a2i * b1i + b2r,
            a2r * b1i + a2i * b1r + b2i)


def _s5_scan(u, a_re, a_im, log_dt, b_re, b_im, h0_re, h0_im):
    f32 = jnp.float32
    lam_re, lam_im = a_re.astype(f32), a_im.astype(f32)
    dt = jnp.exp(log_dt.astype(f32))[:, None]
    mag = jnp.exp(lam_re * dt)
    ab_re, ab_im = mag * jnp.cos(lam_im * dt), mag * jnp.sin(lam_im * dt)
    den = lam_re * lam_re + lam_im * lam_im
    n_re = ab_re - 1.0
    f_re = (n_re * lam_re + ab_im * lam_im) / den
    f_im = (ab_im * lam_re - n_re * lam_im) / den
    br, bi = b_re.astype(f32), b_im.astype(f32)
    bb_re = f_re[..., None] * br - f_im[..., None] * bi
    bb_im = f_re[..., None] * bi + f_im[..., None] * br
    x_re = jnp.einsum('btgc,gpc->btgp', u, bb_re)
    x_im = jnp.einsum('btgc,gpc->btgp', u, bb_im)
    a_r = jnp.broadcast_to(ab_re, x_re.shape)
    a_i = jnp.broadcast_to(ab_im, x_im.shape)
    p_re, p_im, h_re, h_im = lax.associative_scan(_complex_affine_combine, (a_r, a_i, x_re, x_im), axis=1)
    h0r = h0_re.astype(f32)[:, None]
    h0i = h0_im.astype(f32)[:, None]
    h_re = h_re + p_re * h0r - p_im * h0i
    h_im = h_im + p_re * h0i + p_im * h0r
    return h_re, h_im


def _s5_mixer(u, h0_re, h0_im, lp):
    bsz, t, _ = u.shape
    f32 = jnp.float32
    ug = u.reshape(bsz, t, S5_GROUPS, S5_CH)
    fr, fi = _s5_scan(ug, lp['s5_a_re'][0], lp['s5_a_im'][0], lp['s5_log_dt'][0],
                      lp['s5_b_re'], lp['s5_b_im'], h0_re[:, 0], h0_im[:, 0])
    rr, ri = _s5_scan(jnp.flip(ug, 1), lp['s5_a_re'][1], lp['s5_a_im'][1], lp['s5_log_dt'][1],
                      lp['s5_b_re'], lp['s5_b_im'], h0_re[:, 1], h0_im[:, 1])
    fin_re = jnp.stack([fr[:, -1], rr[:, -1]], axis=1)
    fin_im = jnp.stack([fi[:, -1], ri[:, -1]], axis=1)
    h_re = fr + jnp.flip(rr, 1)
    h_im = fi + jnp.flip(ri, 1)
    y = (jnp.einsum('btgp,gcp->btgc', h_re, lp['s5_c_re'].astype(f32))
         - jnp.einsum('btgp,gcp->btgc', h_im, lp['s5_c_im'].astype(f32))
         + lp['s5_d'].astype(f32).reshape(S5_GROUPS, S5_CH) * ug)
    y = jax.nn.gelu(y.reshape(bsz, t, S5_W))
    y = y * jax.nn.sigmoid(y @ lp['s5_glu_w'].astype(f32) + lp['s5_glu_b'].astype(f32))
    return y, fin_re, fin_im


def _mixer(h, rope, st, lp):
    bsz, t, _ = h.shape
    proj = (h @ lp['w_in']).astype(jnp.float32)
    offs = np.cumsum([RET_W] * 4 + [GLA_W] * 4 + [GLA_RANK])
    rq, rk, rv, rg, gq, gk, gv, gg, glr, su = jnp.split(proj, offs, axis=-1)
    heads = lambda a: a.reshape(bsz, t, -1, HEAD_DIM).transpose(0, 2, 1, 3)
    merge = lambda a: a.transpose(0, 2, 1, 3).reshape(bsz, t, -1)
    s_ret, s_gla, s_re, s_im = st

    q, k, v = heads(rq), heads(rk) * HEAD_DIM ** -0.5, heads(rv)
    if rope is not None:
        q, k = _apply_rope(q, *rope), _apply_rope(k, *rope)
    log_gamma = jnp.log1p(-jnp.exp(lp['ret_decay'].astype(jnp.float32)))
    lg = [jnp.broadcast_to(log_gamma[d][None, :, None, None], (bsz, RET_HEADS, t, 1)) for d in range(2)]
    o, ret_f, ret_b = _bidir_scan(q, k, v, lg[0], lg[1], s_ret[:, 0], s_ret[:, 1])
    y_ret = merge(_rmsnorm(o)) * jax.nn.silu(rg)

    q, k, v = heads(gq) * HEAD_DIM ** -0.5, heads(gk), heads(gv)
    la = [heads(jax.nn.log_sigmoid(glr @ lp['gla_w_lr'][d].astype(jnp.float32)
                                   + lp['gla_b_lr'][d].astype(jnp.float32))) / GLA_TAU for d in range(2)]
    o, gla_f, gla_b = _bidir_scan(q, k, v, la[0], la[1], s_gla[:, 0], s_gla[:, 1])
    y_gla = merge(_rmsnorm(o, lp['gla_norm'])) * jax.nn.silu(gg)

    y_s5, fin_re, fin_im = _s5_mixer(su, s_re, s_im, lp)

    y = jnp.concatenate([y_ret, y_gla, y_s5], axis=-1).astype(h.dtype) @ lp['w_out']
    new_st = (jnp.stack([ret_f, ret_b], axis=1), jnp.stack([gla_f, gla_b], axis=1), fin_re, fin_im)
    return y, new_st


def _moe(h, lp):
    shp = h.shape
    x = h.reshape(-1, shp[-1])
    logits = (x @ lp['router_w'] + lp['router_b']).astype(jnp.float32)
    top_val, top_idx = lax.top_k(logits, TOP_K)
    top_w = jax.nn.softmax(top_val, axis=-1)
    combine = jnp.sum(jax.nn.one_hot(top_idx, N_EXPERTS, dtype=jnp.float32) * top_w[..., None], axis=1)
    out = jnp.zeros(x.shape, jnp.float32)
    for e in range(N_EXPERTS):
        gu = x @ lp['moe_w1'][e] + lp['moe_b1'][e]
        gate = jnp.minimum(gu[:, :D_FF], SWIGLU_LIMIT)
        up = jnp.clip(gu[:, D_FF:], -SWIGLU_LIMIT, SWIGLU_LIMIT)
        act = (up + 1.0) * gate * jax.nn.sigmoid(SWIGLU_ALPHA * gate)
        out = out + combine[:, e:e + 1] * (act @ lp['moe_w2'][e] + lp['moe_b2'][e])
    return out.astype(h.dtype).reshape(shp)


def _block(x, cond, rope, st, lp):
    mod = jax.nn.silu(cond) @ lp['ada_w'] + lp['ada_b']
    sh1, sc1, g1, sh2, sc2, g2 = jnp.split(mod[:, None, :], 6, axis=-1)
    h = _rmsnorm(x, lp['norm1']) * (1.0 + sc1) + sh1
    m, new_st = _mixer(h, rope, st, lp)
    x = x + g1 * m
    h = _rmsnorm(x, lp['norm2']) * (1.0 + sc2) + sh2
    x = x + g2 * _moe(h, lp)
    return x, new_st


def setup_inputs(seed: int = 0) -> dict:
    key = jax.random.key(seed)
    kit = iter(list(jax.random.split(key, 40)))

    def nrm(shape, scale=1.0):
        return scale * jax.random.normal(next(kit), shape, jnp.float32)

    decay_base = jnp.asarray(np.linspace(np.log(1.0 / 32.0), np.log(1.0 / 512.0), RET_HEADS).astype(np.float32))
    return {
        'x_prompt': nrm((BATCH, SEQ, D_MODEL)),
        'x_sample': nrm((DEC_BATCH, DEC_SEQ, D_MODEL)),
        'state_ret': nrm((DEC_BATCH, DEPTH, 2, RET_HEADS, HEAD_DIM, HEAD_DIM), 0.5),
        'state_gla': nrm((DEC_BATCH, DEPTH, 2, GLA_HEADS, HEAD_DIM, HEAD_DIM), 0.5),
        'state_s5_re': nrm((DEC_BATCH, DEPTH, 2, S5_GROUPS, S5_STATE), 0.5),
        'state_s5_im': nrm((DEC_BATCH, DEPTH, 2, S5_GROUPS, S5_STATE), 0.5),
        'c': nrm((DEC_BATCH, D_MODEL)),
        'c_ctx': nrm((D_MODEL,)),
        'norm1_g': 1.0 + nrm((DEPTH, D_MODEL), 0.02),
        'norm2_g': 1.0 + nrm((DEPTH, D_MODEL), 0.02),
        'final_g': 1.0 + nrm((D_MODEL,), 0.02),
        'ada_w': nrm((DEPTH, D_MODEL, 6 * D_MODEL), 0.5 * D_MODEL ** -0.5),
        'ada_b': nrm((DEPTH, 6 * D_MODEL), 0.02),
        'w_in': nrm((DEPTH, D_MODEL, IN_COLS), D_MODEL ** -0.5),
        'w_out': nrm((DEPTH, D_MODEL, D_MODEL), D_MODEL ** -0.5),
        'ret_decay': decay_base + nrm((DEPTH, 2, RET_HEADS), 0.05),
        'gla_w_lr': nrm((DEPTH, 2, GLA_RANK, GLA_W), GLA_RANK ** -0.5),
        'gla_b_lr': nrm((DEPTH, 2, GLA_W), 0.1),
        'gla_norm_g': 1.0 + nrm((DEPTH, HEAD_DIM), 0.02),
        's5_a_re': -0.5 + nrm((DEPTH, 2, S5_GROUPS, S5_STATE), 0.01),
        's5_a_im': jnp.pi * jnp.arange(S5_STATE, dtype=jnp.float32) + nrm((DEPTH, 2, S5_GROUPS, S5_STATE), 0.01),
        's5_log_dt': jax.random.uniform(next(kit), (DEPTH, 2, S5_GROUPS), jnp.float32,
                                        minval=float(np.log(0.001)), maxval=float(np.log(0.1))),
        's5_b_re': nrm((DEPTH, S5_GROUPS, S5_STATE, S5_CH), (2 * S5_CH) ** -0.5),
        's5_b_im': nrm((DEPTH, S5_GROUPS, S5_STATE, S5_CH), (2 * S5_CH) ** -0.5),
        's5_c_re': nrm((DEPTH, S5_GROUPS, S5_CH, S5_STATE), S5_STATE ** -0.5),
        's5_c_im': nrm((DEPTH, S5_GROUPS, S5_CH, S5_STATE), S5_STATE ** -0.5),
        's5_d': nrm((DEPTH, S5_W)),
        's5_glu_w': nrm((DEPTH, S5_W, S5_W), S5_W ** -0.5),
        's5_glu_b': nrm((DEPTH, S5_W), 0.02),
        'router_w': nrm((DEPTH, D_MODEL, N_EXPERTS), D_MODEL ** -0.5),
        'router_b': nrm((DEPTH, N_EXPERTS), 0.01),
        'moe_w1': nrm((DEPTH, N_EXPERTS, D_MODEL, 2 * D_FF), D_MODEL ** -0.5),
        'moe_b1': nrm((DEPTH, N_EXPERTS, 2 * D_FF), 0.02),
        'moe_w2': nrm((DEPTH, N_EXPERTS, D_FF, D_MODEL), D_FF ** -0.5),
        'moe_b2': nrm((DEPTH, N_EXPERTS, D_MODEL), 0.02),
    }


def reference(x_prompt, x_sample, state_ret, state_gla, state_s5_re, state_s5_im, c, c_ctx,
              norm1_g, norm2_g, final_g, ada_w, ada_b, w_in, w_out, ret_decay,
              gla_w_lr, gla_b_lr, gla_norm_g, s5_a_re, s5_a_im, s5_log_dt,
              s5_b_re, s5_b_im, s5_c_re, s5_c_im, s5_d, s5_glu_w, s5_glu_b,
              router_w, router_b, moe_w1, moe_b1, moe_w2, moe_b2):
    rope = _grid_rope(x_sample.shape[1])
    bp = x_prompt.shape[0]
    ctx_init = (jnp.zeros((bp, 2, RET_HEADS, HEAD_DIM, HEAD_DIM), jnp.float32),
                jnp.zeros((bp, 2, GLA_HEADS, HEAD_DIM, HEAD_DIM), jnp.float32),
                jnp.zeros((bp, 2, S5_GROUPS, S5_STATE), jnp.float32),
                jnp.zeros((bp, 2, S5_GROUPS, S5_STATE), jnp.float32))
    xp, xs = x_prompt, x_sample
    ret_list, gla_list, re_list, im_list = [], [], [], []
    for l in range(DEPTH):
        lp = dict(norm1=norm1_g[l], norm2=norm2_g[l], ada_w=ada_w[l], ada_b=ada_b[l],
                  w_in=w_in[l], w_out=w_out[l], ret_decay=ret_decay[l],
                  gla_w_lr=gla_w_lr[l], gla_b_lr=gla_b_lr[l], gla_norm=gla_norm_g[l],
                  s5_a_re=s5_a_re[l], s5_a_im=s5_a_im[l], s5_log_dt=s5_log_dt[l],
                  s5_b_re=s5_b_re[l], s5_b_im=s5_b_im[l], s5_c_re=s5_c_re[l], s5_c_im=s5_c_im[l],
                  s5_d=s5_d[l], s5_glu_w=s5_glu_w[l], s5_glu_b=s5_glu_b[l],
                  router_w=router_w[l], router_b=router_b[l],
                  moe_w1=moe_w1[l], moe_b1=moe_b1[l], moe_w2=moe_w2[l], moe_b2=moe_b2[l])
        xp, st = _block(xp, c_ctx[None, :], None, ctx_init, lp)
        ret_list.append(st[0])
        gla_list.append(st[1])
        re_list.append(st[2])
        im_list.append(st[3])
        cache = (state_ret[:, l], state_gla[:, l], state_s5_re[:, l], state_s5_im[:, l])
        xs, _ = _block(xs, c, rope, cache, lp)
    y_prompt = _rmsnorm(xp, final_g)
    y_sample = _rmsnorm(xs, final_g)
    new_state_ret = jnp.stack(ret_list, axis=1)
    new_state_gla = jnp.stack(gla_list, axis=1)
    new_state_s5_re = jnp.stack(re_list, axis=1)
    new_state_s5_im = jnp.stack(im_list, axis=1)
    return (y_prompt, y_sample, new_state_ret, new_state_gla, new_state_s5_re, new_state_s5_im)
```

```python
import functools

import numpy as np
import jax
import jax.numpy as jnp
from jax import lax
from jax.experimental import pallas as pl
from jax.experimental.pallas import tpu as pltpu

F32 = jnp.float32
BF16 = jnp.bfloat16
HIGHEST = lax.Precision.HIGHEST

D_MODEL = 1024
GRID_W = 64
CHUNK = 128
HEAD_DIM = 64
N_HEADS = 4
HG_W = N_HEADS * HEAD_DIM
S5_W = 512
S5_CH = 16
S5_GROUPS = 32
S5_STATE = 64
S5_L = 16
S5_ROW = S5_L * S5_CH
GLA_RANK = 16
GLA_TAU = 16.0
N_EXPERTS = 32
TOP_K = 4
D_FF = 1024
SWIGLU_LIMIT = 7.0
SWIGLU_ALPHA = 1.702
ROPE_BASE = 10000.0
EPS = 1e-6

LANES = 128
SUBLANES = 8
TM = 256
SEG_ALIGN = SUBLANES
ROWS_T = 1280
CHUNKS_T = ROWS_T // SEG_ALIGN
BLK_CHUNKS = 64
BLK_ROWS = BLK_CHUNKS * SEG_ALIGN
NEG_BIG = -1e30
VMEM_LIMIT = 56 * 1024 * 1024

PA_W = 8 * HG_W + 2 * HG_W
W_CAT = 8 * HG_W + S5_W + LANES


def _sigmoid(x):
    return 1.0 / (1.0 + jnp.exp(-x))


def _group_id(i, n_ctx_tiles, tiles_per_lat):
    return jnp.where(i < n_ctx_tiles, 0, 1 + (i - n_ctx_tiles) // tiles_per_lat)


def _ada_kernel(c_ref, w_ref, b_ref, o_ref):
    c = c_ref[...]
    s = c * _sigmoid(c)
    o_ref[0] = jnp.dot(s, w_ref[0], precision=HIGHEST, preferred_element_type=F32) + b_ref[0]


def _ada_call(cond8, ada_w, ada_b):
    depth, d, n6 = ada_w.shape
    tn = 1024
    return pl.pallas_call(
        _ada_kernel,
        grid=(depth, n6 // tn),
        in_specs=[
            pl.BlockSpec((SUBLANES, d), lambda l, j: (0, 0)),
            pl.BlockSpec((1, d, tn), lambda l, j: (l, 0, j)),
            pl.BlockSpec((1, 1, tn), lambda l, j: (l, 0, j)),
        ],
        out_specs=pl.BlockSpec((1, SUBLANES, tn), lambda l, j: (l, 0, j)),
        out_shape=jax.ShapeDtypeStruct((depth, SUBLANES, n6), F32),
        compiler_params=pltpu.CompilerParams(vmem_limit_bytes=VMEM_LIMIT),
        name="ada_mod",
    )(cond8, ada_w, ada_b.reshape(depth, 1, n6))


def _inproj_kernel(x_ref, mod_ref, n1_ref, w_ref, wlr_ref, blr_ref, pa_ref, su_ref):
    x = x_ref[...]
    h = x * lax.rsqrt(jnp.mean(x * x, axis=-1, keepdims=True) + EPS) * n1_ref[...]
    h = h * (1.0 + mod_ref[1:2, :]) + mod_ref[0:1, :]
    r = jnp.dot(h.astype(BF16), w_ref[...], preferred_element_type=F32)
    pa_ref[:, : 8 * HG_W] = r[:, : 8 * HG_W]
    su_ref[...] = r[:, 8 * HG_W: 8 * HG_W + S5_W]
    glr = r[:, 8 * HG_W + S5_W:]
    for d in range(2):
        z = jnp.dot(glr, wlr_ref[d], precision=HIGHEST, preferred_element_type=F32) + blr_ref[d]
        log_sig = jnp.minimum(z, 0.0) - jnp.log(1.0 + jnp.exp(-jnp.abs(z)))
        pa_ref[:, (8 + d) * HG_W: (9 + d) * HG_W] = log_sig * (1.0 / GLA_TAU)


def _inproj_call(x, mod_l, n1, w_cat, wlr, blr, n_ctx_tiles, tiles_per_lat):
    n = x.shape[0]
    gid = functools.partial(_group_id, n_ctx_tiles=n_ctx_tiles, tiles_per_lat=tiles_per_lat)
    return pl.pallas_call(
        _inproj_kernel,
        grid=(n // TM,),
        in_specs=[
            pl.BlockSpec((TM, D_MODEL), lambda i: (i, 0)),
            pl.BlockSpec((None, 6, D_MODEL), lambda i: (gid(i), 0, 0)),
            pl.BlockSpec((1, D_MODEL), lambda i: (0, 0)),
            pl.BlockSpec((D_MODEL, W_CAT), lambda i: (0, 0)),
            pl.BlockSpec((2, LANES, HG_W), lambda i: (0, 0, 0)),
            pl.BlockSpec((2, 1, HG_W), lambda i: (0, 0, 0)),
        ],
        out_specs=[
            pl.BlockSpec((TM, PA_W), lambda i: (i, 0)),
            pl.BlockSpec((TM, S5_W), lambda i: (i, 0)),
        ],
        out_shape=[
            jax.ShapeDtypeStruct((n, PA_W), F32),
            jax.ShapeDtypeStruct((n, S5_W), F32),
        ],
        compiler_params=pltpu.CompilerParams(vmem_limit_bytes=VMEM_LIMIT),
        name="norm1_inproj",
    )(x, mod_l, n1, w_cat, wlr, blr)


def _scan_kernel(*refs, n, use_rope, has_init, want_final):
    refs = list(refs)
    lg_ref = refs.pop(0)
    pa_ref = refs.pop(0)
    if use_rope:
        cos_ref = refs.pop(0)
        sin_ref = refs.pop(0)
    lgam_ref = refs.pop(0)
    gn_ref = refs.pop(0)
    if has_init:
        st0_ref = refs.pop(0)
    y_ref = refs.pop(0)
    if want_final:
        stf_ref = refs.pop(0)
    st_scr = refs.pop(0)
    of_scr = refs.pop(0)

    d = pl.program_id(1)
    c = pl.program_id(2)
    cc = jnp.where(d == 0, c, n - 1 - c)

    @pl.when(c == 0)
    def _():
        if has_init:
            st_scr[...] = st0_ref[...]
        else:
            st_scr[...] = jnp.zeros_like(st_scr)

    ii = lax.broadcasted_iota(jnp.int32, (CHUNK, CHUNK), 0)
    jj = lax.broadcasted_iota(jnp.int32, (CHUNK, CHUNK), 1)
    rel = (1 - 2 * d) * (ii - jj)
    mask = rel >= 0
    relf = rel.astype(F32)
    row = lax.broadcasted_iota(jnp.int32, (CHUNK, 1), 0)
    p1 = jnp.where(d == 0, row + 1, CHUNK - row).astype(F32)
    lane = lax.broadcasted_iota(jnp.int32, (1, HG_W), 1)
    head_mask = [(lane // HEAD_DIM == h).astype(F32) for h in range(N_HEADS)]
    bi = lax.broadcasted_iota(jnp.int32, (HG_W, HG_W), 0) // HEAD_DIM
    bj = lax.broadcasted_iota(jnp.int32, (HG_W, HG_W), 1) // HEAD_DIM
    block_diag = bi == bj

    nt_dims = (((1,), (1,)), ((), ()))
    tn_dims = (((0,), (0,)), ((), ()))

    def group(q, k, v, q_dec, k_upd, chunk_decay, score_fn, g_idx):
        kb = k.astype(BF16)
        vb = v.astype(BF16)
        o = jnp.zeros((CHUNK, HG_W), F32)
        for h in range(N_HEADS):
            s = lax.dot_general((q * head_mask[h]).astype(BF16), kb, nt_dims, preferred_element_type=F32)
            s = score_fn(s, h)
            o = o + jnp.dot(s.astype(BF16), vb, preferred_element_type=F32) * head_mask[h]
        st = st_scr[g_idx]
        o = o + lax.dot_general(q_dec.astype(BF16), st.astype(BF16), nt_dims, preferred_element_type=F32)
        upd = lax.dot_general(vb, k_upd.astype(BF16), tn_dims, preferred_element_type=F32)
        st_scr[g_idx] = st * chunk_decay + jnp.where(block_diag, upd, 0.0)
        return o

    def rope(x):
        partner = jnp.where((lane % HEAD_DIM) < HEAD_DIM // 2,
                            pltpu.roll(x, HG_W - HEAD_DIM // 2, 1), pltpu.roll(x, HEAD_DIM // 2, 1))
        return x * cos_ref[...] + partner * sin_ref[...]

    q = pa_ref[:, 0:HG_W]
    k = pa_ref[:, HG_W:2 * HG_W] * (HEAD_DIM ** -0.5)
    v = pa_ref[:, 2 * HG_W:3 * HG_W]
    if use_rope:
        q = rope(q)
        k = rope(k)
    lgl = lgam_ref[pl.ds(d, 1), :]
    q_dec = q * jnp.exp(p1 * lgl)
    k_upd = k * jnp.exp((CHUNK - p1) * lgl)
    chunk_decay = jnp.exp(CHUNK * lgl)

    def ret_scores(s, h):
        return s * jnp.where(mask, jnp.exp(relf * lg_ref[d, h]), 0.0)

    o_ret = group(q, k, v, q_dec, k_upd, chunk_decay, ret_scores, 0)

    q = pa_ref[:, 4 * HG_W:5 * HG_W] * (HEAD_DIM ** -0.5)
    k = pa_ref[:, 5 * HG_W:6 * HG_W]
    v = pa_ref[:, 6 * HG_W:7 * HG_W]
    la = jnp.where(d == 0, pa_ref[:, 8 * HG_W:9 * HG_W], pa_ref[:, 9 * HG_W:10 * HG_W])
    b = jnp.dot(mask.astype(F32), la, precision=HIGHEST, preferred_element_type=F32)
    b_last = jnp.sum(la, axis=0, keepdims=True)
    q_dec = q * jnp.exp(b)
    k_inv = k * jnp.exp(-b)
    k_upd = k * jnp.exp(b_last - b)
    chunk_decay = jnp.exp(b_last)

    def gla_scores(s, h):
        return jnp.where(mask, s, 0.0)

    o_gla = group(q_dec, k_inv, v, q_dec, k_upd, chunk_decay, gla_scores, 1)

    @pl.when(d == 0)
    def _():
        of_scr[c, :, 0:HG_W] = o_ret
        of_scr[c, :, HG_W:2 * HG_W] = o_gla

    @pl.when(d == 1)
    def _():
        mean_mat = jnp.where(block_diag, 1.0 / HEAD_DIM, 0.0).astype(F32)

        def head_rms(o):
            ms = jnp.dot(o * o, mean_mat, precision=HIGHEST, preferred_element_type=F32)
            return o * lax.rsqrt(ms + EPS)

        o1 = o_ret + of_scr[cc, :, 0:HG_W]
        g1 = pa_ref[:, 3 * HG_W:4 * HG_W]
        y_ref[:, 0:HG_W] = head_rms(o1) * (g1 * _sigmoid(g1))
        o2 = o_gla + of_scr[cc, :, HG_W:2 * HG_W]
        g2 = pa_ref[:, 7 * HG_W:8 * HG_W]
        y_ref[:, HG_W:2 * HG_W] = head_rms(o2) * gn_ref[...] * (g2 * _sigmoid(g2))

    if want_final:
        @pl.when(c == n - 1)
        def _():
            stf_ref[...] = st_scr[...]


def _scan_call(pa, row_block_off, bsz, t, lg, lgam, gn, rope, st0, want_final):
    n = t // CHUNK
    use_rope = rope is not None
    has_init = st0 is not None

    def chunk_idx(d, c):
        return jnp.where(d == 0, c, n - 1 - c)

    in_specs = [
        pl.BlockSpec(memory_space=pltpu.SMEM),
        pl.BlockSpec((CHUNK, PA_W), lambda b, d, c: (row_block_off + b * n + chunk_idx(d, c), 0)),
    ]
    args = [lg, pa]
    if use_rope:
        in_specs += [pl.BlockSpec((CHUNK, HG_W), lambda b, d, c: (chunk_idx(d, c), 0))] * 2
        args += list(rope)
    in_specs += [pl.BlockSpec((2, HG_W), lambda b, d, c: (0, 0)),
                 pl.BlockSpec((1, HG_W), lambda b, d, c: (0, 0))]
    args += [lgam, gn]
    if has_init:
        in_specs.append(pl.BlockSpec((None, None, 2, HG_W, HG_W), lambda b, d, c: (b, d, 0, 0, 0)))
        args.append(st0)
    out_specs = [pl.BlockSpec((CHUNK, 2 * HG_W),
                              lambda b, d, c: (b * n + jnp.where(d == 0, n - 1, n - 1 - c), 0))]
    out_shape = [jax.ShapeDtypeStruct((bsz * t, 2 * HG_W), F32)]
    if want_final:
        out_specs.append(pl.BlockSpec((None, None, 2, HG_W, HG_W), lambda b, d, c: (b, d, 0, 0, 0)))
        out_shape.append(jax.ShapeDtypeStruct((bsz, 2, 2, HG_W, HG_W), F32))
    res = pl.pallas_call(
        functools.partial(_scan_kernel, n=n, use_rope=use_rope, has_init=has_init, want_final=want_final),
        grid=(bsz, 2, n),
        in_specs=in_specs,
        out_specs=out_specs,
        out_shape=out_shape,
        scratch_shapes=[pltpu.VMEM((2, HG_W, HG_W), F32), pltpu.VMEM((n, CHUNK, 2 * HG_W), F32)],
        compiler_params=pltpu.CompilerParams(vmem_limit_bytes=VMEM_LIMIT),
        name="ret_gla_scan",
    )(*args)
    return res if want_final else (res[0], None)


def _s5_kernel(*refs, n, bsz, has_init):
    refs = list(refs)
    u_ref = refs.pop(0)
    m_ref = refs.pop(0)
    sf_ref = refs.pop(0)
    sb_ref = refs.pop(0)
    yf_ref = refs.pop(0)
    yb_ref = refs.pop(0)
    lam_ref = refs.pop(0)
    dv_ref = refs.pop(0)
    if has_init:
        h0_ref = refs.pop(0)
    y_ref = refs.pop(0)
    fin_ref = refs.pop(0)
    sfs, sbs, hpf, hnb = refs

    bm_dims = (((2,), (1,)), ((0,), (0,)))
    u = u_ref[...]
    ub = u.astype(BF16)
    y = lax.dot_general(ub, m_ref[...], bm_dims, preferred_element_type=F32) + u * dv_ref[...]
    sfs[...] = lax.dot_general(ub, sf_ref[...], bm_dims, preferred_element_type=F32)
    sbs[...] = lax.dot_general(ub, sb_ref[...], bm_dims, preferred_element_type=F32)

    gb = u.shape[0]
    a_f, s_f = lam_ref[:, 0:1, :], lam_ref[:, 1:2, :]
    a_b, s_b = lam_ref[:, 2:3, :], lam_ref[:, 3:4, :]
    if has_init:
        hf = h0_ref[:, 0]
        hb = h0_ref[:, 1]
    else:
        hf = jnp.zeros((gb, bsz, 2 * S5_STATE), F32)
        hb = jnp.zeros((gb, bsz, 2 * S5_STATE), F32)
    for c in range(n):
        lo, hi = c * bsz, (c + 1) * bsz
        hpf[:, lo:hi, :] = hf
        hf = hf * a_f + pltpu.roll(hf, S5_STATE, 2) * s_f + sfs[:, lo:hi, :]
        cb = n - 1 - c
        lo, hi = cb * bsz, (cb + 1) * bsz
        hnb[:, lo:hi, :] = hb
        hb = hb * a_b + pltpu.roll(hb, S5_STATE, 2) * s_b + sbs[:, lo:hi, :]
    fin_ref[:, 0] = hf
    fin_ref[:, 1] = hb
    y = y + lax.dot_general(hpf[...].astype(BF16), yf_ref[...], bm_dims, preferred_element_type=F32)
    y = y + lax.dot_general(hnb[...].astype(BF16), yb_ref[...], bm_dims, preferred_element_type=F32)
    y_ref[...] = y


S5_GB = 8


def _s5_call(u, ops, bsz, t, h0):
    n = t // S5_L
    r = n * bsz
    has_init = h0 is not None
    gspec = lambda *shape: pl.BlockSpec((S5_GB,) + shape, lambda g: (g,) + (0,) * len(shape))
    in_specs = [gspec(r, S5_ROW), gspec(S5_ROW, S5_ROW), gspec(S5_ROW, 2 * S5_STATE), gspec(S5_ROW, 2 * S5_STATE),
                gspec(2 * S5_STATE, S5_ROW), gspec(2 * S5_STATE, S5_ROW), gspec(4, 2 * S5_STATE), gspec(1, S5_ROW)]
    args = [u, ops["m"], ops["sf"], ops["sb"], ops["yf"], ops["yb"], ops["lam"], ops["dv"]]
    if has_init:
        in_specs.append(gspec(2, bsz, 2 * S5_STATE))
        args.append(h0)
    return pl.pallas_call(
        functools.partial(_s5_kernel, n=n, bsz=bsz, has_init=has_init),
        grid=(S5_GROUPS // S5_GB,),
        in_specs=in_specs,
        out_specs=[gspec(r, S5_ROW), gspec(2, bsz, 2 * S5_STATE)],
        out_shape=[jax.ShapeDtypeStruct((S5_GROUPS, r, S5_ROW), F32),
                   jax.ShapeDtypeStruct((S5_GROUPS, 2, bsz, 2 * S5_STATE), F32)],
        scratch_shapes=[pltpu.VMEM((S5_GB, r, 2 * S5_STATE), F32)] * 4,
        compiler_params=pltpu.CompilerParams(vmem_limit_bytes=VMEM_LIMIT),
        name="s5_scan",
    )(*args)


def _s5_operators(a_re, a_im, log_dt, b_re, b_im, c_re, c_im, dvec):
    dt = jnp.exp(log_dt)[..., None]
    m = jnp.arange(S5_L + 1, dtype=F32)[:, None, None, None]
    mag = jnp.exp(m * (a_re * dt))
    pr, pi = mag * jnp.cos(m * (a_im * dt)), mag * jnp.sin(m * (a_im * dt))
    den = a_re * a_re + a_im * a_im
    n_re = pr[1] - 1.0
    f_re = (n_re * a_re + pi[1] * a_im) / den
    f_im = (pi[1] * a_re - n_re * a_im) / den
    bb_re = f_re[..., None] * b_re - f_im[..., None] * b_im
    bb_im = f_re[..., None] * b_im + f_im[..., None] * b_re
    pb_re = pr[..., None] * bb_re - pi[..., None] * bb_im
    pb_im = pr[..., None] * bb_im + pi[..., None] * bb_re
    kern = (jnp.einsum('gcp,mdgpe->mdgce', c_re, pb_re, precision=HIGHEST)
            - jnp.einsum('gcp,mdgpe->mdgce', c_im, pb_im, precision=HIGHEST))
    jp = np.arange(S5_L)[:, None]
    jo = np.arange(S5_L)[None, :]
    lag_f = np.clip(jo - jp, 0, S5_L)
    lag_b = np.clip(jp - jo, 0, S5_L)
    kf = jnp.where((jo >= jp)[:, :, None, None, None], kern[lag_f, 0], 0.0)
    kb = jnp.where((jp >= jo)[:, :, None, None, None], kern[lag_b, 1], 0.0)
    m_op = (kf + kb).transpose(2, 0, 4, 1, 3).reshape(S5_GROUPS, S5_ROW, S5_ROW)

    def state_in(pw_idx, d):
        re = pb_re[pw_idx, d].transpose(1, 0, 3, 2).reshape(S5_GROUPS, S5_ROW, S5_STATE)
        im = pb_im[pw_idx, d].transpose(1, 0, 3, 2).reshape(S5_GROUPS, S5_ROW, S5_STATE)
        return jnp.concatenate([re, im], axis=-1)

    def state_out(pw_idx, d):
        cr, ci = c_re[None], c_im[None]
        wr = cr * pr[pw_idx, d][:, :, None, :] - ci * pi[pw_idx, d][:, :, None, :]
        wi = cr * pi[pw_idx, d][:, :, None, :] + ci * pr[pw_idx, d][:, :, None, :]
        re = wr.transpose(1, 3, 0, 2).reshape(S5_GROUPS, S5_STATE, S5_ROW)
        im = (-wi).transpose(1, 3, 0, 2).reshape(S5_GROUPS, S5_STATE, S5_ROW)
        return jnp.concatenate([re, im], axis=1)

    ar = np.arange(S5_L)
    lam = jnp.stack([jnp.concatenate([pr[S5_L, 0], pr[S5_L, 0]], -1), jnp.concatenate([-pi[S5_L, 0], pi[S5_L, 0]], -1),
                     jnp.concatenate([pr[S5_L, 1], pr[S5_L, 1]], -1), jnp.concatenate([-pi[S5_L, 1], pi[S5_L, 1]], -1)],
                    axis=1)
    return dict(
        m=m_op.astype(BF16),
        sf=state_in(S5_L - 1 - ar, 0).astype(BF16), sb=state_in(ar, 1).astype(BF16),
        yf=state_out(ar + 1, 0).astype(BF16), yb=state_out(S5_L - ar, 1).astype(BF16),
        lam=lam.astype(F32),
        dv=jnp.tile(dvec.reshape(S5_GROUPS, 1, S5_CH), (1, 1, S5_L)).astype(F32),
    )


def _s5_rows(su, bsz, t):
    n = t // S5_L
    return su.reshape(bsz, n, S5_L, S5_GROUPS, S5_CH).transpose(3, 1, 0, 2, 4).reshape(S5_GROUPS, n * bsz, S5_ROW)


def _s5_unrows(y, bsz, t):
    n = t // S5_L
    return y.reshape(S5_GROUPS, n, bsz, S5_L, S5_CH).transpose(2, 1, 3, 0, 4).reshape(bsz * t, S5_W)


def _outproj_kernel(yrg_ref, ys_ref, x_ref, mod_ref, n2_ref, wo_ref, gw_ref, gb_ref, rw_ref, rb_ref,
                    x1_ref, xs_ref, pos_ref, wt_ref, meta_ref):
    ys = ys_ref[...]
    s = 0.5 * ys * (1.0 + jnp.tanh(np.sqrt(2.0 / np.pi).astype(np.float32) * (ys + 0.044715 * (ys * ys * ys))))
    s = s * _sigmoid(jnp.dot(s.astype(BF16), gw_ref[...], preferred_element_type=F32) + gb_ref[...])
    m = (jnp.dot(yrg_ref[...].astype(BF16), wo_ref[0:2 * HG_W, :], preferred_element_type=F32)
         + jnp.dot(s.astype(BF16), wo_ref[2 * HG_W:, :], preferred_element_type=F32))
    x1 = x_ref[...] + mod_ref[2:3, :] * m
    x1_ref[...] = x1
    h = x1 * lax.rsqrt(jnp.mean(x1 * x1, axis=-1, keepdims=True) + EPS) * n2_ref[...]
    h = h * (1.0 + mod_ref[4:5, :]) + mod_ref[3:4, :]

    logits = jnp.dot(h, rw_ref[...], precision=HIGHEST, preferred_element_type=F32) + rb_ref[...]
    lane = lax.broadcasted_iota(jnp.int32, (TM, LANES), 1).astype(F32)
    cur = logits
    vals, idxs = [], []
    for _ in range(TOP_K):
        mx = jnp.max(cur, axis=-1, keepdims=True)
        am = jnp.min(jnp.where(cur == mx, lane, float(LANES)), axis=-1, keepdims=True)
        vals.append(mx)
        idxs.append(am)
        cur = jnp.where(lane == am, -jnp.inf, cur)
    es = [jnp.exp(v - vals[0]) for v in vals]
    den = es[0] + es[1] + es[2] + es[3]

    hit = [(lane == am) for am in idxs]
    assign = (hit[0] | hit[1] | hit[2] | hit[3]).astype(BF16)
    ti = lax.broadcasted_iota(jnp.int32, (TM, TM), 0)
    tj = lax.broadcasted_iota(jnp.int32, (TM, TM), 1)
    rank = jnp.dot((tj < ti).astype(BF16), assign, preferred_element_type=F32)
    cnt = jnp.sum(assign.astype(F32), axis=0, keepdims=True)
    cnt_al = jnp.floor((cnt + (SEG_ALIGN - 1)) * (1.0 / SEG_ALIGN)) * SEG_ALIGN
    ei = lax.broadcasted_iota(jnp.int32, (LANES, LANES), 0)
    ej = lax.broadcasted_iota(jnp.int32, (LANES, LANES), 1)
    seg = jnp.dot(jnp.broadcast_to(cnt_al, (SUBLANES, LANES)).astype(BF16), (ei < ej).astype(BF16),
                  preferred_element_type=F32)[0:1, :]
    base = seg + rank
    rlane = lax.broadcasted_iota(jnp.int32, (TM, ROWS_T), 1).astype(F32)
    klane = lax.broadcasted_iota(jnp.int32, (TM, LANES), 1)
    onehot = jnp.zeros((TM, ROWS_T), F32)
    pos_out = jnp.zeros((TM, LANES), F32)
    wt_out = jnp.zeros((TM, LANES), F32)
    for kk in range(TOP_K):
        pk = jnp.sum(jnp.where(hit[kk], base, 0.0), axis=-1, keepdims=True)
        onehot = onehot + (rlane == pk).astype(F32)
        pos_out = jnp.where(klane == kk, pk, pos_out)
        wt_out = jnp.where(klane == kk, es[kk] / den, wt_out)
    xs_ref[0] = lax.dot_general(onehot.astype(BF16), h.astype(BF16), (((0,), (0,)), ((), ())),
                                preferred_element_type=F32)
    pos_ref[...] = pos_out.astype(jnp.int32)
    wt_ref[...] = wt_out
    mrow = lax.broadcasted_iota(jnp.int32, (SUBLANES, LANES), 0)
    meta = jnp.where(mrow == 0, jnp.broadcast_to(cnt_al, (SUBLANES, LANES)),
                     jnp.where(mrow == 1, jnp.broadcast_to(seg, (SUBLANES, LANES)), 0.0))
    meta_ref[0] = meta.astype(jnp.int32)


def _outproj_call(yrg, ys5, x, mod_l, n2, wo, gw, gb, rw, rb, n_ctx_tiles, tiles_per_lat):
    n = x.shape[0]
    nt = n // TM
    gid = functools.partial(_group_id, n_ctx_tiles=n_ctx_tiles, tiles_per_lat=tiles_per_lat)
    const = lambda *shape: pl.BlockSpec(shape, lambda i: (0,) * len(shape))
    return pl.pallas_call(
        _outproj_kernel,
        grid=(nt,),
        in_specs=[
            pl.BlockSpec((TM, 2 * HG_W), lambda i: (i, 0)),
            pl.BlockSpec((TM, S5_W), lambda i: (i, 0)),
            pl.BlockSpec((TM, D_MODEL), lambda i: (i, 0)),
            pl.BlockSpec((None, 6, D_MODEL), lambda i: (gid(i), 0, 0)),
            const(1, D_MODEL), const(D_MODEL, D_MODEL), const(S5_W, S5_W), const(1, S5_W),
            const(D_MODEL, LANES), const(1, LANES),
        ],
        out_specs=[
            pl.BlockSpec((TM, D_MODEL), lambda i: (i, 0)),
            pl.BlockSpec((1, ROWS_T, D_MODEL), lambda i: (i, 0, 0)),
            pl.BlockSpec((TM, LANES), lambda i: (i, 0)),
            pl.BlockSpec((TM, LANES), lambda i: (i, 0)),
            pl.BlockSpec((1, SUBLANES, LANES), lambda i: (i, 0, 0)),
        ],
        out_shape=[
            jax.ShapeDtypeStruct((n, D_MODEL), F32),
            jax.ShapeDtypeStruct((nt, ROWS_T, D_MODEL), F32),
            jax.ShapeDtypeStruct((n, LANES), jnp.int32),
            jax.ShapeDtypeStruct((n, LANES), F32),
            jax.ShapeDtypeStruct((nt, SUBLANES, LANES), jnp.int32),
        ],
        compiler_params=pltpu.CompilerParams(vmem_limit_bytes=VMEM_LIMIT),
        name="outproj_router_dispatch",
    )(yrg, ys5, x, mod_l, n2, wo, gw, gb, rw, rb)


def _expert_schedule(meta, nblk):
    nt = meta.shape[0]
    nch = (meta[:, 0, :N_EXPERTS] // SEG_ALIGN).T
    seg_chunk = (meta[:, 1, :N_EXPERTS] // SEG_ALIGN).T
    tot = jnp.sum(nch, axis=1)
    nb = (tot + BLK_CHUNKS - 1) // BLK_CHUNKS
    blk_end = jnp.cumsum(nb)
    blk_start = blk_end - nb
    j = jnp.arange(nblk, dtype=jnp.int32)
    n_active = blk_end[-1]
    be = jnp.searchsorted(blk_end, jnp.minimum(j, n_active - 1), side='right').astype(jnp.int32)
    be = jnp.clip(be, 0, N_EXPERTS - 1)
    local = j - blk_start[be]
    nvalid = jnp.where(j < n_active, jnp.clip(tot[be] - local * BLK_CHUNKS, 0, BLK_CHUNKS), 0).astype(jnp.int32)
    flat_end = jnp.cumsum(nch.reshape(-1))
    flat_start = flat_end - nch.reshape(-1)
    q = jnp.arange(BLK_CHUNKS, dtype=jnp.int32)[None, :]
    gi = local[:, None] * BLK_CHUNKS + q
    g = flat_start[be * nt][:, None] + gi
    f = jnp.clip(jnp.searchsorted(flat_end, g.reshape(-1), side='right').reshape(g.shape), 0, N_EXPERTS * nt - 1)
    tile = f % nt
    src = tile * CHUNKS_T + seg_chunk.reshape(-1)[f] + (g - flat_start[f])
    src = jnp.where(q < nvalid[:, None], src, 0).astype(jnp.int32)
    return be, nvalid, src.reshape(-1)


def _ffn_kernel(be_ref, nv_ref, src_ref, xs_hbm, w1_ref, b1_ref, w2_ref, b2_ref, ys_hbm,
                xbuf, ybuf, gsem, ssem, *, nblk):
    j = pl.program_id(0)
    slot = j % 2

    def gather_copy(blk, q, s):
        return pltpu.make_async_copy(xs_hbm.at[src_ref[blk * BLK_CHUNKS + q]], xbuf.at[s, q], gsem.at[s])

    def scatter_copy(blk, q, s):
        return pltpu.make_async_copy(ybuf.at[s, q], ys_hbm.at[src_ref[blk * BLK_CHUNKS + q]], ssem.at[s])

    def start_gather(blk, s):
        lax.fori_loop(0, nv_ref[blk], lambda q, _: (gather_copy(blk, q, s).start(), 0)[1], 0)

    def wait_gather(blk, s):
        lax.fori_loop(0, nv_ref[blk], lambda q, _: (gather_copy(blk, q, s).wait(), 0)[1], 0)

    def start_scatter(blk, s):
        lax.fori_loop(0, nv_ref[blk], lambda q, _: (scatter_copy(blk, q, s).start(), 0)[1], 0)

    def wait_scatter(blk, s):
        lax.fori_loop(0, nv_ref[blk], lambda q, _: (scatter_copy(blk, q, s).wait(), 0)[1], 0)

    @pl.when(j == 0)
    def _():
        xbuf[...] = jnp.zeros_like(xbuf)
        start_gather(0, 0)

    @pl.when(j + 1 < nblk)
    def _():
        start_gather(j + 1, 1 - slot)

    wait_gather(j, slot)

    @pl.when(j >= 2)
    def _():
        wait_scatter(j - 2, slot)

    @pl.when(nv_ref[j] > 0)
    def _():
        x = xbuf[slot].reshape(BLK_ROWS, D_MODEL).astype(BF16)
        gu = jnp.dot(x, w1_ref[...], preferred_element_type=F32) + b1_ref[...]
        gate = jnp.minimum(gu[:, :D_FF], SWIGLU_LIMIT)
        up = jnp.clip(gu[:, D_FF:], -SWIGLU_LIMIT, SWIGLU_LIMIT)
        act = (up + 1.0) * gate * _sigmoid(SWIGLU_ALPHA * gate)
        y = jnp.dot(act.astype(BF16), w2_ref[...], preferred_element_type=F32) + b2_ref[...]
        ybuf[slot] = y.reshape(BLK_CHUNKS, SEG_ALIGN, D_MODEL)

    start_scatter(j, slot)

    @pl.when(j == nblk - 1)
    def _():
        wait_scatter(j, slot)
        if nblk >= 2:
            wait_scatter(j - 1, 1 - slot)


def _ffn_call(be, nvalid, src, xs_chunks, w1, b1, w2, b2):
    nblk = be.shape[0]
    nch = xs_chunks.shape[0]
    grid_spec = pltpu.PrefetchScalarGridSpec(
        num_scalar_prefetch=3,
        grid=(nblk,),
        in_specs=[
            pl.BlockSpec(memory_space=pl.ANY),
            pl.BlockSpec((None, D_MODEL, 2 * D_FF), lambda j, be, nv, src: (be[j], 0, 0)),
            pl.BlockSpec((None, 1, 2 * D_FF), lambda j, be, nv, src: (be[j], 0, 0)),
            pl.BlockSpec((None, D_FF, D_MODEL), lambda j, be, nv, src: (be[j], 0, 0)),
            pl.BlockSpec((None, 1, D_MODEL), lambda j, be, nv, src: (be[j], 0, 0)),
        ],
        out_specs=pl.BlockSpec(memory_space=pl.ANY),
        scratch_shapes=[
            pltpu.VMEM((2, BLK_CHUNKS, SEG_ALIGN, D_MODEL), F32),
            pltpu.VMEM((2, BLK_CHUNKS, SEG_ALIGN, D_MODEL), F32),
            pltpu.SemaphoreType.DMA((2,)),
            pltpu.SemaphoreType.DMA((2,)),
        ],
    )
    return pl.pallas_call(
        functools.partial(_ffn_kernel, nblk=nblk),
        grid_spec=grid_spec,
        out_shape=jax.ShapeDtypeStruct((nch, SEG_ALIGN, D_MODEL), F32),
        input_output_aliases={3: 0},
        compiler_params=pltpu.CompilerParams(vmem_limit_bytes=VMEM_LIMIT, dimension_semantics=("arbitrary",)),
        name="expert_ffn",
    )(be, nvalid, src, xs_chunks, w1, b1, w2, b2)


def _combine_kernel(ys_ref, pos_ref, wt_ref, x1_ref, mod_ref, fg_ref, *out_refs, final):
    ys = ys_ref[0].astype(BF16)
    rlane = lax.broadcasted_iota(jnp.int32, (TM, ROWS_T), 1)
    w = jnp.zeros((TM, ROWS_T), F32)
    for kk in range(TOP_K):
        w = w + jnp.where(rlane == pos_ref[:, kk:kk + 1], wt_ref[:, kk:kk + 1], 0.0)
    moe = jnp.dot(w.astype(BF16), ys, preferred_element_type=F32)
    x2 = x1_ref[...] + mod_ref[5:6, :] * moe
    out_refs[0][...] = x2
    if final:
        out_refs[1][...] = x2 * lax.rsqrt(jnp.mean(x2 * x2, axis=-1, keepdims=True) + EPS) * fg_ref[...]


def _combine_call(ys, pos, wts, x1, mod_l, fg, n_ctx_tiles, tiles_per_lat, final):
    n = x1.shape[0]
    nt = n // TM
    gid = functools.partial(_group_id, n_ctx_tiles=n_ctx_tiles, tiles_per_lat=tiles_per_lat)
    tok = pl.BlockSpec((TM, D_MODEL), lambda i: (i, 0))
    res = pl.pallas_call(
        functools.partial(_combine_kernel, final=final),
        grid=(nt,),
        in_specs=[
            pl.BlockSpec((1, ROWS_T, D_MODEL), lambda i: (i, 0, 0)),
            pl.BlockSpec((TM, LANES), lambda i: (i, 0)),
            pl.BlockSpec((TM, LANES), lambda i: (i, 0)),
            tok,
            pl.BlockSpec((None, 6, D_MODEL), lambda i: (gid(i), 0, 0)),
            pl.BlockSpec((1, D_MODEL), lambda i: (0, 0)),
        ],
        out_specs=[tok, tok] if final else [tok],
        out_shape=[jax.ShapeDtypeStruct((n, D_MODEL), F32)] * (2 if final else 1),
        compiler_params=pltpu.CompilerParams(vmem_limit_bytes=VMEM_LIMIT),
        name="moe_combine",
    )(ys, pos, wts, x1, mod_l, fg)
    return res


def _rope_tables(t):
    pos = np.arange(t)
    nf = HEAD_DIM // 4
    inv = jnp.asarray(ROPE_BASE, F32) ** (-jnp.arange(nf, dtype=F32) / nf)
    ang = jnp.concatenate([jnp.asarray(pos // GRID_W, F32)[:, None] * inv,
                           jnp.asarray(pos % GRID_W, F32)[:, None] * inv], axis=-1)
    cos, sin = jnp.cos(ang), jnp.sin(ang)
    return (jnp.tile(jnp.concatenate([cos, cos], -1), (1, N_HEADS)),
            jnp.tile(jnp.concatenate([-sin, sin], -1), (1, N_HEADS)))


def _block_diag_t(s):
    eye = jnp.eye(N_HEADS, dtype=s.dtype)
    out = jnp.einsum('...hkv,hg->...hvgk', s, eye)
    return out.reshape(s.shape[:-3] + (HG_W, HG_W))


def _diag_blocks(st):
    blocks = [st[..., h * HEAD_DIM:(h + 1) * HEAD_DIM, h * HEAD_DIM:(h + 1) * HEAD_DIM] for h in range(N_HEADS)]
    return jnp.swapaxes(jnp.stack(blocks, axis=-3), -1, -2)


def kernel(x_prompt, x_sample, state_ret, state_gla, state_s5_re, state_s5_im, c, c_ctx, norm1_g, norm2_g, final_g, ada_w, ada_b, w_in, w_out, ret_decay, gla_w_lr, gla_b_lr, gla_norm_g, s5_a_re, s5_a_im, s5_log_dt, s5_b_re, s5_b_im, s5_c_re, s5_c_im, s5_d, s5_glu_w, s5_glu_b, router_w, router_b, moe_w1, moe_b1, moe_w2, moe_b2):
    bc, tc, d = x_prompt.shape
    bl, tl, _ = x_sample.shape
    depth = w_in.shape[0]
    nc, nl = bc * tc, bl * tl
    n = nc + nl
    assert d == D_MODEL and nc % TM == 0 and tl % TM == 0 and tc % CHUNK == 0 and tl % CHUNK == 0
    assert 1 + bl <= SUBLANES
    n_ctx_tiles, tiles_per_lat = nc // TM, tl // TM
    nt = n // TM
    nblk = nt * CHUNKS_T // BLK_CHUNKS + N_EXPERTS

    x = jnp.concatenate([x_prompt.reshape(nc, d), x_sample.reshape(nl, d)], axis=0)
    cond8 = jnp.concatenate([c_ctx[None, :], c, jnp.zeros((SUBLANES - 1 - bl, d), F32)], axis=0)
    mod = _ada_call(cond8, ada_w, ada_b).reshape(depth, SUBLANES, 6, d)
    rope = _rope_tables(tl)
    fg = final_g.reshape(1, d)

    ret_states, gla_states, re_states, im_states = [], [], [], []
    y_final = None
    for l in range(depth):
        wl = w_in[l]
        w_cat = jnp.concatenate(
            [wl[:, :8 * HG_W], wl[:, 8 * HG_W + GLA_RANK:],
             jnp.pad(wl[:, 8 * HG_W:8 * HG_W + GLA_RANK], ((0, 0), (0, LANES - GLA_RANK)))], axis=1).astype(BF16)
        wlr = jnp.pad(gla_w_lr[l], ((0, 0), (0, LANES - GLA_RANK), (0, 0)))
        blr = gla_b_lr[l].reshape(2, 1, HG_W)
        pa, su = _inproj_call(x, mod[l], norm1_g[l].reshape(1, d), w_cat, wlr, blr, n_ctx_tiles, tiles_per_lat)

        log_gamma = jnp.log1p(-jnp.exp(ret_decay[l]))
        lgam = jnp.repeat(log_gamma, HEAD_DIM, axis=1)
        gn = jnp.tile(gla_norm_g[l], N_HEADS).reshape(1, HG_W)
        y_c, st_c = _scan_call(pa, 0, bc, tc, log_gamma, lgam, gn, None, None, True)
        st0 = jnp.stack([_block_diag_t(state_ret[:, l]), _block_diag_t(state_gla[:, l])], axis=2)
        y_l, _ = _scan_call(pa, nc // CHUNK, bl, tl, log_gamma, lgam, gn, rope, st0, False)
        ret_states.append(_diag_blocks(st_c[:, :, 0]))
        gla_states.append(_diag_blocks(st_c[:, :, 1]))

        ops = _s5_operators(s5_a_re[l], s5_a_im[l], s5_log_dt[l], s5_b_re[l], s5_b_im[l],
                            s5_c_re[l], s5_c_im[l], s5_d[l])
        ys_c, fin_c = _s5_call(_s5_rows(su[:nc], bc, tc), ops, bc, tc, None)
        h0 = jnp.concatenate([state_s5_re[:, l], state_s5_im[:, l]], axis=-1).transpose(2, 1, 0, 3)
        ys_l, _ = _s5_call(_s5_rows(su[nc:], bl, tl), ops, bl, tl, h0)
        re_states.append(fin_c[..., :S5_STATE].transpose(2, 1, 0, 3))
        im_states.append(fin_c[..., S5_STATE:].transpose(2, 1, 0, 3))

        yrg = jnp.concatenate([y_c, y_l], axis=0)
        ys5 = jnp.concatenate([_s5_unrows(ys_c, bc, tc), _s5_unrows(ys_l, bl, tl)], axis=0)
        rw = jnp.pad(router_w[l], ((0, 0), (0, LANES - N_EXPERTS)))
        rb = jnp.concatenate([router_b[l], jnp.full((LANES - N_EXPERTS,), NEG_BIG, F32)]).reshape(1, LANES)
        x1, xs, pos, wts, meta = _outproj_call(
            yrg, ys5, x, mod[l], norm2_g[l].reshape(1, d), w_out[l].astype(BF16), s5_glu_w[l].astype(BF16),
            s5_glu_b[l].reshape(1, S5_W), rw, rb, n_ctx_tiles, tiles_per_lat)

        be, nvalid, src = _expert_schedule(meta, nblk)
        ys = _ffn_call(be, nvalid, src, xs.reshape(nt * CHUNKS_T, SEG_ALIGN, d),
                       moe_w1[l].astype(BF16), moe_b1[l].reshape(N_EXPERTS, 1, 2 * D_FF),
                       moe_w2[l].astype(BF16), moe_b2[l].reshape(N_EXPERTS, 1, d))
        res = _combine_call(ys.reshape(nt, ROWS_T, d), pos, wts, x1, mod[l], fg,
                            n_ctx_tiles, tiles_per_lat, l == depth - 1)
        x = res[0]
        if l == depth - 1:
            y_final = res[1]

    y_prompt = y_final[:nc].reshape(bc, tc, d)
    y_sample = y_final[nc:].reshape(bl, tl, d)
    return (y_prompt, y_sample, jnp.stack(ret_states, axis=1), jnp.stack(gla_states, axis=1),
            jnp.stack(re_states, axis=1), jnp.stack(im_states, axis=1))
```

```python
import functools

import numpy as np
import jax
import jax.numpy as jnp
from jax import lax
from jax.experimental import pallas as pl
from jax.experimental.pallas import tpu as pltpu

F32 = jnp.float32
BF16 = jnp.bfloat16
HIGHEST = lax.Precision.HIGHEST

D_MODEL = 1024
GRID_W = 64
CHUNK = 128
HEAD_DIM = 64
N_HEADS = 4
HG_W = N_HEADS * HEAD_DIM
S5_W = 512
S5_CH = 16
S5_GROUPS = 32
S5_STATE = 64
S5_L = 16
S5_ROW = S5_L * S5_CH
GLA_RANK = 16
GLA_TAU = 16.0
N_EXPERTS = 32
TOP_K = 4
D_FF = 1024
SWIGLU_LIMIT = 7.0
SWIGLU_ALPHA = 1.702
ROPE_BASE = 10000.0
EPS = 1e-6

LANES = 128
SUBLANES = 8
TM = 256
SEG_ALIGN = SUBLANES
ROWS_T = 1280
CHUNKS_T = ROWS_T // SEG_ALIGN
BLK_CHUNKS = 64
BLK_ROWS = BLK_CHUNKS * SEG_ALIGN
NEG_BIG = -1e30
VMEM_LIMIT = 56 * 1024 * 1024

PA_W = 8 * HG_W + 2 * HG_W
W_CAT = 8 * HG_W + S5_W + LANES


def _sigmoid(x):
    return 1.0 / (1.0 + jnp.exp(-x))


def _group_id(i, n_ctx_tiles, tiles_per_lat):
    return jnp.where(i < n_ctx_tiles, 0, 1 + (i - n_ctx_tiles) // tiles_per_lat)


def _ada_kernel(c_ref, w_ref, b_ref, o_ref):
    c = c_ref[...]
    s = c * _sigmoid(c)
    o_ref[0] = jnp.dot(s, w_ref[0], precision=HIGHEST, preferred_element_type=F32) + b_ref[0]


def _ada_call(cond8, ada_w, ada_b):
    depth, d, n6 = ada_w.shape
    tn = 1024
    return pl.pallas_call(
        _ada_kernel,
        grid=(depth, n6 // tn),
        in_specs=[
            pl.BlockSpec((SUBLANES, d), lambda l, j: (0, 0)),
            pl.BlockSpec((1, d, tn), lambda l, j: (l, 0, j)),
            pl.BlockSpec((1, 1, tn), lambda l, j: (l, 0, j)),
        ],
        out_specs=pl.BlockSpec((1, SUBLANES, tn), lambda l, j: (l, 0, j)),
        out_shape=jax.ShapeDtypeStruct((depth, SUBLANES, n6), F32),
        compiler_params=pltpu.CompilerParams(vmem_limit_bytes=VMEM_LIMIT),
        name="ada_mod",
    )(cond8, ada_w, ada_b.reshape(depth, 1, n6))


def _inproj_kernel(x_ref, mod_ref, n1_ref, w_ref, wlr_ref, blr_ref, pa_ref, su_ref):
    x = x_ref[...]
    h = x * lax.rsqrt(jnp.mean(x * x, axis=-1, keepdims=True) + EPS) * n1_ref[...]
    h = h * (1.0 + mod_ref[1:2, :]) + mod_ref[0:1, :]
    r = jnp.dot(h.astype(BF16), w_ref[...], preferred_element_type=F32)
    pa_ref[:, : 8 * HG_W] = r[:, : 8 * HG_W]
    su_ref[...] = r[:, 8 * HG_W: 8 * HG_W + S5_W]
    glr = r[:, 8 * HG_W + S5_W:]
    for d in range(2):
        z = jnp.dot(glr, wlr_ref[d], precision=HIGHEST, preferred_element_type=F32) + blr_ref[d]
        log_sig = jnp.minimum(z, 0.0) - jnp.log(1.0 + jnp.exp(-jnp.abs(z)))
        pa_ref[:, (8 + d) * HG_W: (9 + d) * HG_W] = log_sig * (1.0 / GLA_TAU)


def _inproj_call(x, mod_l, n1, w_cat, wlr, blr, n_ctx_tiles, tiles_per_lat):
    n = x.shape[0]
    gid = functools.partial(_group_id, n_ctx_tiles=n_ctx_tiles, tiles_per_lat=tiles_per_lat)
    return pl.pallas_call(
        _inproj_kernel,
        grid=(n // TM,),
        in_specs=[
            pl.BlockSpec((TM, D_MODEL), lambda i: (i, 0)),
            pl.BlockSpec((None, 6, D_MODEL), lambda i: (gid(i), 0, 0)),
            pl.BlockSpec((1, D_MODEL), lambda i: (0, 0)),
            pl.BlockSpec((D_MODEL, W_CAT), lambda i: (0, 0)),
            pl.BlockSpec((2, LANES, HG_W), lambda i: (0, 0, 0)),
            pl.BlockSpec((2, 1, HG_W), lambda i: (0, 0, 0)),
        ],
        out_specs=[
            pl.BlockSpec((TM, PA_W), lambda i: (i, 0)),
            pl.BlockSpec((TM, S5_W), lambda i: (i, 0)),
        ],
        out_shape=[
            jax.ShapeDtypeStruct((n, PA_W), F32),
            jax.ShapeDtypeStruct((n, S5_W), F32),
        ],
        compiler_params=pltpu.CompilerParams(vmem_limit_bytes=VMEM_LIMIT),
        name="norm1_inproj",
    )(x, mod_l, n1, w_cat, wlr, blr)


def _scan_kernel(*refs, n, use_rope, has_init, want_final):
    refs = list(refs)
    lg_ref = refs.pop(0)
    pa_ref = refs.pop(0)
    if use_rope:
        cos_ref = refs.pop(0)
        sin_ref = refs.pop(0)
    lgam_ref = refs.pop(0)
    gn_ref = refs.pop(0)
    if has_init:
        st0_ref = refs.pop(0)
    y_ref = refs.pop(0)
    if want_final:
        stf_ref = refs.pop(0)
    st_scr = refs.pop(0)
    of_scr = refs.pop(0)

    d = pl.program_id(1)
    c = pl.program_id(2)
    cc = jnp.where(d == 0, c, n - 1 - c)

    @pl.when(c == 0)
    def _():
        if has_init:
            st_scr[...] = st0_ref[...]
        else:
            st_scr[...] = jnp.zeros_like(st_scr)

    ii = lax.broadcasted_iota(jnp.int32, (CHUNK, CHUNK), 0)
    jj = lax.broadcasted_iota(jnp.int32, (CHUNK, CHUNK), 1)
    rel = (1 - 2 * d) * (ii - jj)
    mask = rel >= 0
    relf = rel.astype(F32)
    row = lax.broadcasted_iota(jnp.int32, (CHUNK, 1), 0)
    p1 = jnp.where(d == 0, row + 1, CHUNK - row).astype(F32)
    lane = lax.broadcasted_iota(jnp.int32, (1, HG_W), 1)
    head_mask = [(lane // HEAD_DIM == h).astype(F32) for h in range(N_HEADS)]
    bi = lax.broadcasted_iota(jnp.int32, (HG_W, HG_W), 0) // HEAD_DIM
    bj = lax.broadcasted_iota(jnp.int32, (HG_W, HG_W), 1) // HEAD_DIM
    block_diag = bi == bj

    nt_dims = (((1,), (1,)), ((), ()))
    tn_dims = (((0,), (0,)), ((), ()))

    def group(q, k, v, q_dec, k_upd, chunk_decay, score_fn, g_idx):
        kb = k.astype(BF16)
        vb = v.astype(BF16)
        o = jnp.zeros((CHUNK, HG_W), F32)
        for h in range(N_HEADS):
            s = lax.dot_general((q * head_mask[h]).astype(BF16), kb, nt_dims, preferred_element_type=F32)
            s = score_fn(s, h)
            o = o + jnp.dot(s.astype(BF16), vb, preferred_element_type=F32) * head_mask[h]
        st = st_scr[g_idx]
        o = o + lax.dot_general(q_dec.astype(BF16), st.astype(BF16), nt_dims, preferred_element_type=F32)
        upd = lax.dot_general(vb, k_upd.astype(BF16), tn_dims, preferred_element_type=F32)
        st_scr[g_idx] = st * chunk_decay + jnp.where(block_diag, upd, 0.0)
        return o

    def rope(x):
        partner = jnp.where((lane % HEAD_DIM) < HEAD_DIM // 2,
                            pltpu.roll(x, HG_W - HEAD_DIM // 2, 1), pltpu.roll(x, HEAD_DIM // 2, 1))
        return x * cos_ref[...] + partner * sin_ref[...]

    q = pa_ref[:, 0:HG_W]
    k = pa_ref[:, HG_W:2 * HG_W] * (HEAD_DIM ** -0.5)
    v = pa_ref[:, 2 * HG_W:3 * HG_W]
    if use_rope:
        q = rope(q)
        k = rope(k)
    lgl = lgam_ref[pl.ds(d, 1), :]
    q_dec = q * jnp.exp(p1 * lgl)
    k_upd = k * jnp.exp((CHUNK - p1) * lgl)
    chunk_decay = jnp.exp(CHUNK * lgl)

    def ret_scores(s, h):
        return s * jnp.where(mask, jnp.exp(relf * lg_ref[d, h]), 0.0)

    o_ret = group(q, k, v, q_dec, k_upd, chunk_decay, ret_scores, 0)

    q = pa_ref[:, 4 * HG_W:5 * HG_W] * (HEAD_DIM ** -0.5)
    k = pa_ref[:, 5 * HG_W:6 * HG_W]
    v = pa_ref[:, 6 * HG_W:7 * HG_W]
    la = jnp.where(d == 0, pa_ref[:, 8 * HG_W:9 * HG_W], pa_ref[:, 9 * HG_W:10 * HG_W])
    b = jnp.dot(mask.astype(F32), la, precision=HIGHEST, preferred_element_type=F32)
    b_last = jnp.sum(la, axis=0, keepdims=True)
    q_dec = q * jnp.exp(b)
    k_inv = k * jnp.exp(-b)
    k_upd = k * jnp.exp(b_last - b)
    chunk_decay = jnp.exp(b_last)

    def gla_scores(s, h):
        return jnp.where(mask, s, 0.0)

    o_gla = group(q_dec, k_inv, v, q_dec, k_upd, chunk_decay, gla_scores, 1)

    @pl.when(d == 0)
    def _():
        of_scr[c, :, 0:HG_W] = o_ret
        of_scr[c, :, HG_W:2 * HG_W] = o_gla

    @pl.when(d == 1)
    def _():
        mean_mat = jnp.where(block_diag, 1.0 / HEAD_DIM, 0.0).astype(F32)

        def head_rms(o):
            ms = jnp.dot(o * o, mean_mat, precision=HIGHEST, preferred_element_type=F32)
            return o * lax.rsqrt(ms + EPS)

        o1 = o_ret + of_scr[cc, :, 0:HG_W]
        g1 = pa_ref[:, 3 * HG_W:4 * HG_W]
        y_ref[:, 0:HG_W] = head_rms(o1) * (g1 * _sigmoid(g1))
        o2 = o_gla + of_scr[cc, :, HG_W:2 * HG_W]
        g2 = pa_ref[:, 7 * HG_W:8 * HG_W]
        y_ref[:, HG_W:2 * HG_W] = head_rms(o2) * gn_ref[...] * (g2 * _sigmoid(g2))

    if want_final:
        @pl.when(c == n - 1)
        def _():
            stf_ref[...] = st_scr[...]


def _scan_call(pa, row_block_off, bsz, t, lg, lgam, gn, rope, st0, want_final):
    n = t // CHUNK
    use_rope = rope is not None
    has_init = st0 is not None

    def chunk_idx(d, c):
        return jnp.where(d == 0, c, n - 1 - c)

    in_specs = [
        pl.BlockSpec(memory_space=pltpu.SMEM),
        pl.BlockSpec((CHUNK, PA_W), lambda b, d, c: (row_block_off + b * n + chunk_idx(d, c), 0)),
    ]
    args = [lg, pa]
    if use_rope:
        in_specs += [pl.BlockSpec((CHUNK, HG_W), lambda b, d, c: (chunk_idx(d, c), 0))] * 2
        args += list(rope)
    in_specs += [pl.BlockSpec((2, HG_W), lambda b, d, c: (0, 0)),
                 pl.BlockSpec((1, HG_W), lambda b, d, c: (0, 0))]
    args += [lgam, gn]
    if has_init:
        in_specs.append(pl.BlockSpec((None, None, 2, HG_W, HG_W), lambda b, d, c: (b, d, 0, 0, 0)))
        args.append(st0)
    out_specs = [pl.BlockSpec((CHUNK, 2 * HG_W),
                              lambda b, d, c: (b * n + jnp.where(d == 0, n - 1, n - 1 - c), 0))]
    out_shape = [jax.ShapeDtypeStruct((bsz * t, 2 * HG_W), F32)]
    if want_final:
        out_specs.append(pl.BlockSpec((None, None, 2, HG_W, HG_W), lambda b, d, c: (b, d, 0, 0, 0)))
        out_shape.append(jax.ShapeDtypeStruct((bsz, 2, 2, HG_W, HG_W), F32))
    res = pl.pallas_call(
        functools.partial(_scan_kernel, n=n, use_rope=use_rope, has_init=has_init, want_final=want_final),
        grid=(bsz, 2, n),
        in_specs=in_specs,
        out_specs=out_specs,
        out_shape=out_shape,
        scratch_shapes=[pltpu.VMEM((2, HG_W, HG_W), F32), pltpu.VMEM((n, CHUNK, 2 * HG_W), F32)],
        compiler_params=pltpu.CompilerParams(vmem_limit_bytes=VMEM_LIMIT),
        name="ret_gla_scan",
    )(*args)
    return res if want_final else (res[0], None)


def _s5_kernel(*refs, n, bsz, has_init):
    refs = list(refs)
    u_ref = refs.pop(0)
    m_ref = refs.pop(0)
    sf_ref = refs.pop(0)
    sb_ref = refs.pop(0)
    yf_ref = refs.pop(0)
    yb_ref = refs.pop(0)
    lam_ref = refs.pop(0)
    dv_ref = refs.pop(0)
    if has_init:
        h0_ref = refs.pop(0)
    y_ref = refs.pop(0)
    fin_ref = refs.pop(0)
    sfs, sbs, hpf, hnb = refs

    bm_dims = (((2,), (1,)), ((0,), (0,)))
    u = u_ref[...]
    ub = u.astype(BF16)
    y = lax.dot_general(ub, m_ref[...], bm_dims, preferred_element_type=F32) + u * dv_ref[...]
    sfs[...] = lax.dot_general(ub, sf_ref[...], bm_dims, preferred_element_type=F32)
    sbs[...] = lax.dot_general(ub, sb_ref[...], bm_dims, preferred_element_type=F32)

    gb = u.shape[0]
    a_f, s_f = lam_ref[:, 0:1, :], lam_ref[:, 1:2, :]
    a_b, s_b = lam_ref[:, 2:3, :], lam_ref[:, 3:4, :]
    if has_init:
        hf = h0_ref[:, 0]
        hb = h0_ref[:, 1]
    else:
        hf = jnp.zeros((gb, bsz, 2 * S5_STATE), F32)
        hb = jnp.zeros((gb, bsz, 2 * S5_STATE), F32)
    for c in range(n):
        lo, hi = c * bsz, (c + 1) * bsz
        hpf[:, lo:hi, :] = hf
        hf = hf * a_f + pltpu.roll(hf, S5_STATE, 2) * s_f + sfs[:, lo:hi, :]
        cb = n - 1 - c
        lo, hi = cb * bsz, (cb + 1) * bsz
        hnb[:, lo:hi, :] = hb
        hb = hb * a_b + pltpu.roll(hb, S5_STATE, 2) * s_b + sbs[:, lo:hi, :]
    fin_ref[:, 0] = hf
    fin_ref[:, 1] = hb
    y = y + lax.dot_general(hpf[...].astype(BF16), yf_ref[...], bm_dims, preferred_element_type=F32)
    y = y + lax.dot_general(hnb[...].astype(BF16), yb_ref[...], bm_dims, preferred_element_type=F32)
    y_ref[...] = y


S5_GB = 8


def _s5_call(u, ops, bsz, t, h0):
    n = t // S5_L
    r = n * bsz
    has_init = h0 is not None
    gspec = lambda *shape: pl.BlockSpec((S5_GB,) + shape, lambda g: (g,) + (0,) * len(shape))
    in_specs = [gspec(r, S5_ROW), gspec(S5_ROW, S5_ROW), gspec(S5_ROW, 2 * S5_STATE), gspec(S5_ROW, 2 * S5_STATE),
                gspec(2 * S5_STATE, S5_ROW), gspec(2 * S5_STATE, S5_ROW), gspec(4, 2 * S5_STATE), gspec(1, S5_ROW)]
    args = [u, ops["m"], ops["sf"], ops["sb"], ops["yf"], ops["yb"], ops["lam"], ops["dv"]]
    if has_init:
        in_specs.append(gspec(2, bsz, 2 * S5_STATE))
        args.append(h0)
    return pl.pallas_call(
        functools.partial(_s5_kernel, n=n, bsz=bsz, has_init=has_init),
        grid=(S5_GROUPS // S5_GB,),
        in_specs=in_specs,
        out_specs=[gspec(r, S5_ROW), gspec(2, bsz, 2 * S5_STATE)],
        out_shape=[jax.ShapeDtypeStruct((S5_GROUPS, r, S5_ROW), F32),
                   jax.ShapeDtypeStruct((S5_GROUPS, 2, bsz, 2 * S5_STATE), F32)],
        scratch_shapes=[pltpu.VMEM((S5_GB, r, 2 * S5_STATE), F32)] * 4,
        compiler_params=pltpu.CompilerParams(vmem_limit_bytes=VMEM_LIMIT),
        name="s5_scan",
    )(*args)


def _s5_operators(a_re, a_im, log_dt, b_re, b_im, c_re, c_im, dvec):
    dt = jnp.exp(log_dt)[..., None]
    m = jnp.arange(S5_L + 1, dtype=F32)[:, None, None, None]
    mag = jnp.exp(m * (a_re * dt))
    pr, pi = mag * jnp.cos(m * (a_im * dt)), mag * jnp.sin(m * (a_im * dt))
    den = a_re * a_re + a_im * a_im
    n_re = pr[1] - 1.0
    f_re = (n_re * a_re + pi[1] * a_im) / den
    f_im = (pi[1] * a_re - n_re * a_im) / den
    bb_re = f_re[..., None] * b_re - f_im[..., None] * b_im
    bb_im = f_re[..., None] * b_im + f_im[..., None] * b_re
    pb_re = pr[..., None] * bb_re - pi[..., None] * bb_im
    pb_im = pr[..., None] * bb_im + pi[..., None] * bb_re
    kern = (jnp.einsum('gcp,mdgpe->mdgce', c_re, pb_re, precision=HIGHEST)
            - jnp.einsum('gcp,mdgpe->mdgce', c_im, pb_im, precision=HIGHEST))
    sel = np.zeros((2, S5_L + 1, S5_L, S5_L), np.float32)
    for jp in range(S5_L):
        for jo in range(S5_L):
            sel[0 if jo >= jp else 1, abs(jo - jp), jp, jo] = 1.0
            if jo == jp:
                sel[1, 0, jp, jo] = 1.0
    m_op = jnp.einsum('dmpj,mdgce->gpejc', sel, kern, precision=HIGHEST).reshape(S5_GROUPS, S5_ROW, S5_ROW)

    def state_in(lo, rev, d):
        re, im = pb_re[lo:lo + S5_L, d], pb_im[lo:lo + S5_L, d]
        if rev:
            re, im = jnp.flip(re, 0), jnp.flip(im, 0)
        re = re.transpose(1, 0, 3, 2).reshape(S5_GROUPS, S5_ROW, S5_STATE)
        im = im.transpose(1, 0, 3, 2).reshape(S5_GROUPS, S5_ROW, S5_STATE)
        return jnp.concatenate([re, im], axis=-1)

    def state_out(lo, rev, d):
        qr, qi = pr[lo:lo + S5_L, d], pi[lo:lo + S5_L, d]
        if rev:
            qr, qi = jnp.flip(qr, 0), jnp.flip(qi, 0)
        cr, ci = c_re[None], c_im[None]
        wr = cr * qr[:, :, None, :] - ci * qi[:, :, None, :]
        wi = cr * qi[:, :, None, :] + ci * qr[:, :, None, :]
        re = wr.transpose(1, 3, 0, 2).reshape(S5_GROUPS, S5_STATE, S5_ROW)
        im = (-wi).transpose(1, 3, 0, 2).reshape(S5_GROUPS, S5_STATE, S5_ROW)
        return jnp.concatenate([re, im], axis=1)

    lam = jnp.stack([jnp.concatenate([pr[S5_L, 0], pr[S5_L, 0]], -1), jnp.concatenate([-pi[S5_L, 0], pi[S5_L, 0]], -1),
                     jnp.concatenate([pr[S5_L, 1], pr[S5_L, 1]], -1), jnp.concatenate([-pi[S5_L, 1], pi[S5_L, 1]], -1)],
                    axis=1)
    return dict(
        m=m_op.astype(BF16),
        sf=state_in(0, True, 0).astype(BF16), sb=state_in(0, False, 1).astype(BF16),
        yf=state_out(1, False, 0).astype(BF16), yb=state_out(1, True, 1).astype(BF16),
        lam=lam.astype(F32),
        dv=jnp.tile(dvec.reshape(S5_GROUPS, 1, S5_CH), (1, 1, S5_L)).astype(F32),
    )


def _s5_rows(su, bsz, t):
    n = t // S5_L
    return su.reshape(bsz, n, S5_L, S5_GROUPS, S5_CH).transpose(3, 1, 0, 2, 4).reshape(S5_GROUPS, n * bsz, S5_ROW)


def _s5_unrows(y, bsz, t):
    n = t // S5_L
    return y.reshape(S5_GROUPS, n, bsz, S5_L, S5_CH).transpose(2, 1, 3, 0, 4).reshape(bsz * t, S5_W)


def _outproj_kernel(yrg_ref, ys_ref, x_ref, mod_ref, n2_ref, wo_ref, gw_ref, gb_ref, rw_ref, rb_ref,
                    x1_ref, xs_ref, pos_ref, wt_ref, meta_ref):
    ys = ys_ref[...]
    s = 0.5 * ys * (1.0 + jnp.tanh(np.sqrt(2.0 / np.pi).astype(np.float32) * (ys + 0.044715 * (ys * ys * ys))))
    s = s * _sigmoid(jnp.dot(s.astype(BF16), gw_ref[...], preferred_element_type=F32) + gb_ref[...])
    m = (jnp.dot(yrg_ref[...].astype(BF16), wo_ref[0:2 * HG_W, :], preferred_element_type=F32)
         + jnp.dot(s.astype(BF16), wo_ref[2 * HG_W:, :], preferred_element_type=F32))
    x1 = x_ref[...] + mod_ref[2:3, :] * m
    x1_ref[...] = x1
    h = x1 * lax.rsqrt(jnp.mean(x1 * x1, axis=-1, keepdims=True) + EPS) * n2_ref[...]
    h = h * (1.0 + mod_ref[4:5, :]) + mod_ref[3:4, :]

    logits = jnp.dot(h, rw_ref[...], precision=HIGHEST, preferred_element_type=F32) + rb_ref[...]
    lane = lax.broadcasted_iota(jnp.int32, (TM, LANES), 1).astype(F32)
    cur = logits
    vals, idxs = [], []
    for _ in range(TOP_K):
        mx = jnp.max(cur, axis=-1, keepdims=True)
        am = jnp.min(jnp.where(cur == mx, lane, float(LANES)), axis=-1, keepdims=True)
        vals.append(mx)
        idxs.append(am)
        cur = jnp.where(lane == am, -jnp.inf, cur)
    es = [jnp.exp(v - vals[0]) for v in vals]
    den = es[0] + es[1] + es[2] + es[3]

    hit = [(lane == am) for am in idxs]
    assign = (hit[0] | hit[1] | hit[2] | hit[3]).astype(BF16)
    ti = lax.broadcasted_iota(jnp.int32, (TM, TM), 0)
    tj = lax.broadcasted_iota(jnp.int32, (TM, TM), 1)
    rank = jnp.dot((tj < ti).astype(BF16), assign, preferred_element_type=F32)
    cnt = jnp.sum(assign.astype(F32), axis=0, keepdims=True)
    cnt_al = jnp.floor((cnt + (SEG_ALIGN - 1)) * (1.0 / SEG_ALIGN)) * SEG_ALIGN
    ei = lax.broadcasted_iota(jnp.int32, (LANES, LANES), 0)
    ej = lax.broadcasted_iota(jnp.int32, (LANES, LANES), 1)
    seg = jnp.dot(jnp.broadcast_to(cnt_al, (SUBLANES, LANES)).astype(BF16), (ei < ej).astype(BF16),
                  preferred_element_type=F32)[0:1, :]
    base = seg + rank
    rlane = lax.broadcasted_iota(jnp.int32, (TM, ROWS_T), 1).astype(F32)
    klane = lax.broadcasted_iota(jnp.int32, (TM, LANES), 1)
    onehot = jnp.zeros((TM, ROWS_T), F32)
    pos_out = jnp.zeros((TM, LANES), F32)
    wt_out = jnp.zeros((TM, LANES), F32)
    for kk in range(TOP_K):
        pk = jnp.sum(jnp.where(hit[kk], base, 0.0), axis=-1, keepdims=True)
        onehot = onehot + (rlane == pk).astype(F32)
        pos_out = jnp.where(klane == kk, pk, pos_out)
        wt_out = jnp.where(klane == kk, es[kk] / den, wt_out)
    xs_ref[0] = lax.dot_general(onehot.astype(BF16), h.astype(BF16), (((0,), (0,)), ((), ())),
                                preferred_element_type=F32)
    pos_ref[...] = pos_out.astype(jnp.int32)
    wt_ref[...] = wt_out
    mrow = lax.broadcasted_iota(jnp.int32, (SUBLANES, LANES), 0)
    meta = jnp.where(mrow == 0, jnp.broadcast_to(cnt_al, (SUBLANES, LANES)),
                     jnp.where(mrow == 1, jnp.broadcast_to(seg, (SUBLANES, LANES)), 0.0))
    meta_ref[0] = meta.astype(jnp.int32)


def _outproj_call(yrg, ys5, x, mod_l, n2, wo, gw, gb, rw, rb, n_ctx_tiles, tiles_per_lat):
    n = x.shape[0]
    nt = n // TM
    gid = functools.partial(_group_id, n_ctx_tiles=n_ctx_tiles, tiles_per_lat=tiles_per_lat)
    const = lambda *shape: pl.BlockSpec(shape, lambda i: (0,) * len(shape))
    return pl.pallas_call(
        _outproj_kernel,
        grid=(nt,),
        in_specs=[
            pl.BlockSpec((TM, 2 * HG_W), lambda i: (i, 0)),
            pl.BlockSpec((TM, S5_W), lambda i: (i, 0)),
            pl.BlockSpec((TM, D_MODEL), lambda i: (i, 0)),
            pl.BlockSpec((None, 6, D_MODEL), lambda i: (gid(i), 0, 0)),
            const(1, D_MODEL), const(D_MODEL, D_MODEL), const(S5_W, S5_W), const(1, S5_W),
            const(D_MODEL, LANES), const(1, LANES),
        ],
        out_specs=[
            pl.BlockSpec((TM, D_MODEL), lambda i: (i, 0)),
            pl.BlockSpec((1, ROWS_T, D_MODEL), lambda i: (i, 0, 0)),
            pl.BlockSpec((TM, LANES), lambda i: (i, 0)),
            pl.BlockSpec((TM, LANES), lambda i: (i, 0)),
            pl.BlockSpec((1, SUBLANES, LANES), lambda i: (i, 0, 0)),
        ],
        out_shape=[
            jax.ShapeDtypeStruct((n, D_MODEL), F32),
            jax.ShapeDtypeStruct((nt, ROWS_T, D_MODEL), F32),
            jax.ShapeDtypeStruct((n, LANES), jnp.int32),
            jax.ShapeDtypeStruct((n, LANES), F32),
            jax.ShapeDtypeStruct((nt, SUBLANES, LANES), jnp.int32),
        ],
        compiler_params=pltpu.CompilerParams(vmem_limit_bytes=VMEM_LIMIT),
        name="outproj_router_dispatch",
    )(yrg, ys5, x, mod_l, n2, wo, gw, gb, rw, rb)


def _expert_schedule(meta, nblk):
    nt = meta.shape[0]
    nch = (meta[:, 0, :N_EXPERTS] // SEG_ALIGN).T
    seg_chunk = (meta[:, 1, :N_EXPERTS] // SEG_ALIGN).T
    cs_end = jnp.cumsum(nch, axis=1)
    cs_start = cs_end - nch
    tot = cs_end[:, -1]
    nb = (tot + BLK_CHUNKS - 1) // BLK_CHUNKS
    blk_end = jnp.cumsum(nb)
    blk_start = blk_end - nb
    n_active = blk_end[-1]
    j = jnp.arange(nblk, dtype=jnp.int32)
    be = jnp.sum(blk_end[None, :] <= jnp.minimum(j, n_active - 1)[:, None], axis=1)
    be = jnp.clip(be, 0, N_EXPERTS - 1).astype(jnp.int32)
    own = be[:, None] == jnp.arange(N_EXPERTS, dtype=jnp.int32)[None, :]
    pick = lambda a: jnp.sum(jnp.where(own[:, :, None], a[None], 0), axis=1)
    bs = (j - jnp.sum(jnp.where(own, blk_start[None, :], 0), axis=1)) * BLK_CHUNKS
    tot_b = jnp.sum(jnp.where(own, tot[None, :], 0), axis=1)
    nvalid = jnp.where(j < n_active, jnp.clip(tot_b - bs, 0, BLK_CHUNKS), 0)
    kfirst = jnp.sum(pick(cs_end) <= bs[:, None], axis=1)
    klast = jnp.sum(pick(cs_start) < (bs + nvalid)[:, None], axis=1)
    src_base = jnp.arange(nt, dtype=jnp.int32)[None, :] * CHUNKS_T + seg_chunk
    i32 = lambda a: a.astype(jnp.int32)
    return (be, i32(nvalid), i32(bs), i32(kfirst), i32(klast),
            i32(cs_start.reshape(-1)), i32(nch.reshape(-1)), i32(src_base.reshape(-1)))


W_CAST_ROWS = 128


def _ffn_kernel(be_ref, nv_ref, bs_ref, kf_ref, kl_ref, cst_ref, nch_ref, srcb_ref,
                xs_hbm, w1_ref, b1_ref, w2_ref, b2_ref, ys_hbm,
                xbuf, ybuf, w1b, w2b, gsem, ssem, *, nblk, nt):
    j = pl.program_id(0)
    slot = j % 2

    def for_pieces(blk, fn):
        e_off = be_ref[blk] * nt
        lo_b = bs_ref[blk]
        hi_b = lo_b + nv_ref[blk]

        def body(i, carry):
            ps = cst_ref[e_off + i]
            lo = jnp.maximum(ps, lo_b)
            n = jnp.minimum(ps + nch_ref[e_off + i], hi_b) - lo

            @pl.when(n > 0)
            def _():
                fn(srcb_ref[e_off + i] + (lo - ps), lo - lo_b, n)
            return carry

        lax.fori_loop(kf_ref[blk], kl_ref[blk], body, 0)

    def start_gather(blk, s):
        for_pieces(blk, lambda src, dst, n: pltpu.make_async_copy(
            xs_hbm.at[pl.ds(src, n)], xbuf.at[s, pl.ds(dst, n)], gsem.at[s]).start())

    def start_scatter(blk, s):
        for_pieces(blk, lambda src, dst, n: pltpu.make_async_copy(
            ybuf.at[s, pl.ds(dst, n)], ys_hbm.at[pl.ds(src, n)], ssem.at[s]).start())

    def wait_gather(blk, s):
        @pl.when(nv_ref[blk] > 0)
        def _():
            pltpu.make_async_copy(xs_hbm.at[pl.ds(0, nv_ref[blk])], xbuf.at[s, pl.ds(0, nv_ref[blk])],
                                  gsem.at[s]).wait()

    def wait_scatter(blk, s):
        @pl.when(nv_ref[blk] > 0)
        def _():
            pltpu.make_async_copy(ybuf.at[s, pl.ds(0, nv_ref[blk])], ys_hbm.at[pl.ds(0, nv_ref[blk])],
                                  ssem.at[s]).wait()

    @pl.when(j == 0)
    def _():
        xbuf[...] = jnp.zeros_like(xbuf)
        start_gather(0, 0)

    @pl.when(j + 1 < nblk)
    def _():
        start_gather(j + 1, 1 - slot)

    @pl.when((j == 0) | (be_ref[j] != be_ref[jnp.maximum(j - 1, 0)]))
    def _():
        def cast_rows(r, carry):
            rows = pl.ds(pl.multiple_of(r * W_CAST_ROWS, W_CAST_ROWS), W_CAST_ROWS)
            w1b[rows, :] = w1_ref[rows, :].astype(BF16)
            w2b[rows, :] = w2_ref[rows, :].astype(BF16)
            return carry
        lax.fori_loop(0, D_MODEL // W_CAST_ROWS, cast_rows, 0)

    wait_gather(j, slot)

    @pl.when(j >= 2)
    def _():
        wait_scatter(j - 2, slot)

    @pl.when(nv_ref[j] > 0)
    def _():
        x = xbuf[slot].reshape(BLK_ROWS, D_MODEL).astype(BF16)
        gu = jnp.dot(x, w1b[...], preferred_element_type=F32) + b1_ref[...]
        gate = jnp.minimum(gu[:, :D_FF], SWIGLU_LIMIT)
        up = jnp.clip(gu[:, D_FF:], -SWIGLU_LIMIT, SWIGLU_LIMIT)
        act = (up + 1.0) * gate * _sigmoid(SWIGLU_ALPHA * gate)
        y = jnp.dot(act.astype(BF16), w2b[...], preferred_element_type=F32) + b2_ref[...]
        ybuf[slot] = y.reshape(BLK_CHUNKS, SEG_ALIGN, D_MODEL)

    start_scatter(j, slot)

    @pl.when(j == nblk - 1)
    def _():
        wait_scatter(j, slot)
        if nblk >= 2:
            wait_scatter(j - 1, 1 - slot)


def _ffn_call(sched, xs_chunks, w1, b1, w2, b2, layer, nt):
    assert D_FF == D_MODEL
    nblk = sched[0].shape[0]
    nch = xs_chunks.shape[0]
    wmap = lambda j, be, *_: (layer, be[j], 0, 0)
    grid_spec = pltpu.PrefetchScalarGridSpec(
        num_scalar_prefetch=len(sched),
        grid=(nblk,),
        in_specs=[
            pl.BlockSpec(memory_space=pl.ANY),
            pl.BlockSpec((None, None, D_MODEL, 2 * D_FF), wmap),
            pl.BlockSpec((None, None, 1, 2 * D_FF), wmap),
            pl.BlockSpec((None, None, D_FF, D_MODEL), wmap),
            pl.BlockSpec((None, None, 1, D_MODEL), wmap),
        ],
        out_specs=pl.BlockSpec(memory_space=pl.ANY),
        scratch_shapes=[
            pltpu.VMEM((2, BLK_CHUNKS, SEG_ALIGN, D_MODEL), F32),
            pltpu.VMEM((2, BLK_CHUNKS, SEG_ALIGN, D_MODEL), F32),
            pltpu.VMEM((D_MODEL, 2 * D_FF), BF16),
            pltpu.VMEM((D_FF, D_MODEL), BF16),
            pltpu.SemaphoreType.DMA((2,)),
            pltpu.SemaphoreType.DMA((2,)),
        ],
    )
    return pl.pallas_call(
        functools.partial(_ffn_kernel, nblk=nblk, nt=nt),
        grid_spec=grid_spec,
        out_shape=jax.ShapeDtypeStruct((nch, SEG_ALIGN, D_MODEL), F32),
        input_output_aliases={len(sched): 0},
        compiler_params=pltpu.CompilerParams(vmem_limit_bytes=VMEM_LIMIT, dimension_semantics=("arbitrary",)),
        name="expert_ffn",
    )(*sched, xs_chunks, w1, b1, w2, b2)


def _combine_kernel(ys_ref, pos_ref, wt_ref, x1_ref, mod_ref, fg_ref, *out_refs, final):
    ys = ys_ref[0].astype(BF16)
    rlane = lax.broadcasted_iota(jnp.int32, (TM, ROWS_T), 1)
    w = jnp.zeros((TM, ROWS_T), F32)
    for kk in range(TOP_K):
        w = w + jnp.where(rlane == pos_ref[:, kk:kk + 1], wt_ref[:, kk:kk + 1], 0.0)
    moe = jnp.dot(w.astype(BF16), ys, preferred_element_type=F32)
    x2 = x1_ref[...] + mod_ref[5:6, :] * moe
    out_refs[0][...] = x2
    if final:
        out_refs[1][...] = x2 * lax.rsqrt(jnp.mean(x2 * x2, axis=-1, keepdims=True) + EPS) * fg_ref[...]


def _combine_call(ys, pos, wts, x1, mod_l, fg, n_ctx_tiles, tiles_per_lat, final):
    n = x1.shape[0]
    nt = n // TM
    gid = functools.partial(_group_id, n_ctx_tiles=n_ctx_tiles, tiles_per_lat=tiles_per_lat)
    tok = pl.BlockSpec((TM, D_MODEL), lambda i: (i, 0))
    res = pl.pallas_call(
        functools.partial(_combine_kernel, final=final),
        grid=(nt,),
        in_specs=[
            pl.BlockSpec((1, ROWS_T, D_MODEL), lambda i: (i, 0, 0)),
            pl.BlockSpec((TM, LANES), lambda i: (i, 0)),
            pl.BlockSpec((TM, LANES), lambda i: (i, 0)),
            tok,
            pl.BlockSpec((None, 6, D_MODEL), lambda i: (gid(i), 0, 0)),
            pl.BlockSpec((1, D_MODEL), lambda i: (0, 0)),
        ],
        out_specs=[tok, tok] if final else [tok],
        out_shape=[jax.ShapeDtypeStruct((n, D_MODEL), F32)] * (2 if final else 1),
        compiler_params=pltpu.CompilerParams(vmem_limit_bytes=VMEM_LIMIT),
        name="moe_combine",
    )(ys, pos, wts, x1, mod_l, fg)
    return res


def _rope_tables(t):
    pos = np.arange(t)
    nf = HEAD_DIM // 4
    inv = jnp.asarray(ROPE_BASE, F32) ** (-jnp.arange(nf, dtype=F32) / nf)
    ang = jnp.concatenate([jnp.asarray(pos // GRID_W, F32)[:, None] * inv,
                           jnp.asarray(pos % GRID_W, F32)[:, None] * inv], axis=-1)
    cos, sin = jnp.cos(ang), jnp.sin(ang)
    return (jnp.tile(jnp.concatenate([cos, cos], -1), (1, N_HEADS)),
            jnp.tile(jnp.concatenate([-sin, sin], -1), (1, N_HEADS)))


def _block_diag_t(s):
    eye = jnp.eye(N_HEADS, dtype=s.dtype)
    out = jnp.einsum('...hkv,hg->...hvgk', s, eye)
    return out.reshape(s.shape[:-3] + (HG_W, HG_W))


def _diag_blocks(st):
    blocks = [st[..., h * HEAD_DIM:(h + 1) * HEAD_DIM, h * HEAD_DIM:(h + 1) * HEAD_DIM] for h in range(N_HEADS)]
    return jnp.swapaxes(jnp.stack(blocks, axis=-3), -1, -2)


def kernel(x_prompt, x_sample, state_ret, state_gla, state_s5_re, state_s5_im, c, c_ctx, norm1_g, norm2_g, final_g, ada_w, ada_b, w_in, w_out, ret_decay, gla_w_lr, gla_b_lr, gla_norm_g, s5_a_re, s5_a_im, s5_log_dt, s5_b_re, s5_b_im, s5_c_re, s5_c_im, s5_d, s5_glu_w, s5_glu_b, router_w, router_b, moe_w1, moe_b1, moe_w2, moe_b2):
    bc, tc, d = x_prompt.shape
    bl, tl, _ = x_sample.shape
    depth = w_in.shape[0]
    nc, nl = bc * tc, bl * tl
    n = nc + nl
    assert d == D_MODEL and nc % TM == 0 and tl % TM == 0 and tc % CHUNK == 0 and tl % CHUNK == 0
    assert 1 + bl <= SUBLANES
    n_ctx_tiles, tiles_per_lat = nc // TM, tl // TM
    nt = n // TM
    nblk = nt * CHUNKS_T // BLK_CHUNKS + N_EXPERTS

    x = jnp.concatenate([x_prompt.reshape(nc, d), x_sample.reshape(nl, d)], axis=0)
    cond8 = jnp.concatenate([c_ctx[None, :], c, jnp.zeros((SUBLANES - 1 - bl, d), F32)], axis=0)
    mod = _ada_call(cond8, ada_w, ada_b).reshape(depth, SUBLANES, 6, d)
    rope = _rope_tables(tl)
    fg = final_g.reshape(1, d)

    ret_states, gla_states, re_states, im_states = [], [], [], []
    y_final = None
    for l in range(depth):
        wl = w_in[l]
        w_cat = jnp.concatenate(
            [wl[:, :8 * HG_W], wl[:, 8 * HG_W + GLA_RANK:],
             jnp.pad(wl[:, 8 * HG_W:8 * HG_W + GLA_RANK], ((0, 0), (0, LANES - GLA_RANK)))], axis=1).astype(BF16)
        wlr = jnp.pad(gla_w_lr[l], ((0, 0), (0, LANES - GLA_RANK), (0, 0)))
        blr = gla_b_lr[l].reshape(2, 1, HG_W)
        pa, su = _inproj_call(x, mod[l], norm1_g[l].reshape(1, d), w_cat, wlr, blr, n_ctx_tiles, tiles_per_lat)

        log_gamma = jnp.log1p(-jnp.exp(ret_decay[l]))
        lgam = jnp.repeat(log_gamma, HEAD_DIM, axis=1)
        gn = jnp.tile(gla_norm_g[l], N_HEADS).reshape(1, HG_W)
        y_c, st_c = _scan_call(pa, 0, bc, tc, log_gamma, lgam, gn, None, None, True)
        st0 = jnp.stack([_block_diag_t(state_ret[:, l]), _block_diag_t(state_gla[:, l])], axis=2)
        y_l, _ = _scan_call(pa, nc // CHUNK, bl, tl, log_gamma, lgam, gn, rope, st0, False)
        ret_states.append(_diag_blocks(st_c[:, :, 0]))
        gla_states.append(_diag_blocks(st_c[:, :, 1]))

        ops = _s5_operators(s5_a_re[l], s5_a_im[l], s5_log_dt[l], s5_b_re[l], s5_b_im[l],
                            s5_c_re[l], s5_c_im[l], s5_d[l])
        ys_c, fin_c = _s5_call(_s5_rows(su[:nc], bc, tc), ops, bc, tc, None)
        h0 = jnp.concatenate([state_s5_re[:, l], state_s5_im[:, l]], axis=-1).transpose(2, 1, 0, 3)
        ys_l, _ = _s5_call(_s5_rows(su[nc:], bl, tl), ops, bl, tl, h0)
        re_states.append(fin_c[..., :S5_STATE].transpose(2, 1, 0, 3))
        im_states.append(fin_c[..., S5_STATE:].transpose(2, 1, 0, 3))

        yrg = jnp.concatenate([y_c, y_l], axis=0)
        ys5 = jnp.concatenate([_s5_unrows(ys_c, bc, tc), _s5_unrows(ys_l, bl, tl)], axis=0)
        rw = jnp.pad(router_w[l], ((0, 0), (0, LANES - N_EXPERTS)))
        rb = jnp.concatenate([router_b[l], jnp.full((LANES - N_EXPERTS,), NEG_BIG, F32)]).reshape(1, LANES)
        x1, xs, pos, wts, meta = _outproj_call(
            yrg, ys5, x, mod[l], norm2_g[l].reshape(1, d), w_out[l].astype(BF16), s5_glu_w[l].astype(BF16),
            s5_glu_b[l].reshape(1, S5_W), rw, rb, n_ctx_tiles, tiles_per_lat)

        sched = _expert_schedule(meta, nblk)
        ys = _ffn_call(sched, xs.reshape(nt * CHUNKS_T, SEG_ALIGN, d),
                       moe_w1, moe_b1.reshape(depth, N_EXPERTS, 1, 2 * D_FF),
                       moe_w2, moe_b2.reshape(depth, N_EXPERTS, 1, d), l, nt)
        res = _combine_call(ys.reshape(nt, ROWS_T, d), pos, wts, x1, mod[l], fg,
                            n_ctx_tiles, tiles_per_lat, l == depth - 1)
        x = res[0]
        if l == depth - 1:
            y_final = res[1]

    y_prompt = y_final[:nc].reshape(bc, tc, d)
    y_sample = y_final[nc:].reshape(bl, tl, d)
    return (y_prompt, y_sample, jnp.stack(ret_states, axis=1), jnp.stack(gla_states, axis=1),
            jnp.stack(re_states, axis=1), jnp.stack(im_states, axis=1))
```

```python
import functools

import numpy as np
import jax
import jax.numpy as jnp
from jax import lax
from jax.experimental import pallas as pl
from jax.experimental.pallas import tpu as pltpu

F32 = jnp.float32
BF16 = jnp.bfloat16
HIGHEST = lax.Precision.HIGHEST

D_MODEL = 1024
GRID_W = 64
CHUNK = 128
HEAD_DIM = 64
N_HEADS = 4
HG_W = N_HEADS * HEAD_DIM
S5_W = 512
S5_CH = 16
S5_GROUPS = 32
S5_STATE = 64
S5_L = 16
S5_ROW = S5_L * S5_CH
GLA_RANK = 16
GLA_TAU = 16.0
N_EXPERTS = 32
TOP_K = 4
D_FF = 1024
SWIGLU_LIMIT = 7.0
SWIGLU_ALPHA = 1.702
ROPE_BASE = 10000.0
EPS = 1e-6

LANES = 128
SUBLANES = 8
TM = 256
SEG_ALIGN = SUBLANES
ROWS_T = 1280
CHUNKS_T = ROWS_T // SEG_ALIGN
BLK_CHUNKS = 64
BLK_ROWS = BLK_CHUNKS * SEG_ALIGN
NEG_BIG = -1e30
VMEM_LIMIT = 56 * 1024 * 1024

PA_W = 8 * HG_W + 2 * HG_W
W_CAT = 8 * HG_W + S5_W + LANES


def _sigmoid(x):
    return 1.0 / (1.0 + jnp.exp(-x))


def _group_id(i, n_ctx_tiles, tiles_per_lat):
    return jnp.where(i < n_ctx_tiles, 0, 1 + (i - n_ctx_tiles) // tiles_per_lat)


def _ada_kernel(c_ref, w_ref, b_ref, o_ref):
    c = c_ref[...]
    s = c * _sigmoid(c)
    o_ref[0] = jnp.dot(s, w_ref[0], precision=HIGHEST, preferred_element_type=F32) + b_ref[0]


def _ada_call(cond8, ada_w, ada_b):
    depth, d, n6 = ada_w.shape
    tn = 1024
    return pl.pallas_call(
        _ada_kernel,
        grid=(depth, n6 // tn),
        in_specs=[
            pl.BlockSpec((SUBLANES, d), lambda l, j: (0, 0)),
            pl.BlockSpec((1, d, tn), lambda l, j: (l, 0, j)),
            pl.BlockSpec((1, 1, tn), lambda l, j: (l, 0, j)),
        ],
        out_specs=pl.BlockSpec((1, SUBLANES, tn), lambda l, j: (l, 0, j)),
        out_shape=jax.ShapeDtypeStruct((depth, SUBLANES, n6), F32),
        compiler_params=pltpu.CompilerParams(vmem_limit_bytes=VMEM_LIMIT),
        name="ada_mod",
    )(cond8, ada_w, ada_b.reshape(depth, 1, n6))


def _inproj_kernel(x_ref, mod_ref, n1_ref, w_ref, wlr_ref, blr_ref, pa_ref, su_ref):
    x = x_ref[...]
    h = x * lax.rsqrt(jnp.mean(x * x, axis=-1, keepdims=True) + EPS) * n1_ref[...]
    h = h * (1.0 + mod_ref[1:2, :]) + mod_ref[0:1, :]
    r = jnp.dot(h.astype(BF16), w_ref[...], preferred_element_type=F32)
    pa_ref[:, : 8 * HG_W] = r[:, : 8 * HG_W]
    su_ref[...] = r[:, 8 * HG_W: 8 * HG_W + S5_W]
    glr = r[:, 8 * HG_W + S5_W:]
    for d in range(2):
        z = jnp.dot(glr, wlr_ref[d], precision=HIGHEST, preferred_element_type=F32) + blr_ref[d]
        log_sig = jnp.minimum(z, 0.0) - jnp.log(1.0 + jnp.exp(-jnp.abs(z)))
        pa_ref[:, (8 + d) * HG_W: (9 + d) * HG_W] = log_sig * (1.0 / GLA_TAU)


def _inproj_call(x, mod_l, n1, w_cat, wlr, blr, n_ctx_tiles, tiles_per_lat):
    n = x.shape[0]
    gid = functools.partial(_group_id, n_ctx_tiles=n_ctx_tiles, tiles_per_lat=tiles_per_lat)
    return pl.pallas_call(
        _inproj_kernel,
        grid=(n // TM,),
        in_specs=[
            pl.BlockSpec((TM, D_MODEL), lambda i: (i, 0)),
            pl.BlockSpec((None, 6, D_MODEL), lambda i: (gid(i), 0, 0)),
            pl.BlockSpec((1, D_MODEL), lambda i: (0, 0)),
            pl.BlockSpec((D_MODEL, W_CAT), lambda i: (0, 0)),
            pl.BlockSpec((2, LANES, HG_W), lambda i: (0, 0, 0)),
            pl.BlockSpec((2, 1, HG_W), lambda i: (0, 0, 0)),
        ],
        out_specs=[
            pl.BlockSpec((TM, PA_W), lambda i: (i, 0)),
            pl.BlockSpec((TM, S5_W), lambda i: (i, 0)),
        ],
        out_shape=[
            jax.ShapeDtypeStruct((n, PA_W), F32),
            jax.ShapeDtypeStruct((n, S5_W), F32),
        ],
        compiler_params=pltpu.CompilerParams(vmem_limit_bytes=VMEM_LIMIT),
        name="norm1_inproj",
    )(x, mod_l, n1, w_cat, wlr, blr)


def _scan_kernel(*refs, n, use_rope, has_init, want_final):
    refs = list(refs)
    lg_ref = refs.pop(0)
    pa_ref = refs.pop(0)
    if use_rope:
        cos_ref = refs.pop(0)
        sin_ref = refs.pop(0)
    lgam_ref = refs.pop(0)
    gn_ref = refs.pop(0)
    if has_init:
        st0_ref = refs.pop(0)
    y_ref = refs.pop(0)
    if want_final:
        stf_ref = refs.pop(0)
    st_scr = refs.pop(0)
    of_scr = refs.pop(0)

    d = pl.program_id(1)
    c = pl.program_id(2)
    cc = jnp.where(d == 0, c, n - 1 - c)

    @pl.when(c == 0)
    def _():
        if has_init:
            st_scr[...] = st0_ref[...]
        else:
            st_scr[...] = jnp.zeros_like(st_scr)

    ii = lax.broadcasted_iota(jnp.int32, (CHUNK, CHUNK), 0)
    jj = lax.broadcasted_iota(jnp.int32, (CHUNK, CHUNK), 1)
    rel = (1 - 2 * d) * (ii - jj)
    mask = rel >= 0
    i4 = lax.broadcasted_iota(jnp.int32, (N_HEADS * CHUNK, CHUNK), 0) % CHUNK
    j4 = lax.broadcasted_iota(jnp.int32, (N_HEADS * CHUNK, CHUNK), 1)
    mask4 = (1 - 2 * d) * (i4 - j4) >= 0
    relf = rel.astype(F32)
    row = lax.broadcasted_iota(jnp.int32, (CHUNK, 1), 0)
    p1 = jnp.where(d == 0, row + 1, CHUNK - row).astype(F32)
    lane = lax.broadcasted_iota(jnp.int32, (1, HG_W), 1)
    head_mask = [(lane // HEAD_DIM == h).astype(F32) for h in range(N_HEADS)]
    bi = lax.broadcasted_iota(jnp.int32, (HG_W, HG_W), 0) // HEAD_DIM
    bj = lax.broadcasted_iota(jnp.int32, (HG_W, HG_W), 1) // HEAD_DIM
    block_diag = bi == bj

    nt_dims = (((1,), (1,)), ((), ()))
    tn_dims = (((0,), (0,)), ((), ()))

    def split3(x):
        hi = x.astype(BF16)
        r1 = x - hi.astype(F32)
        mid = r1.astype(BF16)
        return hi, mid, (r1 - mid.astype(F32)).astype(BF16)

    def group(q, k, v, q_dec, k_upd, chunk_decay, score_weight, g_idx):
        kb = k.astype(BF16)
        vb = v.astype(BF16)
        qs = jnp.concatenate([(q * head_mask[h]).astype(BF16) for h in range(N_HEADS)], axis=0)
        s = lax.dot_general(qs, kb, nt_dims, preferred_element_type=F32)
        s = jnp.where(mask4, s, 0.0) if score_weight is None else s * score_weight
        o4 = jnp.dot(s.astype(BF16), vb, preferred_element_type=F32)
        o = o4[0:CHUNK] * head_mask[0]
        for h in range(1, N_HEADS):
            o = o + o4[h * CHUNK:(h + 1) * CHUNK] * head_mask[h]
        st = st_scr[g_idx]
        o = o + lax.dot_general(q_dec.astype(BF16), st.astype(BF16), nt_dims, preferred_element_type=F32)
        upd = lax.dot_general(vb, k_upd.astype(BF16), tn_dims, preferred_element_type=F32)
        st_scr[g_idx] = st * chunk_decay + jnp.where(block_diag, upd, 0.0)
        return o

    def rope(x):
        partner = jnp.where((lane % HEAD_DIM) < HEAD_DIM // 2,
                            pltpu.roll(x, HG_W - HEAD_DIM // 2, 1), pltpu.roll(x, HEAD_DIM // 2, 1))
        return x * cos_ref[...] + partner * sin_ref[...]

    q = pa_ref[:, 0:HG_W]
    k = pa_ref[:, HG_W:2 * HG_W] * (HEAD_DIM ** -0.5)
    v = pa_ref[:, 2 * HG_W:3 * HG_W]
    if use_rope:
        q = rope(q)
        k = rope(k)
    lgl = lgam_ref[pl.ds(d, 1), :]
    q_dec = q * jnp.exp(p1 * lgl)
    k_upd = k * jnp.exp((CHUNK - p1) * lgl)
    chunk_decay = jnp.exp(CHUNK * lgl)

    ret_weight = jnp.concatenate([jnp.where(mask, jnp.exp(relf * lg_ref[d, h]), 0.0) for h in range(N_HEADS)], axis=0)
    o_ret = group(q, k, v, q_dec, k_upd, chunk_decay, ret_weight, 0)

    q = pa_ref[:, 4 * HG_W:5 * HG_W] * (HEAD_DIM ** -0.5)
    k = pa_ref[:, 5 * HG_W:6 * HG_W]
    v = pa_ref[:, 6 * HG_W:7 * HG_W]
    la = jnp.where(d == 0, pa_ref[:, 8 * HG_W:9 * HG_W], pa_ref[:, 9 * HG_W:10 * HG_W])
    tri = mask.astype(BF16)
    b = sum(jnp.dot(tri, part, preferred_element_type=F32) for part in split3(la))
    b_last = jnp.sum(la, axis=0, keepdims=True)
    q_dec = q * jnp.exp(b)
    k_inv = k * jnp.exp(-b)
    k_upd = k * jnp.exp(b_last - b)
    chunk_decay = jnp.exp(b_last)

    o_gla = group(q_dec, k_inv, v, q_dec, k_upd, chunk_decay, None, 1)

    @pl.when(d == 0)
    def _():
        of_scr[c, :, 0:HG_W] = o_ret
        of_scr[c, :, HG_W:2 * HG_W] = o_gla

    @pl.when(d == 1)
    def _():
        mean_mat = jnp.where(block_diag, 1.0 / HEAD_DIM, 0.0).astype(BF16)

        def head_rms(o):
            parts = jnp.concatenate(split3(o * o), axis=0)
            ms3 = jnp.dot(parts, mean_mat, preferred_element_type=F32)
            ms = ms3[0:CHUNK] + ms3[CHUNK:2 * CHUNK] + ms3[2 * CHUNK:3 * CHUNK]
            return o * lax.rsqrt(ms + EPS)

        o1 = o_ret + of_scr[cc, :, 0:HG_W]
        g1 = pa_ref[:, 3 * HG_W:4 * HG_W]
        y_ref[:, 0:HG_W] = head_rms(o1) * (g1 * _sigmoid(g1))
        o2 = o_gla + of_scr[cc, :, HG_W:2 * HG_W]
        g2 = pa_ref[:, 7 * HG_W:8 * HG_W]
        y_ref[:, HG_W:2 * HG_W] = head_rms(o2) * gn_ref[...] * (g2 * _sigmoid(g2))

    if want_final:
        @pl.when(c == n - 1)
        def _():
            stf_ref[...] = st_scr[...]


def _scan_call(pa, row_block_off, bsz, t, lg, lgam, gn, rope, st0, want_final):
    n = t // CHUNK
    use_rope = rope is not None
    has_init = st0 is not None

    def chunk_idx(d, c):
        return jnp.where(d == 0, c, n - 1 - c)

    in_specs = [
        pl.BlockSpec(memory_space=pltpu.SMEM),
        pl.BlockSpec((CHUNK, PA_W), lambda b, d, c: (row_block_off + b * n + chunk_idx(d, c), 0)),
    ]
    args = [lg, pa]
    if use_rope:
        in_specs += [pl.BlockSpec((CHUNK, HG_W), lambda b, d, c: (chunk_idx(d, c), 0))] * 2
        args += list(rope)
    in_specs += [pl.BlockSpec((2, HG_W), lambda b, d, c: (0, 0)),
                 pl.BlockSpec((1, HG_W), lambda b, d, c: (0, 0))]
    args += [lgam, gn]
    if has_init:
        in_specs.append(pl.BlockSpec((None, None, 2, HG_W, HG_W), lambda b, d, c: (b, d, 0, 0, 0)))
        args.append(st0)
    out_specs = [pl.BlockSpec((CHUNK, 2 * HG_W),
                              lambda b, d, c: (b * n + jnp.where(d == 0, n - 1, n - 1 - c), 0))]
    out_shape = [jax.ShapeDtypeStruct((bsz * t, 2 * HG_W), F32)]
    if want_final:
        out_specs.append(pl.BlockSpec((None, None, 2, HG_W, HG_W), lambda b, d, c: (b, d, 0, 0, 0)))
        out_shape.append(jax.ShapeDtypeStruct((bsz, 2, 2, HG_W, HG_W), F32))
    res = pl.pallas_call(
        functools.partial(_scan_kernel, n=n, use_rope=use_rope, has_init=has_init, want_final=want_final),
        grid=(bsz, 2, n),
        in_specs=in_specs,
        out_specs=out_specs,
        out_shape=out_shape,
        scratch_shapes=[pltpu.VMEM((2, HG_W, HG_W), F32), pltpu.VMEM((n, CHUNK, 2 * HG_W), F32)],
        compiler_params=pltpu.CompilerParams(vmem_limit_bytes=VMEM_LIMIT),
        name="ret_gla_scan",
    )(*args)
    return res if want_final else (res[0], None)


S5_GB = LANES // S5_CH
S5_SEQS = 6


def _s5_table_kernel(pwr_ref, pwi_ref, fz_ref, btr_ref, bti_ref, cr_ref, ci_ref, m_ref, sin_ref, sout_ref):
    ri = lax.broadcasted_iota(jnp.int32, (S5_ROW, S5_L), 0)
    rc = lax.broadcasted_iota(jnp.int32, (S5_ROW, S5_L), 1)
    rep_tok = (ri // S5_CH == rc).astype(F32)
    rep_ch = (ri % S5_CH == rc).astype(F32)

    def expand(sel, t):
        return jnp.dot(sel, t, precision=HIGHEST, preferred_element_type=F32)

    def powers(d, k):
        rows = slice(k * S5_L, (k + 1) * S5_L)
        return expand(rep_tok, pwr_ref[d, rows, :]), expand(rep_tok, pwi_ref[d, rows, :])

    cr, ci = expand(rep_ch, cr_ref[...]), expand(rep_ch, ci_ref[...])
    tok_in = lax.broadcasted_iota(jnp.int32, (S5_ROW, S5_ROW), 0) // S5_CH
    tok_out = lax.broadcasted_iota(jnp.int32, (S5_ROW, S5_ROW), 1) // S5_CH
    nt_dims = (((1,), (1,)), ((), ()))
    m = jnp.zeros((S5_ROW, S5_ROW), F32)
    for d in range(2):
        fr, fi = fz_ref[d, 0:1, :], fz_ref[d, 1:2, :]
        bbr = expand(rep_ch, fr * btr_ref[...] - fi * bti_ref[...])
        bbi = expand(rep_ch, fr * bti_ref[...] + fi * btr_ref[...])
        k_out, k_in, k_sin, k_sout = (0, 1, 2, 4) if d == 0 else (1, 0, 3, 5)
        er, ei = powers(d, k_out)
        cq = jnp.concatenate([cr * er - ci * ei, -(cr * ei + ci * er)], axis=-1)
        er, ei = powers(d, k_in)
        bk = jnp.concatenate([bbr * er - bbi * ei, bbr * ei + bbi * er], axis=-1)
        full = lax.dot_general(bk, cq, nt_dims, precision=HIGHEST, preferred_element_type=F32)
        m = m + jnp.where((tok_out >= tok_in) if d == 0 else (tok_in >= tok_out), full, 0.0)
        er, ei = powers(d, k_sin)
        sin_ref[d] = jnp.concatenate([bbr * er - bbi * ei, bbr * ei + bbi * er], axis=-1).astype(BF16)
        er, ei = powers(d, k_sout)
        sout_ref[d] = jnp.concatenate([cr * er - ci * ei, -(cr * ei + ci * er)], axis=-1).astype(BF16)
    m_ref[...] = m.astype(BF16)


def _s5_power_table(a_re, a_im, log_dt):
    j = np.arange(S5_L, dtype=np.float32)
    half = (S5_L - 1) / 2.0
    expo = jnp.asarray(np.concatenate([j - half, half - j, S5_L - 1 - j, j, j + 1, S5_L - j]))[:, None]
    dt = jnp.exp(log_dt)[..., None, None]
    ar, ai = a_re[..., None, :] * dt, a_im[..., None, :] * dt
    mag = jnp.exp(expo * ar)
    return mag * jnp.cos(expo * ai), mag * jnp.sin(expo * ai)


def _s5_tables(a_re, a_im, log_dt, b_re, b_im, c_re, c_im):
    depth = a_re.shape[0]
    pwr, pwi = _s5_power_table(a_re, a_im, log_dt)
    dt = jnp.exp(log_dt)[..., None]
    mag = jnp.exp(a_re * dt)
    lb_re, lb_im = mag * jnp.cos(a_im * dt), mag * jnp.sin(a_im * dt)
    den = a_re * a_re + a_im * a_im
    n_re = lb_re - 1.0
    fz = jnp.stack([(n_re * a_re + lb_im * a_im) / den, (lb_im * a_re - n_re * a_im) / den], axis=3)
    mag_l = jnp.exp(S5_L * (a_re * dt))
    pl_re, pl_im = mag_l * jnp.cos(S5_L * (a_im * dt)), mag_l * jnp.sin(S5_L * (a_im * dt))
    lam = jnp.stack([jnp.concatenate([pl_re[:, 0], pl_re[:, 0]], -1), jnp.concatenate([-pl_im[:, 0], pl_im[:, 0]], -1),
                     jnp.concatenate([pl_re[:, 1], pl_re[:, 1]], -1), jnp.concatenate([-pl_im[:, 1], pl_im[:, 1]], -1)],
                    axis=2)
    per_lg = lambda *shape: pl.BlockSpec((None, None) + shape, lambda l, g: (l, g) + (0,) * len(shape))
    per_dir = lambda *shape: pl.BlockSpec((None, 2, None) + shape, lambda l, g: (l, 0, g) + (0,) * len(shape))
    m, s_in, s_out = pl.pallas_call(
        _s5_table_kernel,
        grid=(depth, S5_GROUPS),
        in_specs=[per_dir(S5_SEQS * S5_L, S5_STATE), per_dir(S5_SEQS * S5_L, S5_STATE), per_dir(2, S5_STATE),
                  per_lg(S5_CH, S5_STATE), per_lg(S5_CH, S5_STATE), per_lg(S5_CH, S5_STATE), per_lg(S5_CH, S5_STATE)],
        out_specs=[per_lg(S5_ROW, S5_ROW), per_lg(2, S5_ROW, 2 * S5_STATE), per_lg(2, S5_ROW, 2 * S5_STATE)],
        out_shape=[jax.ShapeDtypeStruct((depth, S5_GROUPS, S5_ROW, S5_ROW), BF16),
                   jax.ShapeDtypeStruct((depth, S5_GROUPS, 2, S5_ROW, 2 * S5_STATE), BF16),
                   jax.ShapeDtypeStruct((depth, S5_GROUPS, 2, S5_ROW, 2 * S5_STATE), BF16)],
        compiler_params=pltpu.CompilerParams(vmem_limit_bytes=VMEM_LIMIT),
        name="s5_tables",
    )(pwr, pwi, fz, jnp.swapaxes(b_re, -1, -2), jnp.swapaxes(b_im, -1, -2), c_re, c_im)
    return m, s_in, s_out, lam


def _s5_kernel(*refs, n, bsz, has_init):
    refs = list(refs)
    su_ref = refs.pop(0)
    m_ref = refs.pop(0)
    sin_ref = refs.pop(0)
    sout_ref = refs.pop(0)
    lam_ref = refs.pop(0)
    dv_ref = refs.pop(0)
    if has_init:
        h0_ref = refs.pop(0)
    y_ref = refs.pop(0)
    fin_ref = refs.pop(0)
    sfs, sbs, hpf, hnb = refs

    r = n * bsz
    gran = lax.broadcasted_iota(jnp.int32, (1, LANES), 1) // S5_CH
    halves_per_row = S5_ROW // LANES
    per_half = S5_L // halves_per_row

    def tok_rows(j):
        return pl.ds(j, r, stride=S5_L)

    halves = [[None] * halves_per_row for _ in range(S5_GB)]
    for j in range(S5_L):
        uj = su_ref[tok_rows(j), :]
        for gl in range(S5_GB):
            shift = ((j % per_half) - gl) % S5_GB * S5_CH
            piece = uj if shift == 0 else pltpu.roll(uj, shift, 1)
            cur = halves[gl][j // per_half]
            halves[gl][j // per_half] = jnp.where(gran == j % per_half, piece, 0.0 if cur is None else cur)
    u = jnp.stack([jnp.concatenate(h, axis=-1) for h in halves]).astype(BF16)

    bm_dims = (((2,), (1,)), ((0,), (0,)))
    bm_nt_dims = (((2,), (2,)), ((0,), (0,)))
    y = lax.dot_general(u, m_ref[...], bm_dims, preferred_element_type=F32)
    sfs[...] = lax.dot_general(u, sin_ref[:, 0], bm_dims, preferred_element_type=F32)
    sbs[...] = lax.dot_general(u, sin_ref[:, 1], bm_dims, preferred_element_type=F32)

    a_f, s_f = lam_ref[:, 0:1, :], lam_ref[:, 1:2, :]
    a_b, s_b = lam_ref[:, 2:3, :], lam_ref[:, 3:4, :]
    if has_init:
        hf = h0_ref[:, 0]
        hb = h0_ref[:, 1]
    else:
        hf = jnp.zeros((S5_GB, bsz, 2 * S5_STATE), F32)
        hb = jnp.zeros((S5_GB, bsz, 2 * S5_STATE), F32)
    for c in range(n):
        rows = pl.ds(c, bsz, stride=n)
        hpf[:, rows, :] = hf
        hf = hf * a_f + pltpu.roll(hf, S5_STATE, 2) * s_f + sfs[:, rows, :]
        rows = pl.ds(n - 1 - c, bsz, stride=n)
        hnb[:, rows, :] = hb
        hb = hb * a_b + pltpu.roll(hb, S5_STATE, 2) * s_b + sbs[:, rows, :]
    fin_ref[:, 0] = hf
    fin_ref[:, 1] = hb
    y = y + lax.dot_general(hpf[...].astype(BF16), sout_ref[:, 0], bm_nt_dims, preferred_element_type=F32)
    y = y + lax.dot_general(hnb[...].astype(BF16), sout_ref[:, 1], bm_nt_dims, preferred_element_type=F32)

    for j in range(S5_L):
        acc = None
        for gl in range(S5_GB):
            src = y[gl][:, (j // per_half) * LANES:(j // per_half + 1) * LANES]
            shift = (gl - (j % per_half)) % S5_GB * S5_CH
            piece = src if shift == 0 else pltpu.roll(src, shift, 1)
            acc = jnp.where(gran == gl, piece, 0.0 if acc is None else acc)
        y_ref[tok_rows(j), :] = acc + su_ref[tok_rows(j), :] * dv_ref[...]


def _s5_call(su, row_block, tabs, layer, dvec, bsz, t, h0):
    m, s_in, s_out, lam = tabs
    n = t // S5_L
    r = n * bsz
    rows = bsz * t
    has_init = h0 is not None
    gspec = lambda *shape: pl.BlockSpec((S5_GB,) + shape, lambda g: (g,) + (0,) * len(shape))
    lspec = lambda *shape: pl.BlockSpec((None, S5_GB) + shape, lambda g: (layer, g) + (0,) * len(shape))
    in_specs = [pl.BlockSpec((rows, LANES), lambda g: (row_block, g)),
                lspec(S5_ROW, S5_ROW), lspec(2, S5_ROW, 2 * S5_STATE), lspec(2, S5_ROW, 2 * S5_STATE),
                lspec(4, 2 * S5_STATE), pl.BlockSpec((1, LANES), lambda g: (0, g))]
    args = [su, m, s_in, s_out, lam, dvec]
    if has_init:
        in_specs.append(gspec(2, bsz, 2 * S5_STATE))
        args.append(h0)
    return pl.pallas_call(
        functools.partial(_s5_kernel, n=n, bsz=bsz, has_init=has_init),
        grid=(S5_GROUPS // S5_GB,),
        in_specs=in_specs,
        out_specs=[pl.BlockSpec((rows, LANES), lambda g: (0, g)), gspec(2, bsz, 2 * S5_STATE)],
        out_shape=[jax.ShapeDtypeStruct((rows, S5_W), F32),
                   jax.ShapeDtypeStruct((S5_GROUPS, 2, bsz, 2 * S5_STATE), F32)],
        scratch_shapes=[pltpu.VMEM((S5_GB, r, 2 * S5_STATE), F32)] * 4,
        compiler_params=pltpu.CompilerParams(vmem_limit_bytes=VMEM_LIMIT),
        name="s5_scan",
    )(*args)


def _outproj_kernel(yrg_c_ref, yrg_l_ref, ys_c_ref, ys_l_ref, x_ref, mod_ref, n2_ref, wo_ref, gw_ref, gb_ref,
                    rw_ref, rb_ref, x1_ref, xs_ref, pos_ref, wt_ref, meta_ref, *, n_ctx_tiles):
    is_ctx = pl.program_id(0) < n_ctx_tiles
    ys = jnp.where(is_ctx, ys_c_ref[...], ys_l_ref[...])
    yrg = jnp.where(is_ctx, yrg_c_ref[...], yrg_l_ref[...])
    s = 0.5 * ys * (1.0 + jnp.tanh(np.sqrt(2.0 / np.pi).astype(np.float32) * (ys + 0.044715 * (ys * ys * ys))))
    s = s * _sigmoid(jnp.dot(s.astype(BF16), gw_ref[...], preferred_element_type=F32) + gb_ref[...])
    m = (jnp.dot(yrg.astype(BF16), wo_ref[0:2 * HG_W, :], preferred_element_type=F32)
         + jnp.dot(s.astype(BF16), wo_ref[2 * HG_W:, :], preferred_element_type=F32))
    x1 = x_ref[...] + mod_ref[2:3, :] * m
    x1_ref[...] = x1
    h = x1 * lax.rsqrt(jnp.mean(x1 * x1, axis=-1, keepdims=True) + EPS) * n2_ref[...]
    h = h * (1.0 + mod_ref[4:5, :]) + mod_ref[3:4, :]

    logits = jnp.dot(h, rw_ref[...], precision=HIGHEST, preferred_element_type=F32) + rb_ref[...]
    lane = lax.broadcasted_iota(jnp.int32, (TM, LANES), 1).astype(F32)
    cur = logits
    vals, idxs = [], []
    for _ in range(TOP_K):
        mx = jnp.max(cur, axis=-1, keepdims=True)
        am = jnp.min(jnp.where(cur == mx, lane, float(LANES)), axis=-1, keepdims=True)
        vals.append(mx)
        idxs.append(am)
        cur = jnp.where(lane == am, -jnp.inf, cur)
    es = [jnp.exp(v - vals[0]) for v in vals]
    den = es[0] + es[1] + es[2] + es[3]

    hit = [(lane == am) for am in idxs]
    assign = (hit[0] | hit[1] | hit[2] | hit[3]).astype(BF16)
    ti = lax.broadcasted_iota(jnp.int32, (TM, TM), 0)
    tj = lax.broadcasted_iota(jnp.int32, (TM, TM), 1)
    rank = jnp.dot((tj < ti).astype(BF16), assign, preferred_element_type=F32)
    cnt = jnp.sum(assign.astype(F32), axis=0, keepdims=True)
    cnt_al = jnp.floor((cnt + (SEG_ALIGN - 1)) * (1.0 / SEG_ALIGN)) * SEG_ALIGN
    ei = lax.broadcasted_iota(jnp.int32, (LANES, LANES), 0)
    ej = lax.broadcasted_iota(jnp.int32, (LANES, LANES), 1)
    seg = jnp.dot(jnp.broadcast_to(cnt_al, (SUBLANES, LANES)).astype(BF16), (ei < ej).astype(BF16),
                  preferred_element_type=F32)[0:1, :]
    base = seg + rank
    rlane = lax.broadcasted_iota(jnp.int32, (TM, ROWS_T), 1).astype(F32)
    klane = lax.broadcasted_iota(jnp.int32, (TM, LANES), 1)
    onehot = jnp.zeros((TM, ROWS_T), F32)
    pos_out = jnp.zeros((TM, LANES), F32)
    wt_out = jnp.zeros((TM, LANES), F32)
    for kk in range(TOP_K):
        pk = jnp.sum(jnp.where(hit[kk], base, 0.0), axis=-1, keepdims=True)
        onehot = onehot + (rlane == pk).astype(F32)
        pos_out = jnp.where(klane == kk, pk, pos_out)
        wt_out = jnp.where(klane == kk, es[kk] / den, wt_out)
    xs_ref[0] = lax.dot_general(onehot.astype(BF16), h.astype(BF16), (((0,), (0,)), ((), ())),
                                preferred_element_type=F32)
    pos_ref[...] = pos_out.astype(jnp.int32)
    wt_ref[...] = wt_out
    mrow = lax.broadcasted_iota(jnp.int32, (SUBLANES, LANES), 0)
    meta = jnp.where(mrow == 0, jnp.broadcast_to(cnt_al, (SUBLANES, LANES)),
                     jnp.where(mrow == 1, jnp.broadcast_to(seg, (SUBLANES, LANES)), 0.0))
    meta_ref[0] = meta.astype(jnp.int32)


def _outproj_call(yrg_c, yrg_l, ys_c, ys_l, x, mod_l, n2, wo, gw, gb, rw, rb, n_ctx_tiles, tiles_per_lat):
    n = x.shape[0]
    nt = n // TM
    gid = functools.partial(_group_id, n_ctx_tiles=n_ctx_tiles, tiles_per_lat=tiles_per_lat)
    const = lambda *shape: pl.BlockSpec(shape, lambda i: (0,) * len(shape))
    ctx_map = lambda i: (jnp.minimum(i, n_ctx_tiles - 1), 0)
    lat_map = lambda i: (jnp.maximum(i - n_ctx_tiles, 0), 0)
    return pl.pallas_call(
        functools.partial(_outproj_kernel, n_ctx_tiles=n_ctx_tiles),
        grid=(nt,),
        in_specs=[
            pl.BlockSpec((TM, 2 * HG_W), ctx_map),
            pl.BlockSpec((TM, 2 * HG_W), lat_map),
            pl.BlockSpec((TM, S5_W), ctx_map),
            pl.BlockSpec((TM, S5_W), lat_map),
            pl.BlockSpec((TM, D_MODEL), lambda i: (i, 0)),
            pl.BlockSpec((None, 6, D_MODEL), lambda i: (gid(i), 0, 0)),
            const(1, D_MODEL), const(D_MODEL, D_MODEL), const(S5_W, S5_W), const(1, S5_W),
            const(D_MODEL, LANES), const(1, LANES),
        ],
        out_specs=[
            pl.BlockSpec((TM, D_MODEL), lambda i: (i, 0)),
            pl.BlockSpec((1, ROWS_T, D_MODEL), lambda i: (i, 0, 0)),
            pl.BlockSpec((TM, LANES), lambda i: (i, 0)),
            pl.BlockSpec((TM, LANES), lambda i: (i, 0)),
            pl.BlockSpec((1, SUBLANES, LANES), lambda i: (i, 0, 0)),
        ],
        out_shape=[
            jax.ShapeDtypeStruct((n, D_MODEL), F32),
            jax.ShapeDtypeStruct((nt, ROWS_T, D_MODEL), F32),
            jax.ShapeDtypeStruct((n, LANES), jnp.int32),
            jax.ShapeDtypeStruct((n, LANES), F32),
            jax.ShapeDtypeStruct((nt, SUBLANES, LANES), jnp.int32),
        ],
        compiler_params=pltpu.CompilerParams(vmem_limit_bytes=VMEM_LIMIT),
        name="outproj_router_dispatch",
    )(yrg_c, yrg_l, ys_c, ys_l, x, mod_l, n2, wo, gw, gb, rw, rb)


def _expert_schedule(meta, nblk):
    nt = meta.shape[0]
    nch = (meta[:, 0, :N_EXPERTS] // SEG_ALIGN).T
    seg_chunk = (meta[:, 1, :N_EXPERTS] // SEG_ALIGN).T
    cs_end = jnp.cumsum(nch, axis=1)
    cs_start = cs_end - nch
    tot = cs_end[:, -1]
    nb = (tot + BLK_CHUNKS - 1) // BLK_CHUNKS
    blk_end = jnp.cumsum(nb)
    blk_start = blk_end - nb
    n_active = blk_end[-1]
    j = jnp.arange(nblk, dtype=jnp.int32)
    be = jnp.sum(blk_end[None, :] <= jnp.minimum(j, n_active - 1)[:, None], axis=1)
    be = jnp.clip(be, 0, N_EXPERTS - 1).astype(jnp.int32)
    own = be[:, None] == jnp.arange(N_EXPERTS, dtype=jnp.int32)[None, :]
    pick = lambda a: jnp.sum(jnp.where(own[:, :, None], a[None], 0), axis=1)
    bs = (j - jnp.sum(jnp.where(own, blk_start[None, :], 0), axis=1)) * BLK_CHUNKS
    tot_b = jnp.sum(jnp.where(own, tot[None, :], 0), axis=1)
    nvalid = jnp.where(j < n_active, jnp.clip(tot_b - bs, 0, BLK_CHUNKS), 0)
    kfirst = jnp.sum(pick(cs_end) <= bs[:, None], axis=1)
    klast = jnp.sum(pick(cs_start) < (bs + nvalid)[:, None], axis=1)
    src_base = jnp.arange(nt, dtype=jnp.int32)[None, :] * CHUNKS_T + seg_chunk
    i32 = lambda a: a.astype(jnp.int32)
    return (be, i32(nvalid), i32(bs), i32(kfirst), i32(klast),
            i32(cs_start.reshape(-1)), i32(nch.reshape(-1)), i32(src_base.reshape(-1)))


W_CAST_ROWS = 128


def _ffn_kernel(be_ref, nv_ref, bs_ref, kf_ref, kl_ref, cst_ref, nch_ref, srcb_ref,
                xs_hbm, w1_ref, b1_ref, w2_ref, b2_ref, ys_hbm,
                xbuf, ybuf, w1b, w2b, gsem, ssem, *, nblk, nt):
    j = pl.program_id(0)
    slot = j % 2

    def for_pieces(blk, fn):
        e_off = be_ref[blk] * nt
        lo_b = bs_ref[blk]
        hi_b = lo_b + nv_ref[blk]

        def body(i, carry):
            ps = cst_ref[e_off + i]
            lo = jnp.maximum(ps, lo_b)
            n = jnp.minimum(ps + nch_ref[e_off + i], hi_b) - lo

            @pl.when(n > 0)
            def _():
                fn(srcb_ref[e_off + i] + (lo - ps), lo - lo_b, n)
            return carry

        lax.fori_loop(kf_ref[blk], kl_ref[blk], body, 0)

    def start_gather(blk, s):
        for_pieces(blk, lambda src, dst, n: pltpu.make_async_copy(
            xs_hbm.at[pl.ds(src, n)], xbuf.at[s, pl.ds(dst, n)], gsem.at[s]).start())

    def start_scatter(blk, s):
        for_pieces(blk, lambda src, dst, n: pltpu.make_async_copy(
            ybuf.at[s, pl.ds(dst, n)], ys_hbm.at[pl.ds(src, n)], ssem.at[s]).start())

    def wait_gather(blk, s):
        @pl.when(nv_ref[blk] > 0)
        def _():
            pltpu.make_async_copy(xs_hbm.at[pl.ds(0, nv_ref[blk])], xbuf.at[s, pl.ds(0, nv_ref[blk])],
                                  gsem.at[s]).wait()

    def wait_scatter(blk, s):
        @pl.when(nv_ref[blk] > 0)
        def _():
            pltpu.make_async_copy(ybuf.at[s, pl.ds(0, nv_ref[blk])], ys_hbm.at[pl.ds(0, nv_ref[blk])],
                                  ssem.at[s]).wait()

    @pl.when(j == 0)
    def _():
        xbuf[...] = jnp.zeros_like(xbuf)
        start_gather(0, 0)

    @pl.when(j + 1 < nblk)
    def _():
        start_gather(j + 1, 1 - slot)

    @pl.when((j == 0) | (be_ref[j] != be_ref[jnp.maximum(j - 1, 0)]))
    def _():
        def cast_rows(r, carry):
            rows = pl.ds(pl.multiple_of(r * W_CAST_ROWS, W_CAST_ROWS), W_CAST_ROWS)
            w1b[rows, :] = w1_ref[rows, :].astype(BF16)
            w2b[rows, :] = w2_ref[rows, :].astype(BF16)
            return carry
        lax.fori_loop(0, D_MODEL // W_CAST_ROWS, cast_rows, 0)

    wait_gather(j, slot)

    @pl.when(j >= 2)
    def _():
        wait_scatter(j - 2, slot)

    @pl.when(nv_ref[j] > 0)
    def _():
        x = xbuf[slot].reshape(BLK_ROWS, D_MODEL).astype(BF16)
        gu = jnp.dot(x, w1b[...], preferred_element_type=F32) + b1_ref[...]
        gate = jnp.minimum(gu[:, :D_FF], SWIGLU_LIMIT)
        up = jnp.clip(gu[:, D_FF:], -SWIGLU_LIMIT, SWIGLU_LIMIT)
        act = (up + 1.0) * gate * _sigmoid(SWIGLU_ALPHA * gate)
        y = jnp.dot(act.astype(BF16), w2b[...], preferred_element_type=F32) + b2_ref[...]
        ybuf[slot] = y.reshape(BLK_CHUNKS, SEG_ALIGN, D_MODEL)

    start_scatter(j, slot)

    @pl.when(j == nblk - 1)
    def _():
        wait_scatter(j, slot)
        if nblk >= 2:
            wait_scatter(j - 1, 1 - slot)


def _ffn_call(sched, xs_chunks, w1, b1, w2, b2, layer, nt):
    assert D_FF == D_MODEL
    nblk = sched[0].shape[0]
    nch = xs_chunks.shape[0]
    wmap = lambda j, be, *_: (layer, be[j], 0, 0)
    grid_spec = pltpu.PrefetchScalarGridSpec(
        num_scalar_prefetch=len(sched),
        grid=(nblk,),
        in_specs=[
            pl.BlockSpec(memory_space=pl.ANY),
            pl.BlockSpec((None, None, D_MODEL, 2 * D_FF), wmap),
            pl.BlockSpec((None, None, 1, 2 * D_FF), wmap),
            pl.BlockSpec((None, None, D_FF, D_MODEL), wmap),
            pl.BlockSpec((None, None, 1, D_MODEL), wmap),
        ],
        out_specs=pl.BlockSpec(memory_space=pl.ANY),
        scratch_shapes=[
            pltpu.VMEM((2, BLK_CHUNKS, SEG_ALIGN, D_MODEL), F32),
            pltpu.VMEM((2, BLK_CHUNKS, SEG_ALIGN, D_MODEL), F32),
            pltpu.VMEM((D_MODEL, 2 * D_FF), BF16),
            pltpu.VMEM((D_FF, D_MODEL), BF16),
            pltpu.SemaphoreType.DMA((2,)),
            pltpu.SemaphoreType.DMA((2,)),
        ],
    )
    return pl.pallas_call(
        functools.partial(_ffn_kernel, nblk=nblk, nt=nt),
        grid_spec=grid_spec,
        out_shape=jax.ShapeDtypeStruct((nch, SEG_ALIGN, D_MODEL), F32),
        input_output_aliases={len(sched): 0},
        compiler_params=pltpu.CompilerParams(vmem_limit_bytes=VMEM_LIMIT, dimension_semantics=("arbitrary",)),
        name="expert_ffn",
    )(*sched, xs_chunks, w1, b1, w2, b2)


def _combine_kernel(ys_ref, pos_ref, wt_ref, x1_ref, mod_ref, fg_ref, *out_refs, final):
    ys = ys_ref[0].astype(BF16)
    rlane = lax.broadcasted_iota(jnp.int32, (TM, ROWS_T), 1)
    w = jnp.zeros((TM, ROWS_T), F32)
    for kk in range(TOP_K):
        w = w + jnp.where(rlane == pos_ref[:, kk:kk + 1], wt_ref[:, kk:kk + 1], 0.0)
    moe = jnp.dot(w.astype(BF16), ys, preferred_element_type=F32)
    x2 = x1_ref[...] + mod_ref[5:6, :] * moe
    out_refs[0][...] = x2
    if final:
        out_refs[1][...] = x2 * lax.rsqrt(jnp.mean(x2 * x2, axis=-1, keepdims=True) + EPS) * fg_ref[...]


def _combine_call(ys, pos, wts, x1, mod_l, fg, n_ctx_tiles, tiles_per_lat, final):
    n = x1.shape[0]
    nt = n // TM
    gid = functools.partial(_group_id, n_ctx_tiles=n_ctx_tiles, tiles_per_lat=tiles_per_lat)
    tok = pl.BlockSpec((TM, D_MODEL), lambda i: (i, 0))
    res = pl.pallas_call(
        functools.partial(_combine_kernel, final=final),
        grid=(nt,),
        in_specs=[
            pl.BlockSpec((1, ROWS_T, D_MODEL), lambda i: (i, 0, 0)),
            pl.BlockSpec((TM, LANES), lambda i: (i, 0)),
            pl.BlockSpec((TM, LANES), lambda i: (i, 0)),
            tok,
            pl.BlockSpec((None, 6, D_MODEL), lambda i: (gid(i), 0, 0)),
            pl.BlockSpec((1, D_MODEL), lambda i: (0, 0)),
        ],
        out_specs=[tok, tok] if final else [tok],
        out_shape=[jax.ShapeDtypeStruct((n, D_MODEL), F32)] * (2 if final else 1),
        compiler_params=pltpu.CompilerParams(vmem_limit_bytes=VMEM_LIMIT),
        name="moe_combine",
    )(ys, pos, wts, x1, mod_l, fg)
    return res


def _rope_tables(t):
    pos = np.arange(t)
    nf = HEAD_DIM // 4
    inv = jnp.asarray(ROPE_BASE, F32) ** (-jnp.arange(nf, dtype=F32) / nf)
    ang = jnp.concatenate([jnp.asarray(pos // GRID_W, F32)[:, None] * inv,
                           jnp.asarray(pos % GRID_W, F32)[:, None] * inv], axis=-1)
    cos, sin = jnp.cos(ang), jnp.sin(ang)
    return (jnp.tile(jnp.concatenate([cos, cos], -1), (1, N_HEADS)),
            jnp.tile(jnp.concatenate([-sin, sin], -1), (1, N_HEADS)))


def _block_diag_t(s):
    eye = jnp.eye(N_HEADS, dtype=s.dtype)
    out = jnp.einsum('...hkv,hg->...hvgk', s, eye)
    return out.reshape(s.shape[:-3] + (HG_W, HG_W))


def _diag_blocks(st):
    blocks = [st[..., h * HEAD_DIM:(h + 1) * HEAD_DIM, h * HEAD_DIM:(h + 1) * HEAD_DIM] for h in range(N_HEADS)]
    return jnp.swapaxes(jnp.stack(blocks, axis=-3), -1, -2)


def kernel(x_prompt, x_sample, state_ret, state_gla, state_s5_re, state_s5_im, c, c_ctx, norm1_g, norm2_g, final_g, ada_w, ada_b, w_in, w_out, ret_decay, gla_w_lr, gla_b_lr, gla_norm_g, s5_a_re, s5_a_im, s5_log_dt, s5_b_re, s5_b_im, s5_c_re, s5_c_im, s5_d, s5_glu_w, s5_glu_b, router_w, router_b, moe_w1, moe_b1, moe_w2, moe_b2):
    bc, tc, d = x_prompt.shape
    bl, tl, _ = x_sample.shape
    depth = w_in.shape[0]
    nc, nl = bc * tc, bl * tl
    n = nc + nl
    assert d == D_MODEL and nc % TM == 0 and tl % TM == 0 and tc % CHUNK == 0 and tl % CHUNK == 0
    assert 1 + bl <= SUBLANES and nc % nl == 0
    n_ctx_tiles, tiles_per_lat = nc // TM, tl // TM
    nt = n // TM
    nblk = nt * CHUNKS_T // BLK_CHUNKS + N_EXPERTS

    x = jnp.concatenate([x_prompt.reshape(nc, d), x_sample.reshape(nl, d)], axis=0)
    cond8 = jnp.concatenate([c_ctx[None, :], c, jnp.zeros((SUBLANES - 1 - bl, d), F32)], axis=0)
    mod = _ada_call(cond8, ada_w, ada_b).reshape(depth, SUBLANES, 6, d)
    rope = _rope_tables(tl)
    fg = final_g.reshape(1, d)
    s5_tabs = _s5_tables(s5_a_re, s5_a_im, s5_log_dt, s5_b_re, s5_b_im, s5_c_re, s5_c_im)

    ret_states, gla_states, re_states, im_states = [], [], [], []
    y_final = None
    for l in range(depth):
        wl = w_in[l]
        w_cat = jnp.concatenate(
            [wl[:, :8 * HG_W], wl[:, 8 * HG_W + GLA_RANK:],
             jnp.pad(wl[:, 8 * HG_W:8 * HG_W + GLA_RANK], ((0, 0), (0, LANES - GLA_RANK)))], axis=1).astype(BF16)
        wlr = jnp.pad(gla_w_lr[l], ((0, 0), (0, LANES - GLA_RANK), (0, 0)))
        blr = gla_b_lr[l].reshape(2, 1, HG_W)
        pa, su = _inproj_call(x, mod[l], norm1_g[l].reshape(1, d), w_cat, wlr, blr, n_ctx_tiles, tiles_per_lat)

        log_gamma = jnp.log1p(-jnp.exp(ret_decay[l]))
        lgam = jnp.repeat(log_gamma, HEAD_DIM, axis=1)
        gn = jnp.tile(gla_norm_g[l], N_HEADS).reshape(1, HG_W)
        y_c, st_c = _scan_call(pa, 0, bc, tc, log_gamma, lgam, gn, None, None, True)
        st0 = jnp.stack([_block_diag_t(state_ret[:, l]), _block_diag_t(state_gla[:, l])], axis=2)
        y_l, _ = _scan_call(pa, nc // CHUNK, bl, tl, log_gamma, lgam, gn, rope, st0, False)
        ret_states.append(_diag_blocks(st_c[:, :, 0]))
        gla_states.append(_diag_blocks(st_c[:, :, 1]))

        dvec = s5_d[l].reshape(1, S5_W)
        ys_c, fin_c = _s5_call(su, 0, s5_tabs, l, dvec, bc, tc, None)
        h0 = jnp.concatenate([state_s5_re[:, l], state_s5_im[:, l]], axis=-1).transpose(2, 1, 0, 3)
        ys_l, _ = _s5_call(su, nc // nl, s5_tabs, l, dvec, bl, tl, h0)
        re_states.append(fin_c[..., :S5_STATE].transpose(2, 1, 0, 3))
        im_states.append(fin_c[..., S5_STATE:].transpose(2, 1, 0, 3))

        rw = jnp.pad(router_w[l], ((0, 0), (0, LANES - N_EXPERTS)))
        rb = jnp.concatenate([router_b[l], jnp.full((LANES - N_EXPERTS,), NEG_BIG, F32)]).reshape(1, LANES)
        x1, xs, pos, wts, meta = _outproj_call(
            y_c, y_l, ys_c, ys_l, x, mod[l], norm2_g[l].reshape(1, d), w_out[l].astype(BF16), s5_glu_w[l].astype(BF16),
            s5_glu_b[l].reshape(1, S5_W), rw, rb, n_ctx_tiles, tiles_per_lat)

        sched = _expert_schedule(meta, nblk)
        ys = _ffn_call(sched, xs.reshape(nt * CHUNKS_T, SEG_ALIGN, d),
                       moe_w1, moe_b1.reshape(depth, N_EXPERTS, 1, 2 * D_FF),
                       moe_w2, moe_b2.reshape(depth, N_EXPERTS, 1, d), l, nt)
        res = _combine_call(ys.reshape(nt, ROWS_T, d), pos, wts, x1, mod[l], fg,
                            n_ctx_tiles, tiles_per_lat, l == depth - 1)
        x = res[0]
        if l == depth - 1:
            y_final = res[1]

    y_prompt = y_final[:nc].reshape(bc, tc, d)
    y_sample = y_final[nc:].reshape(bl, tl, d)
    return (y_prompt, y_sample, jnp.stack(ret_states, axis=1), jnp.stack(gla_states, axis=1),
            jnp.stack(re_states, axis=1), jnp.stack(im_states, axis=1))
```

```python
import functools

import numpy as np
import jax
import jax.numpy as jnp
from jax import lax
from jax.experimental import pallas as pl
from jax.experimental.pallas import tpu as pltpu

F32 = jnp.float32
BF16 = jnp.bfloat16
HIGHEST = lax.Precision.HIGHEST

D_MODEL = 1024
GRID_W = 64
CHUNK = 128
HEAD_DIM = 64
N_HEADS = 4
HG_W = N_HEADS * HEAD_DIM
S5_W = 512
S5_CH = 16
S5_GROUPS = 32
S5_STATE = 64
S5_L = 16
S5_ROW = S5_L * S5_CH
GLA_RANK = 16
GLA_TAU = 16.0
N_EXPERTS = 32
TOP_K = 4
D_FF = 1024
SWIGLU_LIMIT = 7.0
SWIGLU_ALPHA = 1.702
ROPE_BASE = 10000.0
EPS = 1e-6

LANES = 128
SUBLANES = 8
TM = 256
SEG_ALIGN = SUBLANES
ROWS_T = 1280
CHUNKS_T = ROWS_T // SEG_ALIGN
BLK_CHUNKS = 64
BLK_ROWS = BLK_CHUNKS * SEG_ALIGN
NEG_BIG = -1e30
VMEM_LIMIT = 56 * 1024 * 1024

PA_W = 8 * HG_W + 2 * HG_W
W_CAT = 8 * HG_W + S5_W + LANES


def _sigmoid(x):
    return 1.0 / (1.0 + jnp.exp(-x))


def _group_id(i, n_ctx_tiles, tiles_per_lat):
    return jnp.where(i < n_ctx_tiles, 0, 1 + (i - n_ctx_tiles) // tiles_per_lat)


def _ada_kernel(c_ref, w_ref, b_ref, o_ref):
    c = c_ref[...]
    s = c * _sigmoid(c)
    o_ref[0] = jnp.dot(s, w_ref[0], precision=HIGHEST, preferred_element_type=F32) + b_ref[0]


def _ada_call(cond8, ada_w, ada_b):
    depth, d, n6 = ada_w.shape
    tn = 1024
    return pl.pallas_call(
        _ada_kernel,
        grid=(depth, n6 // tn),
        in_specs=[
            pl.BlockSpec((SUBLANES, d), lambda l, j: (0, 0)),
            pl.BlockSpec((1, d, tn), lambda l, j: (l, 0, j)),
            pl.BlockSpec((1, 1, tn), lambda l, j: (l, 0, j)),
        ],
        out_specs=pl.BlockSpec((1, SUBLANES, tn), lambda l, j: (l, 0, j)),
        out_shape=jax.ShapeDtypeStruct((depth, SUBLANES, n6), F32),
        compiler_params=pltpu.CompilerParams(vmem_limit_bytes=VMEM_LIMIT),
        name="ada_mod",
    )(cond8, ada_w, ada_b.reshape(depth, 1, n6))


def _inproj_kernel(x_ref, mod_ref, n1_ref, w_ref, wlr_ref, blr_ref, pa_ref, su_ref):
    x = x_ref[...]
    h = x * lax.rsqrt(jnp.mean(x * x, axis=-1, keepdims=True) + EPS) * n1_ref[...]
    h = h * (1.0 + mod_ref[1:2, :]) + mod_ref[0:1, :]
    r = jnp.dot(h.astype(BF16), w_ref[...], preferred_element_type=F32)
    pa_ref[:, : 8 * HG_W] = r[:, : 8 * HG_W]
    su_ref[...] = r[:, 8 * HG_W: 8 * HG_W + S5_W]
    glr = r[:, 8 * HG_W + S5_W:]
    for d in range(2):
        z = jnp.dot(glr, wlr_ref[d], precision=HIGHEST, preferred_element_type=F32) + blr_ref[d]
        log_sig = jnp.minimum(z, 0.0) - jnp.log(1.0 + jnp.exp(-jnp.abs(z)))
        pa_ref[:, (8 + d) * HG_W: (9 + d) * HG_W] = log_sig * (1.0 / GLA_TAU)


def _inproj_call(x, mod_l, n1, w_cat, wlr, blr, n_ctx_tiles, tiles_per_lat):
    n = x.shape[0]
    gid = functools.partial(_group_id, n_ctx_tiles=n_ctx_tiles, tiles_per_lat=tiles_per_lat)
    return pl.pallas_call(
        _inproj_kernel,
        grid=(n // TM,),
        in_specs=[
            pl.BlockSpec((TM, D_MODEL), lambda i: (i, 0)),
            pl.BlockSpec((None, 6, D_MODEL), lambda i: (gid(i), 0, 0)),
            pl.BlockSpec((1, D_MODEL), lambda i: (0, 0)),
            pl.BlockSpec((D_MODEL, W_CAT), lambda i: (0, 0)),
            pl.BlockSpec((2, LANES, HG_W), lambda i: (0, 0, 0)),
            pl.BlockSpec((2, 1, HG_W), lambda i: (0, 0, 0)),
        ],
        out_specs=[
            pl.BlockSpec((TM, PA_W), lambda i: (i, 0)),
            pl.BlockSpec((TM, S5_W), lambda i: (i, 0)),
        ],
        out_shape=[
            jax.ShapeDtypeStruct((n, PA_W), F32),
            jax.ShapeDtypeStruct((n, S5_W), F32),
        ],
        compiler_params=pltpu.CompilerParams(vmem_limit_bytes=VMEM_LIMIT),
        name="norm1_inproj",
    )(x, mod_l, n1, w_cat, wlr, blr)


SCAN_PB = 2


def _scan_kernel(*refs, n, use_rope, has_init, want_final):
    refs = list(refs)
    lg_ref = refs.pop(0)
    pa_ref = refs.pop(0)
    if use_rope:
        cos_ref = refs.pop(0)
        sin_ref = refs.pop(0)
    lgam_ref = refs.pop(0)
    gn_ref = refs.pop(0)
    if has_init:
        st0_ref = refs.pop(0)
    y_ref = refs.pop(0)
    if want_final:
        stf_ref = refs.pop(0)
    st_scr = refs.pop(0)
    of_scr = refs.pop(0)

    d = pl.program_id(1)
    c = pl.program_id(2)
    cc = jnp.where(d == 0, c, n - 1 - c)

    @pl.when(c == 0)
    def _():
        if has_init:
            st_scr[...] = st0_ref[...]
        else:
            st_scr[...] = jnp.zeros_like(st_scr)

    ii = lax.broadcasted_iota(jnp.int32, (CHUNK, CHUNK), 0)
    jj = lax.broadcasted_iota(jnp.int32, (CHUNK, CHUNK), 1)
    rel = (1 - 2 * d) * (ii - jj)
    mask = rel >= 0
    i4 = lax.broadcasted_iota(jnp.int32, (N_HEADS * CHUNK, CHUNK), 0) % CHUNK
    j4 = lax.broadcasted_iota(jnp.int32, (N_HEADS * CHUNK, CHUNK), 1)
    mask4 = (1 - 2 * d) * (i4 - j4) >= 0
    relf = rel.astype(F32)
    row = lax.broadcasted_iota(jnp.int32, (CHUNK, 1), 0)
    p1 = jnp.where(d == 0, row + 1, CHUNK - row).astype(F32)
    lane = lax.broadcasted_iota(jnp.int32, (1, HG_W), 1)
    head_mask = [(lane // HEAD_DIM == h).astype(F32) for h in range(N_HEADS)]
    bi = lax.broadcasted_iota(jnp.int32, (HG_W, HG_W), 0) // HEAD_DIM
    bj = lax.broadcasted_iota(jnp.int32, (HG_W, HG_W), 1) // HEAD_DIM
    block_diag = bi == bj
    tri = mask.astype(BF16)
    lgl = lgam_ref[pl.ds(d, 1), :]
    ret_q_decay = jnp.exp(p1 * lgl)
    ret_k_decay = jnp.exp((CHUNK - p1) * lgl)
    ret_chunk_decay = jnp.exp(CHUNK * lgl)
    ret_weight = jnp.concatenate([jnp.where(mask, jnp.exp(relf * lg_ref[d, h]), 0.0) for h in range(N_HEADS)], axis=0)

    nt_dims = (((1,), (1,)), ((), ()))
    tn_dims = (((0,), (0,)), ((), ()))

    def split3(x):
        hi = x.astype(BF16)
        r1 = x - hi.astype(F32)
        mid = r1.astype(BF16)
        return hi, mid, (r1 - mid.astype(F32)).astype(BF16)

    def group(q, k, v, q_dec, k_upd, chunk_decay, score_weight, bb, g_idx):
        kb = k.astype(BF16)
        vb = v.astype(BF16)
        qs = jnp.concatenate([(q * head_mask[h]).astype(BF16) for h in range(N_HEADS)], axis=0)
        s = lax.dot_general(qs, kb, nt_dims, preferred_element_type=F32)
        s = jnp.where(mask4, s, 0.0) if score_weight is None else s * score_weight
        o4 = jnp.dot(s.astype(BF16), vb, preferred_element_type=F32)
        o = o4[0:CHUNK] * head_mask[0]
        for h in range(1, N_HEADS):
            o = o + o4[h * CHUNK:(h + 1) * CHUNK] * head_mask[h]
        st = st_scr[bb, g_idx]
        o = o + lax.dot_general(q_dec.astype(BF16), st.astype(BF16), nt_dims, preferred_element_type=F32)
        upd = lax.dot_general(vb, k_upd.astype(BF16), tn_dims, preferred_element_type=F32)
        st_scr[bb, g_idx] = st * chunk_decay + jnp.where(block_diag, upd, 0.0)
        return o

    def rope(x):
        partner = jnp.where((lane % HEAD_DIM) < HEAD_DIM // 2,
                            pltpu.roll(x, HG_W - HEAD_DIM // 2, 1), pltpu.roll(x, HEAD_DIM // 2, 1))
        return x * cos_ref[...] + partner * sin_ref[...]

    def one_row(bb):
        q = pa_ref[bb, :, 0:HG_W]
        k = pa_ref[bb, :, HG_W:2 * HG_W] * (HEAD_DIM ** -0.5)
        v = pa_ref[bb, :, 2 * HG_W:3 * HG_W]
        if use_rope:
            q = rope(q)
            k = rope(k)
        o_ret = group(q, k, v, q * ret_q_decay, k * ret_k_decay, ret_chunk_decay, ret_weight, bb, 0)

        q = pa_ref[bb, :, 4 * HG_W:5 * HG_W] * (HEAD_DIM ** -0.5)
        k = pa_ref[bb, :, 5 * HG_W:6 * HG_W]
        v = pa_ref[bb, :, 6 * HG_W:7 * HG_W]
        la = jnp.where(d == 0, pa_ref[bb, :, 8 * HG_W:9 * HG_W], pa_ref[bb, :, 9 * HG_W:10 * HG_W])
        b = sum(jnp.dot(tri, part, preferred_element_type=F32) for part in split3(la))
        b_last = jnp.sum(la, axis=0, keepdims=True)
        q_dec = q * jnp.exp(b)
        o_gla = group(q_dec, k * jnp.exp(-b), v, q_dec, k * jnp.exp(b_last - b), jnp.exp(b_last), None, bb, 1)
        return o_ret, o_gla

    outs = [one_row(bb) for bb in range(SCAN_PB)]

    @pl.when(d == 0)
    def _():
        for bb, (o_ret, o_gla) in enumerate(outs):
            of_scr[c, bb, :, 0:HG_W] = o_ret
            of_scr[c, bb, :, HG_W:2 * HG_W] = o_gla

    @pl.when(d == 1)
    def _():
        mean_mat = jnp.where(block_diag, 1.0 / HEAD_DIM, 0.0).astype(BF16)

        def head_rms(o):
            parts = jnp.concatenate(split3(o * o), axis=0)
            ms3 = jnp.dot(parts, mean_mat, preferred_element_type=F32)
            ms = ms3[0:CHUNK] + ms3[CHUNK:2 * CHUNK] + ms3[2 * CHUNK:3 * CHUNK]
            return o * lax.rsqrt(ms + EPS)

        for bb, (o_ret, o_gla) in enumerate(outs):
            o1 = o_ret + of_scr[cc, bb, :, 0:HG_W]
            g1 = pa_ref[bb, :, 3 * HG_W:4 * HG_W]
            y_ref[bb, :, 0:HG_W] = head_rms(o1) * (g1 * _sigmoid(g1))
            o2 = o_gla + of_scr[cc, bb, :, HG_W:2 * HG_W]
            g2 = pa_ref[bb, :, 7 * HG_W:8 * HG_W]
            y_ref[bb, :, HG_W:2 * HG_W] = head_rms(o2) * gn_ref[...] * (g2 * _sigmoid(g2))

    if want_final:
        @pl.when(c == n - 1)
        def _():
            stf_ref[...] = st_scr[...]


def _scan_call(pa, first_row, bsz, t, lg, lgam, gn, rope, st0, want_final):
    n = t // CHUNK
    use_rope = rope is not None
    has_init = st0 is not None
    pair_rows = SCAN_PB * t
    assert bsz % SCAN_PB == 0 and first_row % pair_rows == 0 and pa.shape[0] % pair_rows == 0
    pair_off = first_row // pair_rows
    pa4 = pa.reshape(pa.shape[0] // pair_rows, SCAN_PB, t, PA_W)

    def chunk_idx(d, c):
        return jnp.where(d == 0, c, n - 1 - c)

    in_specs = [
        pl.BlockSpec(memory_space=pltpu.SMEM),
        pl.BlockSpec((None, SCAN_PB, CHUNK, PA_W), lambda b, d, c: (pair_off + b, 0, chunk_idx(d, c), 0)),
    ]
    args = [lg, pa4]
    if use_rope:
        in_specs += [pl.BlockSpec((CHUNK, HG_W), lambda b, d, c: (chunk_idx(d, c), 0))] * 2
        args += list(rope)
    in_specs += [pl.BlockSpec((2, HG_W), lambda b, d, c: (0, 0)),
                 pl.BlockSpec((1, HG_W), lambda b, d, c: (0, 0))]
    args += [lgam, gn]
    state_spec = pl.BlockSpec((SCAN_PB, None, 2, HG_W, HG_W), lambda b, d, c: (b, d, 0, 0, 0))
    if has_init:
        in_specs.append(state_spec)
        args.append(st0)
    out_specs = [pl.BlockSpec((None, SCAN_PB, CHUNK, 2 * HG_W),
                              lambda b, d, c: (b, 0, jnp.where(d == 0, n - 1, n - 1 - c), 0))]
    out_shape = [jax.ShapeDtypeStruct((bsz // SCAN_PB, SCAN_PB, t, 2 * HG_W), F32)]
    if want_final:
        out_specs.append(state_spec)
        out_shape.append(jax.ShapeDtypeStruct((bsz, 2, 2, HG_W, HG_W), F32))
    res = pl.pallas_call(
        functools.partial(_scan_kernel, n=n, use_rope=use_rope, has_init=has_init, want_final=want_final),
        grid=(bsz // SCAN_PB, 2, n),
        in_specs=in_specs,
        out_specs=out_specs,
        out_shape=out_shape,
        scratch_shapes=[pltpu.VMEM((SCAN_PB, 2, HG_W, HG_W), F32), pltpu.VMEM((n, SCAN_PB, CHUNK, 2 * HG_W), F32)],
        compiler_params=pltpu.CompilerParams(vmem_limit_bytes=VMEM_LIMIT),
        name="ret_gla_scan",
    )(*args)
    y = res[0].reshape(bsz * t, 2 * HG_W)
    return (y, res[1]) if want_final else (y, None)


S5_GB = LANES // S5_CH
S5_SEQS = 6


def _s5_table_kernel(pwr_ref, pwi_ref, fz_ref, btr_ref, bti_ref, cr_ref, ci_ref, m_ref, sin_ref, sout_ref):
    def per_token(t):
        return jnp.concatenate([jnp.broadcast_to(t[j:j + 1, :], (S5_CH, S5_STATE)) for j in range(S5_L)], axis=0)

    def per_channel(t):
        return jnp.concatenate([t] * S5_L, axis=0)

    def powers(d, k):
        rows = slice(k * S5_L, (k + 1) * S5_L)
        return per_token(pwr_ref[d, rows, :]), per_token(pwi_ref[d, rows, :])

    cr, ci = per_channel(cr_ref[...]), per_channel(ci_ref[...])
    tok_in = lax.broadcasted_iota(jnp.int32, (S5_ROW, S5_ROW), 0) // S5_CH
    tok_out = lax.broadcasted_iota(jnp.int32, (S5_ROW, S5_ROW), 1) // S5_CH
    nt_dims = (((1,), (1,)), ((), ()))
    m = jnp.zeros((S5_ROW, S5_ROW), F32)
    for d in range(2):
        fr, fi = fz_ref[d, 0:1, :], fz_ref[d, 1:2, :]
        bbr = per_channel(fr * btr_ref[...] - fi * bti_ref[...])
        bbi = per_channel(fr * bti_ref[...] + fi * btr_ref[...])
        k_out, k_in, k_sin, k_sout = (0, 1, 2, 4) if d == 0 else (1, 0, 3, 5)
        er, ei = powers(d, k_out)
        cq = jnp.concatenate([cr * er - ci * ei, -(cr * ei + ci * er)], axis=-1)
        er, ei = powers(d, k_in)
        bk = jnp.concatenate([bbr * er - bbi * ei, bbr * ei + bbi * er], axis=-1)
        full = lax.dot_general(bk, cq, nt_dims, precision=HIGHEST, preferred_element_type=F32)
        m = m + jnp.where((tok_out >= tok_in) if d == 0 else (tok_in >= tok_out), full, 0.0)
        er, ei = powers(d, k_sin)
        sin_ref[d] = jnp.concatenate([bbr * er - bbi * ei, bbr * ei + bbi * er], axis=-1).astype(BF16)
        er, ei = powers(d, k_sout)
        sout_ref[d] = jnp.concatenate([cr * er - ci * ei, -(cr * ei + ci * er)], axis=-1).astype(BF16)
    m_ref[...] = m.astype(BF16)


def _s5_power_table(a_re, a_im, log_dt):
    j = np.arange(S5_L, dtype=np.float32)
    half = (S5_L - 1) / 2.0
    expo = jnp.asarray(np.concatenate([j - half, half - j, S5_L - 1 - j, j, j + 1, S5_L - j]))[:, None]
    dt = jnp.exp(log_dt)[..., None, None]
    ar, ai = a_re[..., None, :] * dt, a_im[..., None, :] * dt
    mag = jnp.exp(expo * ar)
    return mag * jnp.cos(expo * ai), mag * jnp.sin(expo * ai)


def _s5_tables(a_re, a_im, log_dt, b_re, b_im, c_re, c_im):
    depth = a_re.shape[0]
    pwr, pwi = _s5_power_table(a_re, a_im, log_dt)
    dt = jnp.exp(log_dt)[..., None]
    mag = jnp.exp(a_re * dt)
    lb_re, lb_im = mag * jnp.cos(a_im * dt), mag * jnp.sin(a_im * dt)
    den = a_re * a_re + a_im * a_im
    n_re = lb_re - 1.0
    fz = jnp.stack([(n_re * a_re + lb_im * a_im) / den, (lb_im * a_re - n_re * a_im) / den], axis=3)
    mag_l = jnp.exp(S5_L * (a_re * dt))
    pl_re, pl_im = mag_l * jnp.cos(S5_L * (a_im * dt)), mag_l * jnp.sin(S5_L * (a_im * dt))
    lam = jnp.stack([jnp.concatenate([pl_re[:, 0], pl_re[:, 0]], -1), jnp.concatenate([-pl_im[:, 0], pl_im[:, 0]], -1),
                     jnp.concatenate([pl_re[:, 1], pl_re[:, 1]], -1), jnp.concatenate([-pl_im[:, 1], pl_im[:, 1]], -1)],
                    axis=2)
    per_lg = lambda *shape: pl.BlockSpec((None, None) + shape, lambda l, g: (l, g) + (0,) * len(shape))
    per_dir = lambda *shape: pl.BlockSpec((None, 2, None) + shape, lambda l, g: (l, 0, g) + (0,) * len(shape))
    m, s_in, s_out = pl.pallas_call(
        _s5_table_kernel,
        grid=(depth, S5_GROUPS),
        in_specs=[per_dir(S5_SEQS * S5_L, S5_STATE), per_dir(S5_SEQS * S5_L, S5_STATE), per_dir(2, S5_STATE),
                  per_lg(S5_CH, S5_STATE), per_lg(S5_CH, S5_STATE), per_lg(S5_CH, S5_STATE), per_lg(S5_CH, S5_STATE)],
        out_specs=[per_lg(S5_ROW, S5_ROW), per_lg(2, S5_ROW, 2 * S5_STATE), per_lg(2, S5_ROW, 2 * S5_STATE)],
        out_shape=[jax.ShapeDtypeStruct((depth, S5_GROUPS, S5_ROW, S5_ROW), BF16),
                   jax.ShapeDtypeStruct((depth, S5_GROUPS, 2, S5_ROW, 2 * S5_STATE), BF16),
                   jax.ShapeDtypeStruct((depth, S5_GROUPS, 2, S5_ROW, 2 * S5_STATE), BF16)],
        compiler_params=pltpu.CompilerParams(vmem_limit_bytes=VMEM_LIMIT),
        name="s5_tables",
    )(pwr, pwi, fz, jnp.swapaxes(b_re, -1, -2), jnp.swapaxes(b_im, -1, -2), c_re, c_im)
    return m, s_in, s_out, lam


def _s5_kernel(*refs, n, bsz, has_init):
    refs = list(refs)
    su_ref = refs.pop(0)
    m_ref = refs.pop(0)
    sin_ref = refs.pop(0)
    sout_ref = refs.pop(0)
    lam_ref = refs.pop(0)
    dv_ref = refs.pop(0)
    if has_init:
        h0_ref = refs.pop(0)
    y_ref = refs.pop(0)
    fin_ref = refs.pop(0)
    sfs, sbs, hpf, hnb = refs

    r = n * bsz
    gran = lax.broadcasted_iota(jnp.int32, (1, LANES), 1) // S5_CH
    halves_per_row = S5_ROW // LANES
    per_half = S5_L // halves_per_row

    def tok_rows(j):
        return pl.ds(j, r, stride=S5_L)

    halves = [[None] * halves_per_row for _ in range(S5_GB)]
    for j in range(S5_L):
        uj = su_ref[tok_rows(j), :]
        for gl in range(S5_GB):
            shift = ((j % per_half) - gl) % S5_GB * S5_CH
            piece = uj if shift == 0 else pltpu.roll(uj, shift, 1)
            cur = halves[gl][j // per_half]
            halves[gl][j // per_half] = jnp.where(gran == j % per_half, piece, 0.0 if cur is None else cur)
    u = jnp.stack([jnp.concatenate(h, axis=-1) for h in halves]).astype(BF16)

    bm_dims = (((2,), (1,)), ((0,), (0,)))
    bm_nt_dims = (((2,), (2,)), ((0,), (0,)))
    y = lax.dot_general(u, m_ref[...], bm_dims, preferred_element_type=F32)
    sfs[...] = lax.dot_general(u, sin_ref[:, 0], bm_dims, preferred_element_type=F32)
    sbs[...] = lax.dot_general(u, sin_ref[:, 1], bm_dims, preferred_element_type=F32)

    a_f, s_f = lam_ref[:, 0:1, :], lam_ref[:, 1:2, :]
    a_b, s_b = lam_ref[:, 2:3, :], lam_ref[:, 3:4, :]
    if has_init:
        hf = h0_ref[:, 0]
        hb = h0_ref[:, 1]
    else:
        hf = jnp.zeros((S5_GB, bsz, 2 * S5_STATE), F32)
        hb = jnp.zeros((S5_GB, bsz, 2 * S5_STATE), F32)
    swap = lambda a: pltpu.roll(a, S5_STATE, 2)
    hf_sw, hb_sw = swap(hf), swap(hb)
    for c in range(n):
        rows = pl.ds(c, bsz, stride=n)
        hpf[:, rows, :] = hf
        s_in = sfs[:, rows, :]
        hf, hf_sw = hf * a_f + hf_sw * s_f + s_in, hf_sw * a_f - hf * s_f + swap(s_in)
        rows = pl.ds(n - 1 - c, bsz, stride=n)
        hnb[:, rows, :] = hb
        s_in = sbs[:, rows, :]
        hb, hb_sw = hb * a_b + hb_sw * s_b + s_in, hb_sw * a_b - hb * s_b + swap(s_in)
    fin_ref[:, 0] = hf
    fin_ref[:, 1] = hb
    y = y + lax.dot_general(hpf[...].astype(BF16), sout_ref[:, 0], bm_nt_dims, preferred_element_type=F32)
    y = y + lax.dot_general(hnb[...].astype(BF16), sout_ref[:, 1], bm_nt_dims, preferred_element_type=F32)

    for j in range(S5_L):
        acc = None
        for gl in range(S5_GB):
            src = y[gl][:, (j // per_half) * LANES:(j // per_half + 1) * LANES]
            shift = (gl - (j % per_half)) % S5_GB * S5_CH
            piece = src if shift == 0 else pltpu.roll(src, shift, 1)
            acc = jnp.where(gran == gl, piece, 0.0 if acc is None else acc)
        y_ref[tok_rows(j), :] = acc + su_ref[tok_rows(j), :] * dv_ref[...]


def _s5_call(su, row_block, tabs, layer, dvec, bsz, t, h0):
    m, s_in, s_out, lam = tabs
    n = t // S5_L
    r = n * bsz
    rows = bsz * t
    has_init = h0 is not None
    gspec = lambda *shape: pl.BlockSpec((S5_GB,) + shape, lambda g: (g,) + (0,) * len(shape))
    lspec = lambda *shape: pl.BlockSpec((None, S5_GB) + shape, lambda g: (layer, g) + (0,) * len(shape))
    in_specs = [pl.BlockSpec((rows, LANES), lambda g: (row_block, g)),
                lspec(S5_ROW, S5_ROW), lspec(2, S5_ROW, 2 * S5_STATE), lspec(2, S5_ROW, 2 * S5_STATE),
                lspec(4, 2 * S5_STATE), pl.BlockSpec((1, LANES), lambda g: (0, g))]
    args = [su, m, s_in, s_out, lam, dvec]
    if has_init:
        in_specs.append(gspec(2, bsz, 2 * S5_STATE))
        args.append(h0)
    return pl.pallas_call(
        functools.partial(_s5_kernel, n=n, bsz=bsz, has_init=has_init),
        grid=(S5_GROUPS // S5_GB,),
        in_specs=in_specs,
        out_specs=[pl.BlockSpec((rows, LANES), lambda g: (0, g)), gspec(2, bsz, 2 * S5_STATE)],
        out_shape=[jax.ShapeDtypeStruct((rows, S5_W), F32),
                   jax.ShapeDtypeStruct((S5_GROUPS, 2, bsz, 2 * S5_STATE), F32)],
        scratch_shapes=[pltpu.VMEM((S5_GB, r, 2 * S5_STATE), F32)] * 4,
        compiler_params=pltpu.CompilerParams(vmem_limit_bytes=VMEM_LIMIT),
        name="s5_scan",
    )(*args)


def _outproj_kernel(yrg_c_ref, yrg_l_ref, ys_c_ref, ys_l_ref, x_ref, mod_ref, n2_ref, wo_ref, gw_ref, gb_ref,
                    rw_ref, rb_ref, x1_ref, xs_ref, pos_ref, wt_ref, meta_ref, *, n_ctx_tiles):
    is_ctx = pl.program_id(0) < n_ctx_tiles
    ys = jnp.where(is_ctx, ys_c_ref[...], ys_l_ref[...])
    yrg = jnp.where(is_ctx, yrg_c_ref[...], yrg_l_ref[...])
    s = 0.5 * ys * (1.0 + jnp.tanh(np.sqrt(2.0 / np.pi).astype(np.float32) * (ys + 0.044715 * (ys * ys * ys))))
    s = s * _sigmoid(jnp.dot(s.astype(BF16), gw_ref[...], preferred_element_type=F32) + gb_ref[...])
    m = (jnp.dot(yrg.astype(BF16), wo_ref[0:2 * HG_W, :], preferred_element_type=F32)
         + jnp.dot(s.astype(BF16), wo_ref[2 * HG_W:, :], preferred_element_type=F32))
    x1 = x_ref[...] + mod_ref[2:3, :] * m
    x1_ref[...] = x1
    h = x1 * lax.rsqrt(jnp.mean(x1 * x1, axis=-1, keepdims=True) + EPS) * n2_ref[...]
    h = h * (1.0 + mod_ref[4:5, :]) + mod_ref[3:4, :]

    h_hi = h.astype(BF16)
    h_lo = (h - h_hi.astype(F32)).astype(BF16)
    pp = jnp.dot(jnp.concatenate([h_hi, h_lo], axis=0), rw_ref[...], preferred_element_type=F32)
    logits = ((pp[:TM, :LANES] + pp[TM:, :LANES]) + (pp[:TM, LANES:] + pp[TM:, LANES:])) + rb_ref[...]
    lane = lax.broadcasted_iota(jnp.int32, (TM, LANES), 1).astype(F32)
    cur = logits
    vals, idxs = [], []
    for _ in range(TOP_K):
        mx = jnp.max(cur, axis=-1, keepdims=True)
        am = jnp.min(jnp.where(cur == mx, lane, float(LANES)), axis=-1, keepdims=True)
        vals.append(mx)
        idxs.append(am)
        cur = jnp.where(lane == am, -jnp.inf, cur)
    es = [jnp.exp(v - vals[0]) for v in vals]
    den = es[0] + es[1] + es[2] + es[3]

    hit = [(lane == am) for am in idxs]
    assign = (hit[0] | hit[1] | hit[2] | hit[3]).astype(BF16)
    ti = lax.broadcasted_iota(jnp.int32, (TM, TM), 0)
    tj = lax.broadcasted_iota(jnp.int32, (TM, TM), 1)
    rank = jnp.dot((tj < ti).astype(BF16), assign, preferred_element_type=F32)
    cnt = jnp.sum(assign.astype(F32), axis=0, keepdims=True)
    cnt_al = jnp.floor((cnt + (SEG_ALIGN - 1)) * (1.0 / SEG_ALIGN)) * SEG_ALIGN
    ei = lax.broadcasted_iota(jnp.int32, (LANES, LANES), 0)
    ej = lax.broadcasted_iota(jnp.int32, (LANES, LANES), 1)
    seg = jnp.dot(jnp.broadcast_to(cnt_al, (SUBLANES, LANES)).astype(BF16), (ei < ej).astype(BF16),
                  preferred_element_type=F32)[0:1, :]
    base = seg + rank
    rlane = lax.broadcasted_iota(jnp.int32, (TM, ROWS_T), 1).astype(F32)
    klane = lax.broadcasted_iota(jnp.int32, (TM, LANES), 1)
    onehot = jnp.zeros((TM, ROWS_T), F32)
    pos_out = jnp.zeros((TM, LANES), F32)
    wt_out = jnp.zeros((TM, LANES), F32)
    for kk in range(TOP_K):
        pk = jnp.sum(jnp.where(hit[kk], base, 0.0), axis=-1, keepdims=True)
        onehot = onehot + (rlane == pk).astype(F32)
        pos_out = jnp.where(klane == kk, pk, pos_out)
        wt_out = jnp.where(klane == kk, es[kk] / den, wt_out)
    xs_ref[0] = lax.dot_general(onehot.astype(BF16), h.astype(BF16), (((0,), (0,)), ((), ())),
                                preferred_element_type=F32)
    pos_ref[...] = pos_out.astype(jnp.int32)
    wt_ref[...] = wt_out
    mrow = lax.broadcasted_iota(jnp.int32, (SUBLANES, LANES), 0)
    meta = jnp.where(mrow == 0, jnp.broadcast_to(cnt_al, (SUBLANES, LANES)),
                     jnp.where(mrow == 1, jnp.broadcast_to(seg, (SUBLANES, LANES)), 0.0))
    meta_ref[0] = meta.astype(jnp.int32)


def _outproj_call(yrg_c, yrg_l, ys_c, ys_l, x, mod_l, n2, wo, gw, gb, rw, rb, n_ctx_tiles, tiles_per_lat):
    n = x.shape[0]
    nt = n // TM
    gid = functools.partial(_group_id, n_ctx_tiles=n_ctx_tiles, tiles_per_lat=tiles_per_lat)
    const = lambda *shape: pl.BlockSpec(shape, lambda i: (0,) * len(shape))
    ctx_map = lambda i: (jnp.minimum(i, n_ctx_tiles - 1), 0)
    lat_map = lambda i: (jnp.maximum(i - n_ctx_tiles, 0), 0)
    return pl.pallas_call(
        functools.partial(_outproj_kernel, n_ctx_tiles=n_ctx_tiles),
        grid=(nt,),
        in_specs=[
            pl.BlockSpec((TM, 2 * HG_W), ctx_map),
            pl.BlockSpec((TM, 2 * HG_W), lat_map),
            pl.BlockSpec((TM, S5_W), ctx_map),
            pl.BlockSpec((TM, S5_W), lat_map),
            pl.BlockSpec((TM, D_MODEL), lambda i: (i, 0)),
            pl.BlockSpec((None, 6, D_MODEL), lambda i: (gid(i), 0, 0)),
            const(1, D_MODEL), const(D_MODEL, D_MODEL), const(S5_W, S5_W), const(1, S5_W),
            const(D_MODEL, 2 * LANES), const(1, LANES),
        ],
        out_specs=[
            pl.BlockSpec((TM, D_MODEL), lambda i: (i, 0)),
            pl.BlockSpec((1, ROWS_T, D_MODEL), lambda i: (i, 0, 0)),
            pl.BlockSpec((TM, LANES), lambda i: (i, 0)),
            pl.BlockSpec((TM, LANES), lambda i: (i, 0)),
            pl.BlockSpec((1, SUBLANES, LANES), lambda i: (i, 0, 0)),
        ],
        out_shape=[
            jax.ShapeDtypeStruct((n, D_MODEL), F32),
            jax.ShapeDtypeStruct((nt, ROWS_T, D_MODEL), F32),
            jax.ShapeDtypeStruct((n, LANES), jnp.int32),
            jax.ShapeDtypeStruct((n, LANES), F32),
            jax.ShapeDtypeStruct((nt, SUBLANES, LANES), jnp.int32),
        ],
        compiler_params=pltpu.CompilerParams(vmem_limit_bytes=VMEM_LIMIT),
        name="outproj_router_dispatch",
    )(yrg_c, yrg_l, ys_c, ys_l, x, mod_l, n2, wo, gw, gb, rw, rb)


def _expert_schedule(meta, nblk):
    nt = meta.shape[0]
    nch = (meta[:, 0, :N_EXPERTS] // SEG_ALIGN).T
    seg_chunk = (meta[:, 1, :N_EXPERTS] // SEG_ALIGN).T
    cs_end = jnp.cumsum(nch, axis=1)
    cs_start = cs_end - nch
    tot = cs_end[:, -1]
    nb = (tot + BLK_CHUNKS - 1) // BLK_CHUNKS
    blk_end = jnp.cumsum(nb)
    blk_start = blk_end - nb
    n_active = blk_end[-1]
    j = jnp.arange(nblk, dtype=jnp.int32)
    be = jnp.sum(blk_end[None, :] <= jnp.minimum(j, n_active - 1)[:, None], axis=1)
    be = jnp.clip(be, 0, N_EXPERTS - 1).astype(jnp.int32)
    own = be[:, None] == jnp.arange(N_EXPERTS, dtype=jnp.int32)[None, :]
    pick = lambda a: jnp.sum(jnp.where(own[:, :, None], a[None], 0), axis=1)
    bs = (j - jnp.sum(jnp.where(own, blk_start[None, :], 0), axis=1)) * BLK_CHUNKS
    tot_b = jnp.sum(jnp.where(own, tot[None, :], 0), axis=1)
    nvalid = jnp.where(j < n_active, jnp.clip(tot_b - bs, 0, BLK_CHUNKS), 0)
    kfirst = jnp.sum(pick(cs_end) <= bs[:, None], axis=1)
    klast = jnp.sum(pick(cs_start) < (bs + nvalid)[:, None], axis=1)
    src_base = jnp.arange(nt, dtype=jnp.int32)[None, :] * CHUNKS_T + seg_chunk
    i32 = lambda a: a.astype(jnp.int32)
    return (be, i32(nvalid), i32(bs), i32(kfirst), i32(klast),
            i32(cs_start.reshape(-1)), i32(nch.reshape(-1)), i32(src_base.reshape(-1)))


W_CAST_ROWS = 128


def _ffn_kernel(be_ref, nv_ref, bs_ref, kf_ref, kl_ref, cst_ref, nch_ref, srcb_ref,
                xs_hbm, w1_ref, b1_ref, w2_ref, b2_ref, ys_hbm,
                xbuf, ybuf, w1b, w2b, gsem, ssem, *, nblk, nt):
    j = pl.program_id(0)
    slot = j % 2

    def for_pieces(blk, fn):
        e_off = be_ref[blk] * nt
        lo_b = bs_ref[blk]
        hi_b = lo_b + nv_ref[blk]

        def body(i, carry):
            ps = cst_ref[e_off + i]
            lo = jnp.maximum(ps, lo_b)
            n = jnp.minimum(ps + nch_ref[e_off + i], hi_b) - lo

            @pl.when(n > 0)
            def _():
                fn(srcb_ref[e_off + i] + (lo - ps), lo - lo_b, n)
            return carry

        lax.fori_loop(kf_ref[blk], kl_ref[blk], body, 0)

    def start_gather(blk, s):
        for_pieces(blk, lambda src, dst, n: pltpu.make_async_copy(
            xs_hbm.at[pl.ds(src, n)], xbuf.at[s, pl.ds(dst, n)], gsem.at[s]).start())

    def start_scatter(blk, s):
        for_pieces(blk, lambda src, dst, n: pltpu.make_async_copy(
            ybuf.at[s, pl.ds(dst, n)], ys_hbm.at[pl.ds(src, n)], ssem.at[s]).start())

    def wait_gather(blk, s):
        @pl.when(nv_ref[blk] > 0)
        def _():
            pltpu.make_async_copy(xs_hbm.at[pl.ds(0, nv_ref[blk])], xbuf.at[s, pl.ds(0, nv_ref[blk])],
                                  gsem.at[s]).wait()

    def wait_scatter(blk, s):
        @pl.when(nv_ref[blk] > 0)
        def _():
            pltpu.make_async_copy(ybuf.at[s, pl.ds(0, nv_ref[blk])], ys_hbm.at[pl.ds(0, nv_ref[blk])],
                                  ssem.at[s]).wait()

    @pl.when(j == 0)
    def _():
        xbuf[...] = jnp.zeros_like(xbuf)
        start_gather(0, 0)

    @pl.when(j + 1 < nblk)
    def _():
        start_gather(j + 1, 1 - slot)

    @pl.when((j == 0) | (be_ref[j] != be_ref[jnp.maximum(j - 1, 0)]))
    def _():
        def cast_rows(r, carry):
            rows = pl.ds(pl.multiple_of(r * W_CAST_ROWS, W_CAST_ROWS), W_CAST_ROWS)
            w1b[rows, :] = w1_ref[rows, :].astype(BF16)
            w2b[rows, :] = w2_ref[rows, :].astype(BF16)
            return carry
        lax.fori_loop(0, D_MODEL // W_CAST_ROWS, cast_rows, 0)

    wait_gather(j, slot)

    @pl.when(j >= 2)
    def _():
        wait_scatter(j - 2, slot)

    @pl.when(nv_ref[j] > 0)
    def _():
        x = xbuf[slot].reshape(BLK_ROWS, D_MODEL).astype(BF16)
        gu = jnp.dot(x, w1b[...], preferred_element_type=F32) + b1_ref[...]
        gate = jnp.minimum(gu[:, :D_FF], SWIGLU_LIMIT)
        up = jnp.clip(gu[:, D_FF:], -SWIGLU_LIMIT, SWIGLU_LIMIT)
        act = (up + 1.0) * gate * _sigmoid(SWIGLU_ALPHA * gate)
        y = jnp.dot(act.astype(BF16), w2b[...], preferred_element_type=F32) + b2_ref[...]
        ybuf[slot] = y.reshape(BLK_CHUNKS, SEG_ALIGN, D_MODEL)

    start_scatter(j, slot)

    @pl.when(j == nblk - 1)
    def _():
        wait_scatter(j, slot)
        if nblk >= 2:
            wait_scatter(j - 1, 1 - slot)


def _ffn_call(sched, xs_chunks, w1, b1, w2, b2, layer, nt):
    assert D_FF == D_MODEL
    nblk = sched[0].shape[0]
    nch = xs_chunks.shape[0]
    wmap = lambda j, be, *_: (layer, be[j], 0, 0)
    grid_spec = pltpu.PrefetchScalarGridSpec(
        num_scalar_prefetch=len(sched),
        grid=(nblk,),
        in_specs=[
            pl.BlockSpec(memory_space=pl.ANY),
            pl.BlockSpec((None, None, D_MODEL, 2 * D_FF), wmap),
            pl.BlockSpec((None, None, 1, 2 * D_FF), wmap),
            pl.BlockSpec((None, None, D_FF, D_MODEL), wmap),
            pl.BlockSpec((None, None, 1, D_MODEL), wmap),
        ],
        out_specs=pl.BlockSpec(memory_space=pl.ANY),
        scratch_shapes=[
            pltpu.VMEM((2, BLK_CHUNKS, SEG_ALIGN, D_MODEL), F32),
            pltpu.VMEM((2, BLK_CHUNKS, SEG_ALIGN, D_MODEL), F32),
            pltpu.VMEM((D_MODEL, 2 * D_FF), BF16),
            pltpu.VMEM((D_FF, D_MODEL), BF16),
            pltpu.SemaphoreType.DMA((2,)),
            pltpu.SemaphoreType.DMA((2,)),
        ],
    )
    return pl.pallas_call(
        functools.partial(_ffn_kernel, nblk=nblk, nt=nt),
        grid_spec=grid_spec,
        out_shape=jax.ShapeDtypeStruct((nch, SEG_ALIGN, D_MODEL), F32),
        input_output_aliases={len(sched): 0},
        compiler_params=pltpu.CompilerParams(vmem_limit_bytes=VMEM_LIMIT, dimension_semantics=("arbitrary",)),
        name="expert_ffn",
    )(*sched, xs_chunks, w1, b1, w2, b2)


def _combine_kernel(ys_ref, pos_ref, wt_ref, x1_ref, mod_ref, fg_ref, *out_refs, final, n_ctx_tiles):
    ys = ys_ref[0].astype(BF16)
    rlane = lax.broadcasted_iota(jnp.int32, (TM, ROWS_T), 1)
    w = jnp.zeros((TM, ROWS_T), F32)
    for kk in range(TOP_K):
        w = w + jnp.where(rlane == pos_ref[:, kk:kk + 1], wt_ref[:, kk:kk + 1], 0.0)
    moe = jnp.dot(w.astype(BF16), ys, preferred_element_type=F32)
    x2 = x1_ref[...] + mod_ref[5:6, :] * moe
    if not final:
        out_refs[0][...] = x2
    else:
        y = x2 * lax.rsqrt(jnp.mean(x2 * x2, axis=-1, keepdims=True) + EPS) * fg_ref[...]
        is_ctx = pl.program_id(0) < n_ctx_tiles

        @pl.when(is_ctx)
        def _():
            out_refs[0][...] = y

        @pl.when(jnp.logical_not(is_ctx))
        def _():
            out_refs[1][...] = y


def _combine_call(ys, pos, wts, x1, mod_l, fg, n_ctx_tiles, tiles_per_lat, final):
    n = x1.shape[0]
    nt = n // TM
    gid = functools.partial(_group_id, n_ctx_tiles=n_ctx_tiles, tiles_per_lat=tiles_per_lat)
    tok = pl.BlockSpec((TM, D_MODEL), lambda i: (i, 0))
    if final:
        out_specs = [pl.BlockSpec((TM, D_MODEL), lambda i: (jnp.minimum(i, n_ctx_tiles - 1), 0)),
                     pl.BlockSpec((TM, D_MODEL), lambda i: (jnp.maximum(i - n_ctx_tiles, 0), 0))]
        out_shape = [jax.ShapeDtypeStruct((n_ctx_tiles * TM, D_MODEL), F32),
                     jax.ShapeDtypeStruct((n - n_ctx_tiles * TM, D_MODEL), F32)]
    else:
        out_specs, out_shape = [tok], [jax.ShapeDtypeStruct((n, D_MODEL), F32)]
    res = pl.pallas_call(
        functools.partial(_combine_kernel, final=final, n_ctx_tiles=n_ctx_tiles),
        grid=(nt,),
        in_specs=[
            pl.BlockSpec((1, ROWS_T, D_MODEL), lambda i: (i, 0, 0)),
            pl.BlockSpec((TM, LANES), lambda i: (i, 0)),
            pl.BlockSpec((TM, LANES), lambda i: (i, 0)),
            tok,
            pl.BlockSpec((None, 6, D_MODEL), lambda i: (gid(i), 0, 0)),
            pl.BlockSpec((1, D_MODEL), lambda i: (0, 0)),
        ],
        out_specs=out_specs,
        out_shape=out_shape,
        compiler_params=pltpu.CompilerParams(vmem_limit_bytes=VMEM_LIMIT),
        name="moe_combine",
    )(ys, pos, wts, x1, mod_l, fg)
    return res


def _rope_tables(t):
    pos = np.arange(t)
    nf = HEAD_DIM // 4
    inv = jnp.asarray(ROPE_BASE, F32) ** (-jnp.arange(nf, dtype=F32) / nf)
    ang = jnp.concatenate([jnp.asarray(pos // GRID_W, F32)[:, None] * inv,
                           jnp.asarray(pos % GRID_W, F32)[:, None] * inv], axis=-1)
    cos, sin = jnp.cos(ang), jnp.sin(ang)
    return (jnp.tile(jnp.concatenate([cos, cos], -1), (1, N_HEADS)),
            jnp.tile(jnp.concatenate([-sin, sin], -1), (1, N_HEADS)))


def _block_diag_t(s):
    eye = jnp.eye(N_HEADS, dtype=s.dtype)
    out = jnp.einsum('...hkv,hg->...hvgk', s, eye)
    return out.reshape(s.shape[:-3] + (HG_W, HG_W))


def _diag_blocks(st):
    blocks = [st[..., h * HEAD_DIM:(h + 1) * HEAD_DIM, h * HEAD_DIM:(h + 1) * HEAD_DIM] for h in range(N_HEADS)]
    return jnp.swapaxes(jnp.stack(blocks, axis=-3), -1, -2)


def kernel(x_prompt, x_sample, state_ret, state_gla, state_s5_re, state_s5_im, c, c_ctx, norm1_g, norm2_g, final_g, ada_w, ada_b, w_in, w_out, ret_decay, gla_w_lr, gla_b_lr, gla_norm_g, s5_a_re, s5_a_im, s5_log_dt, s5_b_re, s5_b_im, s5_c_re, s5_c_im, s5_d, s5_glu_w, s5_glu_b, router_w, router_b, moe_w1, moe_b1, moe_w2, moe_b2):
    bc, tc, d = x_prompt.shape
    bl, tl, _ = x_sample.shape
    depth = w_in.shape[0]
    nc, nl = bc * tc, bl * tl
    n = nc + nl
    assert d == D_MODEL and nc % TM == 0 and tl % TM == 0 and tc % CHUNK == 0 and tl % CHUNK == 0
    assert 1 + bl <= SUBLANES and nc % nl == 0
    n_ctx_tiles, tiles_per_lat = nc // TM, tl // TM
    nt = n // TM
    nblk = nt * CHUNKS_T // BLK_CHUNKS + N_EXPERTS

    x = jnp.concatenate([x_prompt.reshape(nc, d), x_sample.reshape(nl, d)], axis=0)
    cond8 = jnp.concatenate([c_ctx[None, :], c, jnp.zeros((SUBLANES - 1 - bl, d), F32)], axis=0)
    mod = _ada_call(cond8, ada_w, ada_b).reshape(depth, SUBLANES, 6, d)
    rope = _rope_tables(tl)
    fg = final_g.reshape(1, d)
    s5_tabs = _s5_tables(s5_a_re, s5_a_im, s5_log_dt, s5_b_re, s5_b_im, s5_c_re, s5_c_im)

    ret_states, gla_states, re_states, im_states = [], [], [], []
    for l in range(depth):
        wl = w_in[l]
        w_cat = jnp.concatenate(
            [wl[:, :8 * HG_W], wl[:, 8 * HG_W + GLA_RANK:],
             jnp.pad(wl[:, 8 * HG_W:8 * HG_W + GLA_RANK], ((0, 0), (0, LANES - GLA_RANK)))], axis=1).astype(BF16)
        wlr = jnp.pad(gla_w_lr[l], ((0, 0), (0, LANES - GLA_RANK), (0, 0)))
        blr = gla_b_lr[l].reshape(2, 1, HG_W)
        pa, su = _inproj_call(x, mod[l], norm1_g[l].reshape(1, d), w_cat, wlr, blr, n_ctx_tiles, tiles_per_lat)

        log_gamma = jnp.log1p(-jnp.exp(ret_decay[l]))
        lgam = jnp.repeat(log_gamma, HEAD_DIM, axis=1)
        gn = jnp.tile(gla_norm_g[l], N_HEADS).reshape(1, HG_W)
        y_c, st_c = _scan_call(pa, 0, bc, tc, log_gamma, lgam, gn, None, None, True)
        st0 = jnp.stack([_block_diag_t(state_ret[:, l]), _block_diag_t(state_gla[:, l])], axis=2)
        y_l, _ = _scan_call(pa, nc, bl, tl, log_gamma, lgam, gn, rope, st0, False)
        ret_states.append(_diag_blocks(st_c[:, :, 0]))
        gla_states.append(_diag_blocks(st_c[:, :, 1]))

        dvec = s5_d[l].reshape(1, S5_W)
        ys_c, fin_c = _s5_call(su, 0, s5_tabs, l, dvec, bc, tc, None)
        h0 = jnp.concatenate([state_s5_re[:, l], state_s5_im[:, l]], axis=-1).transpose(2, 1, 0, 3)
        ys_l, _ = _s5_call(su, nc // nl, s5_tabs, l, dvec, bl, tl, h0)
        re_states.append(fin_c[..., :S5_STATE].transpose(2, 1, 0, 3))
        im_states.append(fin_c[..., S5_STATE:].transpose(2, 1, 0, 3))

        rw = jnp.pad(router_w[l], ((0, 0), (0, LANES - N_EXPERTS)))
        rw_hi = rw.astype(BF16)
        rw = jnp.concatenate([rw_hi, (rw - rw_hi.astype(F32)).astype(BF16)], axis=1)
        rb = jnp.concatenate([router_b[l], jnp.full((LANES - N_EXPERTS,), NEG_BIG, F32)]).reshape(1, LANES)
        x1, xs, pos, wts, meta = _outproj_call(
            y_c, y_l, ys_c, ys_l, x, mod[l], norm2_g[l].reshape(1, d), w_out[l].astype(BF16), s5_glu_w[l].astype(BF16),
            s5_glu_b[l].reshape(1, S5_W), rw, rb, n_ctx_tiles, tiles_per_lat)

        sched = _expert_schedule(meta, nblk)
        ys = _ffn_call(sched, xs.reshape(nt * CHUNKS_T, SEG_ALIGN, d),
                       moe_w1, moe_b1.reshape(depth, N_EXPERTS, 1, 2 * D_FF),
                       moe_w2, moe_b2.reshape(depth, N_EXPERTS, 1, d), l, nt)
        res = _combine_call(ys.reshape(nt, ROWS_T, d), pos, wts, x1, mod[l], fg,
                            n_ctx_tiles, tiles_per_lat, l == depth - 1)
        if l < depth - 1:
            x = res[0]
    y_prompt = res[0].reshape(bc, tc, d)
    y_sample = res[1].reshape(bl, tl, d)
    return (y_prompt, y_sample, jnp.stack(ret_states, axis=1), jnp.stack(gla_states, axis=1),
            jnp.stack(re_states, axis=1), jnp.stack(im_states, axis=1))
```

```python
import functools

import numpy as np
import jax
import jax.numpy as jnp
from jax import lax
from jax.experimental import pallas as pl
from jax.experimental.pallas import tpu as pltpu

F32 = jnp.float32
BF16 = jnp.bfloat16
HIGHEST = lax.Precision.HIGHEST

D_MODEL = 1024
GRID_W = 64
CHUNK = 128
HEAD_DIM = 64
N_HEADS = 4
HG_W = N_HEADS * HEAD_DIM
S5_W = 512
S5_CH = 16
S5_GROUPS = 32
S5_STATE = 64
S5_L = 16
S5_ROW = S5_L * S5_CH
GLA_RANK = 16
GLA_TAU = 16.0
N_EXPERTS = 32
TOP_K = 4
D_FF = 1024
SWIGLU_LIMIT = 7.0
SWIGLU_ALPHA = 1.702
ROPE_BASE = 10000.0
EPS = 1e-6

LANES = 128
SUBLANES = 8
TM = 256
SEG_ALIGN = SUBLANES
ROWS_T = 1280
CHUNKS_T = ROWS_T // SEG_ALIGN
BLK_CHUNKS = 128
SUB_CHUNKS = 16
BLK_ROWS = BLK_CHUNKS * SEG_ALIGN
NEG_BIG = -1e30
VMEM_LIMIT = 56 * 1024 * 1024

PA_W = 8 * HG_W + 2 * HG_W
W_CAT = 8 * HG_W + S5_W + LANES


def _sigmoid(x):
    return 1.0 / (1.0 + jnp.exp(-x))


def _group_id(i, n_ctx_tiles, tiles_per_lat):
    return jnp.where(i < n_ctx_tiles, 0, 1 + (i - n_ctx_tiles) // tiles_per_lat)


def _ada_kernel(c_ref, w_ref, b_ref, o_ref):
    c = c_ref[...]
    s = c * _sigmoid(c)
    o_ref[0] = jnp.dot(s, w_ref[0], precision=HIGHEST, preferred_element_type=F32) + b_ref[0]


def _ada_call(cond8, ada_w, ada_b):
    depth, d, n6 = ada_w.shape
    tn = 1024
    return pl.pallas_call(
        _ada_kernel,
        grid=(depth, n6 // tn),
        in_specs=[
            pl.BlockSpec((SUBLANES, d), lambda l, j: (0, 0)),
            pl.BlockSpec((1, d, tn), lambda l, j: (l, 0, j)),
            pl.BlockSpec((1, 1, tn), lambda l, j: (l, 0, j)),
        ],
        out_specs=pl.BlockSpec((1, SUBLANES, tn), lambda l, j: (l, 0, j)),
        out_shape=jax.ShapeDtypeStruct((depth, SUBLANES, n6), F32),
        compiler_params=pltpu.CompilerParams(vmem_limit_bytes=VMEM_LIMIT),
        name="ada_mod",
    )(cond8, ada_w, ada_b.reshape(depth, 1, n6))


def _inproj_kernel(xc_ref, xl_ref, mod_ref, n1_ref, w_ref, wlr_ref, blr_ref, pa_ref, su_ref, *, n_ctx_tiles):
    x = jnp.where(pl.program_id(0) < n_ctx_tiles, xc_ref[...], xl_ref[...])
    h = x * lax.rsqrt(jnp.mean(x * x, axis=-1, keepdims=True) + EPS) * n1_ref[...]
    h = h * (1.0 + mod_ref[1:2, :]) + mod_ref[0:1, :]
    r = jnp.dot(h.astype(BF16), w_ref[...], preferred_element_type=F32)
    pa_ref[:, : 8 * HG_W] = r[:, : 8 * HG_W]
    su_ref[...] = r[:, 8 * HG_W: 8 * HG_W + S5_W]
    glr = r[:, 8 * HG_W + S5_W:]
    for d in range(2):
        z = jnp.dot(glr, wlr_ref[d], precision=HIGHEST, preferred_element_type=F32) + blr_ref[d]
        log_sig = jnp.minimum(z, 0.0) - jnp.log(1.0 + jnp.exp(-jnp.abs(z)))
        pa_ref[:, (8 + d) * HG_W: (9 + d) * HG_W] = log_sig * (1.0 / GLA_TAU)


def _path_maps(n_ctx_tiles):
    return (lambda i: (jnp.minimum(i, n_ctx_tiles - 1), 0)), (lambda i: (jnp.maximum(i - n_ctx_tiles, 0), 0))


def _inproj_call(x_c, x_l, mod_l, n1, w_cat, wlr, blr, n_ctx_tiles, tiles_per_lat):
    n = x_c.shape[0] + x_l.shape[0]
    gid = functools.partial(_group_id, n_ctx_tiles=n_ctx_tiles, tiles_per_lat=tiles_per_lat)
    ctx_map, lat_map = _path_maps(n_ctx_tiles)
    return pl.pallas_call(
        functools.partial(_inproj_kernel, n_ctx_tiles=n_ctx_tiles),
        grid=(n // TM,),
        in_specs=[
            pl.BlockSpec((TM, D_MODEL), ctx_map),
            pl.BlockSpec((TM, D_MODEL), lat_map),
            pl.BlockSpec((None, 6, D_MODEL), lambda i: (gid(i), 0, 0)),
            pl.BlockSpec((1, D_MODEL), lambda i: (0, 0)),
            pl.BlockSpec((D_MODEL, W_CAT), lambda i: (0, 0)),
            pl.BlockSpec((2, LANES, HG_W), lambda i: (0, 0, 0)),
            pl.BlockSpec((2, 1, HG_W), lambda i: (0, 0, 0)),
        ],
        out_specs=[
            pl.BlockSpec((TM, PA_W), lambda i: (i, 0)),
            pl.BlockSpec((TM, S5_W), lambda i: (i, 0)),
        ],
        out_shape=[
            jax.ShapeDtypeStruct((n, PA_W), F32),
            jax.ShapeDtypeStruct((n, S5_W), F32),
        ],
        compiler_params=pltpu.CompilerParams(vmem_limit_bytes=VMEM_LIMIT),
        name="norm1_inproj",
    )(x_c, x_l, mod_l, n1, w_cat, wlr, blr)


SCAN_PB = 2


def _scan_kernel(*refs, n, use_rope, has_init, want_final):
    refs = list(refs)
    lg_ref = refs.pop(0)
    pa_ref = refs.pop(0)
    if use_rope:
        cos_ref = refs.pop(0)
        sin_ref = refs.pop(0)
    lgam_ref = refs.pop(0)
    gn_ref = refs.pop(0)
    if has_init:
        st0_ref = refs.pop(0)
    y_ref = refs.pop(0)
    if want_final:
        fin_refs = [refs.pop(0), refs.pop(0)]
    st_scr = refs.pop(0)
    of_scr = refs.pop(0)

    d = pl.program_id(1)
    c = pl.program_id(2)
    cc = jnp.where(d == 0, c, n - 1 - c)

    @pl.when(c == 0)
    def _():
        if has_init:
            st_scr[...] = st0_ref[...]
        else:
            st_scr[...] = jnp.zeros_like(st_scr)

    ii = lax.broadcasted_iota(jnp.int32, (CHUNK, CHUNK), 0)
    jj = lax.broadcasted_iota(jnp.int32, (CHUNK, CHUNK), 1)
    rel = (1 - 2 * d) * (ii - jj)
    mask = rel >= 0
    i4 = lax.broadcasted_iota(jnp.int32, (CHUNK, N_HEADS * CHUNK), 0)
    j4 = lax.broadcasted_iota(jnp.int32, (CHUNK, N_HEADS * CHUNK), 1) % CHUNK
    mask4 = (1 - 2 * d) * (i4 - j4) >= 0
    relf = rel.astype(F32)
    row = lax.broadcasted_iota(jnp.int32, (CHUNK, 1), 0)
    p1 = jnp.where(d == 0, row + 1, CHUNK - row).astype(F32)
    lane = lax.broadcasted_iota(jnp.int32, (1, HG_W), 1)
    head_mask = [(lane // HEAD_DIM == h).astype(F32) for h in range(N_HEADS)]
    bi = lax.broadcasted_iota(jnp.int32, (HG_W, HG_W), 0) // HEAD_DIM
    bj = lax.broadcasted_iota(jnp.int32, (HG_W, HG_W), 1) // HEAD_DIM
    block_diag = bi == bj
    tri = mask.astype(BF16)
    lgl = lgam_ref[pl.ds(d, 1), :]
    ret_q_decay = jnp.exp(p1 * lgl)
    ret_k_decay = jnp.exp((CHUNK - p1) * lgl)
    ret_chunk_decay = jnp.exp(CHUNK * lgl)
    ret_weight = jnp.concatenate([jnp.where(mask, jnp.exp(relf * lg_ref[d, h]), 0.0) for h in range(N_HEADS)], axis=1)

    nt_dims = (((1,), (1,)), ((), ()))
    tn_dims = (((0,), (0,)), ((), ()))

    def split3(x):
        hi = x.astype(BF16)
        r1 = x - hi.astype(F32)
        mid = r1.astype(BF16)
        return hi, mid, (r1 - mid.astype(F32)).astype(BF16)

    def group(q, k, v, q_dec, k_upd, chunk_decay, score_weight, bb, g_idx):
        vb = v.astype(BF16)
        ks = jnp.concatenate([(k * head_mask[h]).astype(BF16) for h in range(N_HEADS)], axis=0)
        vs = jnp.concatenate([(v * head_mask[h]).astype(BF16) for h in range(N_HEADS)], axis=0)
        s = lax.dot_general(q.astype(BF16), ks, nt_dims, preferred_element_type=F32)
        s = jnp.where(mask4, s, 0.0) if score_weight is None else s * score_weight
        o = jnp.dot(s.astype(BF16), vs, preferred_element_type=F32)
        st = st_scr[bb, g_idx]
        o = o + lax.dot_general(q_dec.astype(BF16), st.astype(BF16), nt_dims, preferred_element_type=F32)
        upd = lax.dot_general(vb, k_upd.astype(BF16), tn_dims, preferred_element_type=F32)
        st_scr[bb, g_idx] = st * chunk_decay + jnp.where(block_diag, upd, 0.0)
        return o

    def rope(x):
        partner = jnp.where((lane % HEAD_DIM) < HEAD_DIM // 2,
                            pltpu.roll(x, HG_W - HEAD_DIM // 2, 1), pltpu.roll(x, HEAD_DIM // 2, 1))
        return x * cos_ref[...] + partner * sin_ref[...]

    def one_row(bb):
        q = pa_ref[bb, :, 0:HG_W]
        k = pa_ref[bb, :, HG_W:2 * HG_W] * (HEAD_DIM ** -0.5)
        v = pa_ref[bb, :, 2 * HG_W:3 * HG_W]
        if use_rope:
            q = rope(q)
            k = rope(k)
        o_ret = group(q, k, v, q * ret_q_decay, k * ret_k_decay, ret_chunk_decay, ret_weight, bb, 0)

        q = pa_ref[bb, :, 4 * HG_W:5 * HG_W] * (HEAD_DIM ** -0.5)
        k = pa_ref[bb, :, 5 * HG_W:6 * HG_W]
        v = pa_ref[bb, :, 6 * HG_W:7 * HG_W]
        la = jnp.where(d == 0, pa_ref[bb, :, 8 * HG_W:9 * HG_W], pa_ref[bb, :, 9 * HG_W:10 * HG_W])
        b = sum(jnp.dot(tri, part, preferred_element_type=F32) for part in split3(la))
        b_last = jnp.sum(la, axis=0, keepdims=True)
        q_dec = q * jnp.exp(b)
        o_gla = group(q_dec, k * jnp.exp(-b), v, q_dec, k * jnp.exp(b_last - b), jnp.exp(b_last), None, bb, 1)
        return o_ret, o_gla

    outs = [one_row(bb) for bb in range(SCAN_PB)]

    @pl.when(d == 0)
    def _():
        for bb, (o_ret, o_gla) in enumerate(outs):
            of_scr[c, bb, :, 0:HG_W] = o_ret
            of_scr[c, bb, :, HG_W:2 * HG_W] = o_gla

    @pl.when(d == 1)
    def _():
        mean_mat = jnp.where(block_diag, 1.0 / HEAD_DIM, 0.0).astype(BF16)

        def head_rms(o):
            parts = jnp.concatenate(split3(o * o), axis=0)
            ms3 = jnp.dot(parts, mean_mat, preferred_element_type=F32)
            ms = ms3[0:CHUNK] + ms3[CHUNK:2 * CHUNK] + ms3[2 * CHUNK:3 * CHUNK]
            return o * lax.rsqrt(ms + EPS)

        for bb, (o_ret, o_gla) in enumerate(outs):
            o1 = o_ret + of_scr[cc, bb, :, 0:HG_W]
            g1 = pa_ref[bb, :, 3 * HG_W:4 * HG_W]
            y_ref[bb, :, 0:HG_W] = head_rms(o1) * (g1 * _sigmoid(g1))
            o2 = o_gla + of_scr[cc, bb, :, HG_W:2 * HG_W]
            g2 = pa_ref[bb, :, 7 * HG_W:8 * HG_W]
            y_ref[bb, :, HG_W:2 * HG_W] = head_rms(o2) * gn_ref[...] * (g2 * _sigmoid(g2))

    if want_final:
        @pl.when(c == n - 1)
        def _():
            for bb in range(SCAN_PB):
                for g_idx in range(2):
                    st_t = st_scr[bb, g_idx].T
                    for h in range(N_HEADS):
                        hs = slice(h * HEAD_DIM, (h + 1) * HEAD_DIM)
                        fin_refs[g_idx][bb, h] = st_t[hs, hs]


def _scan_call(pa, first_row, bsz, t, lg, lgam, gn, rope, st0, want_final):
    n = t // CHUNK
    use_rope = rope is not None
    has_init = st0 is not None
    pair_rows = SCAN_PB * t
    assert bsz % SCAN_PB == 0 and first_row % pair_rows == 0 and pa.shape[0] % pair_rows == 0
    pair_off = first_row // pair_rows
    pa4 = pa.reshape(pa.shape[0] // pair_rows, SCAN_PB, t, PA_W)

    def chunk_idx(d, c):
        return jnp.where(d == 0, c, n - 1 - c)

    in_specs = [
        pl.BlockSpec(memory_space=pltpu.SMEM),
        pl.BlockSpec((None, SCAN_PB, CHUNK, PA_W), lambda b, d, c: (pair_off + b, 0, chunk_idx(d, c), 0)),
    ]
    args = [lg, pa4]
    if use_rope:
        in_specs += [pl.BlockSpec((CHUNK, HG_W), lambda b, d, c: (chunk_idx(d, c), 0))] * 2
        args += list(rope)
    in_specs += [pl.BlockSpec((2, HG_W), lambda b, d, c: (0, 0)),
                 pl.BlockSpec((1, HG_W), lambda b, d, c: (0, 0))]
    args += [lgam, gn]
    state_spec = pl.BlockSpec((SCAN_PB, None, 2, HG_W, HG_W), lambda b, d, c: (b, d, 0, 0, 0))
    if has_init:
        in_specs.append(state_spec)
        args.append(st0)
    out_specs = [pl.BlockSpec((None, SCAN_PB, CHUNK, 2 * HG_W),
                              lambda b, d, c: (b, 0, jnp.where(d == 0, n - 1, n - 1 - c), 0))]
    out_shape = [jax.ShapeDtypeStruct((bsz // SCAN_PB, SCAN_PB, t, 2 * HG_W), F32)]
    if want_final:
        fin_spec = pl.BlockSpec((SCAN_PB, None, N_HEADS, HEAD_DIM, HEAD_DIM), lambda b, d, c: (b, d, 0, 0, 0))
        out_specs += [fin_spec, fin_spec]
        out_shape += [jax.ShapeDtypeStruct((bsz, 2, N_HEADS, HEAD_DIM, HEAD_DIM), F32)] * 2
    res = pl.pallas_call(
        functools.partial(_scan_kernel, n=n, use_rope=use_rope, has_init=has_init, want_final=want_final),
        grid=(bsz // SCAN_PB, 2, n),
        in_specs=in_specs,
        out_specs=out_specs,
        out_shape=out_shape,
        scratch_shapes=[pltpu.VMEM((SCAN_PB, 2, HG_W, HG_W), F32), pltpu.VMEM((n, SCAN_PB, CHUNK, 2 * HG_W), F32)],
        compiler_params=pltpu.CompilerParams(vmem_limit_bytes=VMEM_LIMIT),
        name="ret_gla_scan",
    )(*args)
    y = res[0].reshape(bsz * t, 2 * HG_W)
    return (y, res[1], res[2]) if want_final else (y, None, None)


S5_GB = LANES // S5_CH
S5_SEQS = 6


def _s5_table_kernel(pwr_ref, pwi_ref, fz_ref, btr_ref, bti_ref, cr_ref, ci_ref, m_ref, sin_ref, sout_ref):
    def per_token(t):
        return jnp.concatenate([jnp.broadcast_to(t[j:j + 1, :], (S5_CH, S5_STATE)) for j in range(S5_L)], axis=0)

    def per_channel(t):
        return jnp.concatenate([t] * S5_L, axis=0)

    def powers(d, k):
        rows = slice(k * S5_L, (k + 1) * S5_L)
        return per_token(pwr_ref[d, rows, :]), per_token(pwi_ref[d, rows, :])

    cr, ci = per_channel(cr_ref[...]), per_channel(ci_ref[...])
    tok_in = lax.broadcasted_iota(jnp.int32, (S5_ROW, S5_ROW), 0) // S5_CH
    tok_out = lax.broadcasted_iota(jnp.int32, (S5_ROW, S5_ROW), 1) // S5_CH
    nt_dims = (((1,), (1,)), ((), ()))
    m = jnp.zeros((S5_ROW, S5_ROW), F32)
    for d in range(2):
        fr, fi = fz_ref[d, 0:1, :], fz_ref[d, 1:2, :]
        bbr = per_channel(fr * btr_ref[...] - fi * bti_ref[...])
        bbi = per_channel(fr * bti_ref[...] + fi * btr_ref[...])
        k_out, k_in, k_sin, k_sout = (0, 1, 2, 4) if d == 0 else (1, 0, 3, 5)
        er, ei = powers(d, k_out)
        cq = jnp.concatenate([cr * er - ci * ei, -(cr * ei + ci * er)], axis=-1)
        er, ei = powers(d, k_in)
        bk = jnp.concatenate([bbr * er - bbi * ei, bbr * ei + bbi * er], axis=-1)
        full = lax.dot_general(bk, cq, nt_dims, precision=HIGHEST, preferred_element_type=F32)
        m = m + jnp.where((tok_out >= tok_in) if d == 0 else (tok_in >= tok_out), full, 0.0)
        er, ei = powers(d, k_sin)
        sin_ref[d] = jnp.concatenate([bbr * er - bbi * ei, bbr * ei + bbi * er], axis=-1).astype(BF16)
        er, ei = powers(d, k_sout)
        sout_ref[d] = jnp.concatenate([cr * er - ci * ei, -(cr * ei + ci * er)], axis=-1).astype(BF16)
    m_ref[...] = m.astype(BF16)


def _s5_power_table(a_re, a_im, log_dt):
    j = np.arange(S5_L, dtype=np.float32)
    half = (S5_L - 1) / 2.0
    expo = jnp.asarray(np.concatenate([j - half, half - j, S5_L - 1 - j, j, j + 1, S5_L - j]))[:, None]
    dt = jnp.exp(log_dt)[..., None, None]
    ar, ai = a_re[..., None, :] * dt, a_im[..., None, :] * dt
    mag = jnp.exp(expo * ar)
    return mag * jnp.cos(expo * ai), mag * jnp.sin(expo * ai)


def _s5_tables(a_re, a_im, log_dt, b_re, b_im, c_re, c_im):
    depth = a_re.shape[0]
    pwr, pwi = _s5_power_table(a_re, a_im, log_dt)
    dt = jnp.exp(log_dt)[..., None]
    mag = jnp.exp(a_re * dt)
    lb_re, lb_im = mag * jnp.cos(a_im * dt), mag * jnp.sin(a_im * dt)
    den = a_re * a_re + a_im * a_im
    n_re = lb_re - 1.0
    fz = jnp.stack([(n_re * a_re + lb_im * a_im) / den, (lb_im * a_re - n_re * a_im) / den], axis=3)
    mag_l = jnp.exp(S5_L * (a_re * dt))
    pl_re, pl_im = mag_l * jnp.cos(S5_L * (a_im * dt)), mag_l * jnp.sin(S5_L * (a_im * dt))
    lam = jnp.stack([jnp.concatenate([pl_re[:, 0], pl_re[:, 0]], -1), jnp.concatenate([-pl_im[:, 0], pl_im[:, 0]], -1),
                     jnp.concatenate([pl_re[:, 1], pl_re[:, 1]], -1), jnp.concatenate([-pl_im[:, 1], pl_im[:, 1]], -1)],
                    axis=2)
    per_lg = lambda *shape: pl.BlockSpec((None, None) + shape, lambda l, g: (l, g) + (0,) * len(shape))
    per_dir = lambda *shape: pl.BlockSpec((None, 2, None) + shape, lambda l, g: (l, 0, g) + (0,) * len(shape))
    m, s_in, s_out = pl.pallas_call(
        _s5_table_kernel,
        grid=(depth, S5_GROUPS),
        in_specs=[per_dir(S5_SEQS * S5_L, S5_STATE), per_dir(S5_SEQS * S5_L, S5_STATE), per_dir(2, S5_STATE),
                  per_lg(S5_CH, S5_STATE), per_lg(S5_CH, S5_STATE), per_lg(S5_CH, S5_STATE), per_lg(S5_CH, S5_STATE)],
        out_specs=[per_lg(S5_ROW, S5_ROW), per_lg(2, S5_ROW, 2 * S5_STATE), per_lg(2, S5_ROW, 2 * S5_STATE)],
        out_shape=[jax.ShapeDtypeStruct((depth, S5_GROUPS, S5_ROW, S5_ROW), BF16),
                   jax.ShapeDtypeStruct((depth, S5_GROUPS, 2, S5_ROW, 2 * S5_STATE), BF16),
                   jax.ShapeDtypeStruct((depth, S5_GROUPS, 2, S5_ROW, 2 * S5_STATE), BF16)],
        compiler_params=pltpu.CompilerParams(vmem_limit_bytes=VMEM_LIMIT),
        name="s5_tables",
    )(pwr, pwi, fz, jnp.swapaxes(b_re, -1, -2), jnp.swapaxes(b_im, -1, -2), c_re, c_im)
    return m, s_in, s_out, lam


def _s5_kernel(*refs, n, bsz, has_init):
    refs = list(refs)
    su_ref = refs.pop(0)
    m_ref = refs.pop(0)
    sin_ref = refs.pop(0)
    sout_ref = refs.pop(0)
    lam_ref = refs.pop(0)
    dv_ref = refs.pop(0)
    if has_init:
        h0_ref = refs.pop(0)
    y_ref = refs.pop(0)
    fin_ref = refs.pop(0)
    sfs, sbs, hpf, hnb = refs

    r = n * bsz
    gran = lax.broadcasted_iota(jnp.int32, (1, LANES), 1) // S5_CH
    halves_per_row = S5_ROW // LANES
    per_half = S5_L // halves_per_row

    def tok_rows(j):
        return pl.ds(j, r, stride=S5_L)

    halves = [[None] * halves_per_row for _ in range(S5_GB)]
    for j in range(S5_L):
        uj = su_ref[tok_rows(j), :]
        for gl in range(S5_GB):
            shift = ((j % per_half) - gl) % S5_GB * S5_CH
            piece = uj if shift == 0 else pltpu.roll(uj, shift, 1)
            cur = halves[gl][j // per_half]
            halves[gl][j // per_half] = jnp.where(gran == j % per_half, piece, 0.0 if cur is None else cur)
    u = jnp.stack([jnp.concatenate(h, axis=-1) for h in halves]).astype(BF16)

    bm_dims = (((2,), (1,)), ((0,), (0,)))
    bm_nt_dims = (((2,), (2,)), ((0,), (0,)))
    y = lax.dot_general(u, m_ref[...], bm_dims, preferred_element_type=F32)
    sfs[...] = lax.dot_general(u, sin_ref[:, 0], bm_dims, preferred_element_type=F32)
    sbs[...] = lax.dot_general(u, sin_ref[:, 1], bm_dims, preferred_element_type=F32)

    a_f, s_f = lam_ref[:, 0:1, :], lam_ref[:, 1:2, :]
    a_b, s_b = lam_ref[:, 2:3, :], lam_ref[:, 3:4, :]
    if has_init:
        hf = h0_ref[:, 0]
        hb = h0_ref[:, 1]
    else:
        hf = jnp.zeros((S5_GB, bsz, 2 * S5_STATE), F32)
        hb = jnp.zeros((S5_GB, bsz, 2 * S5_STATE), F32)
    swap = lambda a: pltpu.roll(a, S5_STATE, 2)
    hf_sw, hb_sw = swap(hf), swap(hb)
    for c in range(n):
        rows = pl.ds(c, bsz, stride=n)
        hpf[:, rows, :] = hf
        s_in = sfs[:, rows, :]
        hf, hf_sw = hf * a_f + hf_sw * s_f + s_in, hf_sw * a_f - hf * s_f + swap(s_in)
        rows = pl.ds(n - 1 - c, bsz, stride=n)
        hnb[:, rows, :] = hb
        s_in = sbs[:, rows, :]
        hb, hb_sw = hb * a_b + hb_sw * s_b + s_in, hb_sw * a_b - hb * s_b + swap(s_in)
    fin_ref[:, 0] = hf
    fin_ref[:, 1] = hb
    y = y + lax.dot_general(hpf[...].astype(BF16), sout_ref[:, 0], bm_nt_dims, preferred_element_type=F32)
    y = y + lax.dot_general(hnb[...].astype(BF16), sout_ref[:, 1], bm_nt_dims, preferred_element_type=F32)

    for j in range(S5_L):
        acc = None
        for gl in range(S5_GB):
            src = y[gl][:, (j // per_half) * LANES:(j // per_half + 1) * LANES]
            shift = (gl - (j % per_half)) % S5_GB * S5_CH
            piece = src if shift == 0 else pltpu.roll(src, shift, 1)
            acc = jnp.where(gran == gl, piece, 0.0 if acc is None else acc)
        y_ref[tok_rows(j), :] = acc + su_ref[tok_rows(j), :] * dv_ref[...]


def _s5_call(su, row_block, tabs, layer, dvec, bsz, t, h0):
    m, s_in, s_out, lam = tabs
    n = t // S5_L
    r = n * bsz
    rows = bsz * t
    has_init = h0 is not None
    gspec = lambda *shape: pl.BlockSpec((S5_GB,) + shape, lambda g: (g,) + (0,) * len(shape))
    lspec = lambda *shape: pl.BlockSpec((None, S5_GB) + shape, lambda g: (layer, g) + (0,) * len(shape))
    in_specs = [pl.BlockSpec((rows, LANES), lambda g: (row_block, g)),
                lspec(S5_ROW, S5_ROW), lspec(2, S5_ROW, 2 * S5_STATE), lspec(2, S5_ROW, 2 * S5_STATE),
                lspec(4, 2 * S5_STATE), pl.BlockSpec((1, LANES), lambda g: (0, g))]
    args = [su, m, s_in, s_out, lam, dvec]
    if has_init:
        in_specs.append(gspec(2, bsz, 2 * S5_STATE))
        args.append(h0)
    return pl.pallas_call(
        functools.partial(_s5_kernel, n=n, bsz=bsz, has_init=has_init),
        grid=(S5_GROUPS // S5_GB,),
        in_specs=in_specs,
        out_specs=[pl.BlockSpec((rows, LANES), lambda g: (0, g)), gspec(2, bsz, 2 * S5_STATE)],
        out_shape=[jax.ShapeDtypeStruct((rows, S5_W), F32),
                   jax.ShapeDtypeStruct((S5_GROUPS, 2, bsz, 2 * S5_STATE), F32)],
        scratch_shapes=[pltpu.VMEM((S5_GB, r, 2 * S5_STATE), F32)] * 4,
        compiler_params=pltpu.CompilerParams(vmem_limit_bytes=VMEM_LIMIT),
        name="s5_scan",
    )(*args)


def _outproj_kernel(yrg_c_ref, yrg_l_ref, ys_c_ref, ys_l_ref, xc_ref, xl_ref, mod_ref, n2_ref, wo_ref, gw_ref, gb_ref,
                    rw_ref, rb_ref, x1_ref, xs_ref, pos_ref, wt_ref, meta_ref, *, n_ctx_tiles):
    is_ctx = pl.program_id(0) < n_ctx_tiles
    x_in = jnp.where(is_ctx, xc_ref[...], xl_ref[...])
    ys = jnp.where(is_ctx, ys_c_ref[...], ys_l_ref[...])
    yrg = jnp.where(is_ctx, yrg_c_ref[...], yrg_l_ref[...])
    s = 0.5 * ys * (1.0 + jnp.tanh(np.sqrt(2.0 / np.pi).astype(np.float32) * (ys + 0.044715 * (ys * ys * ys))))
    s = s * _sigmoid(jnp.dot(s.astype(BF16), gw_ref[...], preferred_element_type=F32) + gb_ref[...])
    m = (jnp.dot(yrg.astype(BF16), wo_ref[0:2 * HG_W, :], preferred_element_type=F32)
         + jnp.dot(s.astype(BF16), wo_ref[2 * HG_W:, :], preferred_element_type=F32))
    x1 = x_in + mod_ref[2:3, :] * m
    x1_ref[...] = x1
    h = x1 * lax.rsqrt(jnp.mean(x1 * x1, axis=-1, keepdims=True) + EPS) * n2_ref[...]
    h = h * (1.0 + mod_ref[4:5, :]) + mod_ref[3:4, :]

    h_hi = h.astype(BF16)
    h_lo = (h - h_hi.astype(F32)).astype(BF16)
    pp = jnp.dot(jnp.concatenate([h_hi, h_lo], axis=0), rw_ref[...], preferred_element_type=F32)
    logits = ((pp[:TM, :LANES] + pp[TM:, :LANES]) + (pp[:TM, LANES:] + pp[TM:, LANES:])) + rb_ref[...]
    lane = lax.broadcasted_iota(jnp.int32, (TM, LANES), 1).astype(F32)
    cur = logits
    vals, idxs = [], []
    for _ in range(TOP_K):
        mx = jnp.max(cur, axis=-1, keepdims=True)
        am = jnp.min(jnp.where(cur == mx, lane, float(LANES)), axis=-1, keepdims=True)
        vals.append(mx)
        idxs.append(am)
        cur = jnp.where(lane == am, -jnp.inf, cur)
    es = [jnp.exp(v - vals[0]) for v in vals]
    den = es[0] + es[1] + es[2] + es[3]

    hit = [(lane == am) for am in idxs]
    assign = (hit[0] | hit[1] | hit[2] | hit[3]).astype(BF16)
    ti = lax.broadcasted_iota(jnp.int32, (TM, TM), 0)
    tj = lax.broadcasted_iota(jnp.int32, (TM, TM), 1)
    rank = jnp.dot((tj < ti).astype(BF16), assign, preferred_element_type=F32)
    cnt = jnp.sum(assign.astype(F32), axis=0, keepdims=True)
    cnt_al = jnp.floor((cnt + (SEG_ALIGN - 1)) * (1.0 / SEG_ALIGN)) * SEG_ALIGN
    ei = lax.broadcasted_iota(jnp.int32, (LANES, LANES), 0)
    ej = lax.broadcasted_iota(jnp.int32, (LANES, LANES), 1)
    seg = jnp.dot(jnp.broadcast_to(cnt_al, (SUBLANES, LANES)).astype(BF16), (ei < ej).astype(BF16),
                  preferred_element_type=F32)[0:1, :]
    base = seg + rank
    rlane = lax.broadcasted_iota(jnp.int32, (TM, ROWS_T), 1).astype(F32)
    klane = lax.broadcasted_iota(jnp.int32, (TM, LANES), 1)
    onehot = jnp.zeros((TM, ROWS_T), F32)
    pos_out = jnp.zeros((TM, LANES), F32)
    wt_out = jnp.zeros((TM, LANES), F32)
    for kk in range(TOP_K):
        pk = jnp.sum(jnp.where(hit[kk], base, 0.0), axis=-1, keepdims=True)
        onehot = onehot + (rlane == pk).astype(F32)
        pos_out = jnp.where(klane == kk, pk, pos_out)
        wt_out = jnp.where(klane == kk, es[kk] / den, wt_out)
    xs_ref[0] = lax.dot_general(onehot.astype(BF16), h.astype(BF16), (((0,), (0,)), ((), ())),
                                preferred_element_type=F32)
    pos_ref[...] = pos_out.astype(jnp.int32)
    wt_ref[...] = wt_out
    mrow = lax.broadcasted_iota(jnp.int32, (SUBLANES, LANES), 0)
    meta = jnp.where(mrow == 0, jnp.broadcast_to(cnt_al, (SUBLANES, LANES)),
                     jnp.where(mrow == 1, jnp.broadcast_to(seg, (SUBLANES, LANES)), 0.0))
    meta_ref[0] = meta.astype(jnp.int32)


def _outproj_call(yrg_c, yrg_l, ys_c, ys_l, x_c, x_l, mod_l, n2, wo, gw, gb, rw, rb, n_ctx_tiles, tiles_per_lat):
    n = x_c.shape[0] + x_l.shape[0]
    nt = n // TM
    gid = functools.partial(_group_id, n_ctx_tiles=n_ctx_tiles, tiles_per_lat=tiles_per_lat)
    const = lambda *shape: pl.BlockSpec(shape, lambda i: (0,) * len(shape))
    ctx_map, lat_map = _path_maps(n_ctx_tiles)
    return pl.pallas_call(
        functools.partial(_outproj_kernel, n_ctx_tiles=n_ctx_tiles),
        grid=(nt,),
        in_specs=[
            pl.BlockSpec((TM, 2 * HG_W), ctx_map),
            pl.BlockSpec((TM, 2 * HG_W), lat_map),
            pl.BlockSpec((TM, S5_W), ctx_map),
            pl.BlockSpec((TM, S5_W), lat_map),
            pl.BlockSpec((TM, D_MODEL), ctx_map),
            pl.BlockSpec((TM, D_MODEL), lat_map),
            pl.BlockSpec((None, 6, D_MODEL), lambda i: (gid(i), 0, 0)),
            const(1, D_MODEL), const(D_MODEL, D_MODEL), const(S5_W, S5_W), const(1, S5_W),
            const(D_MODEL, 2 * LANES), const(1, LANES),
        ],
        out_specs=[
            pl.BlockSpec((TM, D_MODEL), lambda i: (i, 0)),
            pl.BlockSpec((1, ROWS_T, D_MODEL), lambda i: (i, 0, 0)),
            pl.BlockSpec((TM, LANES), lambda i: (i, 0)),
            pl.BlockSpec((TM, LANES), lambda i: (i, 0)),
            pl.BlockSpec((1, SUBLANES, LANES), lambda i: (i, 0, 0)),
        ],
        out_shape=[
            jax.ShapeDtypeStruct((n, D_MODEL), F32),
            jax.ShapeDtypeStruct((nt, ROWS_T, D_MODEL), F32),
            jax.ShapeDtypeStruct((n, LANES), jnp.int32),
            jax.ShapeDtypeStruct((n, LANES), F32),
            jax.ShapeDtypeStruct((nt, SUBLANES, LANES), jnp.int32),
        ],
        compiler_params=pltpu.CompilerParams(vmem_limit_bytes=VMEM_LIMIT),
        name="outproj_router_dispatch",
    )(yrg_c, yrg_l, ys_c, ys_l, x_c, x_l, mod_l, n2, wo, gw, gb, rw, rb)


def _expert_schedule(meta, nblk):
    nt = meta.shape[0]
    nch = (meta[:, 0, :N_EXPERTS] // SEG_ALIGN).T
    seg_chunk = (meta[:, 1, :N_EXPERTS] // SEG_ALIGN).T
    cs_end = jnp.cumsum(nch, axis=1)
    cs_start = cs_end - nch
    tot = cs_end[:, -1]
    nb = (tot + BLK_CHUNKS - 1) // BLK_CHUNKS
    blk_end = jnp.cumsum(nb)
    blk_start = blk_end - nb
    n_active = blk_end[-1]
    j = jnp.arange(nblk, dtype=jnp.int32)
    be = jnp.sum(blk_end[None, :] <= jnp.minimum(j, n_active - 1)[:, None], axis=1)
    be = jnp.clip(be, 0, N_EXPERTS - 1).astype(jnp.int32)
    own = be[:, None] == jnp.arange(N_EXPERTS, dtype=jnp.int32)[None, :]
    pick = lambda a: jnp.sum(jnp.where(own[:, :, None], a[None], 0), axis=1)
    bs = (j - jnp.sum(jnp.where(own, blk_start[None, :], 0), axis=1)) * BLK_CHUNKS
    tot_b = jnp.sum(jnp.where(own, tot[None, :], 0), axis=1)
    nvalid = jnp.where(j < n_active, jnp.clip(tot_b - bs, 0, BLK_CHUNKS), 0)
    kfirst = jnp.sum(pick(cs_end) <= bs[:, None], axis=1)
    klast = jnp.sum(pick(cs_start) < (bs + nvalid)[:, None], axis=1)
    src_base = jnp.arange(nt, dtype=jnp.int32)[None, :] * CHUNKS_T + seg_chunk
    ids = jnp.arange(N_EXPERTS, dtype=jnp.int32)
    later = (ids[None, :] > ids[:, None]) & (tot > 0)[None, :]
    nxt_e = jnp.min(jnp.where(later, ids[None, :], N_EXPERTS), axis=1)
    nxt = jnp.sum(jnp.where(own, nxt_e[None, :], 0), axis=1)
    nxt = jnp.where(nxt >= N_EXPERTS, -1, nxt)
    i32 = lambda a: a.astype(jnp.int32)
    return (be, i32(nvalid), i32(bs), i32(kfirst), i32(klast), i32(nxt),
            i32(cs_start.reshape(-1)), i32(nch.reshape(-1)), i32(src_base.reshape(-1)))


W_CAST_ROWS = 128


def _ffn_kernel(be_ref, nv_ref, bs_ref, kf_ref, kl_ref, nxt_ref, cst_ref, nch_ref, srcb_ref,
                xs_hbm, w1_hbm, b1_ref, w2_hbm, b2_ref, ys_hbm,
                xbuf, ybuf, w1s, w2s, w1b, w2b, gsem, ssem, wsem, *, nblk, nt, layer):
    j = pl.program_id(0)
    slot = j % 2

    def for_pieces(blk, fn):
        e_off = be_ref[blk] * nt
        lo_b = bs_ref[blk]
        hi_b = lo_b + nv_ref[blk]

        def body(i, carry):
            ps = cst_ref[e_off + i]
            lo = jnp.maximum(ps, lo_b)
            n = jnp.minimum(ps + nch_ref[e_off + i], hi_b) - lo

            @pl.when(n > 0)
            def _():
                fn(srcb_ref[e_off + i] + (lo - ps), lo - lo_b, n)
            return carry

        lax.fori_loop(kf_ref[blk], kl_ref[blk], body, 0)

    def start_gather(blk, s):
        for_pieces(blk, lambda src, dst, n: pltpu.make_async_copy(
            xs_hbm.at[pl.ds(src, n)], xbuf.at[s, pl.ds(dst, n)], gsem.at[s]).start())

    def start_scatter(blk, s):
        for_pieces(blk, lambda src, dst, n: pltpu.make_async_copy(
            ybuf.at[s, pl.ds(dst, n)], ys_hbm.at[pl.ds(src, n)], ssem.at[s]).start())

    def wait_gather(blk, s):
        @pl.when(nv_ref[blk] > 0)
        def _():
            pltpu.make_async_copy(xs_hbm.at[pl.ds(0, nv_ref[blk])], xbuf.at[s, pl.ds(0, nv_ref[blk])],
                                  gsem.at[s]).wait()

    def wait_scatter(blk, s):
        @pl.when(nv_ref[blk] > 0)
        def _():
            pltpu.make_async_copy(ybuf.at[s, pl.ds(0, nv_ref[blk])], ys_hbm.at[pl.ds(0, nv_ref[blk])],
                                  ssem.at[s]).wait()

    def weight_copies(e):
        return (pltpu.make_async_copy(w1_hbm.at[layer, e], w1s, wsem.at[0]),
                pltpu.make_async_copy(w2_hbm.at[layer, e], w2s, wsem.at[1]))

    @pl.when(j == 0)
    def _():
        for cp in weight_copies(be_ref[0]):
            cp.start()
        xbuf[...] = jnp.zeros_like(xbuf)
        start_gather(0, 0)

    @pl.when(j + 1 < nblk)
    def _():
        start_gather(j + 1, 1 - slot)

    @pl.when((j == 0) | (be_ref[j] != be_ref[jnp.maximum(j - 1, 0)]))
    def _():
        for cp in weight_copies(be_ref[j]):
            cp.wait()

        def cast_rows(r, carry):
            rows = pl.ds(pl.multiple_of(r * W_CAST_ROWS, W_CAST_ROWS), W_CAST_ROWS)
            w1b[rows, :] = w1s[rows, :].astype(BF16)
            w2b[rows, :] = w2s[rows, :].astype(BF16)
            return carry
        lax.fori_loop(0, D_MODEL // W_CAST_ROWS, cast_rows, 0)

        @pl.when(nxt_ref[j] >= 0)
        def _():
            for cp in weight_copies(nxt_ref[j]):
                cp.start()

    wait_gather(j, slot)

    @pl.when(j >= 2)
    def _():
        wait_scatter(j - 2, slot)

    def compute_rows(sb, carry):
        cs = pl.ds(pl.multiple_of(sb * SUB_CHUNKS, SUB_CHUNKS), SUB_CHUNKS)
        x = xbuf[slot, cs].reshape(SUB_CHUNKS * SEG_ALIGN, D_MODEL).astype(BF16)
        gu = jnp.dot(x, w1b[...], preferred_element_type=F32) + b1_ref[...]
        gate = jnp.minimum(gu[:, :D_FF], SWIGLU_LIMIT)
        up = jnp.clip(gu[:, D_FF:], -SWIGLU_LIMIT, SWIGLU_LIMIT)
        act = (up + 1.0) * gate * _sigmoid(SWIGLU_ALPHA * gate)
        y = jnp.dot(act.astype(BF16), w2b[...], preferred_element_type=F32) + b2_ref[...]
        ybuf[slot, cs] = y.reshape(SUB_CHUNKS, SEG_ALIGN, D_MODEL)
        return carry

    lax.fori_loop(0, (nv_ref[j] + SUB_CHUNKS - 1) // SUB_CHUNKS, compute_rows, 0)

    start_scatter(j, slot)

    @pl.when(j == nblk - 1)
    def _():
        wait_scatter(j, slot)
        if nblk >= 2:
            wait_scatter(j - 1, 1 - slot)


def _ffn_call(sched, xs_chunks, w1, b1, w2, b2, layer, nt):
    assert D_FF == D_MODEL
    nblk = sched[0].shape[0]
    nch = xs_chunks.shape[0]
    bmap = lambda j, be, *_: (layer, be[j], 0, 0)
    grid_spec = pltpu.PrefetchScalarGridSpec(
        num_scalar_prefetch=len(sched),
        grid=(nblk,),
        in_specs=[
            pl.BlockSpec(memory_space=pl.ANY),
            pl.BlockSpec(memory_space=pl.ANY),
            pl.BlockSpec((None, None, 1, 2 * D_FF), bmap),
            pl.BlockSpec(memory_space=pl.ANY),
            pl.BlockSpec((None, None, 1, D_MODEL), bmap),
        ],
        out_specs=pl.BlockSpec(memory_space=pl.ANY),
        scratch_shapes=[
            pltpu.VMEM((2, BLK_CHUNKS, SEG_ALIGN, D_MODEL), F32),
            pltpu.VMEM((2, BLK_CHUNKS, SEG_ALIGN, D_MODEL), F32),
            pltpu.VMEM((D_MODEL, 2 * D_FF), F32),
            pltpu.VMEM((D_FF, D_MODEL), F32),
            pltpu.VMEM((D_MODEL, 2 * D_FF), BF16),
            pltpu.VMEM((D_FF, D_MODEL), BF16),
            pltpu.SemaphoreType.DMA((2,)),
            pltpu.SemaphoreType.DMA((2,)),
            pltpu.SemaphoreType.DMA((2,)),
        ],
    )
    return pl.pallas_call(
        functools.partial(_ffn_kernel, nblk=nblk, nt=nt, layer=layer),
        grid_spec=grid_spec,
        out_shape=jax.ShapeDtypeStruct((nch, SEG_ALIGN, D_MODEL), F32),
        input_output_aliases={len(sched): 0},
        compiler_params=pltpu.CompilerParams(vmem_limit_bytes=VMEM_LIMIT, dimension_semantics=("arbitrary",)),
        name="expert_ffn",
    )(*sched, xs_chunks, w1, b1, w2, b2)


def _combine_kernel(ys_ref, pos_ref, wt_ref, x1_ref, mod_ref, fg_ref, *out_refs, final, n_ctx_tiles):
    ys = ys_ref[0].astype(BF16)
    rlane = lax.broadcasted_iota(jnp.int32, (TM, ROWS_T), 1)
    w = jnp.zeros((TM, ROWS_T), F32)
    for kk in range(TOP_K):
        w = w + jnp.where(rlane == pos_ref[:, kk:kk + 1], wt_ref[:, kk:kk + 1], 0.0)
    moe = jnp.dot(w.astype(BF16), ys, preferred_element_type=F32)
    y = x1_ref[...] + mod_ref[5:6, :] * moe
    if final:
        y = y * lax.rsqrt(jnp.mean(y * y, axis=-1, keepdims=True) + EPS) * fg_ref[...]
    is_ctx = pl.program_id(0) < n_ctx_tiles

    @pl.when(is_ctx)
    def _():
        out_refs[0][...] = y

    @pl.when(jnp.logical_not(is_ctx))
    def _():
        out_refs[1][...] = y


def _combine_call(ys, pos, wts, x1, mod_l, fg, n_ctx_tiles, tiles_per_lat, final):
    n = x1.shape[0]
    nt = n // TM
    gid = functools.partial(_group_id, n_ctx_tiles=n_ctx_tiles, tiles_per_lat=tiles_per_lat)
    tok = pl.BlockSpec((TM, D_MODEL), lambda i: (i, 0))
    ctx_map, lat_map = _path_maps(n_ctx_tiles)
    out_specs = [pl.BlockSpec((TM, D_MODEL), ctx_map), pl.BlockSpec((TM, D_MODEL), lat_map)]
    out_shape = [jax.ShapeDtypeStruct((n_ctx_tiles * TM, D_MODEL), F32),
                 jax.ShapeDtypeStruct((n - n_ctx_tiles * TM, D_MODEL), F32)]
    res = pl.pallas_call(
        functools.partial(_combine_kernel, final=final, n_ctx_tiles=n_ctx_tiles),
        grid=(nt,),
        in_specs=[
            pl.BlockSpec((1, ROWS_T, D_MODEL), lambda i: (i, 0, 0)),
            pl.BlockSpec((TM, LANES), lambda i: (i, 0)),
            pl.BlockSpec((TM, LANES), lambda i: (i, 0)),
            tok,
            pl.BlockSpec((None, 6, D_MODEL), lambda i: (gid(i), 0, 0)),
            pl.BlockSpec((1, D_MODEL), lambda i: (0, 0)),
        ],
        out_specs=out_specs,
        out_shape=out_shape,
        compiler_params=pltpu.CompilerParams(vmem_limit_bytes=VMEM_LIMIT),
        name="moe_combine",
    )(ys, pos, wts, x1, mod_l, fg)
    return res


def _rope_tables(t):
    pos = np.arange(t)
    nf = HEAD_DIM // 4
    inv = jnp.asarray(ROPE_BASE, F32) ** (-jnp.arange(nf, dtype=F32) / nf)
    ang = jnp.concatenate([jnp.asarray(pos // GRID_W, F32)[:, None] * inv,
                           jnp.asarray(pos % GRID_W, F32)[:, None] * inv], axis=-1)
    cos, sin = jnp.cos(ang), jnp.sin(ang)
    return (jnp.tile(jnp.concatenate([cos, cos], -1), (1, N_HEADS)),
            jnp.tile(jnp.concatenate([-sin, sin], -1), (1, N_HEADS)))


def _block_diag_t(s):
    eye = jnp.eye(N_HEADS, dtype=s.dtype)
    out = jnp.einsum('...hkv,hg->...hvgk', s, eye)
    return out.reshape(s.shape[:-3] + (HG_W, HG_W))


def kernel(x_prompt, x_sample, state_ret, state_gla, state_s5_re, state_s5_im, c, c_ctx, norm1_g, norm2_g, final_g, ada_w, ada_b, w_in, w_out, ret_decay, gla_w_lr, gla_b_lr, gla_norm_g, s5_a_re, s5_a_im, s5_log_dt, s5_b_re, s5_b_im, s5_c_re, s5_c_im, s5_d, s5_glu_w, s5_glu_b, router_w, router_b, moe_w1, moe_b1, moe_w2, moe_b2):
    bc, tc, d = x_prompt.shape
    bl, tl, _ = x_sample.shape
    depth = w_in.shape[0]
    nc, nl = bc * tc, bl * tl
    n = nc + nl
    assert d == D_MODEL and nc % TM == 0 and tl % TM == 0 and tc % CHUNK == 0 and tl % CHUNK == 0
    assert 1 + bl <= SUBLANES and nc % nl == 0
    n_ctx_tiles, tiles_per_lat = nc // TM, tl // TM
    nt = n // TM
    nblk = nt * CHUNKS_T // BLK_CHUNKS + N_EXPERTS

    x_c, x_l = x_prompt.reshape(nc, d), x_sample.reshape(nl, d)
    cond8 = jnp.concatenate([c_ctx[None, :], c, jnp.zeros((SUBLANES - 1 - bl, d), F32)], axis=0)
    mod = _ada_call(cond8, ada_w, ada_b).reshape(depth, SUBLANES, 6, d)
    rope = _rope_tables(tl)
    fg = final_g.reshape(1, d)
    s5_tabs = _s5_tables(s5_a_re, s5_a_im, s5_log_dt, s5_b_re, s5_b_im, s5_c_re, s5_c_im)

    ret_states, gla_states, re_states, im_states = [], [], [], []
    for l in range(depth):
        wl = w_in[l]
        w_cat = jnp.concatenate(
            [wl[:, :8 * HG_W], wl[:, 8 * HG_W + GLA_RANK:],
             jnp.pad(wl[:, 8 * HG_W:8 * HG_W + GLA_RANK], ((0, 0), (0, LANES - GLA_RANK)))], axis=1).astype(BF16)
        wlr = jnp.pad(gla_w_lr[l], ((0, 0), (0, LANES - GLA_RANK), (0, 0)))
        blr = gla_b_lr[l].reshape(2, 1, HG_W)
        pa, su = _inproj_call(x_c, x_l, mod[l], norm1_g[l].reshape(1, d), w_cat, wlr, blr, n_ctx_tiles, tiles_per_lat)

        log_gamma = jnp.log1p(-jnp.exp(ret_decay[l]))
        lgam = jnp.repeat(log_gamma, HEAD_DIM, axis=1)
        gn = jnp.tile(gla_norm_g[l], N_HEADS).reshape(1, HG_W)
        y_c, fin_ret, fin_gla = _scan_call(pa, 0, bc, tc, log_gamma, lgam, gn, None, None, True)
        st0 = jnp.stack([_block_diag_t(state_ret[:, l]), _block_diag_t(state_gla[:, l])], axis=2)
        y_l, _, _ = _scan_call(pa, nc, bl, tl, log_gamma, lgam, gn, rope, st0, False)
        ret_states.append(fin_ret)
        gla_states.append(fin_gla)

        dvec = s5_d[l].reshape(1, S5_W)
        ys_c, fin_c = _s5_call(su, 0, s5_tabs, l, dvec, bc, tc, None)
        h0 = jnp.concatenate([state_s5_re[:, l], state_s5_im[:, l]], axis=-1).transpose(2, 1, 0, 3)
        ys_l, _ = _s5_call(su, nc // nl, s5_tabs, l, dvec, bl, tl, h0)
        re_states.append(fin_c[..., :S5_STATE].transpose(2, 1, 0, 3))
        im_states.append(fin_c[..., S5_STATE:].transpose(2, 1, 0, 3))

        rw = jnp.pad(router_w[l], ((0, 0), (0, LANES - N_EXPERTS)))
        rw_hi = rw.astype(BF16)
        rw = jnp.concatenate([rw_hi, (rw - rw_hi.astype(F32)).astype(BF16)], axis=1)
        rb = jnp.concatenate([router_b[l], jnp.full((LANES - N_EXPERTS,), NEG_BIG, F32)]).reshape(1, LANES)
        x1, xs, pos, wts, meta = _outproj_call(
            y_c, y_l, ys_c, ys_l, x_c, x_l, mod[l], norm2_g[l].reshape(1, d), w_out[l].astype(BF16), s5_glu_w[l].astype(BF16),
            s5_glu_b[l].reshape(1, S5_W), rw, rb, n_ctx_tiles, tiles_per_lat)

        sched = _expert_schedule(meta, nblk)
        ys = _ffn_call(sched, xs.reshape(nt * CHUNKS_T, SEG_ALIGN, d),
                       moe_w1, moe_b1.reshape(depth, N_EXPERTS, 1, 2 * D_FF),
                       moe_w2, moe_b2.reshape(depth, N_EXPERTS, 1, d), l, nt)
        res = _combine_call(ys.reshape(nt, ROWS_T, d), pos, wts, x1, mod[l], fg,
                            n_ctx_tiles, tiles_per_lat, l == depth - 1)
        x_c, x_l = res
    y_prompt = x_c.reshape(bc, tc, d)
    y_sample = x_l.reshape(bl, tl, d)
    return (y_prompt, y_sample, jnp.stack(ret_states, axis=1), jnp.stack(gla_states, axis=1),
            jnp.stack(re_states, axis=1), jnp.stack(im_states, axis=1))
```

```python
import functools

import numpy as np
import jax
import jax.numpy as jnp
from jax import lax
from jax.experimental import pallas as pl
from jax.experimental.pallas import tpu as pltpu

F32 = jnp.float32
BF16 = jnp.bfloat16
HIGHEST = lax.Precision.HIGHEST

D_MODEL = 1024
GRID_W = 64
CHUNK = 128
HEAD_DIM = 64
N_HEADS = 4
HG_W = N_HEADS * HEAD_DIM
S5_W = 512
S5_CH = 16
S5_GROUPS = 32
S5_STATE = 64
S5_L = 16
S5_ROW = S5_L * S5_CH
GLA_RANK = 16
GLA_TAU = 16.0
N_EXPERTS = 32
TOP_K = 4
D_FF = 1024
SWIGLU_LIMIT = 7.0
SWIGLU_ALPHA = 1.702
ROPE_BASE = 10000.0
EPS = 1e-6

LANES = 128
SUBLANES = 8
TM = 256
TM_IN = 512
SEG_ALIGN = SUBLANES
ROWS_T = 1280
CHUNKS_T = ROWS_T // SEG_ALIGN
BLK_CHUNKS = 128
SUB_CHUNKS = 64
TAIL_CHUNKS = 32
BLK_ROWS = BLK_CHUNKS * SEG_ALIGN
NEG_BIG = -1e30
VMEM_LIMIT = 56 * 1024 * 1024

PA_W = 8 * HG_W + 2 * HG_W
W_CAT = 8 * HG_W + S5_W + LANES


def _sigmoid(x):
    return 1.0 / (1.0 + jnp.exp(-x))


def _group_id(i, n_ctx_tiles, tiles_per_lat):
    return jnp.where(i < n_ctx_tiles, 0, 1 + (i - n_ctx_tiles) // tiles_per_lat)


def _ada_kernel(c_ref, w_ref, b_ref, o_ref):
    c = c_ref[...]
    s = c * _sigmoid(c)
    o_ref[0] = jnp.dot(s, w_ref[0], precision=HIGHEST, preferred_element_type=F32) + b_ref[0]


def _ada_call(cond8, ada_w, ada_b):
    depth, d, n6 = ada_w.shape
    tn = 1024
    return pl.pallas_call(
        _ada_kernel,
        grid=(depth, n6 // tn),
        in_specs=[
            pl.BlockSpec((SUBLANES, d), lambda l, j: (0, 0)),
            pl.BlockSpec((1, d, tn), lambda l, j: (l, 0, j)),
            pl.BlockSpec((1, 1, tn), lambda l, j: (l, 0, j)),
        ],
        out_specs=pl.BlockSpec((1, SUBLANES, tn), lambda l, j: (l, 0, j)),
        out_shape=jax.ShapeDtypeStruct((depth, SUBLANES, n6), F32),
        compiler_params=pltpu.CompilerParams(vmem_limit_bytes=VMEM_LIMIT),
        name="ada_mod",
    )(cond8, ada_w, ada_b.reshape(depth, 1, n6))


def _inproj_kernel(xc_ref, xl_ref, mod_ref, n1_ref, w_ref, wlr_ref, blr_ref, pa_ref, su_ref, *, n_ctx_tiles):
    x = jnp.where(pl.program_id(0) < n_ctx_tiles, xc_ref[...], xl_ref[...])
    h = x * lax.rsqrt(jnp.mean(x * x, axis=-1, keepdims=True) + EPS) * n1_ref[...]
    h = h * (1.0 + mod_ref[1:2, :]) + mod_ref[0:1, :]
    r = jnp.dot(h.astype(BF16), w_ref[...], preferred_element_type=F32)
    pa_ref[:, : 8 * HG_W] = r[:, : 8 * HG_W]
    su_ref[...] = r[:, 8 * HG_W: 8 * HG_W + S5_W]
    glr = r[:, 8 * HG_W + S5_W:]
    for d in range(2):
        z = jnp.dot(glr, wlr_ref[d], precision=HIGHEST, preferred_element_type=F32) + blr_ref[d]
        log_sig = jnp.minimum(z, 0.0) - jnp.log(1.0 + jnp.exp(-jnp.abs(z)))
        pa_ref[:, (8 + d) * HG_W: (9 + d) * HG_W] = log_sig * (1.0 / GLA_TAU)


def _path_maps(n_ctx_tiles):
    return (lambda i: (jnp.minimum(i, n_ctx_tiles - 1), 0)), (lambda i: (jnp.maximum(i - n_ctx_tiles, 0), 0))


def _inproj_call(x_c, x_l, mod_l, n1, w_cat, wlr, blr, lat_seq):
    n = x_c.shape[0] + x_l.shape[0]
    assert x_c.shape[0] % TM_IN == 0 and lat_seq % TM_IN == 0
    n_ctx_tiles, tiles_per_lat = x_c.shape[0] // TM_IN, lat_seq // TM_IN
    gid = functools.partial(_group_id, n_ctx_tiles=n_ctx_tiles, tiles_per_lat=tiles_per_lat)
    ctx_map, lat_map = _path_maps(n_ctx_tiles)
    return pl.pallas_call(
        functools.partial(_inproj_kernel, n_ctx_tiles=n_ctx_tiles),
        grid=(n // TM_IN,),
        in_specs=[
            pl.BlockSpec((TM_IN, D_MODEL), ctx_map),
            pl.BlockSpec((TM_IN, D_MODEL), lat_map),
            pl.BlockSpec((None, 6, D_MODEL), lambda i: (gid(i), 0, 0)),
            pl.BlockSpec((1, D_MODEL), lambda i: (0, 0)),
            pl.BlockSpec((D_MODEL, W_CAT), lambda i: (0, 0)),
            pl.BlockSpec((2, LANES, HG_W), lambda i: (0, 0, 0)),
            pl.BlockSpec((2, 1, HG_W), lambda i: (0, 0, 0)),
        ],
        out_specs=[
            pl.BlockSpec((TM_IN, PA_W), lambda i: (i, 0)),
            pl.BlockSpec((TM_IN, S5_W), lambda i: (i, 0)),
        ],
        out_shape=[
            jax.ShapeDtypeStruct((n, PA_W), F32),
            jax.ShapeDtypeStruct((n, S5_W), F32),
        ],
        compiler_params=pltpu.CompilerParams(vmem_limit_bytes=VMEM_LIMIT),
        name="norm1_inproj",
    )(x_c, x_l, mod_l, n1, w_cat, wlr, blr)


SCAN_PB = 2


def _scan_kernel(*refs, n, use_rope, has_init, want_final):
    refs = list(refs)
    lg_ref = refs.pop(0)
    pa_ref = refs.pop(0)
    if use_rope:
        cos_ref = refs.pop(0)
        sin_ref = refs.pop(0)
    lgam_ref = refs.pop(0)
    gn_ref = refs.pop(0)
    if has_init:
        st0_ref = refs.pop(0)
    y_ref = refs.pop(0)
    if want_final:
        fin_refs = [refs.pop(0), refs.pop(0)]
    st_scr = refs.pop(0)
    of_scr = refs.pop(0)

    d = pl.program_id(1)
    c = pl.program_id(2)
    cc = jnp.where(d == 0, c, n - 1 - c)

    @pl.when(c == 0)
    def _():
        if has_init:
            st_scr[...] = st0_ref[...]
        else:
            st_scr[...] = jnp.zeros_like(st_scr)

    ii = lax.broadcasted_iota(jnp.int32, (CHUNK, CHUNK), 0)
    jj = lax.broadcasted_iota(jnp.int32, (CHUNK, CHUNK), 1)
    rel = (1 - 2 * d) * (ii - jj)
    mask = rel >= 0
    i4 = lax.broadcasted_iota(jnp.int32, (CHUNK, N_HEADS * CHUNK), 0)
    j4 = lax.broadcasted_iota(jnp.int32, (CHUNK, N_HEADS * CHUNK), 1) % CHUNK
    mask4 = (1 - 2 * d) * (i4 - j4) >= 0
    relf = rel.astype(F32)
    row = lax.broadcasted_iota(jnp.int32, (CHUNK, 1), 0)
    p1 = jnp.where(d == 0, row + 1, CHUNK - row).astype(F32)
    lane = lax.broadcasted_iota(jnp.int32, (1, HG_W), 1)
    head_mask = [(lane // HEAD_DIM == h).astype(F32) for h in range(N_HEADS)]
    bi = lax.broadcasted_iota(jnp.int32, (HG_W, HG_W), 0) // HEAD_DIM
    bj = lax.broadcasted_iota(jnp.int32, (HG_W, HG_W), 1) // HEAD_DIM
    block_diag = bi == bj
    tri = mask.astype(BF16)
    lgl = lgam_ref[pl.ds(d, 1), :]
    ret_q_decay = jnp.exp(p1 * lgl)
    ret_k_decay = jnp.exp((CHUNK - p1) * lgl)
    ret_chunk_decay = jnp.exp(CHUNK * lgl)
    ret_weight = jnp.concatenate([jnp.where(mask, jnp.exp(relf * lg_ref[d, h]), 0.0) for h in range(N_HEADS)], axis=1)

    nt_dims = (((1,), (1,)), ((), ()))
    tn_dims = (((0,), (0,)), ((), ()))

    def split3(x):
        hi = x.astype(BF16)
        r1 = x - hi.astype(F32)
        mid = r1.astype(BF16)
        return hi, mid, (r1 - mid.astype(F32)).astype(BF16)

    def group(q, k, v, q_dec, k_upd, chunk_decay, score_weight, bb, g_idx):
        vb = v.astype(BF16)
        ks = jnp.concatenate([(k * head_mask[h]).astype(BF16) for h in range(N_HEADS)], axis=0)
        vs = jnp.concatenate([(v * head_mask[h]).astype(BF16) for h in range(N_HEADS)], axis=0)
        s = lax.dot_general(q.astype(BF16), ks, nt_dims, preferred_element_type=F32)
        s = jnp.where(mask4, s, 0.0) if score_weight is None else s * score_weight
        o = jnp.dot(s.astype(BF16), vs, preferred_element_type=F32)
        st = st_scr[bb, g_idx]
        o = o + lax.dot_general(q_dec.astype(BF16), st.astype(BF16), nt_dims, preferred_element_type=F32)
        upd = lax.dot_general(vb, k_upd.astype(BF16), tn_dims, preferred_element_type=F32)
        st_scr[bb, g_idx] = st * chunk_decay + jnp.where(block_diag, upd, 0.0)
        return o

    def rope(x):
        partner = jnp.where((lane % HEAD_DIM) < HEAD_DIM // 2,
                            pltpu.roll(x, HG_W - HEAD_DIM // 2, 1), pltpu.roll(x, HEAD_DIM // 2, 1))
        return x * cos_ref[...] + partner * sin_ref[...]

    def one_row(bb):
        q = pa_ref[bb, :, 0:HG_W]
        k = pa_ref[bb, :, HG_W:2 * HG_W] * (HEAD_DIM ** -0.5)
        v = pa_ref[bb, :, 2 * HG_W:3 * HG_W]
        if use_rope:
            q = rope(q)
            k = rope(k)
        o_ret = group(q, k, v, q * ret_q_decay, k * ret_k_decay, ret_chunk_decay, ret_weight, bb, 0)

        q = pa_ref[bb, :, 4 * HG_W:5 * HG_W] * (HEAD_DIM ** -0.5)
        k = pa_ref[bb, :, 5 * HG_W:6 * HG_W]
        v = pa_ref[bb, :, 6 * HG_W:7 * HG_W]
        la = jnp.where(d == 0, pa_ref[bb, :, 8 * HG_W:9 * HG_W], pa_ref[bb, :, 9 * HG_W:10 * HG_W])
        b = sum(jnp.dot(tri, part, preferred_element_type=F32) for part in split3(la))
        b_last = jnp.sum(la, axis=0, keepdims=True)
        q_dec = q * jnp.exp(b)
        o_gla = group(q_dec, k * jnp.exp(-b), v, q_dec, k * jnp.exp(b_last - b), jnp.exp(b_last), None, bb, 1)
        return o_ret, o_gla

    outs = [one_row(bb) for bb in range(SCAN_PB)]

    @pl.when(d == 0)
    def _():
        for bb, (o_ret, o_gla) in enumerate(outs):
            of_scr[c, bb, :, 0:HG_W] = o_ret
            of_scr[c, bb, :, HG_W:2 * HG_W] = o_gla

    @pl.when(d == 1)
    def _():
        mean_mat = jnp.where(block_diag, 1.0 / HEAD_DIM, 0.0).astype(BF16)

        def head_rms(o):
            parts = jnp.concatenate(split3(o * o), axis=0)
            ms3 = jnp.dot(parts, mean_mat, preferred_element_type=F32)
            ms = ms3[0:CHUNK] + ms3[CHUNK:2 * CHUNK] + ms3[2 * CHUNK:3 * CHUNK]
            return o * lax.rsqrt(ms + EPS)

        for bb, (o_ret, o_gla) in enumerate(outs):
            o1 = o_ret + of_scr[cc, bb, :, 0:HG_W]
            g1 = pa_ref[bb, :, 3 * HG_W:4 * HG_W]
            y_ref[bb, :, 0:HG_W] = head_rms(o1) * (g1 * _sigmoid(g1))
            o2 = o_gla + of_scr[cc, bb, :, HG_W:2 * HG_W]
            g2 = pa_ref[bb, :, 7 * HG_W:8 * HG_W]
            y_ref[bb, :, HG_W:2 * HG_W] = head_rms(o2) * gn_ref[...] * (g2 * _sigmoid(g2))

    if want_final:
        @pl.when(c == n - 1)
        def _():
            for bb in range(SCAN_PB):
                for g_idx in range(2):
                    st_t = st_scr[bb, g_idx].T
                    for h in range(N_HEADS):
                        hs = slice(h * HEAD_DIM, (h + 1) * HEAD_DIM)
                        fin_refs[g_idx][bb, h] = st_t[hs, hs]


def _scan_call(pa, first_row, bsz, t, lg, lgam, gn, rope, st0, want_final):
    n = t // CHUNK
    use_rope = rope is not None
    has_init = st0 is not None
    pair_rows = SCAN_PB * t
    assert bsz % SCAN_PB == 0 and first_row % pair_rows == 0 and pa.shape[0] % pair_rows == 0
    pair_off = first_row // pair_rows
    pa4 = pa.reshape(pa.shape[0] // pair_rows, SCAN_PB, t, PA_W)

    def chunk_idx(d, c):
        return jnp.where(d == 0, c, n - 1 - c)

    in_specs = [
        pl.BlockSpec(memory_space=pltpu.SMEM),
        pl.BlockSpec((None, SCAN_PB, CHUNK, PA_W), lambda b, d, c: (pair_off + b, 0, chunk_idx(d, c), 0)),
    ]
    args = [lg, pa4]
    if use_rope:
        in_specs += [pl.BlockSpec((CHUNK, HG_W), lambda b, d, c: (chunk_idx(d, c), 0))] * 2
        args += list(rope)
    in_specs += [pl.BlockSpec((2, HG_W), lambda b, d, c: (0, 0)),
                 pl.BlockSpec((1, HG_W), lambda b, d, c: (0, 0))]
    args += [lgam, gn]
    state_spec = pl.BlockSpec((SCAN_PB, None, 2, HG_W, HG_W), lambda b, d, c: (b, d, 0, 0, 0))
    if has_init:
        in_specs.append(state_spec)
        args.append(st0)
    out_specs = [pl.BlockSpec((None, SCAN_PB, CHUNK, 2 * HG_W),
                              lambda b, d, c: (b, 0, jnp.where(d == 0, n - 1, n - 1 - c), 0))]
    out_shape = [jax.ShapeDtypeStruct((bsz // SCAN_PB, SCAN_PB, t, 2 * HG_W), F32)]
    if want_final:
        fin_spec = pl.BlockSpec((SCAN_PB, None, N_HEADS, HEAD_DIM, HEAD_DIM), lambda b, d, c: (b, d, 0, 0, 0))
        out_specs += [fin_spec, fin_spec]
        out_shape += [jax.ShapeDtypeStruct((bsz, 2, N_HEADS, HEAD_DIM, HEAD_DIM), F32)] * 2
    res = pl.pallas_call(
        functools.partial(_scan_kernel, n=n, use_rope=use_rope, has_init=has_init, want_final=want_final),
        grid=(bsz // SCAN_PB, 2, n),
        in_specs=in_specs,
        out_specs=out_specs,
        out_shape=out_shape,
        scratch_shapes=[pltpu.VMEM((SCAN_PB, 2, HG_W, HG_W), F32), pltpu.VMEM((n, SCAN_PB, CHUNK, 2 * HG_W), F32)],
        compiler_params=pltpu.CompilerParams(vmem_limit_bytes=VMEM_LIMIT),
        name="ret_gla_scan",
    )(*args)
    y = res[0].reshape(bsz * t, 2 * HG_W)
    return (y, res[1], res[2]) if want_final else (y, None, None)


S5_GB = LANES // S5_CH
S5_SEQS = 6


def _s5_table_kernel(pwr_ref, pwi_ref, fz_ref, btr_ref, bti_ref, cr_ref, ci_ref, m_ref, sin_ref, sout_ref):
    def per_token(t):
        return jnp.concatenate([jnp.broadcast_to(t[j:j + 1, :], (S5_CH, S5_STATE)) for j in range(S5_L)], axis=0)

    def per_channel(t):
        return jnp.concatenate([t] * S5_L, axis=0)

    def powers(d, k):
        rows = slice(k * S5_L, (k + 1) * S5_L)
        return per_token(pwr_ref[d, rows, :]), per_token(pwi_ref[d, rows, :])

    cr, ci = per_channel(cr_ref[...]), per_channel(ci_ref[...])
    tok_in = lax.broadcasted_iota(jnp.int32, (S5_ROW, S5_ROW), 0) // S5_CH
    tok_out = lax.broadcasted_iota(jnp.int32, (S5_ROW, S5_ROW), 1) // S5_CH
    nt_dims = (((1,), (1,)), ((), ()))
    m = jnp.zeros((S5_ROW, S5_ROW), F32)
    for d in range(2):
        fr, fi = fz_ref[d, 0:1, :], fz_ref[d, 1:2, :]
        bbr = per_channel(fr * btr_ref[...] - fi * bti_ref[...])
        bbi = per_channel(fr * bti_ref[...] + fi * btr_ref[...])
        k_out, k_in, k_sin, k_sout = (0, 1, 2, 4) if d == 0 else (1, 0, 3, 5)
        er, ei = powers(d, k_out)
        cq = jnp.concatenate([cr * er - ci * ei, -(cr * ei + ci * er)], axis=-1)
        er, ei = powers(d, k_in)
        bk = jnp.concatenate([bbr * er - bbi * ei, bbr * ei + bbi * er], axis=-1)
        full = lax.dot_general(bk, cq, nt_dims, precision=HIGHEST, preferred_element_type=F32)
        m = m + jnp.where((tok_out >= tok_in) if d == 0 else (tok_in >= tok_out), full, 0.0)
        er, ei = powers(d, k_sin)
        sin_ref[d] = jnp.concatenate([bbr * er - bbi * ei, bbr * ei + bbi * er], axis=-1).astype(BF16)
        er, ei = powers(d, k_sout)
        sout_ref[d] = jnp.concatenate([cr * er - ci * ei, -(cr * ei + ci * er)], axis=-1).astype(BF16)
    m_ref[...] = m.astype(BF16)


def _s5_power_table(a_re, a_im, log_dt):
    j = np.arange(S5_L, dtype=np.float32)
    half = (S5_L - 1) / 2.0
    expo = jnp.asarray(np.concatenate([j - half, half - j, S5_L - 1 - j, j, j + 1, S5_L - j]))[:, None]
    dt = jnp.exp(log_dt)[..., None, None]
    ar, ai = a_re[..., None, :] * dt, a_im[..., None, :] * dt
    mag = jnp.exp(expo * ar)
    return mag * jnp.cos(expo * ai), mag * jnp.sin(expo * ai)


def _s5_tables(a_re, a_im, log_dt, b_re, b_im, c_re, c_im):
    depth = a_re.shape[0]
    pwr, pwi = _s5_power_table(a_re, a_im, log_dt)
    dt = jnp.exp(log_dt)[..., None]
    mag = jnp.exp(a_re * dt)
    lb_re, lb_im = mag * jnp.cos(a_im * dt), mag * jnp.sin(a_im * dt)
    den = a_re * a_re + a_im * a_im
    n_re = lb_re - 1.0
    fz = jnp.stack([(n_re * a_re + lb_im * a_im) / den, (lb_im * a_re - n_re * a_im) / den], axis=3)
    mag_l = jnp.exp(S5_L * (a_re * dt))
    pl_re, pl_im = mag_l * jnp.cos(S5_L * (a_im * dt)), mag_l * jnp.sin(S5_L * (a_im * dt))
    lam = jnp.stack([jnp.concatenate([pl_re[:, 0], pl_re[:, 0]], -1), jnp.concatenate([-pl_im[:, 0], pl_im[:, 0]], -1),
                     jnp.concatenate([pl_re[:, 1], pl_re[:, 1]], -1), jnp.concatenate([-pl_im[:, 1], pl_im[:, 1]], -1)],
                    axis=2)
    per_lg = lambda *shape: pl.BlockSpec((None, None) + shape, lambda l, g: (l, g) + (0,) * len(shape))
    per_dir = lambda *shape: pl.BlockSpec((None, 2, None) + shape, lambda l, g: (l, 0, g) + (0,) * len(shape))
    m, s_in, s_out = pl.pallas_call(
        _s5_table_kernel,
        grid=(depth, S5_GROUPS),
        in_specs=[per_dir(S5_SEQS * S5_L, S5_STATE), per_dir(S5_SEQS * S5_L, S5_STATE), per_dir(2, S5_STATE),
                  per_lg(S5_CH, S5_STATE), per_lg(S5_CH, S5_STATE), per_lg(S5_CH, S5_STATE), per_lg(S5_CH, S5_STATE)],
        out_specs=[per_lg(S5_ROW, S5_ROW), per_lg(2, S5_ROW, 2 * S5_STATE), per_lg(2, S5_ROW, 2 * S5_STATE)],
        out_shape=[jax.ShapeDtypeStruct((depth, S5_GROUPS, S5_ROW, S5_ROW), BF16),
                   jax.ShapeDtypeStruct((depth, S5_GROUPS, 2, S5_ROW, 2 * S5_STATE), BF16),
                   jax.ShapeDtypeStruct((depth, S5_GROUPS, 2, S5_ROW, 2 * S5_STATE), BF16)],
        compiler_params=pltpu.CompilerParams(vmem_limit_bytes=VMEM_LIMIT),
        name="s5_tables",
    )(pwr, pwi, fz, jnp.swapaxes(b_re, -1, -2), jnp.swapaxes(b_im, -1, -2), c_re, c_im)
    return m, s_in, s_out, lam


def _s5_kernel(*refs, n, bsz, has_init):
    refs = list(refs)
    su_ref = refs.pop(0)
    m_ref = refs.pop(0)
    sin_ref = refs.pop(0)
    sout_ref = refs.pop(0)
    lam_ref = refs.pop(0)
    dv_ref = refs.pop(0)
    if has_init:
        h0_ref = refs.pop(0)
    y_ref = refs.pop(0)
    fin_ref = refs.pop(0)
    sfs, sbs, hpf, hnb = refs

    r = n * bsz
    gran = lax.broadcasted_iota(jnp.int32, (1, LANES), 1) // S5_CH
    halves_per_row = S5_ROW // LANES
    per_half = S5_L // halves_per_row

    def tok_rows(j):
        return pl.ds(j, r, stride=S5_L)

    halves = [[None] * halves_per_row for _ in range(S5_GB)]
    for j in range(S5_L):
        uj = su_ref[tok_rows(j), :]
        for gl in range(S5_GB):
            shift = ((j % per_half) - gl) % S5_GB * S5_CH
            piece = uj if shift == 0 else pltpu.roll(uj, shift, 1)
            cur = halves[gl][j // per_half]
            halves[gl][j // per_half] = jnp.where(gran == j % per_half, piece, 0.0 if cur is None else cur)
    u = jnp.stack([jnp.concatenate(h, axis=-1) for h in halves]).astype(BF16)

    bm_dims = (((2,), (1,)), ((0,), (0,)))
    bm_nt_dims = (((2,), (2,)), ((0,), (0,)))
    y = lax.dot_general(u, m_ref[...], bm_dims, preferred_element_type=F32)
    sfs[...] = lax.dot_general(u, sin_ref[:, 0], bm_dims, preferred_element_type=F32)
    sbs[...] = lax.dot_general(u, sin_ref[:, 1], bm_dims, preferred_element_type=F32)

    a_f, s_f = lam_ref[:, 0:1, :], lam_ref[:, 1:2, :]
    a_b, s_b = lam_ref[:, 2:3, :], lam_ref[:, 3:4, :]
    if has_init:
        hf = h0_ref[:, 0]
        hb = h0_ref[:, 1]
    else:
        hf = jnp.zeros((S5_GB, bsz, 2 * S5_STATE), F32)
        hb = jnp.zeros((S5_GB, bsz, 2 * S5_STATE), F32)
    swap = lambda a: pltpu.roll(a, S5_STATE, 2)
    hf_sw, hb_sw = swap(hf), swap(hb)
    for c in range(n):
        rows = pl.ds(c, bsz, stride=n)
        hpf[:, rows, :] = hf
        s_in = sfs[:, rows, :]
        hf, hf_sw = hf * a_f + hf_sw * s_f + s_in, hf_sw * a_f - hf * s_f + swap(s_in)
        rows = pl.ds(n - 1 - c, bsz, stride=n)
        hnb[:, rows, :] = hb
        s_in = sbs[:, rows, :]
        hb, hb_sw = hb * a_b + hb_sw * s_b + s_in, hb_sw * a_b - hb * s_b + swap(s_in)
    fin_ref[:, 0] = hf
    fin_ref[:, 1] = hb
    y = y + lax.dot_general(hpf[...].astype(BF16), sout_ref[:, 0], bm_nt_dims, preferred_element_type=F32)
    y = y + lax.dot_general(hnb[...].astype(BF16), sout_ref[:, 1], bm_nt_dims, preferred_element_type=F32)

    for j in range(S5_L):
        acc = None
        for gl in range(S5_GB):
            src = y[gl][:, (j // per_half) * LANES:(j // per_half + 1) * LANES]
            shift = (gl - (j % per_half)) % S5_GB * S5_CH
            piece = src if shift == 0 else pltpu.roll(src, shift, 1)
            acc = jnp.where(gran == gl, piece, 0.0 if acc is None else acc)
        y_ref[tok_rows(j), :] = acc + su_ref[tok_rows(j), :] * dv_ref[...]


def _s5_call(su, row_block, tabs, layer, dvec, bsz, t, h0):
    m, s_in, s_out, lam = tabs
    n = t // S5_L
    r = n * bsz
    rows = bsz * t
    has_init = h0 is not None
    gspec = lambda *shape: pl.BlockSpec((S5_GB,) + shape, lambda g: (g,) + (0,) * len(shape))
    lspec = lambda *shape: pl.BlockSpec((None, S5_GB) + shape, lambda g: (layer, g) + (0,) * len(shape))
    in_specs = [pl.BlockSpec((rows, LANES), lambda g: (row_block, g)),
                lspec(S5_ROW, S5_ROW), lspec(2, S5_ROW, 2 * S5_STATE), lspec(2, S5_ROW, 2 * S5_STATE),
                lspec(4, 2 * S5_STATE), pl.BlockSpec((1, LANES), lambda g: (0, g))]
    args = [su, m, s_in, s_out, lam, dvec]
    if has_init:
        in_specs.append(gspec(2, bsz, 2 * S5_STATE))
        args.append(h0)
    return pl.pallas_call(
        functools.partial(_s5_kernel, n=n, bsz=bsz, has_init=has_init),
        grid=(S5_GROUPS // S5_GB,),
        in_specs=in_specs,
        out_specs=[pl.BlockSpec((rows, LANES), lambda g: (0, g)), gspec(2, bsz, 2 * S5_STATE)],
        out_shape=[jax.ShapeDtypeStruct((rows, S5_W), F32),
                   jax.ShapeDtypeStruct((S5_GROUPS, 2, bsz, 2 * S5_STATE), F32)],
        scratch_shapes=[pltpu.VMEM((S5_GB, r, 2 * S5_STATE), F32)] * 4,
        compiler_params=pltpu.CompilerParams(vmem_limit_bytes=VMEM_LIMIT),
        name="s5_scan",
    )(*args)


def _outproj_kernel(yrg_c_ref, yrg_l_ref, ys_c_ref, ys_l_ref, xc_ref, xl_ref, mod_ref, n2_ref, wo_ref, gw_ref, gb_ref,
                    rw_ref, rb_ref, x1_ref, xs_ref, pos_ref, wt_ref, meta_ref, *, n_ctx_tiles):
    is_ctx = pl.program_id(0) < n_ctx_tiles
    x_in = jnp.where(is_ctx, xc_ref[...], xl_ref[...])
    ys = jnp.where(is_ctx, ys_c_ref[...], ys_l_ref[...])
    yrg = jnp.where(is_ctx, yrg_c_ref[...], yrg_l_ref[...])
    s = 0.5 * ys * (1.0 + jnp.tanh(np.sqrt(2.0 / np.pi).astype(np.float32) * (ys + 0.044715 * (ys * ys * ys))))
    s = s * _sigmoid(jnp.dot(s.astype(BF16), gw_ref[...], preferred_element_type=F32) + gb_ref[...])
    m = (jnp.dot(yrg.astype(BF16), wo_ref[0:2 * HG_W, :], preferred_element_type=F32)
         + jnp.dot(s.astype(BF16), wo_ref[2 * HG_W:, :], preferred_element_type=F32))
    x1 = x_in + mod_ref[2:3, :] * m
    x1_ref[...] = x1
    h = x1 * lax.rsqrt(jnp.mean(x1 * x1, axis=-1, keepdims=True) + EPS) * n2_ref[...]
    h = h * (1.0 + mod_ref[4:5, :]) + mod_ref[3:4, :]

    h_hi = h.astype(BF16)
    h_lo = (h - h_hi.astype(F32)).astype(BF16)
    pp = jnp.dot(jnp.concatenate([h_hi, h_lo], axis=0), rw_ref[...], preferred_element_type=F32)
    logits = ((pp[:TM, :LANES] + pp[TM:, :LANES]) + (pp[:TM, LANES:] + pp[TM:, LANES:])) + rb_ref[...]
    lane = lax.broadcasted_iota(jnp.int32, (TM, LANES), 1).astype(F32)
    cur = logits
    vals, idxs = [], []
    for _ in range(TOP_K):
        mx = jnp.max(cur, axis=-1, keepdims=True)
        am = jnp.min(jnp.where(cur == mx, lane, float(LANES)), axis=-1, keepdims=True)
        vals.append(mx)
        idxs.append(am)
        cur = jnp.where(lane == am, -jnp.inf, cur)
    es = [jnp.exp(v - vals[0]) for v in vals]
    den = es[0] + es[1] + es[2] + es[3]

    hit = [(lane == am) for am in idxs]
    assign = (hit[0] | hit[1] | hit[2] | hit[3]).astype(BF16)
    ti = lax.broadcasted_iota(jnp.int32, (TM, TM), 0)
    tj = lax.broadcasted_iota(jnp.int32, (TM, TM), 1)
    rank = jnp.dot((tj < ti).astype(BF16), assign, preferred_element_type=F32)
    cnt = jnp.sum(assign.astype(F32), axis=0, keepdims=True)
    cnt_al = jnp.floor((cnt + (SEG_ALIGN - 1)) * (1.0 / SEG_ALIGN)) * SEG_ALIGN
    ei = lax.broadcasted_iota(jnp.int32, (LANES, LANES), 0)
    ej = lax.broadcasted_iota(jnp.int32, (LANES, LANES), 1)
    seg = jnp.dot(jnp.broadcast_to(cnt_al, (SUBLANES, LANES)).astype(BF16), (ei < ej).astype(BF16),
                  preferred_element_type=F32)[0:1, :]
    base = seg + rank
    rlane = lax.broadcasted_iota(jnp.int32, (TM, ROWS_T), 1).astype(F32)
    klane = lax.broadcasted_iota(jnp.int32, (TM, LANES), 1)
    onehot = jnp.zeros((TM, ROWS_T), F32)
    pos_out = jnp.zeros((TM, LANES), F32)
    wt_out = jnp.zeros((TM, LANES), F32)
    for kk in range(TOP_K):
        pk = jnp.sum(jnp.where(hit[kk], base, 0.0), axis=-1, keepdims=True)
        onehot = onehot + (rlane == pk).astype(F32)
        pos_out = jnp.where(klane == kk, pk, pos_out)
        wt_out = jnp.where(klane == kk, es[kk] / den, wt_out)
    xs_ref[0] = lax.dot_general(onehot.astype(BF16), h.astype(BF16), (((0,), (0,)), ((), ())),
                                preferred_element_type=F32)
    pos_ref[...] = pos_out.astype(jnp.int32)
    wt_ref[...] = wt_out
    mrow = lax.broadcasted_iota(jnp.int32, (SUBLANES, LANES), 0)
    meta = jnp.where(mrow == 0, jnp.broadcast_to(cnt_al, (SUBLANES, LANES)),
                     jnp.where(mrow == 1, jnp.broadcast_to(seg, (SUBLANES, LANES)), 0.0))
    meta_ref[0] = meta.astype(jnp.int32)


def _outproj_call(yrg_c, yrg_l, ys_c, ys_l, x_c, x_l, mod_l, n2, wo, gw, gb, rw, rb, n_ctx_tiles, tiles_per_lat):
    n = x_c.shape[0] + x_l.shape[0]
    nt = n // TM
    gid = functools.partial(_group_id, n_ctx_tiles=n_ctx_tiles, tiles_per_lat=tiles_per_lat)
    const = lambda *shape: pl.BlockSpec(shape, lambda i: (0,) * len(shape))
    ctx_map, lat_map = _path_maps(n_ctx_tiles)
    return pl.pallas_call(
        functools.partial(_outproj_kernel, n_ctx_tiles=n_ctx_tiles),
        grid=(nt,),
        in_specs=[
            pl.BlockSpec((TM, 2 * HG_W), ctx_map),
            pl.BlockSpec((TM, 2 * HG_W), lat_map),
            pl.BlockSpec((TM, S5_W), ctx_map),
            pl.BlockSpec((TM, S5_W), lat_map),
            pl.BlockSpec((TM, D_MODEL), ctx_map),
            pl.BlockSpec((TM, D_MODEL), lat_map),
            pl.BlockSpec((None, 6, D_MODEL), lambda i: (gid(i), 0, 0)),
            const(1, D_MODEL), const(D_MODEL, D_MODEL), const(S5_W, S5_W), const(1, S5_W),
            const(D_MODEL, 2 * LANES), const(1, LANES),
        ],
        out_specs=[
            pl.BlockSpec((TM, D_MODEL), lambda i: (i, 0)),
            pl.BlockSpec((1, ROWS_T, D_MODEL), lambda i: (i, 0, 0)),
            pl.BlockSpec((TM, LANES), lambda i: (i, 0)),
            pl.BlockSpec((TM, LANES), lambda i: (i, 0)),
            pl.BlockSpec((1, SUBLANES, LANES), lambda i: (i, 0, 0)),
        ],
        out_shape=[
            jax.ShapeDtypeStruct((n, D_MODEL), F32),
            jax.ShapeDtypeStruct((nt, ROWS_T, D_MODEL), F32),
            jax.ShapeDtypeStruct((n, LANES), jnp.int32),
            jax.ShapeDtypeStruct((n, LANES), F32),
            jax.ShapeDtypeStruct((nt, SUBLANES, LANES), jnp.int32),
        ],
        compiler_params=pltpu.CompilerParams(vmem_limit_bytes=VMEM_LIMIT),
        name="outproj_router_dispatch",
    )(yrg_c, yrg_l, ys_c, ys_l, x_c, x_l, mod_l, n2, wo, gw, gb, rw, rb)


def _expert_schedule(meta, nblk):
    nt = meta.shape[0]
    nch = (meta[:, 0, :N_EXPERTS] // SEG_ALIGN).T
    seg_chunk = (meta[:, 1, :N_EXPERTS] // SEG_ALIGN).T
    cs_end = jnp.cumsum(nch, axis=1)
    cs_start = cs_end - nch
    tot = cs_end[:, -1]
    nb = (tot + BLK_CHUNKS - 1) // BLK_CHUNKS
    blk_end = jnp.cumsum(nb)
    blk_start = blk_end - nb
    n_active = blk_end[-1]
    j = jnp.arange(nblk, dtype=jnp.int32)
    be = jnp.sum(blk_end[None, :] <= jnp.minimum(j, n_active - 1)[:, None], axis=1)
    be = jnp.clip(be, 0, N_EXPERTS - 1).astype(jnp.int32)
    own = be[:, None] == jnp.arange(N_EXPERTS, dtype=jnp.int32)[None, :]
    pick = lambda a: jnp.sum(jnp.where(own[:, :, None], a[None], 0), axis=1)
    bs = (j - jnp.sum(jnp.where(own, blk_start[None, :], 0), axis=1)) * BLK_CHUNKS
    tot_b = jnp.sum(jnp.where(own, tot[None, :], 0), axis=1)
    nvalid = jnp.where(j < n_active, jnp.clip(tot_b - bs, 0, BLK_CHUNKS), 0)
    kfirst = jnp.sum(pick(cs_end) <= bs[:, None], axis=1)
    klast = jnp.sum(pick(cs_start) < (bs + nvalid)[:, None], axis=1)
    src_base = jnp.arange(nt, dtype=jnp.int32)[None, :] * CHUNKS_T + seg_chunk
    ids = jnp.arange(N_EXPERTS, dtype=jnp.int32)
    later = (ids[None, :] > ids[:, None]) & (tot > 0)[None, :]
    nxt_e = jnp.min(jnp.where(later, ids[None, :], N_EXPERTS), axis=1)
    nxt = jnp.sum(jnp.where(own, nxt_e[None, :], 0), axis=1)
    nxt = jnp.where(nxt >= N_EXPERTS, -1, nxt)
    i32 = lambda a: a.astype(jnp.int32)
    return (be, i32(nvalid), i32(bs), i32(kfirst), i32(klast), i32(nxt),
            i32(cs_start.reshape(-1)), i32(nch.reshape(-1)), i32(src_base.reshape(-1)))


W_CAST_ROWS = 128


def _ffn_kernel(be_ref, nv_ref, bs_ref, kf_ref, kl_ref, nxt_ref, cst_ref, nch_ref, srcb_ref,
                xs_hbm, w1_hbm, b1_ref, w2_hbm, b2_ref, ys_hbm,
                xbuf, ybuf, w1s, w2s, w1b, w2b, gsem, ssem, wsem, *, nblk, nt, layer):
    j = pl.program_id(0)
    slot = j % 2

    def for_pieces(blk, fn):
        e_off = be_ref[blk] * nt
        lo_b = bs_ref[blk]
        hi_b = lo_b + nv_ref[blk]

        def body(i, carry):
            ps = cst_ref[e_off + i]
            lo = jnp.maximum(ps, lo_b)
            n = jnp.minimum(ps + nch_ref[e_off + i], hi_b) - lo

            @pl.when(n > 0)
            def _():
                fn(srcb_ref[e_off + i] + (lo - ps), lo - lo_b, n)
            return carry

        lax.fori_loop(kf_ref[blk], kl_ref[blk], body, 0)

    def start_gather(blk, s):
        for_pieces(blk, lambda src, dst, n: pltpu.make_async_copy(
            xs_hbm.at[pl.ds(src, n)], xbuf.at[s, pl.ds(dst, n)], gsem.at[s]).start())

    def start_scatter(blk, s):
        for_pieces(blk, lambda src, dst, n: pltpu.make_async_copy(
            ybuf.at[s, pl.ds(dst, n)], ys_hbm.at[pl.ds(src, n)], ssem.at[s]).start())

    def wait_gather(blk, s):
        @pl.when(nv_ref[blk] > 0)
        def _():
            pltpu.make_async_copy(xs_hbm.at[pl.ds(0, nv_ref[blk])], xbuf.at[s, pl.ds(0, nv_ref[blk])],
                                  gsem.at[s]).wait()

    def wait_scatter(blk, s):
        @pl.when(nv_ref[blk] > 0)
        def _():
            pltpu.make_async_copy(ybuf.at[s, pl.ds(0, nv_ref[blk])], ys_hbm.at[pl.ds(0, nv_ref[blk])],
                                  ssem.at[s]).wait()

    def weight_copies(e):
        return (pltpu.make_async_copy(w1_hbm.at[layer, e], w1s, wsem.at[0]),
                pltpu.make_async_copy(w2_hbm.at[layer, e], w2s, wsem.at[1]))

    @pl.when(j == 0)
    def _():
        for cp in weight_copies(be_ref[0]):
            cp.start()
        xbuf[...] = jnp.zeros_like(xbuf)
        start_gather(0, 0)

    @pl.when(j + 1 < nblk)
    def _():
        start_gather(j + 1, 1 - slot)

    @pl.when((j == 0) | (be_ref[j] != be_ref[jnp.maximum(j - 1, 0)]))
    def _():
        for cp in weight_copies(be_ref[j]):
            cp.wait()

        def cast_rows(r, carry):
            rows = pl.ds(pl.multiple_of(r * W_CAST_ROWS, W_CAST_ROWS), W_CAST_ROWS)
            w1b[rows, :] = w1s[rows, :].astype(BF16)
            w2b[rows, :] = w2s[rows, :].astype(BF16)
            return carry
        lax.fori_loop(0, D_MODEL // W_CAST_ROWS, cast_rows, 0)

        @pl.when(nxt_ref[j] >= 0)
        def _():
            for cp in weight_copies(nxt_ref[j]):
                cp.start()

    wait_gather(j, slot)

    @pl.when(j >= 2)
    def _():
        wait_scatter(j - 2, slot)

    def compute_rows(first_chunk, nchunks):
        cs = pl.ds(pl.multiple_of(first_chunk, TAIL_CHUNKS), nchunks)
        x = xbuf[slot, cs].reshape(nchunks * SEG_ALIGN, D_MODEL).astype(BF16)
        gu = jnp.dot(x, w1b[...], preferred_element_type=F32) + b1_ref[...]
        gate = jnp.minimum(gu[:, :D_FF], SWIGLU_LIMIT)
        up = jnp.clip(gu[:, D_FF:], -SWIGLU_LIMIT, SWIGLU_LIMIT)
        act = (up + 1.0) * gate * _sigmoid(SWIGLU_ALPHA * gate)
        y = jnp.dot(act.astype(BF16), w2b[...], preferred_element_type=F32) + b2_ref[...]
        ybuf[slot, cs] = y.reshape(nchunks, SEG_ALIGN, D_MODEL)

    nv = nv_ref[j]
    n_full = (nv + (SUB_CHUNKS - TAIL_CHUNKS) - 1) // SUB_CHUNKS
    lax.fori_loop(0, n_full, lambda sb, carry: (compute_rows(sb * SUB_CHUNKS, SUB_CHUNKS), carry)[1], 0)

    @pl.when(nv > n_full * SUB_CHUNKS)
    def _():
        compute_rows(n_full * SUB_CHUNKS, TAIL_CHUNKS)

    start_scatter(j, slot)

    @pl.when(j == nblk - 1)
    def _():
        wait_scatter(j, slot)
        if nblk >= 2:
            wait_scatter(j - 1, 1 - slot)


def _ffn_call(sched, xs_chunks, w1, b1, w2, b2, layer, nt):
    assert D_FF == D_MODEL
    nblk = sched[0].shape[0]
    nch = xs_chunks.shape[0]
    bmap = lambda j, be, *_: (layer, be[j], 0, 0)
    grid_spec = pltpu.PrefetchScalarGridSpec(
        num_scalar_prefetch=len(sched),
        grid=(nblk,),
        in_specs=[
            pl.BlockSpec(memory_space=pl.ANY),
            pl.BlockSpec(memory_space=pl.ANY),
            pl.BlockSpec((None, None, 1, 2 * D_FF), bmap),
            pl.BlockSpec(memory_space=pl.ANY),
            pl.BlockSpec((None, None, 1, D_MODEL), bmap),
        ],
        out_specs=pl.BlockSpec(memory_space=pl.ANY),
        scratch_shapes=[
            pltpu.VMEM((2, BLK_CHUNKS, SEG_ALIGN, D_MODEL), F32),
            pltpu.VMEM((2, BLK_CHUNKS, SEG_ALIGN, D_MODEL), F32),
            pltpu.VMEM((D_MODEL, 2 * D_FF), F32),
            pltpu.VMEM((D_FF, D_MODEL), F32),
            pltpu.VMEM((D_MODEL, 2 * D_FF), BF16),
            pltpu.VMEM((D_FF, D_MODEL), BF16),
            pltpu.SemaphoreType.DMA((2,)),
            pltpu.SemaphoreType.DMA((2,)),
            pltpu.SemaphoreType.DMA((2,)),
        ],
    )
    return pl.pallas_call(
        functools.partial(_ffn_kernel, nblk=nblk, nt=nt, layer=layer),
        grid_spec=grid_spec,
        out_shape=jax.ShapeDtypeStruct((nch, SEG_ALIGN, D_MODEL), F32),
        input_output_aliases={len(sched): 0},
        compiler_params=pltpu.CompilerParams(vmem_limit_bytes=VMEM_LIMIT, dimension_semantics=("arbitrary",)),
        name="expert_ffn",
    )(*sched, xs_chunks, w1, b1, w2, b2)


def _combine_kernel(ys_ref, pos_ref, wt_ref, x1_ref, mod_ref, fg_ref, *out_refs, final, n_ctx_tiles):
    ys = ys_ref[0].astype(BF16)
    rlane = lax.broadcasted_iota(jnp.int32, (TM, ROWS_T), 1)
    w = jnp.zeros((TM, ROWS_T), F32)
    for kk in range(TOP_K):
        w = w + jnp.where(rlane == pos_ref[:, kk:kk + 1], wt_ref[:, kk:kk + 1], 0.0)
    moe = jnp.dot(w.astype(BF16), ys, preferred_element_type=F32)
    y = x1_ref[...] + mod_ref[5:6, :] * moe
    if final:
        y = y * lax.rsqrt(jnp.mean(y * y, axis=-1, keepdims=True) + EPS) * fg_ref[...]
    is_ctx = pl.program_id(0) < n_ctx_tiles

    @pl.when(is_ctx)
    def _():
        out_refs[0][...] = y

    @pl.when(jnp.logical_not(is_ctx))
    def _():
        out_refs[1][...] = y


def _combine_call(ys, pos, wts, x1, mod_l, fg, n_ctx_tiles, tiles_per_lat, final):
    n = x1.shape[0]
    nt = n // TM
    gid = functools.partial(_group_id, n_ctx_tiles=n_ctx_tiles, tiles_per_lat=tiles_per_lat)
    tok = pl.BlockSpec((TM, D_MODEL), lambda i: (i, 0))
    ctx_map, lat_map = _path_maps(n_ctx_tiles)
    out_specs = [pl.BlockSpec((TM, D_MODEL), ctx_map), pl.BlockSpec((TM, D_MODEL), lat_map)]
    out_shape = [jax.ShapeDtypeStruct((n_ctx_tiles * TM, D_MODEL), F32),
                 jax.ShapeDtypeStruct((n - n_ctx_tiles * TM, D_MODEL), F32)]
    res = pl.pallas_call(
        functools.partial(_combine_kernel, final=final, n_ctx_tiles=n_ctx_tiles),
        grid=(nt,),
        in_specs=[
            pl.BlockSpec((1, ROWS_T, D_MODEL), lambda i: (i, 0, 0)),
            pl.BlockSpec((TM, LANES), lambda i: (i, 0)),
            pl.BlockSpec((TM, LANES), lambda i: (i, 0)),
            tok,
            pl.BlockSpec((None, 6, D_MODEL), lambda i: (gid(i), 0, 0)),
            pl.BlockSpec((1, D_MODEL), lambda i: (0, 0)),
        ],
        out_specs=out_specs,
        out_shape=out_shape,
        compiler_params=pltpu.CompilerParams(vmem_limit_bytes=VMEM_LIMIT),
        name="moe_combine",
    )(ys, pos, wts, x1, mod_l, fg)
    return res


def _rope_tables(t):
    pos = np.arange(t)
    nf = HEAD_DIM // 4
    inv = jnp.asarray(ROPE_BASE, F32) ** (-jnp.arange(nf, dtype=F32) / nf)
    ang = jnp.concatenate([jnp.asarray(pos // GRID_W, F32)[:, None] * inv,
                           jnp.asarray(pos % GRID_W, F32)[:, None] * inv], axis=-1)
    cos, sin = jnp.cos(ang), jnp.sin(ang)
    return (jnp.tile(jnp.concatenate([cos, cos], -1), (1, N_HEADS)),
            jnp.tile(jnp.concatenate([-sin, sin], -1), (1, N_HEADS)))


def _block_diag_t(s):
    eye = jnp.eye(N_HEADS, dtype=s.dtype)
    out = jnp.einsum('...hkv,hg->...hvgk', s, eye)
    return out.reshape(s.shape[:-3] + (HG_W, HG_W))


def kernel(x_prompt, x_sample, state_ret, state_gla, state_s5_re, state_s5_im, c, c_ctx, norm1_g, norm2_g, final_g, ada_w, ada_b, w_in, w_out, ret_decay, gla_w_lr, gla_b_lr, gla_norm_g, s5_a_re, s5_a_im, s5_log_dt, s5_b_re, s5_b_im, s5_c_re, s5_c_im, s5_d, s5_glu_w, s5_glu_b, router_w, router_b, moe_w1, moe_b1, moe_w2, moe_b2):
    bc, tc, d = x_prompt.shape
    bl, tl, _ = x_sample.shape
    depth = w_in.shape[0]
    nc, nl = bc * tc, bl * tl
    n = nc + nl
    assert d == D_MODEL and nc % TM == 0 and tl % TM == 0 and tc % CHUNK == 0 and tl % CHUNK == 0
    assert 1 + bl <= SUBLANES and nc % nl == 0
    n_ctx_tiles, tiles_per_lat = nc // TM, tl // TM
    nt = n // TM
    nblk = nt * CHUNKS_T // BLK_CHUNKS + N_EXPERTS

    x_c, x_l = x_prompt.reshape(nc, d), x_sample.reshape(nl, d)
    cond8 = jnp.concatenate([c_ctx[None, :], c, jnp.zeros((SUBLANES - 1 - bl, d), F32)], axis=0)
    mod = _ada_call(cond8, ada_w, ada_b).reshape(depth, SUBLANES, 6, d)
    rope = _rope_tables(tl)
    fg = final_g.reshape(1, d)
    s5_tabs = _s5_tables(s5_a_re, s5_a_im, s5_log_dt, s5_b_re, s5_b_im, s5_c_re, s5_c_im)

    ret_states, gla_states, re_states, im_states = [], [], [], []
    for l in range(depth):
        wl = w_in[l]
        w_cat = jnp.concatenate(
            [wl[:, :8 * HG_W], wl[:, 8 * HG_W + GLA_RANK:],
             jnp.pad(wl[:, 8 * HG_W:8 * HG_W + GLA_RANK], ((0, 0), (0, LANES - GLA_RANK)))], axis=1).astype(BF16)
        wlr = jnp.pad(gla_w_lr[l], ((0, 0), (0, LANES - GLA_RANK), (0, 0)))
        blr = gla_b_lr[l].reshape(2, 1, HG_W)
        pa, su = _inproj_call(x_c, x_l, mod[l], norm1_g[l].reshape(1, d), w_cat, wlr, blr, tl)

        log_gamma = jnp.log1p(-jnp.exp(ret_decay[l]))
        lgam = jnp.repeat(log_gamma, HEAD_DIM, axis=1)
        gn = jnp.tile(gla_norm_g[l], N_HEADS).reshape(1, HG_W)
        y_c, fin_ret, fin_gla = _scan_call(pa, 0, bc, tc, log_gamma, lgam, gn, None, None, True)
        st0 = jnp.stack([_block_diag_t(state_ret[:, l]), _block_diag_t(state_gla[:, l])], axis=2)
        y_l, _, _ = _scan_call(pa, nc, bl, tl, log_gamma, lgam, gn, rope, st0, False)
        ret_states.append(fin_ret)
        gla_states.append(fin_gla)

        dvec = s5_d[l].reshape(1, S5_W)
        ys_c, fin_c = _s5_call(su, 0, s5_tabs, l, dvec, bc, tc, None)
        h0 = jnp.concatenate([state_s5_re[:, l], state_s5_im[:, l]], axis=-1).transpose(2, 1, 0, 3)
        ys_l, _ = _s5_call(su, nc // nl, s5_tabs, l, dvec, bl, tl, h0)
        re_states.append(fin_c[..., :S5_STATE].transpose(2, 1, 0, 3))
        im_states.append(fin_c[..., S5_STATE:].transpose(2, 1, 0, 3))

        rw = jnp.pad(router_w[l], ((0, 0), (0, LANES - N_EXPERTS)))
        rw_hi = rw.astype(BF16)
        rw = jnp.concatenate([rw_hi, (rw - rw_hi.astype(F32)).astype(BF16)], axis=1)
        rb = jnp.concatenate([router_b[l], jnp.full((LANES - N_EXPERTS,), NEG_BIG, F32)]).reshape(1, LANES)
        x1, xs, pos, wts, meta = _outproj_call(
            y_c, y_l, ys_c, ys_l, x_c, x_l, mod[l], norm2_g[l].reshape(1, d), w_out[l].astype(BF16), s5_glu_w[l].astype(BF16),
            s5_glu_b[l].reshape(1, S5_W), rw, rb, n_ctx_tiles, tiles_per_lat)

        sched = _expert_schedule(meta, nblk)
        ys = _ffn_call(sched, xs.reshape(nt * CHUNKS_T, SEG_ALIGN, d),
                       moe_w1, moe_b1.reshape(depth, N_EXPERTS, 1, 2 * D_FF),
                       moe_w2, moe_b2.reshape(depth, N_EXPERTS, 1, d), l, nt)
        res = _combine_call(ys.reshape(nt, ROWS_T, d), pos, wts, x1, mod[l], fg,
                            n_ctx_tiles, tiles_per_lat, l == depth - 1)
        x_c, x_l = res
    y_prompt = x_c.reshape(bc, tc, d)
    y_sample = x_l.reshape(bl, tl, d)
    return (y_prompt, y_sample, jnp.stack(ret_states, axis=1), jnp.stack(gla_states, axis=1),
            jnp.stack(re_states, axis=1), jnp.stack(im_states, axis=1))
```

```python
import functools

import numpy as np
import jax
import jax.numpy as jnp
from jax import lax
from jax.experimental import pallas as pl
from jax.experimental.pallas import tpu as pltpu

F32 = jnp.float32
BF16 = jnp.bfloat16
HIGHEST = lax.Precision.HIGHEST

D_MODEL = 1024
GRID_W = 64
CHUNK = 128
HEAD_DIM = 64
N_HEADS = 4
HG_W = N_HEADS * HEAD_DIM
S5_W = 512
S5_CH = 16
S5_GROUPS = 32
S5_STATE = 64
S5_L = 16
S5_ROW = S5_L * S5_CH
GLA_RANK = 16
GLA_TAU = 16.0
N_EXPERTS = 32
TOP_K = 4
D_FF = 1024
SWIGLU_LIMIT = 7.0
SWIGLU_ALPHA = 1.702
ROPE_BASE = 10000.0
EPS = 1e-6

LANES = 128
SUBLANES = 8
TM = 256
TM_IN = 512
SEG_ALIGN = SUBLANES
ROWS_T = 1280
CHUNKS_T = ROWS_T // SEG_ALIGN
BLK_CHUNKS = 128
SUB_CHUNKS = 64
TAIL_CHUNKS = 32
BLK_ROWS = BLK_CHUNKS * SEG_ALIGN
NEG_BIG = -1e30
VMEM_LIMIT = 56 * 1024 * 1024

PA_W = 8 * HG_W + 2 * HG_W
W_CAT = 8 * HG_W + S5_W + LANES


def _sigmoid(x):
    return 1.0 / (1.0 + jnp.exp(-x))


def _rows_to_chunks(x):
    x3 = x.reshape(x.shape[0] // SEG_ALIGN, SEG_ALIGN, D_MODEL)
    return jnp.concatenate([x3[:, :, :D_MODEL // 2], x3[:, :, D_MODEL // 2:]], axis=1).astype(BF16)


def _chunks_to_rows(c):
    c3 = c.astype(F32)
    rows = jnp.concatenate([c3[:, :SEG_ALIGN, :], c3[:, SEG_ALIGN:, :]], axis=2)
    return rows.reshape(c.shape[0] * SEG_ALIGN, D_MODEL).astype(BF16)


def _group_id(i, n_ctx_tiles, tiles_per_lat):
    return jnp.where(i < n_ctx_tiles, 0, 1 + (i - n_ctx_tiles) // tiles_per_lat)


def _ada_kernel(c_ref, w_ref, b_ref, o_ref):
    c = c_ref[...]
    s = c * _sigmoid(c)
    o_ref[0] = jnp.dot(s, w_ref[0], precision=HIGHEST, preferred_element_type=F32) + b_ref[0]


def _ada_call(cond8, ada_w, ada_b):
    depth, d, n6 = ada_w.shape
    tn = 1024
    return pl.pallas_call(
        _ada_kernel,
        grid=(depth, n6 // tn),
        in_specs=[
            pl.BlockSpec((SUBLANES, d), lambda l, j: (0, 0)),
            pl.BlockSpec((1, d, tn), lambda l, j: (l, 0, j)),
            pl.BlockSpec((1, 1, tn), lambda l, j: (l, 0, j)),
        ],
        out_specs=pl.BlockSpec((1, SUBLANES, tn), lambda l, j: (l, 0, j)),
        out_shape=jax.ShapeDtypeStruct((depth, SUBLANES, n6), F32),
        compiler_params=pltpu.CompilerParams(vmem_limit_bytes=VMEM_LIMIT),
        name="ada_mod",
    )(cond8, ada_w, ada_b.reshape(depth, 1, n6))


def _inproj_kernel(xc_ref, xl_ref, mod_ref, n1_ref, w_ref, wlr_ref, blr_ref, pa_ref, su_ref, *, n_ctx_tiles):
    x = jnp.where(pl.program_id(0) < n_ctx_tiles, xc_ref[...], xl_ref[...])
    h = x * lax.rsqrt(jnp.mean(x * x, axis=-1, keepdims=True) + EPS) * n1_ref[...]
    h = h * (1.0 + mod_ref[1:2, :]) + mod_ref[0:1, :]
    r = jnp.dot(h.astype(BF16), w_ref[...], preferred_element_type=F32)
    pa_ref[:, : 8 * HG_W] = r[:, : 8 * HG_W]
    su_ref[...] = r[:, 8 * HG_W: 8 * HG_W + S5_W]
    glr = r[:, 8 * HG_W + S5_W:]
    for d in range(2):
        z = jnp.dot(glr, wlr_ref[d], precision=HIGHEST, preferred_element_type=F32) + blr_ref[d]
        log_sig = jnp.minimum(z, 0.0) - jnp.log(1.0 + jnp.exp(-jnp.abs(z)))
        pa_ref[:, (8 + d) * HG_W: (9 + d) * HG_W] = log_sig * (1.0 / GLA_TAU)


def _path_maps(n_ctx_tiles):
    return (lambda i: (jnp.minimum(i, n_ctx_tiles - 1), 0)), (lambda i: (jnp.maximum(i - n_ctx_tiles, 0), 0))


def _inproj_call(x_c, x_l, mod_l, n1, w_cat, wlr, blr, lat_seq):
    n = x_c.shape[0] + x_l.shape[0]
    assert x_c.shape[0] % TM_IN == 0 and lat_seq % TM_IN == 0
    n_ctx_tiles, tiles_per_lat = x_c.shape[0] // TM_IN, lat_seq // TM_IN
    gid = functools.partial(_group_id, n_ctx_tiles=n_ctx_tiles, tiles_per_lat=tiles_per_lat)
    ctx_map, lat_map = _path_maps(n_ctx_tiles)
    return pl.pallas_call(
        functools.partial(_inproj_kernel, n_ctx_tiles=n_ctx_tiles),
        grid=(n // TM_IN,),
        in_specs=[
            pl.BlockSpec((TM_IN, D_MODEL), ctx_map),
            pl.BlockSpec((TM_IN, D_MODEL), lat_map),
            pl.BlockSpec((None, 6, D_MODEL), lambda i: (gid(i), 0, 0)),
            pl.BlockSpec((1, D_MODEL), lambda i: (0, 0)),
            pl.BlockSpec((D_MODEL, W_CAT), lambda i: (0, 0)),
            pl.BlockSpec((2, LANES, HG_W), lambda i: (0, 0, 0)),
            pl.BlockSpec((2, 1, HG_W), lambda i: (0, 0, 0)),
        ],
        out_specs=[
            pl.BlockSpec((TM_IN, PA_W), lambda i: (i, 0)),
            pl.BlockSpec((TM_IN, S5_W), lambda i: (i, 0)),
        ],
        out_shape=[
            jax.ShapeDtypeStruct((n, PA_W), F32),
            jax.ShapeDtypeStruct((n, S5_W), F32),
        ],
        compiler_params=pltpu.CompilerParams(vmem_limit_bytes=VMEM_LIMIT),
        name="norm1_inproj",
    )(x_c, x_l, mod_l, n1, w_cat, wlr, blr)


SCAN_PB = 2


def _scan_kernel(*refs, n, use_rope, has_init, want_final):
    refs = list(refs)
    lg_ref = refs.pop(0)
    pa_ref = refs.pop(0)
    if use_rope:
        cos_ref = refs.pop(0)
        sin_ref = refs.pop(0)
    lgam_ref = refs.pop(0)
    gn_ref = refs.pop(0)
    if has_init:
        st0_ref = refs.pop(0)
    y_ref = refs.pop(0)
    if want_final:
        fin_refs = [refs.pop(0), refs.pop(0)]
    st_scr = refs.pop(0)
    of_scr = refs.pop(0)

    d = pl.program_id(1)
    c = pl.program_id(2)
    cc = jnp.where(d == 0, c, n - 1 - c)

    @pl.when(c == 0)
    def _():
        if has_init:
            st_scr[...] = st0_ref[...]
        else:
            st_scr[...] = jnp.zeros_like(st_scr)

    ii = lax.broadcasted_iota(jnp.int32, (CHUNK, CHUNK), 0)
    jj = lax.broadcasted_iota(jnp.int32, (CHUNK, CHUNK), 1)
    rel = (1 - 2 * d) * (ii - jj)
    mask = rel >= 0
    i4 = lax.broadcasted_iota(jnp.int32, (CHUNK, N_HEADS * CHUNK), 0)
    j4 = lax.broadcasted_iota(jnp.int32, (CHUNK, N_HEADS * CHUNK), 1) % CHUNK
    mask4 = (1 - 2 * d) * (i4 - j4) >= 0
    relf = rel.astype(F32)
    row = lax.broadcasted_iota(jnp.int32, (CHUNK, 1), 0)
    p1 = jnp.where(d == 0, row + 1, CHUNK - row).astype(F32)
    lane = lax.broadcasted_iota(jnp.int32, (1, HG_W), 1)
    head_mask = [(lane // HEAD_DIM == h).astype(F32) for h in range(N_HEADS)]
    bi = lax.broadcasted_iota(jnp.int32, (HG_W, HG_W), 0) // HEAD_DIM
    bj = lax.broadcasted_iota(jnp.int32, (HG_W, HG_W), 1) // HEAD_DIM
    block_diag = bi == bj
    tri = mask.astype(BF16)
    lgl = lgam_ref[pl.ds(d, 1), :]
    ret_q_decay = jnp.exp(p1 * lgl)
    ret_k_decay = jnp.exp((CHUNK - p1) * lgl)
    ret_chunk_decay = jnp.exp(CHUNK * lgl)
    ret_weight = jnp.concatenate([jnp.where(mask, jnp.exp(relf * lg_ref[d, h]), 0.0) for h in range(N_HEADS)], axis=1)

    nt_dims = (((1,), (1,)), ((), ()))
    tn_dims = (((0,), (0,)), ((), ()))

    def split3(x):
        hi = x.astype(BF16)
        r1 = x - hi.astype(F32)
        mid = r1.astype(BF16)
        return hi, mid, (r1 - mid.astype(F32)).astype(BF16)

    def group(q, k, v, q_dec, k_upd, chunk_decay, score_weight, bb, g_idx):
        vb = v.astype(BF16)
        ks = jnp.concatenate([(k * head_mask[h]).astype(BF16) for h in range(N_HEADS)], axis=0)
        vs = jnp.concatenate([(v * head_mask[h]).astype(BF16) for h in range(N_HEADS)], axis=0)
        s = lax.dot_general(q.astype(BF16), ks, nt_dims, preferred_element_type=F32)
        s = jnp.where(mask4, s, 0.0) if score_weight is None else s * score_weight
        o = jnp.dot(s.astype(BF16), vs, preferred_element_type=F32)
        st = st_scr[bb, g_idx]
        o = o + lax.dot_general(q_dec.astype(BF16), st.astype(BF16), nt_dims, preferred_element_type=F32)
        upd = lax.dot_general(vb, k_upd.astype(BF16), tn_dims, preferred_element_type=F32)
        st_scr[bb, g_idx] = st * chunk_decay + jnp.where(block_diag, upd, 0.0)
        return o

    def rope(x):
        partner = jnp.where((lane % HEAD_DIM) < HEAD_DIM // 2,
                            pltpu.roll(x, HG_W - HEAD_DIM // 2, 1), pltpu.roll(x, HEAD_DIM // 2, 1))
        return x * cos_ref[...] + partner * sin_ref[...]

    def one_row(bb):
        q = pa_ref[bb, :, 0:HG_W]
        k = pa_ref[bb, :, HG_W:2 * HG_W] * (HEAD_DIM ** -0.5)
        v = pa_ref[bb, :, 2 * HG_W:3 * HG_W]
        if use_rope:
            q = rope(q)
            k = rope(k)
        o_ret = group(q, k, v, q * ret_q_decay, k * ret_k_decay, ret_chunk_decay, ret_weight, bb, 0)

        q = pa_ref[bb, :, 4 * HG_W:5 * HG_W] * (HEAD_DIM ** -0.5)
        k = pa_ref[bb, :, 5 * HG_W:6 * HG_W]
        v = pa_ref[bb, :, 6 * HG_W:7 * HG_W]
        la = jnp.where(d == 0, pa_ref[bb, :, 8 * HG_W:9 * HG_W], pa_ref[bb, :, 9 * HG_W:10 * HG_W])
        b = sum(jnp.dot(tri, part, preferred_element_type=F32) for part in split3(la))
        b_last = jnp.sum(la, axis=0, keepdims=True)
        q_dec = q * jnp.exp(b)
        o_gla = group(q_dec, k * jnp.exp(-b), v, q_dec, k * jnp.exp(b_last - b), jnp.exp(b_last), None, bb, 1)
        return o_ret, o_gla

    outs = [one_row(bb) for bb in range(SCAN_PB)]

    @pl.when(d == 0)
    def _():
        for bb, (o_ret, o_gla) in enumerate(outs):
            of_scr[c, bb, :, 0:HG_W] = o_ret
            of_scr[c, bb, :, HG_W:2 * HG_W] = o_gla

    @pl.when(d == 1)
    def _():
        mean_mat = jnp.where(block_diag, 1.0 / HEAD_DIM, 0.0).astype(BF16)

        def head_rms(o):
            parts = jnp.concatenate(split3(o * o), axis=0)
            ms3 = jnp.dot(parts, mean_mat, preferred_element_type=F32)
            ms = ms3[0:CHUNK] + ms3[CHUNK:2 * CHUNK] + ms3[2 * CHUNK:3 * CHUNK]
            return o * lax.rsqrt(ms + EPS)

        for bb, (o_ret, o_gla) in enumerate(outs):
            o1 = o_ret + of_scr[cc, bb, :, 0:HG_W]
            g1 = pa_ref[bb, :, 3 * HG_W:4 * HG_W]
            y_ref[bb, :, 0:HG_W] = head_rms(o1) * (g1 * _sigmoid(g1))
            o2 = o_gla + of_scr[cc, bb, :, HG_W:2 * HG_W]
            g2 = pa_ref[bb, :, 7 * HG_W:8 * HG_W]
            y_ref[bb, :, HG_W:2 * HG_W] = head_rms(o2) * gn_ref[...] * (g2 * _sigmoid(g2))

    if want_final:
        @pl.when(c == n - 1)
        def _():
            for bb in range(SCAN_PB):
                for g_idx in range(2):
                    st_t = st_scr[bb, g_idx].T
                    for h in range(N_HEADS):
                        hs = slice(h * HEAD_DIM, (h + 1) * HEAD_DIM)
                        fin_refs[g_idx][bb, h] = st_t[hs, hs]


def _scan_call(pa, first_row, bsz, t, lg, lgam, gn, rope, st0, want_final):
    n = t // CHUNK
    use_rope = rope is not None
    has_init = st0 is not None
    pair_rows = SCAN_PB * t
    assert bsz % SCAN_PB == 0 and first_row % pair_rows == 0 and pa.shape[0] % pair_rows == 0
    pair_off = first_row // pair_rows
    pa4 = pa.reshape(pa.shape[0] // pair_rows, SCAN_PB, t, PA_W)

    def chunk_idx(d, c):
        return jnp.where(d == 0, c, n - 1 - c)

    in_specs = [
        pl.BlockSpec(memory_space=pltpu.SMEM),
        pl.BlockSpec((None, SCAN_PB, CHUNK, PA_W), lambda b, d, c: (pair_off + b, 0, chunk_idx(d, c), 0)),
    ]
    args = [lg, pa4]
    if use_rope:
        in_specs += [pl.BlockSpec((CHUNK, HG_W), lambda b, d, c: (chunk_idx(d, c), 0))] * 2
        args += list(rope)
    in_specs += [pl.BlockSpec((2, HG_W), lambda b, d, c: (0, 0)),
                 pl.BlockSpec((1, HG_W), lambda b, d, c: (0, 0))]
    args += [lgam, gn]
    state_spec = pl.BlockSpec((SCAN_PB, None, 2, HG_W, HG_W), lambda b, d, c: (b, d, 0, 0, 0))
    if has_init:
        in_specs.append(state_spec)
        args.append(st0)
    out_specs = [pl.BlockSpec((None, SCAN_PB, CHUNK, 2 * HG_W),
                              lambda b, d, c: (b, 0, jnp.where(d == 0, n - 1, n - 1 - c), 0))]
    out_shape = [jax.ShapeDtypeStruct((bsz // SCAN_PB, SCAN_PB, t, 2 * HG_W), F32)]
    if want_final:
        fin_spec = pl.BlockSpec((SCAN_PB, None, N_HEADS, HEAD_DIM, HEAD_DIM), lambda b, d, c: (b, d, 0, 0, 0))
        out_specs += [fin_spec, fin_spec]
        out_shape += [jax.ShapeDtypeStruct((bsz, 2, N_HEADS, HEAD_DIM, HEAD_DIM), F32)] * 2
    res = pl.pallas_call(
        functools.partial(_scan_kernel, n=n, use_rope=use_rope, has_init=has_init, want_final=want_final),
        grid=(bsz // SCAN_PB, 2, n),
        in_specs=in_specs,
        out_specs=out_specs,
        out_shape=out_shape,
        scratch_shapes=[pltpu.VMEM((SCAN_PB, 2, HG_W, HG_W), F32), pltpu.VMEM((n, SCAN_PB, CHUNK, 2 * HG_W), F32)],
        compiler_params=pltpu.CompilerParams(vmem_limit_bytes=VMEM_LIMIT),
        name="ret_gla_scan",
    )(*args)
    y = res[0].reshape(bsz * t, 2 * HG_W)
    return (y, res[1], res[2]) if want_final else (y, None, None)


S5_GB = LANES // S5_CH
S5_SEQS = 6


def _s5_table_kernel(pwr_ref, pwi_ref, fz_ref, btr_ref, bti_ref, cr_ref, ci_ref, m_ref, sin_ref, sout_ref):
    def per_token(t):
        return jnp.concatenate([jnp.broadcast_to(t[j:j + 1, :], (S5_CH, S5_STATE)) for j in range(S5_L)], axis=0)

    def per_channel(t):
        return jnp.concatenate([t] * S5_L, axis=0)

    def powers(d, k):
        rows = slice(k * S5_L, (k + 1) * S5_L)
        return per_token(pwr_ref[d, rows, :]), per_token(pwi_ref[d, rows, :])

    cr, ci = per_channel(cr_ref[...]), per_channel(ci_ref[...])
    tok_in = lax.broadcasted_iota(jnp.int32, (S5_ROW, S5_ROW), 0) // S5_CH
    tok_out = lax.broadcasted_iota(jnp.int32, (S5_ROW, S5_ROW), 1) // S5_CH
    nt_dims = (((1,), (1,)), ((), ()))
    m = jnp.zeros((S5_ROW, S5_ROW), F32)
    for d in range(2):
        fr, fi = fz_ref[d, 0:1, :], fz_ref[d, 1:2, :]
        bbr = per_channel(fr * btr_ref[...] - fi * bti_ref[...])
        bbi = per_channel(fr * bti_ref[...] + fi * btr_ref[...])
        k_out, k_in, k_sin, k_sout = (0, 1, 2, 4) if d == 0 else (1, 0, 3, 5)
        er, ei = powers(d, k_out)
        cq = jnp.concatenate([cr * er - ci * ei, -(cr * ei + ci * er)], axis=-1)
        er, ei = powers(d, k_in)
        bk = jnp.concatenate([bbr * er - bbi * ei, bbr * ei + bbi * er], axis=-1)
        full = lax.dot_general(bk, cq, nt_dims, precision=HIGHEST, preferred_element_type=F32)
        m = m + jnp.where((tok_out >= tok_in) if d == 0 else (tok_in >= tok_out), full, 0.0)
        er, ei = powers(d, k_sin)
        sin_ref[d] = jnp.concatenate([bbr * er - bbi * ei, bbr * ei + bbi * er], axis=-1).astype(BF16)
        er, ei = powers(d, k_sout)
        sout_ref[d] = jnp.concatenate([cr * er - ci * ei, -(cr * ei + ci * er)], axis=-1).astype(BF16)
    m_ref[...] = m.astype(BF16)


def _s5_power_table(a_re, a_im, log_dt):
    j = np.arange(S5_L, dtype=np.float32)
    half = (S5_L - 1) / 2.0
    expo = jnp.asarray(np.concatenate([j - half, half - j, S5_L - 1 - j, j, j + 1, S5_L - j]))[:, None]
    dt = jnp.exp(log_dt)[..., None, None]
    ar, ai = a_re[..., None, :] * dt, a_im[..., None, :] * dt
    mag = jnp.exp(expo * ar)
    return mag * jnp.cos(expo * ai), mag * jnp.sin(expo * ai)


def _s5_tables(a_re, a_im, log_dt, b_re, b_im, c_re, c_im):
    depth = a_re.shape[0]
    pwr, pwi = _s5_power_table(a_re, a_im, log_dt)
    dt = jnp.exp(log_dt)[..., None]
    mag = jnp.exp(a_re * dt)
    lb_re, lb_im = mag * jnp.cos(a_im * dt), mag * jnp.sin(a_im * dt)
    den = a_re * a_re + a_im * a_im
    n_re = lb_re - 1.0
    fz = jnp.stack([(n_re * a_re + lb_im * a_im) / den, (lb_im * a_re - n_re * a_im) / den], axis=3)
    mag_l = jnp.exp(S5_L * (a_re * dt))
    pl_re, pl_im = mag_l * jnp.cos(S5_L * (a_im * dt)), mag_l * jnp.sin(S5_L * (a_im * dt))
    lam = jnp.stack([jnp.concatenate([pl_re[:, 0], pl_re[:, 0]], -1), jnp.concatenate([-pl_im[:, 0], pl_im[:, 0]], -1),
                     jnp.concatenate([pl_re[:, 1], pl_re[:, 1]], -1), jnp.concatenate([-pl_im[:, 1], pl_im[:, 1]], -1)],
                    axis=2)
    per_lg = lambda *shape: pl.BlockSpec((None, None) + shape, lambda l, g: (l, g) + (0,) * len(shape))
    per_dir = lambda *shape: pl.BlockSpec((None, 2, None) + shape, lambda l, g: (l, 0, g) + (0,) * len(shape))
    m, s_in, s_out = pl.pallas_call(
        _s5_table_kernel,
        grid=(depth, S5_GROUPS),
        in_specs=[per_dir(S5_SEQS * S5_L, S5_STATE), per_dir(S5_SEQS * S5_L, S5_STATE), per_dir(2, S5_STATE),
                  per_lg(S5_CH, S5_STATE), per_lg(S5_CH, S5_STATE), per_lg(S5_CH, S5_STATE), per_lg(S5_CH, S5_STATE)],
        out_specs=[per_lg(S5_ROW, S5_ROW), per_lg(2, S5_ROW, 2 * S5_STATE), per_lg(2, S5_ROW, 2 * S5_STATE)],
        out_shape=[jax.ShapeDtypeStruct((depth, S5_GROUPS, S5_ROW, S5_ROW), BF16),
                   jax.ShapeDtypeStruct((depth, S5_GROUPS, 2, S5_ROW, 2 * S5_STATE), BF16),
                   jax.ShapeDtypeStruct((depth, S5_GROUPS, 2, S5_ROW, 2 * S5_STATE), BF16)],
        compiler_params=pltpu.CompilerParams(vmem_limit_bytes=VMEM_LIMIT),
        name="s5_tables",
    )(pwr, pwi, fz, jnp.swapaxes(b_re, -1, -2), jnp.swapaxes(b_im, -1, -2), c_re, c_im)
    return m, s_in, s_out, lam


def _s5_kernel(*refs, n, bsz, has_init):
    refs = list(refs)
    su_ref = refs.pop(0)
    m_ref = refs.pop(0)
    sin_ref = refs.pop(0)
    sout_ref = refs.pop(0)
    lam_ref = refs.pop(0)
    dv_ref = refs.pop(0)
    if has_init:
        h0_ref = refs.pop(0)
    y_ref = refs.pop(0)
    fin_ref = refs.pop(0)
    sfs, sbs, hpf, hnb = refs

    r = n * bsz
    gran = lax.broadcasted_iota(jnp.int32, (1, LANES), 1) // S5_CH
    halves_per_row = S5_ROW // LANES
    per_half = S5_L // halves_per_row

    def tok_rows(j):
        return pl.ds(j, r, stride=S5_L)

    halves = [[None] * halves_per_row for _ in range(S5_GB)]
    for j in range(S5_L):
        uj = su_ref[tok_rows(j), :]
        for gl in range(S5_GB):
            shift = ((j % per_half) - gl) % S5_GB * S5_CH
            piece = uj if shift == 0 else pltpu.roll(uj, shift, 1)
            cur = halves[gl][j // per_half]
            halves[gl][j // per_half] = jnp.where(gran == j % per_half, piece, 0.0 if cur is None else cur)
    u = jnp.stack([jnp.concatenate(h, axis=-1) for h in halves]).astype(BF16)

    bm_dims = (((2,), (1,)), ((0,), (0,)))
    bm_nt_dims = (((2,), (2,)), ((0,), (0,)))
    y = lax.dot_general(u, m_ref[...], bm_dims, preferred_element_type=F32)
    sfs[...] = lax.dot_general(u, sin_ref[:, 0], bm_dims, preferred_element_type=F32)
    sbs[...] = lax.dot_general(u, sin_ref[:, 1], bm_dims, preferred_element_type=F32)

    a_f, s_f = lam_ref[:, 0:1, :], lam_ref[:, 1:2, :]
    a_b, s_b = lam_ref[:, 2:3, :], lam_ref[:, 3:4, :]
    if has_init:
        hf = h0_ref[:, 0]
        hb = h0_ref[:, 1]
    else:
        hf = jnp.zeros((S5_GB, bsz, 2 * S5_STATE), F32)
        hb = jnp.zeros((S5_GB, bsz, 2 * S5_STATE), F32)
    swap = lambda a: pltpu.roll(a, S5_STATE, 2)
    hf_sw, hb_sw = swap(hf), swap(hb)
    for c in range(n):
        rows = pl.ds(c, bsz, stride=n)
        hpf[:, rows, :] = hf
        s_in = sfs[:, rows, :]
        hf, hf_sw = hf * a_f + hf_sw * s_f + s_in, hf_sw * a_f - hf * s_f + swap(s_in)
        rows = pl.ds(n - 1 - c, bsz, stride=n)
        hnb[:, rows, :] = hb
        s_in = sbs[:, rows, :]
        hb, hb_sw = hb * a_b + hb_sw * s_b + s_in, hb_sw * a_b - hb * s_b + swap(s_in)
    fin_ref[:, 0] = hf
    fin_ref[:, 1] = hb
    y = y + lax.dot_general(hpf[...].astype(BF16), sout_ref[:, 0], bm_nt_dims, preferred_element_type=F32)
    y = y + lax.dot_general(hnb[...].astype(BF16), sout_ref[:, 1], bm_nt_dims, preferred_element_type=F32)

    for j in range(S5_L):
        acc = None
        for gl in range(S5_GB):
            src = y[gl][:, (j // per_half) * LANES:(j // per_half + 1) * LANES]
            shift = (gl - (j % per_half)) % S5_GB * S5_CH
            piece = src if shift == 0 else pltpu.roll(src, shift, 1)
            acc = jnp.where(gran == gl, piece, 0.0 if acc is None else acc)
        y_ref[tok_rows(j), :] = acc + su_ref[tok_rows(j), :] * dv_ref[...]


def _s5_call(su, row_block, tabs, layer, dvec, bsz, t, h0):
    m, s_in, s_out, lam = tabs
    n = t // S5_L
    r = n * bsz
    rows = bsz * t
    has_init = h0 is not None
    gspec = lambda *shape: pl.BlockSpec((S5_GB,) + shape, lambda g: (g,) + (0,) * len(shape))
    lspec = lambda *shape: pl.BlockSpec((None, S5_GB) + shape, lambda g: (layer, g) + (0,) * len(shape))
    in_specs = [pl.BlockSpec((rows, LANES), lambda g: (row_block, g)),
                lspec(S5_ROW, S5_ROW), lspec(2, S5_ROW, 2 * S5_STATE), lspec(2, S5_ROW, 2 * S5_STATE),
                lspec(4, 2 * S5_STATE), pl.BlockSpec((1, LANES), lambda g: (0, g))]
    args = [su, m, s_in, s_out, lam, dvec]
    if has_init:
        in_specs.append(gspec(2, bsz, 2 * S5_STATE))
        args.append(h0)
    return pl.pallas_call(
        functools.partial(_s5_kernel, n=n, bsz=bsz, has_init=has_init),
        grid=(S5_GROUPS // S5_GB,),
        in_specs=in_specs,
        out_specs=[pl.BlockSpec((rows, LANES), lambda g: (0, g)), gspec(2, bsz, 2 * S5_STATE)],
        out_shape=[jax.ShapeDtypeStruct((rows, S5_W), F32),
                   jax.ShapeDtypeStruct((S5_GROUPS, 2, bsz, 2 * S5_STATE), F32)],
        scratch_shapes=[pltpu.VMEM((S5_GB, r, 2 * S5_STATE), F32)] * 4,
        compiler_params=pltpu.CompilerParams(vmem_limit_bytes=VMEM_LIMIT),
        name="s5_scan",
    )(*args)


def _outproj_kernel(yrg_c_ref, yrg_l_ref, ys_c_ref, ys_l_ref, xc_ref, xl_ref, mod_ref, n2_ref, wo_ref, gw_ref, gb_ref,
                    rw_ref, rb_ref, x1_ref, xs_ref, pos_ref, wt_ref, meta_ref, *, n_ctx_tiles):
    is_ctx = pl.program_id(0) < n_ctx_tiles
    x_in = jnp.where(is_ctx, xc_ref[...], xl_ref[...])
    ys = jnp.where(is_ctx, ys_c_ref[...], ys_l_ref[...])
    yrg = jnp.where(is_ctx, yrg_c_ref[...], yrg_l_ref[...])
    s = 0.5 * ys * (1.0 + jnp.tanh(np.sqrt(2.0 / np.pi).astype(np.float32) * (ys + 0.044715 * (ys * ys * ys))))
    s = s * _sigmoid(jnp.dot(s.astype(BF16), gw_ref[...], preferred_element_type=F32) + gb_ref[...])
    m = (jnp.dot(yrg.astype(BF16), wo_ref[0:2 * HG_W, :], preferred_element_type=F32)
         + jnp.dot(s.astype(BF16), wo_ref[2 * HG_W:, :], preferred_element_type=F32))
    x1 = x_in + mod_ref[2:3, :] * m
    x1_ref[...] = x1
    h = x1 * lax.rsqrt(jnp.mean(x1 * x1, axis=-1, keepdims=True) + EPS) * n2_ref[...]
    h = h * (1.0 + mod_ref[4:5, :]) + mod_ref[3:4, :]

    h_hi = h.astype(BF16)
    h_lo = (h - h_hi.astype(F32)).astype(BF16)
    pp = jnp.dot(jnp.concatenate([h_hi, h_lo], axis=0), rw_ref[...], preferred_element_type=F32)
    logits = ((pp[:TM, :LANES] + pp[TM:, :LANES]) + (pp[:TM, LANES:] + pp[TM:, LANES:])) + rb_ref[...]
    lane = lax.broadcasted_iota(jnp.int32, (TM, LANES), 1).astype(F32)
    cur = logits
    vals, idxs = [], []
    for _ in range(TOP_K):
        mx = jnp.max(cur, axis=-1, keepdims=True)
        am = jnp.min(jnp.where(cur == mx, lane, float(LANES)), axis=-1, keepdims=True)
        vals.append(mx)
        idxs.append(am)
        cur = jnp.where(lane == am, -jnp.inf, cur)
    es = [jnp.exp(v - vals[0]) for v in vals]
    den = es[0] + es[1] + es[2] + es[3]

    hit = [(lane == am) for am in idxs]
    assign = (hit[0] | hit[1] | hit[2] | hit[3]).astype(BF16)
    ti = lax.broadcasted_iota(jnp.int32, (TM, TM), 0)
    tj = lax.broadcasted_iota(jnp.int32, (TM, TM), 1)
    rank = jnp.dot((tj < ti).astype(BF16), assign, preferred_element_type=F32)
    cnt = jnp.sum(assign.astype(F32), axis=0, keepdims=True)
    cnt_al = jnp.floor((cnt + (SEG_ALIGN - 1)) * (1.0 / SEG_ALIGN)) * SEG_ALIGN
    ei = lax.broadcasted_iota(jnp.int32, (LANES, LANES), 0)
    ej = lax.broadcasted_iota(jnp.int32, (LANES, LANES), 1)
    seg = jnp.dot(jnp.broadcast_to(cnt_al, (SUBLANES, LANES)).astype(BF16), (ei < ej).astype(BF16),
                  preferred_element_type=F32)[0:1, :]
    base = seg + rank
    rlane = lax.broadcasted_iota(jnp.int32, (TM, ROWS_T), 1).astype(F32)
    klane = lax.broadcasted_iota(jnp.int32, (TM, LANES), 1)
    onehot = jnp.zeros((TM, ROWS_T), F32)
    pos_out = jnp.zeros((TM, LANES), F32)
    wt_out = jnp.zeros((TM, LANES), F32)
    for kk in range(TOP_K):
        pk = jnp.sum(jnp.where(hit[kk], base, 0.0), axis=-1, keepdims=True)
        onehot = onehot + (rlane == pk).astype(F32)
        pos_out = jnp.where(klane == kk, pk, pos_out)
        wt_out = jnp.where(klane == kk, es[kk] / den, wt_out)
    xs_ref[0] = _rows_to_chunks(lax.dot_general(onehot.astype(BF16), h.astype(BF16), (((0,), (0,)), ((), ())),
                                                preferred_element_type=F32))
    pos_ref[...] = pos_out.astype(jnp.int32)
    wt_ref[...] = wt_out
    mrow = lax.broadcasted_iota(jnp.int32, (SUBLANES, LANES), 0)
    meta = jnp.where(mrow == 0, jnp.broadcast_to(cnt_al, (SUBLANES, LANES)),
                     jnp.where(mrow == 1, jnp.broadcast_to(seg, (SUBLANES, LANES)), 0.0))
    meta_ref[0] = meta.astype(jnp.int32)


def _outproj_call(yrg_c, yrg_l, ys_c, ys_l, x_c, x_l, mod_l, n2, wo, gw, gb, rw, rb, n_ctx_tiles, tiles_per_lat):
    n = x_c.shape[0] + x_l.shape[0]
    nt = n // TM
    gid = functools.partial(_group_id, n_ctx_tiles=n_ctx_tiles, tiles_per_lat=tiles_per_lat)
    const = lambda *shape: pl.BlockSpec(shape, lambda i: (0,) * len(shape))
    ctx_map, lat_map = _path_maps(n_ctx_tiles)
    return pl.pallas_call(
        functools.partial(_outproj_kernel, n_ctx_tiles=n_ctx_tiles),
        grid=(nt,),
        in_specs=[
            pl.BlockSpec((TM, 2 * HG_W), ctx_map),
            pl.BlockSpec((TM, 2 * HG_W), lat_map),
            pl.BlockSpec((TM, S5_W), ctx_map),
            pl.BlockSpec((TM, S5_W), lat_map),
            pl.BlockSpec((TM, D_MODEL), ctx_map),
            pl.BlockSpec((TM, D_MODEL), lat_map),
            pl.BlockSpec((None, 6, D_MODEL), lambda i: (gid(i), 0, 0)),
            const(1, D_MODEL), const(D_MODEL, D_MODEL), const(S5_W, S5_W), const(1, S5_W),
            const(D_MODEL, 2 * LANES), const(1, LANES),
        ],
        out_specs=[
            pl.BlockSpec((TM, D_MODEL), lambda i: (i, 0)),
            pl.BlockSpec((1, CHUNKS_T, 2 * SEG_ALIGN, D_MODEL // 2), lambda i: (i, 0, 0, 0)),
            pl.BlockSpec((TM, LANES), lambda i: (i, 0)),
            pl.BlockSpec((TM, LANES), lambda i: (i, 0)),
            pl.BlockSpec((1, SUBLANES, LANES), lambda i: (i, 0, 0)),
        ],
        out_shape=[
            jax.ShapeDtypeStruct((n, D_MODEL), F32),
            jax.ShapeDtypeStruct((nt, CHUNKS_T, 2 * SEG_ALIGN, D_MODEL // 2), BF16),
            jax.ShapeDtypeStruct((n, LANES), jnp.int32),
            jax.ShapeDtypeStruct((n, LANES), F32),
            jax.ShapeDtypeStruct((nt, SUBLANES, LANES), jnp.int32),
        ],
        compiler_params=pltpu.CompilerParams(vmem_limit_bytes=VMEM_LIMIT),
        name="outproj_router_dispatch",
    )(yrg_c, yrg_l, ys_c, ys_l, x_c, x_l, mod_l, n2, wo, gw, gb, rw, rb)


def _expert_schedule(meta, nblk):
    nt = meta.shape[0]
    nch = (meta[:, 0, :N_EXPERTS] // SEG_ALIGN).T
    seg_chunk = (meta[:, 1, :N_EXPERTS] // SEG_ALIGN).T
    cs_end = jnp.cumsum(nch, axis=1)
    cs_start = cs_end - nch
    tot = cs_end[:, -1]
    nb = (tot + BLK_CHUNKS - 1) // BLK_CHUNKS
    blk_end = jnp.cumsum(nb)
    blk_start = blk_end - nb
    n_active = blk_end[-1]
    j = jnp.arange(nblk, dtype=jnp.int32)
    be = jnp.sum(blk_end[None, :] <= jnp.minimum(j, n_active - 1)[:, None], axis=1)
    be = jnp.clip(be, 0, N_EXPERTS - 1).astype(jnp.int32)
    own = be[:, None] == jnp.arange(N_EXPERTS, dtype=jnp.int32)[None, :]
    pick = lambda a: jnp.sum(jnp.where(own[:, :, None], a[None], 0), axis=1)
    bs = (j - jnp.sum(jnp.where(own, blk_start[None, :], 0), axis=1)) * BLK_CHUNKS
    tot_b = jnp.sum(jnp.where(own, tot[None, :], 0), axis=1)
    nvalid = jnp.where(j < n_active, jnp.clip(tot_b - bs, 0, BLK_CHUNKS), 0)
    kfirst = jnp.sum(pick(cs_end) <= bs[:, None], axis=1)
    klast = jnp.sum(pick(cs_start) < (bs + nvalid)[:, None], axis=1)
    src_base = jnp.arange(nt, dtype=jnp.int32)[None, :] * CHUNKS_T + seg_chunk
    ids = jnp.arange(N_EXPERTS, dtype=jnp.int32)
    later = (ids[None, :] > ids[:, None]) & (tot > 0)[None, :]
    nxt_e = jnp.min(jnp.where(later, ids[None, :], N_EXPERTS), axis=1)
    nxt = jnp.sum(jnp.where(own, nxt_e[None, :], 0), axis=1)
    nxt = jnp.where(nxt >= N_EXPERTS, -1, nxt)
    i32 = lambda a: a.astype(jnp.int32)
    return (be, i32(nvalid), i32(bs), i32(kfirst), i32(klast), i32(nxt),
            i32(cs_start.reshape(-1)), i32(nch.reshape(-1)), i32(src_base.reshape(-1)))


W_CAST_ROWS = 128


def _ffn_kernel(be_ref, nv_ref, bs_ref, kf_ref, kl_ref, nxt_ref, cst_ref, nch_ref, srcb_ref,
                xs_hbm, w1_hbm, b1_ref, w2_hbm, b2_ref, ys_hbm,
                xbuf, ybuf, w1s, w2s, w1b, w2b, gsem, ssem, wsem, *, nblk, nt, layer):
    j = pl.program_id(0)
    slot = j % 2

    def for_pieces(blk, fn):
        e_off = be_ref[blk] * nt
        lo_b = bs_ref[blk]
        hi_b = lo_b + nv_ref[blk]

        def body(i, carry):
            ps = cst_ref[e_off + i]
            lo = jnp.maximum(ps, lo_b)
            n = jnp.minimum(ps + nch_ref[e_off + i], hi_b) - lo

            @pl.when(n > 0)
            def _():
                fn(srcb_ref[e_off + i] + (lo - ps), lo - lo_b, n)
            return carry

        lax.fori_loop(kf_ref[blk], kl_ref[blk], body, 0)

    def start_gather(blk, s):
        for_pieces(blk, lambda src, dst, n: pltpu.make_async_copy(
            xs_hbm.at[pl.ds(src, n)], xbuf.at[s, pl.ds(dst, n)], gsem.at[s]).start())

    def start_scatter(blk, s):
        for_pieces(blk, lambda src, dst, n: pltpu.make_async_copy(
            ybuf.at[s, pl.ds(dst, n)], ys_hbm.at[pl.ds(src, n)], ssem.at[s]).start())

    def wait_gather(blk, s):
        @pl.when(nv_ref[blk] > 0)
        def _():
            pltpu.make_async_copy(xs_hbm.at[pl.ds(0, nv_ref[blk])], xbuf.at[s, pl.ds(0, nv_ref[blk])],
                                  gsem.at[s]).wait()

    def wait_scatter(blk, s):
        @pl.when(nv_ref[blk] > 0)
        def _():
            pltpu.make_async_copy(ybuf.at[s, pl.ds(0, nv_ref[blk])], ys_hbm.at[pl.ds(0, nv_ref[blk])],
                                  ssem.at[s]).wait()

    def weight_copies(e):
        return (pltpu.make_async_copy(w1_hbm.at[layer, e], w1s, wsem.at[0]),
                pltpu.make_async_copy(w2_hbm.at[layer, e], w2s, wsem.at[1]))

    @pl.when(j == 0)
    def _():
        for cp in weight_copies(be_ref[0]):
            cp.start()
        xbuf[...] = jnp.zeros_like(xbuf)
        start_gather(0, 0)

    @pl.when(j + 1 < nblk)
    def _():
        start_gather(j + 1, 1 - slot)

    @pl.when((j == 0) | (be_ref[j] != be_ref[jnp.maximum(j - 1, 0)]))
    def _():
        for cp in weight_copies(be_ref[j]):
            cp.wait()

        def cast_rows(r, carry):
            rows = pl.ds(pl.multiple_of(r * W_CAST_ROWS, W_CAST_ROWS), W_CAST_ROWS)
            w1b[rows, :] = w1s[rows, :].astype(BF16)
            w2b[rows, :] = w2s[rows, :].astype(BF16)
            return carry
        lax.fori_loop(0, D_MODEL // W_CAST_ROWS, cast_rows, 0)

        @pl.when(nxt_ref[j] >= 0)
        def _():
            for cp in weight_copies(nxt_ref[j]):
                cp.start()

    wait_gather(j, slot)

    @pl.when(j >= 2)
    def _():
        wait_scatter(j - 2, slot)

    def compute_rows(first_chunk, nchunks):
        cs = pl.ds(pl.multiple_of(first_chunk, TAIL_CHUNKS), nchunks)
        x = _chunks_to_rows(xbuf[slot, cs])
        gu = jnp.dot(x, w1b[...], preferred_element_type=F32) + b1_ref[...]
        gate = jnp.minimum(gu[:, :D_FF], SWIGLU_LIMIT)
        up = jnp.clip(gu[:, D_FF:], -SWIGLU_LIMIT, SWIGLU_LIMIT)
        act = (up + 1.0) * gate * _sigmoid(SWIGLU_ALPHA * gate)
        y = jnp.dot(act.astype(BF16), w2b[...], preferred_element_type=F32) + b2_ref[...]
        ybuf[slot, cs] = _rows_to_chunks(y)

    nv = nv_ref[j]
    n_full = (nv + (SUB_CHUNKS - TAIL_CHUNKS) - 1) // SUB_CHUNKS
    lax.fori_loop(0, n_full, lambda sb, carry: (compute_rows(sb * SUB_CHUNKS, SUB_CHUNKS), carry)[1], 0)

    @pl.when(nv > n_full * SUB_CHUNKS)
    def _():
        compute_rows(n_full * SUB_CHUNKS, TAIL_CHUNKS)

    start_scatter(j, slot)

    @pl.when(j == nblk - 1)
    def _():
        wait_scatter(j, slot)
        if nblk >= 2:
            wait_scatter(j - 1, 1 - slot)


def _ffn_call(sched, xs_chunks, w1, b1, w2, b2, layer, nt):
    assert D_FF == D_MODEL
    nblk = sched[0].shape[0]
    bmap = lambda j, be, *_: (layer, be[j], 0, 0)
    grid_spec = pltpu.PrefetchScalarGridSpec(
        num_scalar_prefetch=len(sched),
        grid=(nblk,),
        in_specs=[
            pl.BlockSpec(memory_space=pl.ANY),
            pl.BlockSpec(memory_space=pl.ANY),
            pl.BlockSpec((None, None, 1, 2 * D_FF), bmap),
            pl.BlockSpec(memory_space=pl.ANY),
            pl.BlockSpec((None, None, 1, D_MODEL), bmap),
        ],
        out_specs=pl.BlockSpec(memory_space=pl.ANY),
        scratch_shapes=[
            pltpu.VMEM((2, BLK_CHUNKS, 2 * SEG_ALIGN, D_MODEL // 2), BF16),
            pltpu.VMEM((2, BLK_CHUNKS, 2 * SEG_ALIGN, D_MODEL // 2), BF16),
            pltpu.VMEM((D_MODEL, 2 * D_FF), F32),
            pltpu.VMEM((D_FF, D_MODEL), F32),
            pltpu.VMEM((D_MODEL, 2 * D_FF), BF16),
            pltpu.VMEM((D_FF, D_MODEL), BF16),
            pltpu.SemaphoreType.DMA((2,)),
            pltpu.SemaphoreType.DMA((2,)),
            pltpu.SemaphoreType.DMA((2,)),
        ],
    )
    return pl.pallas_call(
        functools.partial(_ffn_kernel, nblk=nblk, nt=nt, layer=layer),
        grid_spec=grid_spec,
        out_shape=jax.ShapeDtypeStruct(xs_chunks.shape, BF16),
        input_output_aliases={len(sched): 0},
        compiler_params=pltpu.CompilerParams(vmem_limit_bytes=VMEM_LIMIT, dimension_semantics=("arbitrary",)),
        name="expert_ffn",
    )(*sched, xs_chunks, w1, b1, w2, b2)


def _combine_kernel(ys_ref, pos_ref, wt_ref, x1_ref, mod_ref, fg_ref, *out_refs, final, n_ctx_tiles):
    ys = _chunks_to_rows(ys_ref[0])
    rlane = lax.broadcasted_iota(jnp.int32, (TM, ROWS_T), 1)
    w = jnp.zeros((TM, ROWS_T), F32)
    for kk in range(TOP_K):
        w = w + jnp.where(rlane == pos_ref[:, kk:kk + 1], wt_ref[:, kk:kk + 1], 0.0)
    moe = jnp.dot(w.astype(BF16), ys, preferred_element_type=F32)
    y = x1_ref[...] + mod_ref[5:6, :] * moe
    if final:
        y = y * lax.rsqrt(jnp.mean(y * y, axis=-1, keepdims=True) + EPS) * fg_ref[...]
    is_ctx = pl.program_id(0) < n_ctx_tiles

    @pl.when(is_ctx)
    def _():
        out_refs[0][...] = y

    @pl.when(jnp.logical_not(is_ctx))
    def _():
        out_refs[1][...] = y


def _combine_call(ys, pos, wts, x1, mod_l, fg, n_ctx_tiles, tiles_per_lat, final):
    n = x1.shape[0]
    nt = n // TM
    gid = functools.partial(_group_id, n_ctx_tiles=n_ctx_tiles, tiles_per_lat=tiles_per_lat)
    tok = pl.BlockSpec((TM, D_MODEL), lambda i: (i, 0))
    ctx_map, lat_map = _path_maps(n_ctx_tiles)
    out_specs = [pl.BlockSpec((TM, D_MODEL), ctx_map), pl.BlockSpec((TM, D_MODEL), lat_map)]
    out_shape = [jax.ShapeDtypeStruct((n_ctx_tiles * TM, D_MODEL), F32),
                 jax.ShapeDtypeStruct((n - n_ctx_tiles * TM, D_MODEL), F32)]
    res = pl.pallas_call(
        functools.partial(_combine_kernel, final=final, n_ctx_tiles=n_ctx_tiles),
        grid=(nt,),
        in_specs=[
            pl.BlockSpec((1, CHUNKS_T, 2 * SEG_ALIGN, D_MODEL // 2), lambda i: (i, 0, 0, 0)),
            pl.BlockSpec((TM, LANES), lambda i: (i, 0)),
            pl.BlockSpec((TM, LANES), lambda i: (i, 0)),
            tok,
            pl.BlockSpec((None, 6, D_MODEL), lambda i: (gid(i), 0, 0)),
            pl.BlockSpec((1, D_MODEL), lambda i: (0, 0)),
        ],
        out_specs=out_specs,
        out_shape=out_shape,
        compiler_params=pltpu.CompilerParams(vmem_limit_bytes=VMEM_LIMIT),
        name="moe_combine",
    )(ys, pos, wts, x1, mod_l, fg)
    return res


def _rope_tables(t):
    pos = np.arange(t)
    nf = HEAD_DIM // 4
    inv = jnp.asarray(ROPE_BASE, F32) ** (-jnp.arange(nf, dtype=F32) / nf)
    ang = jnp.concatenate([jnp.asarray(pos // GRID_W, F32)[:, None] * inv,
                           jnp.asarray(pos % GRID_W, F32)[:, None] * inv], axis=-1)
    cos, sin = jnp.cos(ang), jnp.sin(ang)
    return (jnp.tile(jnp.concatenate([cos, cos], -1), (1, N_HEADS)),
            jnp.tile(jnp.concatenate([-sin, sin], -1), (1, N_HEADS)))


def _block_diag_t(s):
    eye = jnp.eye(N_HEADS, dtype=s.dtype)
    out = jnp.einsum('...hkv,hg->...hvgk', s, eye)
    return out.reshape(s.shape[:-3] + (HG_W, HG_W))


def kernel(x_prompt, x_sample, state_ret, state_gla, state_s5_re, state_s5_im, c, c_ctx, norm1_g, norm2_g, final_g, ada_w, ada_b, w_in, w_out, ret_decay, gla_w_lr, gla_b_lr, gla_norm_g, s5_a_re, s5_a_im, s5_log_dt, s5_b_re, s5_b_im, s5_c_re, s5_c_im, s5_d, s5_glu_w, s5_glu_b, router_w, router_b, moe_w1, moe_b1, moe_w2, moe_b2):
    bc, tc, d = x_prompt.shape
    bl, tl, _ = x_sample.shape
    depth = w_in.shape[0]
    nc, nl = bc * tc, bl * tl
    n = nc + nl
    assert d == D_MODEL and nc % TM == 0 and tl % TM == 0 and tc % CHUNK == 0 and tl % CHUNK == 0
    assert 1 + bl <= SUBLANES and nc % nl == 0
    n_ctx_tiles, tiles_per_lat = nc // TM, tl // TM
    nt = n // TM
    nblk = nt * CHUNKS_T // BLK_CHUNKS + N_EXPERTS

    x_c, x_l = x_prompt.reshape(nc, d), x_sample.reshape(nl, d)
    cond8 = jnp.concatenate([c_ctx[None, :], c, jnp.zeros((SUBLANES - 1 - bl, d), F32)], axis=0)
    mod = _ada_call(cond8, ada_w, ada_b).reshape(depth, SUBLANES, 6, d)
    rope = _rope_tables(tl)
    fg = final_g.reshape(1, d)
    s5_tabs = _s5_tables(s5_a_re, s5_a_im, s5_log_dt, s5_b_re, s5_b_im, s5_c_re, s5_c_im)

    ret_states, gla_states, re_states, im_states = [], [], [], []
    for l in range(depth):
        wl = w_in[l]
        w_cat = jnp.concatenate(
            [wl[:, :8 * HG_W], wl[:, 8 * HG_W + GLA_RANK:],
             jnp.pad(wl[:, 8 * HG_W:8 * HG_W + GLA_RANK], ((0, 0), (0, LANES - GLA_RANK)))], axis=1).astype(BF16)
        wlr = jnp.pad(gla_w_lr[l], ((0, 0), (0, LANES - GLA_RANK), (0, 0)))
        blr = gla_b_lr[l].reshape(2, 1, HG_W)
        pa, su = _inproj_call(x_c, x_l, mod[l], norm1_g[l].reshape(1, d), w_cat, wlr, blr, tl)

        log_gamma = jnp.log1p(-jnp.exp(ret_decay[l]))
        lgam = jnp.repeat(log_gamma, HEAD_DIM, axis=1)
        gn = jnp.tile(gla_norm_g[l], N_HEADS).reshape(1, HG_W)
        y_c, fin_ret, fin_gla = _scan_call(pa, 0, bc, tc, log_gamma, lgam, gn, None, None, True)
        st0 = jnp.stack([_block_diag_t(state_ret[:, l]), _block_diag_t(state_gla[:, l])], axis=2)
        y_l, _, _ = _scan_call(pa, nc, bl, tl, log_gamma, lgam, gn, rope, st0, False)
        ret_states.append(fin_ret)
        gla_states.append(fin_gla)

        dvec = s5_d[l].reshape(1, S5_W)
        ys_c, fin_c = _s5_call(su, 0, s5_tabs, l, dvec, bc, tc, None)
        h0 = jnp.concatenate([state_s5_re[:, l], state_s5_im[:, l]], axis=-1).transpose(2, 1, 0, 3)
        ys_l, _ = _s5_call(su, nc // nl, s5_tabs, l, dvec, bl, tl, h0)
        re_states.append(fin_c[..., :S5_STATE].transpose(2, 1, 0, 3))
        im_states.append(fin_c[..., S5_STATE:].transpose(2, 1, 0, 3))

        rw = jnp.pad(router_w[l], ((0, 0), (0, LANES - N_EXPERTS)))
        rw_hi = rw.astype(BF16)
        rw = jnp.concatenate([rw_hi, (rw - rw_hi.astype(F32)).astype(BF16)], axis=1)
        rb = jnp.concatenate([router_b[l], jnp.full((LANES - N_EXPERTS,), NEG_BIG, F32)]).reshape(1, LANES)
        x1, xs, pos, wts, meta = _outproj_call(
            y_c, y_l, ys_c, ys_l, x_c, x_l, mod[l], norm2_g[l].reshape(1, d), w_out[l].astype(BF16), s5_glu_w[l].astype(BF16),
            s5_glu_b[l].reshape(1, S5_W), rw, rb, n_ctx_tiles, tiles_per_lat)

        sched = _expert_schedule(meta, nblk)
        ys = _ffn_call(sched, xs.reshape(nt * CHUNKS_T, 2 * SEG_ALIGN, d // 2),
                       moe_w1, moe_b1.reshape(depth, N_EXPERTS, 1, 2 * D_FF),
                       moe_w2, moe_b2.reshape(depth, N_EXPERTS, 1, d), l, nt)
        res = _combine_call(ys.reshape(nt, CHUNKS_T, 2 * SEG_ALIGN, d // 2), pos, wts, x1, mod[l], fg,
                            n_ctx_tiles, tiles_per_lat, l == depth - 1)
        x_c, x_l = res
    y_prompt = x_c.reshape(bc, tc, d)
    y_sample = x_l.reshape(bl, tl, d)
    return (y_prompt, y_sample, jnp.stack(ret_states, axis=1), jnp.stack(gla_states, axis=1),
            jnp.stack(re_states, axis=1), jnp.stack(im_states, axis=1))
```

```python
import functools

import numpy as np
import jax
import jax.numpy as jnp
from jax import lax
from jax.experimental import pallas as pl
from jax.experimental.pallas import tpu as pltpu

F32 = jnp.float32
BF16 = jnp.bfloat16
HIGHEST = lax.Precision.HIGHEST

D_MODEL = 1024
GRID_W = 64
CHUNK = 128
HEAD_DIM = 64
N_HEADS = 4
HG_W = N_HEADS * HEAD_DIM
S5_W = 512
S5_CH = 16
S5_GROUPS = 32
S5_STATE = 64
S5_L = 16
S5_ROW = S5_L * S5_CH
GLA_RANK = 16
GLA_TAU = 16.0
N_EXPERTS = 32
TOP_K = 4
D_FF = 1024
SWIGLU_LIMIT = 7.0
SWIGLU_ALPHA = 1.702
ROPE_BASE = 10000.0
EPS = 1e-6

LANES = 128
SUBLANES = 8
TM = 256
TM_IN = 512
SEG_ALIGN = SUBLANES
ROWS_T = 1280
CHUNKS_T = ROWS_T // SEG_ALIGN
BLK_CHUNKS = 128
SUB_CHUNKS = 64
TAIL_CHUNKS = 32
BLK_ROWS = BLK_CHUNKS * SEG_ALIGN
NEG_BIG = -1e30
VMEM_LIMIT = 56 * 1024 * 1024

PA_W = 8 * HG_W + 2 * HG_W
W_CAT = 8 * HG_W + S5_W + LANES


def _sigmoid(x):
    return 1.0 / (1.0 + jnp.exp(-x))


def _rows_to_chunks(x):
    x3 = x.reshape(x.shape[0] // SEG_ALIGN, SEG_ALIGN, D_MODEL)
    return jnp.concatenate([x3[:, :, :D_MODEL // 2], x3[:, :, D_MODEL // 2:]], axis=1).astype(BF16)


def _chunks_to_rows(c):
    c3 = c.astype(F32)
    rows = jnp.concatenate([c3[:, :SEG_ALIGN, :], c3[:, SEG_ALIGN:, :]], axis=2)
    return rows.reshape(c.shape[0] * SEG_ALIGN, D_MODEL).astype(BF16)


def _group_id(i, n_ctx_tiles, tiles_per_lat):
    return jnp.where(i < n_ctx_tiles, 0, 1 + (i - n_ctx_tiles) // tiles_per_lat)


def _ada_kernel(c_ref, w_ref, b_ref, o_ref):
    c = c_ref[...]
    s = c * _sigmoid(c)
    o_ref[0] = jnp.dot(s, w_ref[0], precision=HIGHEST, preferred_element_type=F32) + b_ref[0]


def _ada_call(cond8, ada_w, ada_b):
    depth, d, n6 = ada_w.shape
    tn = 1024
    return pl.pallas_call(
        _ada_kernel,
        grid=(depth, n6 // tn),
        in_specs=[
            pl.BlockSpec((SUBLANES, d), lambda l, j: (0, 0)),
            pl.BlockSpec((1, d, tn), lambda l, j: (l, 0, j)),
            pl.BlockSpec((1, 1, tn), lambda l, j: (l, 0, j)),
        ],
        out_specs=pl.BlockSpec((1, SUBLANES, tn), lambda l, j: (l, 0, j)),
        out_shape=jax.ShapeDtypeStruct((depth, SUBLANES, n6), F32),
        compiler_params=pltpu.CompilerParams(vmem_limit_bytes=VMEM_LIMIT),
        name="ada_mod",
    )(cond8, ada_w, ada_b.reshape(depth, 1, n6))


def _inproj_kernel(xc_ref, xl_ref, mod_ref, n1_ref, w_ref, wlr_ref, blr_ref, pa_ref, su_ref, *, n_ctx_tiles):
    x = jnp.where(pl.program_id(0) < n_ctx_tiles, xc_ref[...], xl_ref[...])
    h = x * lax.rsqrt(jnp.mean(x * x, axis=-1, keepdims=True) + EPS) * n1_ref[...]
    h = h * (1.0 + mod_ref[1:2, :]) + mod_ref[0:1, :]
    r = jnp.dot(h.astype(BF16), w_ref[...], preferred_element_type=F32)
    pa_ref[:, : 8 * HG_W] = r[:, : 8 * HG_W]
    su_ref[...] = r[:, 8 * HG_W: 8 * HG_W + S5_W]
    glr = r[:, 8 * HG_W + S5_W:]
    for d in range(2):
        z = jnp.dot(glr, wlr_ref[d], precision=HIGHEST, preferred_element_type=F32) + blr_ref[d]
        log_sig = jnp.minimum(z, 0.0) - jnp.log(1.0 + jnp.exp(-jnp.abs(z)))
        pa_ref[:, (8 + d) * HG_W: (9 + d) * HG_W] = log_sig * (1.0 / GLA_TAU)


def _path_maps(n_ctx_tiles):
    return (lambda i: (jnp.minimum(i, n_ctx_tiles - 1), 0)), (lambda i: (jnp.maximum(i - n_ctx_tiles, 0), 0))


def _inproj_call(x_c, x_l, mod_l, n1, w_cat, wlr, blr, lat_seq):
    n = x_c.shape[0] + x_l.shape[0]
    assert x_c.shape[0] % TM_IN == 0 and lat_seq % TM_IN == 0
    n_ctx_tiles, tiles_per_lat = x_c.shape[0] // TM_IN, lat_seq // TM_IN
    gid = functools.partial(_group_id, n_ctx_tiles=n_ctx_tiles, tiles_per_lat=tiles_per_lat)
    ctx_map, lat_map = _path_maps(n_ctx_tiles)
    return pl.pallas_call(
        functools.partial(_inproj_kernel, n_ctx_tiles=n_ctx_tiles),
        grid=(n // TM_IN,),
        in_specs=[
            pl.BlockSpec((TM_IN, D_MODEL), ctx_map),
            pl.BlockSpec((TM_IN, D_MODEL), lat_map),
            pl.BlockSpec((None, 6, D_MODEL), lambda i: (gid(i), 0, 0)),
            pl.BlockSpec((1, D_MODEL), lambda i: (0, 0)),
            pl.BlockSpec((D_MODEL, W_CAT), lambda i: (0, 0)),
            pl.BlockSpec((2, LANES, HG_W), lambda i: (0, 0, 0)),
            pl.BlockSpec((2, 1, HG_W), lambda i: (0, 0, 0)),
        ],
        out_specs=[
            pl.BlockSpec((TM_IN, PA_W), lambda i: (i, 0)),
            pl.BlockSpec((TM_IN, S5_W), lambda i: (i, 0)),
        ],
        out_shape=[
            jax.ShapeDtypeStruct((n, PA_W), F32),
            jax.ShapeDtypeStruct((n, S5_W), F32),
        ],
        compiler_params=pltpu.CompilerParams(vmem_limit_bytes=VMEM_LIMIT),
        name="norm1_inproj",
    )(x_c, x_l, mod_l, n1, w_cat, wlr, blr)


SCAN_PB = 2


def _scan_kernel(*refs, n, use_rope, has_init, want_final):
    refs = list(refs)
    lg_ref = refs.pop(0)
    pa_ref = refs.pop(0)
    if use_rope:
        cos_ref = refs.pop(0)
        sin_ref = refs.pop(0)
    lgam_ref = refs.pop(0)
    gn_ref = refs.pop(0)
    if has_init:
        st0_ref = refs.pop(0)
    y_ref = refs.pop(0)
    if want_final:
        fin_refs = [refs.pop(0), refs.pop(0)]
    st_scr = refs.pop(0)
    of_scr = refs.pop(0)

    d = pl.program_id(1)
    c = pl.program_id(2)
    cc = jnp.where(d == 0, c, n - 1 - c)

    @pl.when(c == 0)
    def _():
        if has_init:
            st_scr[...] = st0_ref[...]
        else:
            st_scr[...] = jnp.zeros_like(st_scr)

    ii = lax.broadcasted_iota(jnp.int32, (CHUNK, CHUNK), 0)
    jj = lax.broadcasted_iota(jnp.int32, (CHUNK, CHUNK), 1)
    rel = (1 - 2 * d) * (ii - jj)
    mask = rel >= 0
    i4 = lax.broadcasted_iota(jnp.int32, (CHUNK, N_HEADS * CHUNK), 0)
    j4 = lax.broadcasted_iota(jnp.int32, (CHUNK, N_HEADS * CHUNK), 1) % CHUNK
    mask4 = (1 - 2 * d) * (i4 - j4) >= 0
    relf = rel.astype(F32)
    row = lax.broadcasted_iota(jnp.int32, (CHUNK, 1), 0)
    p1 = jnp.where(d == 0, row + 1, CHUNK - row).astype(F32)
    lane = lax.broadcasted_iota(jnp.int32, (1, HG_W), 1)
    head_mask = [(lane // HEAD_DIM == h).astype(F32) for h in range(N_HEADS)]
    bi = lax.broadcasted_iota(jnp.int32, (HG_W, HG_W), 0) // HEAD_DIM
    bj = lax.broadcasted_iota(jnp.int32, (HG_W, HG_W), 1) // HEAD_DIM
    block_diag = bi == bj
    tri = mask.astype(BF16)
    lgl = lgam_ref[pl.ds(d, 1), :]
    ret_q_decay = jnp.exp(p1 * lgl)
    ret_k_decay = jnp.exp((CHUNK - p1) * lgl)
    ret_chunk_decay = jnp.exp(CHUNK * lgl)
    ret_weight = jnp.concatenate([jnp.where(mask, jnp.exp(relf * lg_ref[d, h]), 0.0) for h in range(N_HEADS)], axis=1)

    nt_dims = (((1,), (1,)), ((), ()))
    tn_dims = (((0,), (0,)), ((), ()))

    def split3(x):
        hi = x.astype(BF16)
        r1 = x - hi.astype(F32)
        mid = r1.astype(BF16)
        return hi, mid, (r1 - mid.astype(F32)).astype(BF16)

    def group(q, k, v, q_dec, k_upd, chunk_decay, score_weight, bb, g_idx):
        vb = v.astype(BF16)
        ks = jnp.concatenate([(k * head_mask[h]).astype(BF16) for h in range(N_HEADS)], axis=0)
        vs = jnp.concatenate([(v * head_mask[h]).astype(BF16) for h in range(N_HEADS)], axis=0)
        s = lax.dot_general(q.astype(BF16), ks, nt_dims, preferred_element_type=F32)
        s = jnp.where(mask4, s, 0.0) if score_weight is None else s * score_weight
        o = jnp.dot(s.astype(BF16), vs, preferred_element_type=F32)
        st = st_scr[bb, g_idx]
        o = o + lax.dot_general(q_dec.astype(BF16), st.astype(BF16), nt_dims, preferred_element_type=F32)
        upd = lax.dot_general(vb, k_upd.astype(BF16), tn_dims, preferred_element_type=F32)
        st_scr[bb, g_idx] = st * chunk_decay + jnp.where(block_diag, upd, 0.0)
        return o

    def rope(x):
        partner = jnp.where((lane % HEAD_DIM) < HEAD_DIM // 2,
                            pltpu.roll(x, HG_W - HEAD_DIM // 2, 1), pltpu.roll(x, HEAD_DIM // 2, 1))
        return x * cos_ref[...] + partner * sin_ref[...]

    def one_row(bb):
        q = pa_ref[bb, :, 0:HG_W]
        k = pa_ref[bb, :, HG_W:2 * HG_W] * (HEAD_DIM ** -0.5)
        v = pa_ref[bb, :, 2 * HG_W:3 * HG_W]
        if use_rope:
            q = rope(q)
            k = rope(k)
        o_ret = group(q, k, v, q * ret_q_decay, k * ret_k_decay, ret_chunk_decay, ret_weight, bb, 0)

        q = pa_ref[bb, :, 4 * HG_W:5 * HG_W] * (HEAD_DIM ** -0.5)
        k = pa_ref[bb, :, 5 * HG_W:6 * HG_W]
        v = pa_ref[bb, :, 6 * HG_W:7 * HG_W]
        la = jnp.where(d == 0, pa_ref[bb, :, 8 * HG_W:9 * HG_W], pa_ref[bb, :, 9 * HG_W:10 * HG_W])
        b = sum(jnp.dot(tri, part, preferred_element_type=F32) for part in split3(la))
        b_last = jnp.sum(la, axis=0, keepdims=True)
        q_dec = q * jnp.exp(b)
        o_gla = group(q_dec, k * jnp.exp(-b), v, q_dec, k * jnp.exp(b_last - b), jnp.exp(b_last), None, bb, 1)
        return o_ret, o_gla

    outs = [one_row(bb) for bb in range(SCAN_PB)]

    @pl.when(d == 0)
    def _():
        for bb, (o_ret, o_gla) in enumerate(outs):
            of_scr[c, bb, :, 0:HG_W] = o_ret
            of_scr[c, bb, :, HG_W:2 * HG_W] = o_gla

    @pl.when(d == 1)
    def _():
        mean_mat = jnp.where(block_diag, 1.0 / HEAD_DIM, 0.0).astype(BF16)

        def head_rms(o):
            parts = jnp.concatenate(split3(o * o), axis=0)
            ms3 = jnp.dot(parts, mean_mat, preferred_element_type=F32)
            ms = ms3[0:CHUNK] + ms3[CHUNK:2 * CHUNK] + ms3[2 * CHUNK:3 * CHUNK]
            return o * lax.rsqrt(ms + EPS)

        for bb, (o_ret, o_gla) in enumerate(outs):
            o1 = o_ret + of_scr[cc, bb, :, 0:HG_W]
            g1 = pa_ref[bb, :, 3 * HG_W:4 * HG_W]
            y_ref[bb, :, 0:HG_W] = head_rms(o1) * (g1 * _sigmoid(g1))
            o2 = o_gla + of_scr[cc, bb, :, HG_W:2 * HG_W]
            g2 = pa_ref[bb, :, 7 * HG_W:8 * HG_W]
            y_ref[bb, :, HG_W:2 * HG_W] = head_rms(o2) * gn_ref[...] * (g2 * _sigmoid(g2))

    if want_final:
        @pl.when(c == n - 1)
        def _():
            for bb in range(SCAN_PB):
                for g_idx in range(2):
                    st_t = st_scr[bb, g_idx].T
                    for h in range(N_HEADS):
                        hs = slice(h * HEAD_DIM, (h + 1) * HEAD_DIM)
                        fin_refs[g_idx][bb, h] = st_t[hs, hs]


def _scan_call(pa, first_row, bsz, t, lg, lgam, gn, rope, st0, want_final):
    n = t // CHUNK
    use_rope = rope is not None
    has_init = st0 is not None
    pair_rows = SCAN_PB * t
    assert bsz % SCAN_PB == 0 and first_row % pair_rows == 0 and pa.shape[0] % pair_rows == 0
    pair_off = first_row // pair_rows
    pa4 = pa.reshape(pa.shape[0] // pair_rows, SCAN_PB, t, PA_W)

    def chunk_idx(d, c):
        return jnp.where(d == 0, c, n - 1 - c)

    in_specs = [
        pl.BlockSpec(memory_space=pltpu.SMEM),
        pl.BlockSpec((None, SCAN_PB, CHUNK, PA_W), lambda b, d, c: (pair_off + b, 0, chunk_idx(d, c), 0)),
    ]
    args = [lg, pa4]
    if use_rope:
        in_specs += [pl.BlockSpec((CHUNK, HG_W), lambda b, d, c: (chunk_idx(d, c), 0))] * 2
        args += list(rope)
    in_specs += [pl.BlockSpec((2, HG_W), lambda b, d, c: (0, 0)),
                 pl.BlockSpec((1, HG_W), lambda b, d, c: (0, 0))]
    args += [lgam, gn]
    state_spec = pl.BlockSpec((SCAN_PB, None, 2, HG_W, HG_W), lambda b, d, c: (b, d, 0, 0, 0))
    if has_init:
        in_specs.append(state_spec)
        args.append(st0)
    out_specs = [pl.BlockSpec((None, SCAN_PB, CHUNK, 2 * HG_W),
                              lambda b, d, c: (b, 0, jnp.where(d == 0, n - 1, n - 1 - c), 0))]
    out_shape = [jax.ShapeDtypeStruct((bsz // SCAN_PB, SCAN_PB, t, 2 * HG_W), F32)]
    if want_final:
        fin_spec = pl.BlockSpec((SCAN_PB, None, N_HEADS, HEAD_DIM, HEAD_DIM), lambda b, d, c: (b, d, 0, 0, 0))
        out_specs += [fin_spec, fin_spec]
        out_shape += [jax.ShapeDtypeStruct((bsz, 2, N_HEADS, HEAD_DIM, HEAD_DIM), F32)] * 2
    res = pl.pallas_call(
        functools.partial(_scan_kernel, n=n, use_rope=use_rope, has_init=has_init, want_final=want_final),
        grid=(bsz // SCAN_PB, 2, n),
        in_specs=in_specs,
        out_specs=out_specs,
        out_shape=out_shape,
        scratch_shapes=[pltpu.VMEM((SCAN_PB, 2, HG_W, HG_W), F32), pltpu.VMEM((n, SCAN_PB, CHUNK, 2 * HG_W), F32)],
        compiler_params=pltpu.CompilerParams(vmem_limit_bytes=VMEM_LIMIT),
        name="ret_gla_scan",
    )(*args)
    y = res[0].reshape(bsz * t, 2 * HG_W)
    return (y, res[1], res[2]) if want_final else (y, None, None)


S5_GB = LANES // S5_CH
S5_NEXP = 2 * S5_L + 1


def _s5_table_kernel(pwr_ref, pwi_ref, fz_ref, btr_ref, bti_ref, cr_ref, ci_ref, m_ref, sin_ref, sout_ref):
    def per_token(t, rows):
        return jnp.concatenate([jnp.broadcast_to(t[i:i + 1, :], (S5_CH, S5_STATE)) for i in rows], axis=0)

    def per_channel(t):
        return jnp.concatenate([t] * S5_L, axis=0)

    pos = range(S5_L)
    seqs = ([j for j in pos], [S5_L - 1 - j for j in pos], [2 * S5_L - 1 - j for j in pos],
            [S5_L + j for j in pos], [S5_L + j + 1 for j in pos], [2 * S5_L - j for j in pos])

    def powers(d, k):
        return per_token(pwr_ref[d], seqs[k]), per_token(pwi_ref[d], seqs[k])

    cr, ci = per_channel(cr_ref[...]), per_channel(ci_ref[...])
    tok_in = lax.broadcasted_iota(jnp.int32, (S5_ROW, S5_ROW), 0) // S5_CH
    tok_out = lax.broadcasted_iota(jnp.int32, (S5_ROW, S5_ROW), 1) // S5_CH
    nt_dims = (((1,), (1,)), ((), ()))
    m = jnp.zeros((S5_ROW, S5_ROW), F32)
    for d in range(2):
        fr, fi = fz_ref[d, 0:1, :], fz_ref[d, 1:2, :]
        bbr = per_channel(fr * btr_ref[...] - fi * bti_ref[...])
        bbi = per_channel(fr * bti_ref[...] + fi * btr_ref[...])
        k_out, k_in, k_sin, k_sout = (0, 1, 2, 4) if d == 0 else (1, 0, 3, 5)
        er, ei = powers(d, k_out)
        cq = jnp.concatenate([cr * er - ci * ei, -(cr * ei + ci * er)], axis=-1)
        er, ei = powers(d, k_in)
        bk = jnp.concatenate([bbr * er - bbi * ei, bbr * ei + bbi * er], axis=-1)
        full = lax.dot_general(bk, cq, nt_dims, precision=HIGHEST, preferred_element_type=F32)
        m = m + jnp.where((tok_out >= tok_in) if d == 0 else (tok_in >= tok_out), full, 0.0)
        er, ei = powers(d, k_sin)
        sin_ref[d] = jnp.concatenate([bbr * er - bbi * ei, bbr * ei + bbi * er], axis=-1).astype(BF16)
        er, ei = powers(d, k_sout)
        sout_ref[d] = jnp.concatenate([cr * er - ci * ei, -(cr * ei + ci * er)], axis=-1).astype(BF16)
    m_ref[...] = m.astype(BF16)


def _s5_power_table(a_re, a_im, log_dt):
    expo = np.concatenate([np.arange(S5_L, dtype=np.float32) - (S5_L - 1) / 2.0, np.arange(S5_L + 1, dtype=np.float32)])
    expo = jnp.asarray(expo)[:, None]
    dt = jnp.exp(log_dt)[..., None, None]
    ar, ai = a_re[..., None, :] * dt, a_im[..., None, :] * dt
    mag = jnp.exp(expo * ar)
    return mag * jnp.cos(expo * ai), mag * jnp.sin(expo * ai)


def _s5_tables(a_re, a_im, log_dt, b_re, b_im, c_re, c_im):
    depth = a_re.shape[0]
    pwr, pwi = _s5_power_table(a_re, a_im, log_dt)
    dt = jnp.exp(log_dt)[..., None]
    mag = jnp.exp(a_re * dt)
    lb_re, lb_im = mag * jnp.cos(a_im * dt), mag * jnp.sin(a_im * dt)
    den = a_re * a_re + a_im * a_im
    n_re = lb_re - 1.0
    fz = jnp.stack([(n_re * a_re + lb_im * a_im) / den, (lb_im * a_re - n_re * a_im) / den], axis=3)
    mag_l = jnp.exp(S5_L * (a_re * dt))
    pl_re, pl_im = mag_l * jnp.cos(S5_L * (a_im * dt)), mag_l * jnp.sin(S5_L * (a_im * dt))
    lam = jnp.stack([jnp.concatenate([pl_re[:, 0], pl_re[:, 0]], -1), jnp.concatenate([-pl_im[:, 0], pl_im[:, 0]], -1),
                     jnp.concatenate([pl_re[:, 1], pl_re[:, 1]], -1), jnp.concatenate([-pl_im[:, 1], pl_im[:, 1]], -1)],
                    axis=2)
    per_lg = lambda *shape: pl.BlockSpec((None, None) + shape, lambda l, g: (l, g) + (0,) * len(shape))
    per_dir = lambda *shape: pl.BlockSpec((None, 2, None) + shape, lambda l, g: (l, 0, g) + (0,) * len(shape))
    m, s_in, s_out = pl.pallas_call(
        _s5_table_kernel,
        grid=(depth, S5_GROUPS),
        in_specs=[per_dir(S5_NEXP, S5_STATE), per_dir(S5_NEXP, S5_STATE), per_dir(2, S5_STATE),
                  per_lg(S5_CH, S5_STATE), per_lg(S5_CH, S5_STATE), per_lg(S5_CH, S5_STATE), per_lg(S5_CH, S5_STATE)],
        out_specs=[per_lg(S5_ROW, S5_ROW), per_lg(2, S5_ROW, 2 * S5_STATE), per_lg(2, S5_ROW, 2 * S5_STATE)],
        out_shape=[jax.ShapeDtypeStruct((depth, S5_GROUPS, S5_ROW, S5_ROW), BF16),
                   jax.ShapeDtypeStruct((depth, S5_GROUPS, 2, S5_ROW, 2 * S5_STATE), BF16),
                   jax.ShapeDtypeStruct((depth, S5_GROUPS, 2, S5_ROW, 2 * S5_STATE), BF16)],
        compiler_params=pltpu.CompilerParams(vmem_limit_bytes=VMEM_LIMIT),
        name="s5_tables",
    )(pwr, pwi, fz, jnp.swapaxes(b_re, -1, -2), jnp.swapaxes(b_im, -1, -2), c_re, c_im)
    return m, s_in, s_out, lam


def _s5_kernel(*refs, n, bsz, has_init):
    refs = list(refs)
    su_ref = refs.pop(0)
    m_ref = refs.pop(0)
    sin_ref = refs.pop(0)
    sout_ref = refs.pop(0)
    lam_ref = refs.pop(0)
    dv_ref = refs.pop(0)
    if has_init:
        h0_ref = refs.pop(0)
    y_ref = refs.pop(0)
    fin_ref = refs.pop(0)
    sfs, sbs, hpf, hnb = refs

    r = n * bsz
    gran = lax.broadcasted_iota(jnp.int32, (1, LANES), 1) // S5_CH
    halves_per_row = S5_ROW // LANES
    per_half = S5_L // halves_per_row

    def tok_rows(j):
        return pl.ds(j, r, stride=S5_L)

    def granule_transpose(arrs):
        arrs = list(arrs)
        bit = S5_GB // 2
        while bit:
            upper = (gran & bit) != 0
            for i in range(S5_GB):
                if i & bit:
                    continue
                lo, hi = arrs[i], arrs[i | bit]
                arrs[i] = jnp.where(upper, pltpu.roll(hi, bit * S5_CH, 1), lo)
                arrs[i | bit] = jnp.where(upper, hi, pltpu.roll(lo, LANES - bit * S5_CH, 1))
            bit //= 2
        return arrs

    halves = [granule_transpose([su_ref[tok_rows(hh * per_half + i), :] for i in range(per_half)])
              for hh in range(halves_per_row)]
    u = jnp.stack([jnp.concatenate([halves[hh][gl] for hh in range(halves_per_row)], axis=-1)
                   for gl in range(S5_GB)]).astype(BF16)

    bm_dims = (((2,), (1,)), ((0,), (0,)))
    bm_nt_dims = (((2,), (2,)), ((0,), (0,)))
    y = lax.dot_general(u, m_ref[...], bm_dims, preferred_element_type=F32)
    sfs[...] = lax.dot_general(u, sin_ref[:, 0], bm_dims, preferred_element_type=F32)
    sbs[...] = lax.dot_general(u, sin_ref[:, 1], bm_dims, preferred_element_type=F32)

    a_f, s_f = lam_ref[:, 0:1, :], lam_ref[:, 1:2, :]
    a_b, s_b = lam_ref[:, 2:3, :], lam_ref[:, 3:4, :]
    if has_init:
        hf = h0_ref[:, 0]
        hb = h0_ref[:, 1]
    else:
        hf = jnp.zeros((S5_GB, bsz, 2 * S5_STATE), F32)
        hb = jnp.zeros((S5_GB, bsz, 2 * S5_STATE), F32)
    swap = lambda a: pltpu.roll(a, S5_STATE, 2)
    hf_sw, hb_sw = swap(hf), swap(hb)
    for c in range(n):
        rows = pl.ds(c, bsz, stride=n)
        hpf[:, rows, :] = hf
        s_in = sfs[:, rows, :]
        hf, hf_sw = hf * a_f + hf_sw * s_f + s_in, hf_sw * a_f - hf * s_f + swap(s_in)
        rows = pl.ds(n - 1 - c, bsz, stride=n)
        hnb[:, rows, :] = hb
        s_in = sbs[:, rows, :]
        hb, hb_sw = hb * a_b + hb_sw * s_b + s_in, hb_sw * a_b - hb * s_b + swap(s_in)
    fin_ref[:, 0] = hf
    fin_ref[:, 1] = hb
    y = y + lax.dot_general(hpf[...].astype(BF16), sout_ref[:, 0], bm_nt_dims, preferred_element_type=F32)
    y = y + lax.dot_general(hnb[...].astype(BF16), sout_ref[:, 1], bm_nt_dims, preferred_element_type=F32)

    for hh in range(halves_per_row):
        by_token = granule_transpose([y[gl][:, hh * LANES:(hh + 1) * LANES] for gl in range(S5_GB)])
        for i in range(per_half):
            rows = tok_rows(hh * per_half + i)
            y_ref[rows, :] = by_token[i] + su_ref[rows, :] * dv_ref[...]


def _s5_call(su, row_block, tabs, layer, dvec, bsz, t, h0):
    m, s_in, s_out, lam = tabs
    n = t // S5_L
    r = n * bsz
    rows = bsz * t
    has_init = h0 is not None
    gspec = lambda *shape: pl.BlockSpec((S5_GB,) + shape, lambda g: (g,) + (0,) * len(shape))
    lspec = lambda *shape: pl.BlockSpec((None, S5_GB) + shape, lambda g: (layer, g) + (0,) * len(shape))
    in_specs = [pl.BlockSpec((rows, LANES), lambda g: (row_block, g)),
                lspec(S5_ROW, S5_ROW), lspec(2, S5_ROW, 2 * S5_STATE), lspec(2, S5_ROW, 2 * S5_STATE),
                lspec(4, 2 * S5_STATE), pl.BlockSpec((1, LANES), lambda g: (0, g))]
    args = [su, m, s_in, s_out, lam, dvec]
    if has_init:
        in_specs.append(gspec(2, bsz, 2 * S5_STATE))
        args.append(h0)
    return pl.pallas_call(
        functools.partial(_s5_kernel, n=n, bsz=bsz, has_init=has_init),
        grid=(S5_GROUPS // S5_GB,),
        in_specs=in_specs,
        out_specs=[pl.BlockSpec((rows, LANES), lambda g: (0, g)), gspec(2, bsz, 2 * S5_STATE)],
        out_shape=[jax.ShapeDtypeStruct((rows, S5_W), F32),
                   jax.ShapeDtypeStruct((S5_GROUPS, 2, bsz, 2 * S5_STATE), F32)],
        scratch_shapes=[pltpu.VMEM((S5_GB, r, 2 * S5_STATE), F32)] * 4,
        compiler_params=pltpu.CompilerParams(vmem_limit_bytes=VMEM_LIMIT),
        name="s5_scan",
    )(*args)


def _outproj_kernel(yrg_c_ref, yrg_l_ref, ys_c_ref, ys_l_ref, xc_ref, xl_ref, mod_ref, n2_ref, wo_ref, gw_ref, gb_ref,
                    rw_ref, rb_ref, x1_ref, xs_ref, pos_ref, wt_ref, meta_ref, *, n_ctx_tiles):
    is_ctx = pl.program_id(0) < n_ctx_tiles
    x_in = jnp.where(is_ctx, xc_ref[...], xl_ref[...])
    ys = jnp.where(is_ctx, ys_c_ref[...], ys_l_ref[...])
    yrg = jnp.where(is_ctx, yrg_c_ref[...], yrg_l_ref[...])
    s = 0.5 * ys * (1.0 + jnp.tanh(np.sqrt(2.0 / np.pi).astype(np.float32) * (ys + 0.044715 * (ys * ys * ys))))
    s = s * _sigmoid(jnp.dot(s.astype(BF16), gw_ref[...], preferred_element_type=F32) + gb_ref[...])
    m = (jnp.dot(yrg.astype(BF16), wo_ref[0:2 * HG_W, :], preferred_element_type=F32)
         + jnp.dot(s.astype(BF16), wo_ref[2 * HG_W:, :], preferred_element_type=F32))
    x1 = x_in + mod_ref[2:3, :] * m
    x1_ref[...] = x1
    h = x1 * lax.rsqrt(jnp.mean(x1 * x1, axis=-1, keepdims=True) + EPS) * n2_ref[...]
    h = h * (1.0 + mod_ref[4:5, :]) + mod_ref[3:4, :]

    h_hi = h.astype(BF16)
    h_lo = (h - h_hi.astype(F32)).astype(BF16)
    pp = jnp.dot(jnp.concatenate([h_hi, h_lo], axis=0), rw_ref[...], preferred_element_type=F32)
    logits = ((pp[:TM, :LANES] + pp[TM:, :LANES]) + (pp[:TM, LANES:] + pp[TM:, LANES:])) + rb_ref[...]
    lane = lax.broadcasted_iota(jnp.int32, (TM, LANES), 1).astype(F32)
    cur = logits
    vals, idxs = [], []
    for _ in range(TOP_K):
        mx = jnp.max(cur, axis=-1, keepdims=True)
        am = jnp.min(jnp.where(cur == mx, lane, float(LANES)), axis=-1, keepdims=True)
        vals.append(mx)
        idxs.append(am)
        cur = jnp.where(lane == am, -jnp.inf, cur)
    es = [jnp.exp(v - vals[0]) for v in vals]
    den = es[0] + es[1] + es[2] + es[3]

    hit = [(lane == am) for am in idxs]
    assign = (hit[0] | hit[1] | hit[2] | hit[3]).astype(BF16)
    ti = lax.broadcasted_iota(jnp.int32, (TM, TM), 0)
    tj = lax.broadcasted_iota(jnp.int32, (TM, TM), 1)
    rank = jnp.dot((tj < ti).astype(BF16), assign, preferred_element_type=F32)
    cnt = jnp.sum(assign.astype(F32), axis=0, keepdims=True)
    cnt_al = jnp.floor((cnt + (SEG_ALIGN - 1)) * (1.0 / SEG_ALIGN)) * SEG_ALIGN
    ei = lax.broadcasted_iota(jnp.int32, (LANES, LANES), 0)
    ej = lax.broadcasted_iota(jnp.int32, (LANES, LANES), 1)
    seg = jnp.dot(jnp.broadcast_to(cnt_al, (SUBLANES, LANES)).astype(BF16), (ei < ej).astype(BF16),
                  preferred_element_type=F32)[0:1, :]
    base = seg + rank
    rlane = lax.broadcasted_iota(jnp.int32, (TM, ROWS_T), 1).astype(F32)
    klane = lax.broadcasted_iota(jnp.int32, (TM, LANES), 1)
    onehot = jnp.zeros((TM, ROWS_T), F32)
    pos_out = jnp.zeros((TM, LANES), F32)
    wt_out = jnp.zeros((TM, LANES), F32)
    for kk in range(TOP_K):
        pk = jnp.sum(jnp.where(hit[kk], base, 0.0), axis=-1, keepdims=True)
        onehot = onehot + (rlane == pk).astype(F32)
        pos_out = jnp.where(klane == kk, pk, pos_out)
        wt_out = jnp.where(klane == kk, es[kk] / den, wt_out)
    xs_ref[0] = _rows_to_chunks(lax.dot_general(onehot.astype(BF16), h.astype(BF16), (((0,), (0,)), ((), ())),
                                                preferred_element_type=F32))
    pos_ref[...] = pos_out.astype(jnp.int32)
    wt_ref[...] = wt_out
    mrow = lax.broadcasted_iota(jnp.int32, (SUBLANES, LANES), 0)
    meta = jnp.where(mrow == 0, jnp.broadcast_to(cnt_al, (SUBLANES, LANES)),
                     jnp.where(mrow == 1, jnp.broadcast_to(seg, (SUBLANES, LANES)), 0.0))
    meta_ref[0] = meta.astype(jnp.int32)


def _outproj_call(yrg_c, yrg_l, ys_c, ys_l, x_c, x_l, mod_l, n2, wo, gw, gb, rw, rb, n_ctx_tiles, tiles_per_lat):
    n = x_c.shape[0] + x_l.shape[0]
    nt = n // TM
    gid = functools.partial(_group_id, n_ctx_tiles=n_ctx_tiles, tiles_per_lat=tiles_per_lat)
    const = lambda *shape: pl.BlockSpec(shape, lambda i: (0,) * len(shape))
    ctx_map, lat_map = _path_maps(n_ctx_tiles)
    return pl.pallas_call(
        functools.partial(_outproj_kernel, n_ctx_tiles=n_ctx_tiles),
        grid=(nt,),
        in_specs=[
            pl.BlockSpec((TM, 2 * HG_W), ctx_map),
            pl.BlockSpec((TM, 2 * HG_W), lat_map),
            pl.BlockSpec((TM, S5_W), ctx_map),
            pl.BlockSpec((TM, S5_W), lat_map),
            pl.BlockSpec((TM, D_MODEL), ctx_map),
            pl.BlockSpec((TM, D_MODEL), lat_map),
            pl.BlockSpec((None, 6, D_MODEL), lambda i: (gid(i), 0, 0)),
            const(1, D_MODEL), const(D_MODEL, D_MODEL), const(S5_W, S5_W), const(1, S5_W),
            const(D_MODEL, 2 * LANES), const(1, LANES),
        ],
        out_specs=[
            pl.BlockSpec((TM, D_MODEL), lambda i: (i, 0)),
            pl.BlockSpec((1, CHUNKS_T, 2 * SEG_ALIGN, D_MODEL // 2), lambda i: (i, 0, 0, 0)),
            pl.BlockSpec((TM, LANES), lambda i: (i, 0)),
            pl.BlockSpec((TM, LANES), lambda i: (i, 0)),
            pl.BlockSpec((1, SUBLANES, LANES), lambda i: (i, 0, 0)),
        ],
        out_shape=[
            jax.ShapeDtypeStruct((n, D_MODEL), F32),
            jax.ShapeDtypeStruct((nt, CHUNKS_T, 2 * SEG_ALIGN, D_MODEL // 2), BF16),
            jax.ShapeDtypeStruct((n, LANES), jnp.int32),
            jax.ShapeDtypeStruct((n, LANES), F32),
            jax.ShapeDtypeStruct((nt, SUBLANES, LANES), jnp.int32),
        ],
        compiler_params=pltpu.CompilerParams(vmem_limit_bytes=VMEM_LIMIT),
        name="outproj_router_dispatch",
    )(yrg_c, yrg_l, ys_c, ys_l, x_c, x_l, mod_l, n2, wo, gw, gb, rw, rb)


def _expert_schedule(meta, nblk):
    nt = meta.shape[0]
    nch = (meta[:, 0, :N_EXPERTS] // SEG_ALIGN).T
    seg_chunk = (meta[:, 1, :N_EXPERTS] // SEG_ALIGN).T
    cs_end = jnp.cumsum(nch, axis=1)
    cs_start = cs_end - nch
    tot = cs_end[:, -1]
    nb = (tot + BLK_CHUNKS - 1) // BLK_CHUNKS
    blk_end = jnp.cumsum(nb)
    blk_start = blk_end - nb
    n_active = blk_end[-1]
    j = jnp.arange(nblk, dtype=jnp.int32)
    be = jnp.sum(blk_end[None, :] <= jnp.minimum(j, n_active - 1)[:, None], axis=1)
    be = jnp.clip(be, 0, N_EXPERTS - 1).astype(jnp.int32)
    own = be[:, None] == jnp.arange(N_EXPERTS, dtype=jnp.int32)[None, :]
    pick = lambda a: jnp.sum(jnp.where(own[:, :, None], a[None], 0), axis=1)
    bs = (j - jnp.sum(jnp.where(own, blk_start[None, :], 0), axis=1)) * BLK_CHUNKS
    tot_b = jnp.sum(jnp.where(own, tot[None, :], 0), axis=1)
    nvalid = jnp.where(j < n_active, jnp.clip(tot_b - bs, 0, BLK_CHUNKS), 0)
    kfirst = jnp.sum(pick(cs_end) <= bs[:, None], axis=1)
    klast = jnp.sum(pick(cs_start) < (bs + nvalid)[:, None], axis=1)
    src_base = jnp.arange(nt, dtype=jnp.int32)[None, :] * CHUNKS_T + seg_chunk
    ids = jnp.arange(N_EXPERTS, dtype=jnp.int32)
    later = (ids[None, :] > ids[:, None]) & (tot > 0)[None, :]
    nxt_e = jnp.min(jnp.where(later, ids[None, :], N_EXPERTS), axis=1)
    nxt = jnp.sum(jnp.where(own, nxt_e[None, :], 0), axis=1)
    nxt = jnp.where(nxt >= N_EXPERTS, -1, nxt)
    i32 = lambda a: a.astype(jnp.int32)
    return (be, i32(nvalid), i32(bs), i32(kfirst), i32(klast), i32(nxt),
            i32(cs_start.reshape(-1)), i32(nch.reshape(-1)), i32(src_base.reshape(-1)))


W_CAST_ROWS = 128


def _ffn_kernel(be_ref, nv_ref, bs_ref, kf_ref, kl_ref, nxt_ref, cst_ref, nch_ref, srcb_ref,
                xs_hbm, w1_hbm, b1_ref, w2_hbm, b2_ref, ys_hbm,
                xbuf, ybuf, w1s, w2s, w1b, w2b, gsem, ssem, wsem, *, nblk, nt, layer):
    j = pl.program_id(0)
    slot = j % 2

    def for_pieces(blk, fn):
        e_off = be_ref[blk] * nt
        lo_b = bs_ref[blk]
        hi_b = lo_b + nv_ref[blk]

        def body(i, carry):
            ps = cst_ref[e_off + i]
            lo = jnp.maximum(ps, lo_b)
            n = jnp.minimum(ps + nch_ref[e_off + i], hi_b) - lo

            @pl.when(n > 0)
            def _():
                fn(srcb_ref[e_off + i] + (lo - ps), lo - lo_b, n)
            return carry

        lax.fori_loop(kf_ref[blk], kl_ref[blk], body, 0)

    def start_gather(blk, s):
        for_pieces(blk, lambda src, dst, n: pltpu.make_async_copy(
            xs_hbm.at[pl.ds(src, n)], xbuf.at[s, pl.ds(dst, n)], gsem.at[s]).start())

    def start_scatter(blk, s):
        for_pieces(blk, lambda src, dst, n: pltpu.make_async_copy(
            ybuf.at[s, pl.ds(dst, n)], ys_hbm.at[pl.ds(src, n)], ssem.at[s]).start())

    def wait_gather(blk, s):
        @pl.when(nv_ref[blk] > 0)
        def _():
            pltpu.make_async_copy(xs_hbm.at[pl.ds(0, nv_ref[blk])], xbuf.at[s, pl.ds(0, nv_ref[blk])],
                                  gsem.at[s]).wait()

    def wait_scatter(blk, s):
        @pl.when(nv_ref[blk] > 0)
        def _():
            pltpu.make_async_copy(ybuf.at[s, pl.ds(0, nv_ref[blk])], ys_hbm.at[pl.ds(0, nv_ref[blk])],
                                  ssem.at[s]).wait()

    def weight_copies(e):
        return (pltpu.make_async_copy(w1_hbm.at[layer, e], w1s, wsem.at[0]),
                pltpu.make_async_copy(w2_hbm.at[layer, e], w2s, wsem.at[1]))

    @pl.when(j == 0)
    def _():
        for cp in weight_copies(be_ref[0]):
            cp.start()
        xbuf[...] = jnp.zeros_like(xbuf)
        start_gather(0, 0)

    @pl.when(j + 1 < nblk)
    def _():
        start_gather(j + 1, 1 - slot)

    @pl.when((j == 0) | (be_ref[j] != be_ref[jnp.maximum(j - 1, 0)]))
    def _():
        for cp in weight_copies(be_ref[j]):
            cp.wait()

        def cast_rows(r, carry):
            rows = pl.ds(pl.multiple_of(r * W_CAST_ROWS, W_CAST_ROWS), W_CAST_ROWS)
            w1b[rows, :] = w1s[rows, :].astype(BF16)
            w2b[rows, :] = w2s[rows, :].astype(BF16)
            return carry
        lax.fori_loop(0, D_MODEL // W_CAST_ROWS, cast_rows, 0)

        @pl.when(nxt_ref[j] >= 0)
        def _():
            for cp in weight_copies(nxt_ref[j]):
                cp.start()

    wait_gather(j, slot)

    @pl.when(j >= 2)
    def _():
        wait_scatter(j - 2, slot)

    def compute_rows(first_chunk, nchunks):
        cs = pl.ds(pl.multiple_of(first_chunk, TAIL_CHUNKS), nchunks)
        x = _chunks_to_rows(xbuf[slot, cs])
        gu = jnp.dot(x, w1b[...], preferred_element_type=F32) + b1_ref[...]
        gate = jnp.minimum(gu[:, :D_FF], SWIGLU_LIMIT)
        up = jnp.clip(gu[:, D_FF:], -SWIGLU_LIMIT, SWIGLU_LIMIT)
        act = (up + 1.0) * gate * _sigmoid(SWIGLU_ALPHA * gate)
        y = jnp.dot(act.astype(BF16), w2b[...], preferred_element_type=F32) + b2_ref[...]
        ybuf[slot, cs] = _rows_to_chunks(y)

    nv = nv_ref[j]
    n_full = (nv + (SUB_CHUNKS - TAIL_CHUNKS) - 1) // SUB_CHUNKS
    lax.fori_loop(0, n_full, lambda sb, carry: (compute_rows(sb * SUB_CHUNKS, SUB_CHUNKS), carry)[1], 0)

    @pl.when(nv > n_full * SUB_CHUNKS)
    def _():
        compute_rows(n_full * SUB_CHUNKS, TAIL_CHUNKS)

    start_scatter(j, slot)

    @pl.when(j == nblk - 1)
    def _():
        wait_scatter(j, slot)
        if nblk >= 2:
            wait_scatter(j - 1, 1 - slot)


def _ffn_call(sched, xs_chunks, w1, b1, w2, b2, layer, nt):
    assert D_FF == D_MODEL
    nblk = sched[0].shape[0]
    bmap = lambda j, be, *_: (layer, be[j], 0, 0)
    grid_spec = pltpu.PrefetchScalarGridSpec(
        num_scalar_prefetch=len(sched),
        grid=(nblk,),
        in_specs=[
            pl.BlockSpec(memory_space=pl.ANY),
            pl.BlockSpec(memory_space=pl.ANY),
            pl.BlockSpec((None, None, 1, 2 * D_FF), bmap),
            pl.BlockSpec(memory_space=pl.ANY),
            pl.BlockSpec((None, None, 1, D_MODEL), bmap),
        ],
        out_specs=pl.BlockSpec(memory_space=pl.ANY),
        scratch_shapes=[
            pltpu.VMEM((2, BLK_CHUNKS, 2 * SEG_ALIGN, D_MODEL // 2), BF16),
            pltpu.VMEM((2, BLK_CHUNKS, 2 * SEG_ALIGN, D_MODEL // 2), BF16),
            pltpu.VMEM((D_MODEL, 2 * D_FF), F32),
            pltpu.VMEM((D_FF, D_MODEL), F32),
            pltpu.VMEM((D_MODEL, 2 * D_FF), BF16),
            pltpu.VMEM((D_FF, D_MODEL), BF16),
            pltpu.SemaphoreType.DMA((2,)),
            pltpu.SemaphoreType.DMA((2,)),
            pltpu.SemaphoreType.DMA((2,)),
        ],
    )
    return pl.pallas_call(
        functools.partial(_ffn_kernel, nblk=nblk, nt=nt, layer=layer),
        grid_spec=grid_spec,
        out_shape=jax.ShapeDtypeStruct(xs_chunks.shape, BF16),
        input_output_aliases={len(sched): 0},
        compiler_params=pltpu.CompilerParams(vmem_limit_bytes=VMEM_LIMIT, dimension_semantics=("arbitrary",)),
        name="expert_ffn",
    )(*sched, xs_chunks, w1, b1, w2, b2)


def _combine_kernel(ys_ref, pos_ref, wt_ref, x1_ref, mod_ref, fg_ref, *out_refs, final, n_ctx_tiles):
    ys = _chunks_to_rows(ys_ref[0])
    rlane = lax.broadcasted_iota(jnp.int32, (TM, ROWS_T), 1)
    w = jnp.zeros((TM, ROWS_T), F32)
    for kk in range(TOP_K):
        w = w + jnp.where(rlane == pos_ref[:, kk:kk + 1], wt_ref[:, kk:kk + 1], 0.0)
    moe = jnp.dot(w.astype(BF16), ys, preferred_element_type=F32)
    y = x1_ref[...] + mod_ref[5:6, :] * moe
    if final:
        y = y * lax.rsqrt(jnp.mean(y * y, axis=-1, keepdims=True) + EPS) * fg_ref[...]
    is_ctx = pl.program_id(0) < n_ctx_tiles

    @pl.when(is_ctx)
    def _():
        out_refs[0][...] = y

    @pl.when(jnp.logical_not(is_ctx))
    def _():
        out_refs[1][...] = y


def _combine_call(ys, pos, wts, x1, mod_l, fg, n_ctx_tiles, tiles_per_lat, final):
    n = x1.shape[0]
    nt = n // TM
    gid = functools.partial(_group_id, n_ctx_tiles=n_ctx_tiles, tiles_per_lat=tiles_per_lat)
    tok = pl.BlockSpec((TM, D_MODEL), lambda i: (i, 0))
    ctx_map, lat_map = _path_maps(n_ctx_tiles)
    out_specs = [pl.BlockSpec((TM, D_MODEL), ctx_map), pl.BlockSpec((TM, D_MODEL), lat_map)]
    out_shape = [jax.ShapeDtypeStruct((n_ctx_tiles * TM, D_MODEL), F32),
                 jax.ShapeDtypeStruct((n - n_ctx_tiles * TM, D_MODEL), F32)]
    res = pl.pallas_call(
        functools.partial(_combine_kernel, final=final, n_ctx_tiles=n_ctx_tiles),
        grid=(nt,),
        in_specs=[
            pl.BlockSpec((1, CHUNKS_T, 2 * SEG_ALIGN, D_MODEL // 2), lambda i: (i, 0, 0, 0)),
            pl.BlockSpec((TM, LANES), lambda i: (i, 0)),
            pl.BlockSpec((TM, LANES), lambda i: (i, 0)),
            tok,
            pl.BlockSpec((None, 6, D_MODEL), lambda i: (gid(i), 0, 0)),
            pl.BlockSpec((1, D_MODEL), lambda i: (0, 0)),
        ],
        out_specs=out_specs,
        out_shape=out_shape,
        compiler_params=pltpu.CompilerParams(vmem_limit_bytes=VMEM_LIMIT),
        name="moe_combine",
    )(ys, pos, wts, x1, mod_l, fg)
    return res


def _rope_tables(t):
    pos = np.arange(t)
    nf = HEAD_DIM // 4
    inv = jnp.asarray(ROPE_BASE, F32) ** (-jnp.arange(nf, dtype=F32) / nf)
    ang = jnp.concatenate([jnp.asarray(pos // GRID_W, F32)[:, None] * inv,
                           jnp.asarray(pos % GRID_W, F32)[:, None] * inv], axis=-1)
    cos, sin = jnp.cos(ang), jnp.sin(ang)
    return (jnp.tile(jnp.concatenate([cos, cos], -1), (1, N_HEADS)),
            jnp.tile(jnp.concatenate([-sin, sin], -1), (1, N_HEADS)))


def _block_diag_t(s):
    eye = jnp.eye(N_HEADS, dtype=s.dtype)
    out = jnp.einsum('...hkv,hg->...hvgk', s, eye)
    return out.reshape(s.shape[:-3] + (HG_W, HG_W))


def kernel(x_prompt, x_sample, state_ret, state_gla, state_s5_re, state_s5_im, c, c_ctx, norm1_g, norm2_g, final_g, ada_w, ada_b, w_in, w_out, ret_decay, gla_w_lr, gla_b_lr, gla_norm_g, s5_a_re, s5_a_im, s5_log_dt, s5_b_re, s5_b_im, s5_c_re, s5_c_im, s5_d, s5_glu_w, s5_glu_b, router_w, router_b, moe_w1, moe_b1, moe_w2, moe_b2):
    bc, tc, d = x_prompt.shape
    bl, tl, _ = x_sample.shape
    depth = w_in.shape[0]
    nc, nl = bc * tc, bl * tl
    n = nc + nl
    assert d == D_MODEL and nc % TM == 0 and tl % TM == 0 and tc % CHUNK == 0 and tl % CHUNK == 0
    assert 1 + bl <= SUBLANES and nc % nl == 0
    n_ctx_tiles, tiles_per_lat = nc // TM, tl // TM
    nt = n // TM
    nblk = nt * CHUNKS_T // BLK_CHUNKS + N_EXPERTS

    x_c, x_l = x_prompt.reshape(nc, d), x_sample.reshape(nl, d)
    cond8 = jnp.concatenate([c_ctx[None, :], c, jnp.zeros((SUBLANES - 1 - bl, d), F32)], axis=0)
    mod = _ada_call(cond8, ada_w, ada_b).reshape(depth, SUBLANES, 6, d)
    rope = _rope_tables(tl)
    fg = final_g.reshape(1, d)
    s5_tabs = _s5_tables(s5_a_re, s5_a_im, s5_log_dt, s5_b_re, s5_b_im, s5_c_re, s5_c_im)

    ret_states, gla_states, re_states, im_states = [], [], [], []
    for l in range(depth):
        wl = w_in[l]
        w_cat = jnp.concatenate(
            [wl[:, :8 * HG_W], wl[:, 8 * HG_W + GLA_RANK:],
             jnp.pad(wl[:, 8 * HG_W:8 * HG_W + GLA_RANK], ((0, 0), (0, LANES - GLA_RANK)))], axis=1).astype(BF16)
        wlr = jnp.pad(gla_w_lr[l], ((0, 0), (0, LANES - GLA_RANK), (0, 0)))
        blr = gla_b_lr[l].reshape(2, 1, HG_W)
        pa, su = _inproj_call(x_c, x_l, mod[l], norm1_g[l].reshape(1, d), w_cat, wlr, blr, tl)

        log_gamma = jnp.log1p(-jnp.exp(ret_decay[l]))
        lgam = jnp.repeat(log_gamma, HEAD_DIM, axis=1)
        gn = jnp.tile(gla_norm_g[l], N_HEADS).reshape(1, HG_W)
        y_c, fin_ret, fin_gla = _scan_call(pa, 0, bc, tc, log_gamma, lgam, gn, None, None, True)
        st0 = jnp.stack([_block_diag_t(state_ret[:, l]), _block_diag_t(state_gla[:, l])], axis=2)
        y_l, _, _ = _scan_call(pa, nc, bl, tl, log_gamma, lgam, gn, rope, st0, False)
        ret_states.append(fin_ret)
        gla_states.append(fin_gla)

        dvec = s5_d[l].reshape(1, S5_W)
        ys_c, fin_c = _s5_call(su, 0, s5_tabs, l, dvec, bc, tc, None)
        h0 = jnp.concatenate([state_s5_re[:, l], state_s5_im[:, l]], axis=-1).transpose(2, 1, 0, 3)
        ys_l, _ = _s5_call(su, nc // nl, s5_tabs, l, dvec, bl, tl, h0)
        re_states.append(fin_c[..., :S5_STATE].transpose(2, 1, 0, 3))
        im_states.append(fin_c[..., S5_STATE:].transpose(2, 1, 0, 3))

        rw = jnp.pad(router_w[l], ((0, 0), (0, LANES - N_EXPERTS)))
        rw_hi = rw.astype(BF16)
        rw = jnp.concatenate([rw_hi, (rw - rw_hi.astype(F32)).astype(BF16)], axis=1)
        rb = jnp.concatenate([router_b[l], jnp.full((LANES - N_EXPERTS,), NEG_BIG, F32)]).reshape(1, LANES)
        x1, xs, pos, wts, meta = _outproj_call(
            y_c, y_l, ys_c, ys_l, x_c, x_l, mod[l], norm2_g[l].reshape(1, d), w_out[l].astype(BF16), s5_glu_w[l].astype(BF16),
            s5_glu_b[l].reshape(1, S5_W), rw, rb, n_ctx_tiles, tiles_per_lat)

        sched = _expert_schedule(meta, nblk)
        ys = _ffn_call(sched, xs.reshape(nt * CHUNKS_T, 2 * SEG_ALIGN, d // 2),
                       moe_w1, moe_b1.reshape(depth, N_EXPERTS, 1, 2 * D_FF),
                       moe_w2, moe_b2.reshape(depth, N_EXPERTS, 1, d), l, nt)
        res = _combine_call(ys.reshape(nt, CHUNKS_T, 2 * SEG_ALIGN, d // 2), pos, wts, x1, mod[l], fg,
                            n_ctx_tiles, tiles_per_lat, l == depth - 1)
        x_c, x_l = res
    y_prompt = x_c.reshape(bc, tc, d)
    y_sample = x_l.reshape(bl, tl, d)
    return (y_prompt, y_sample, jnp.stack(ret_states, axis=1), jnp.stack(gla_states, axis=1),
            jnp.stack(re_states, axis=1), jnp.stack(im_states, axis=1))
```

```python
import functools

import numpy as np
import jax
import jax.numpy as jnp
from jax import lax
from jax.experimental import pallas as pl
from jax.experimental.pallas import tpu as pltpu

F32 = jnp.float32
BF16 = jnp.bfloat16
HIGHEST = lax.Precision.HIGHEST

D_MODEL = 1024
GRID_W = 64
CHUNK = 128
HEAD_DIM = 64
N_HEADS = 4
HG_W = N_HEADS * HEAD_DIM
S5_W = 512
S5_CH = 16
S5_GROUPS = 32
S5_STATE = 64
S5_L = 16
S5_ROW = S5_L * S5_CH
GLA_RANK = 16
GLA_TAU = 16.0
N_EXPERTS = 32
TOP_K = 4
D_FF = 1024
SWIGLU_LIMIT = 7.0
SWIGLU_ALPHA = 1.702
ROPE_BASE = 10000.0
EPS = 1e-6

LANES = 128
SUBLANES = 8
TM = 256
TM_IN = 512
SEG_ALIGN = SUBLANES
ROWS_T = 1280
CHUNKS_T = ROWS_T // SEG_ALIGN
BLK_CHUNKS = 128
SUB_CHUNKS = 64
TAIL_CHUNKS = 32
BLK_ROWS = BLK_CHUNKS * SEG_ALIGN
NEG_BIG = -1e30
VMEM_LIMIT = 56 * 1024 * 1024

PA_W = 8 * HG_W + 2 * HG_W
W_CAT = 8 * HG_W + S5_W + LANES


def _sigmoid(x):
    return 1.0 / (1.0 + jnp.exp(-x))


def _rows_to_chunks(x):
    x3 = x.reshape(x.shape[0] // SEG_ALIGN, SEG_ALIGN, D_MODEL)
    return jnp.concatenate([x3[:, :, :D_MODEL // 2], x3[:, :, D_MODEL // 2:]], axis=1).astype(BF16)


def _chunks_to_rows(c):
    c3 = c.astype(F32)
    rows = jnp.concatenate([c3[:, :SEG_ALIGN, :], c3[:, SEG_ALIGN:, :]], axis=2)
    return rows.reshape(c.shape[0] * SEG_ALIGN, D_MODEL).astype(BF16)


def _group_id(i, n_ctx_tiles, tiles_per_lat):
    return jnp.where(i < n_ctx_tiles, 0, 1 + (i - n_ctx_tiles) // tiles_per_lat)


def _ada_kernel(c_ref, w_ref, b_ref, o_ref):
    c = c_ref[...]
    s = c * _sigmoid(c)
    o_ref[0] = jnp.dot(s, w_ref[0], precision=HIGHEST, preferred_element_type=F32) + b_ref[0]


def _ada_call(cond8, ada_w, ada_b):
    depth, d, n6 = ada_w.shape
    tn = 1024
    return pl.pallas_call(
        _ada_kernel,
        grid=(depth, n6 // tn),
        in_specs=[
            pl.BlockSpec((SUBLANES, d), lambda l, j: (0, 0)),
            pl.BlockSpec((1, d, tn), lambda l, j: (l, 0, j)),
            pl.BlockSpec((1, 1, tn), lambda l, j: (l, 0, j)),
        ],
        out_specs=pl.BlockSpec((1, SUBLANES, tn), lambda l, j: (l, 0, j)),
        out_shape=jax.ShapeDtypeStruct((depth, SUBLANES, n6), F32),
        compiler_params=pltpu.CompilerParams(vmem_limit_bytes=VMEM_LIMIT),
        name="ada_mod",
    )(cond8, ada_w, ada_b.reshape(depth, 1, n6))


def _inproj_kernel(xc_ref, xl_ref, mod_ref, n1_ref, w_ref, wlr_ref, blr_ref, pa_ref, su_ref, *, n_ctx_tiles):
    x = jnp.where(pl.program_id(0) < n_ctx_tiles, xc_ref[...], xl_ref[...])
    h = x * lax.rsqrt(jnp.mean(x * x, axis=-1, keepdims=True) + EPS) * n1_ref[...]
    h = h * (1.0 + mod_ref[1:2, :]) + mod_ref[0:1, :]
    r = jnp.dot(h.astype(BF16), w_ref[...], preferred_element_type=F32)
    pa_ref[:, : 8 * HG_W] = r[:, : 8 * HG_W]
    su_ref[...] = r[:, 8 * HG_W: 8 * HG_W + S5_W]
    glr = r[:, 8 * HG_W + S5_W:]
    for d in range(2):
        z = jnp.dot(glr, wlr_ref[d], precision=HIGHEST, preferred_element_type=F32) + blr_ref[d]
        log_sig = jnp.minimum(z, 0.0) - jnp.log(1.0 + jnp.exp(-jnp.abs(z)))
        pa_ref[:, (8 + d) * HG_W: (9 + d) * HG_W] = log_sig * (1.0 / GLA_TAU)


def _path_maps(n_ctx_tiles):
    return (lambda i: (jnp.minimum(i, n_ctx_tiles - 1), 0)), (lambda i: (jnp.maximum(i - n_ctx_tiles, 0), 0))


def _inproj_call(x_c, x_l, mod_l, n1, w_cat, wlr, blr, lat_seq):
    n = x_c.shape[0] + x_l.shape[0]
    assert x_c.shape[0] % TM_IN == 0 and lat_seq % TM_IN == 0
    n_ctx_tiles, tiles_per_lat = x_c.shape[0] // TM_IN, lat_seq // TM_IN
    gid = functools.partial(_group_id, n_ctx_tiles=n_ctx_tiles, tiles_per_lat=tiles_per_lat)
    ctx_map, lat_map = _path_maps(n_ctx_tiles)
    return pl.pallas_call(
        functools.partial(_inproj_kernel, n_ctx_tiles=n_ctx_tiles),
        grid=(n // TM_IN,),
        in_specs=[
            pl.BlockSpec((TM_IN, D_MODEL), ctx_map),
            pl.BlockSpec((TM_IN, D_MODEL), lat_map),
            pl.BlockSpec((None, 6, D_MODEL), lambda i: (gid(i), 0, 0)),
            pl.BlockSpec((1, D_MODEL), lambda i: (0, 0)),
            pl.BlockSpec((D_MODEL, W_CAT), lambda i: (0, 0)),
            pl.BlockSpec((2, LANES, HG_W), lambda i: (0, 0, 0)),
            pl.BlockSpec((2, 1, HG_W), lambda i: (0, 0, 0)),
        ],
        out_specs=[
            pl.BlockSpec((TM_IN, PA_W), lambda i: (i, 0)),
            pl.BlockSpec((TM_IN, S5_W), lambda i: (i, 0)),
        ],
        out_shape=[
            jax.ShapeDtypeStruct((n, PA_W), F32),
            jax.ShapeDtypeStruct((n, S5_W), F32),
        ],
        compiler_params=pltpu.CompilerParams(vmem_limit_bytes=VMEM_LIMIT),
        name="norm1_inproj",
    )(x_c, x_l, mod_l, n1, w_cat, wlr, blr)


SCAN_PB = 2


def _scan_kernel(*refs, n, use_rope, has_init, want_final):
    refs = list(refs)
    lg_ref = refs.pop(0)
    pa_ref = refs.pop(0)
    if use_rope:
        cos_ref = refs.pop(0)
        sin_ref = refs.pop(0)
    lgam_ref = refs.pop(0)
    gn_ref = refs.pop(0)
    if has_init:
        st0_ref = refs.pop(0)
    y_ref = refs.pop(0)
    if want_final:
        fin_refs = [refs.pop(0), refs.pop(0)]
    st_scr = refs.pop(0)
    of_scr = refs.pop(0)

    d = pl.program_id(1)
    c = pl.program_id(2)
    cc = jnp.where(d == 0, c, n - 1 - c)

    @pl.when(c == 0)
    def _():
        if has_init:
            st_scr[...] = st0_ref[...]
        else:
            st_scr[...] = jnp.zeros_like(st_scr)

    ii = lax.broadcasted_iota(jnp.int32, (CHUNK, CHUNK), 0)
    jj = lax.broadcasted_iota(jnp.int32, (CHUNK, CHUNK), 1)
    rel = (1 - 2 * d) * (ii - jj)
    mask = rel >= 0
    i4 = lax.broadcasted_iota(jnp.int32, (CHUNK, N_HEADS * CHUNK), 0)
    j4 = lax.broadcasted_iota(jnp.int32, (CHUNK, N_HEADS * CHUNK), 1) % CHUNK
    mask4 = (1 - 2 * d) * (i4 - j4) >= 0
    relf = rel.astype(F32)
    row = lax.broadcasted_iota(jnp.int32, (CHUNK, 1), 0)
    p1 = jnp.where(d == 0, row + 1, CHUNK - row).astype(F32)
    lane = lax.broadcasted_iota(jnp.int32, (1, HG_W), 1)
    head_mask = [(lane // HEAD_DIM == h).astype(F32) for h in range(N_HEADS)]
    bi = lax.broadcasted_iota(jnp.int32, (HG_W, HG_W), 0) // HEAD_DIM
    bj = lax.broadcasted_iota(jnp.int32, (HG_W, HG_W), 1) // HEAD_DIM
    block_diag = bi == bj
    tri = mask.astype(BF16)
    lgl = lgam_ref[pl.ds(d, 1), :]
    ret_q_decay = jnp.exp(p1 * lgl)
    ret_k_decay = jnp.exp((CHUNK - p1) * lgl)
    ret_chunk_decay = jnp.exp(CHUNK * lgl)
    ret_weight = jnp.concatenate([jnp.where(mask, jnp.exp(relf * lg_ref[d, h]), 0.0) for h in range(N_HEADS)], axis=1)

    nt_dims = (((1,), (1,)), ((), ()))
    tn_dims = (((0,), (0,)), ((), ()))

    def split3(x):
        hi = x.astype(BF16)
        r1 = x - hi.astype(F32)
        mid = r1.astype(BF16)
        return hi, mid, (r1 - mid.astype(F32)).astype(BF16)

    def group(q, k, v, q_dec, k_upd, chunk_decay, score_weight, bb, g_idx):
        vb = v.astype(BF16)
        ks = jnp.concatenate([(k * head_mask[h]).astype(BF16) for h in range(N_HEADS)], axis=0)
        vs = jnp.concatenate([(v * head_mask[h]).astype(BF16) for h in range(N_HEADS)], axis=0)
        s = lax.dot_general(q.astype(BF16), ks, nt_dims, preferred_element_type=F32)
        s = jnp.where(mask4, s, 0.0) if score_weight is None else s * score_weight
        o = jnp.dot(s.astype(BF16), vs, preferred_element_type=F32)
        st = st_scr[bb, g_idx]
        o = o + lax.dot_general(q_dec.astype(BF16), st.astype(BF16), nt_dims, preferred_element_type=F32)
        upd = lax.dot_general(vb, k_upd.astype(BF16), tn_dims, preferred_element_type=F32)
        st_scr[bb, g_idx] = st * chunk_decay + jnp.where(block_diag, upd, 0.0)
        return o

    def rope(x):
        partner = jnp.where((lane % HEAD_DIM) < HEAD_DIM // 2,
                            pltpu.roll(x, HG_W - HEAD_DIM // 2, 1), pltpu.roll(x, HEAD_DIM // 2, 1))
        return x * cos_ref[...] + partner * sin_ref[...]

    def one_row(bb):
        q = pa_ref[bb, :, 0:HG_W]
        k = pa_ref[bb, :, HG_W:2 * HG_W] * (HEAD_DIM ** -0.5)
        v = pa_ref[bb, :, 2 * HG_W:3 * HG_W]
        if use_rope:
            q = rope(q)
            k = rope(k)
        o_ret = group(q, k, v, q * ret_q_decay, k * ret_k_decay, ret_chunk_decay, ret_weight, bb, 0)

        q = pa_ref[bb, :, 4 * HG_W:5 * HG_W] * (HEAD_DIM ** -0.5)
        k = pa_ref[bb, :, 5 * HG_W:6 * HG_W]
        v = pa_ref[bb, :, 6 * HG_W:7 * HG_W]
        la = jnp.where(d == 0, pa_ref[bb, :, 8 * HG_W:9 * HG_W], pa_ref[bb, :, 9 * HG_W:10 * HG_W])
        b = sum(jnp.dot(tri, part, preferred_element_type=F32) for part in split3(la))
        b_last = jnp.sum(la, axis=0, keepdims=True)
        q_dec = q * jnp.exp(b)
        o_gla = group(q_dec, k * jnp.exp(-b), v, q_dec, k * jnp.exp(b_last - b), jnp.exp(b_last), None, bb, 1)
        return o_ret, o_gla

    outs = [one_row(bb) for bb in range(SCAN_PB)]

    @pl.when(d == 0)
    def _():
        for bb, (o_ret, o_gla) in enumerate(outs):
            of_scr[c, bb, :, 0:HG_W] = o_ret
            of_scr[c, bb, :, HG_W:2 * HG_W] = o_gla

    @pl.when(d == 1)
    def _():
        mean_mat = jnp.where(block_diag, 1.0 / HEAD_DIM, 0.0).astype(BF16)

        def head_rms(o):
            parts = jnp.concatenate(split3(o * o), axis=0)
            ms3 = jnp.dot(parts, mean_mat, preferred_element_type=F32)
            ms = ms3[0:CHUNK] + ms3[CHUNK:2 * CHUNK] + ms3[2 * CHUNK:3 * CHUNK]
            return o * lax.rsqrt(ms + EPS)

        for bb, (o_ret, o_gla) in enumerate(outs):
            o1 = o_ret + of_scr[cc, bb, :, 0:HG_W]
            g1 = pa_ref[bb, :, 3 * HG_W:4 * HG_W]
            y_ref[bb, :, 0:HG_W] = head_rms(o1) * (g1 * _sigmoid(g1))
            o2 = o_gla + of_scr[cc, bb, :, HG_W:2 * HG_W]
            g2 = pa_ref[bb, :, 7 * HG_W:8 * HG_W]
            y_ref[bb, :, HG_W:2 * HG_W] = head_rms(o2) * gn_ref[...] * (g2 * _sigmoid(g2))

    if want_final:
        @pl.when(c == n - 1)
        def _():
            for bb in range(SCAN_PB):
                for g_idx in range(2):
                    st_t = st_scr[bb, g_idx].T
                    for h in range(N_HEADS):
                        hs = slice(h * HEAD_DIM, (h + 1) * HEAD_DIM)
                        fin_refs[g_idx][bb, h] = st_t[hs, hs]


def _scan_call(pa, first_row, bsz, t, lg, lgam, gn, rope, st0, want_final):
    n = t // CHUNK
    use_rope = rope is not None
    has_init = st0 is not None
    pair_rows = SCAN_PB * t
    assert bsz % SCAN_PB == 0 and first_row % pair_rows == 0 and pa.shape[0] % pair_rows == 0
    pair_off = first_row // pair_rows
    pa4 = pa.reshape(pa.shape[0] // pair_rows, SCAN_PB, t, PA_W)

    def chunk_idx(d, c):
        return jnp.where(d == 0, c, n - 1 - c)

    in_specs = [
        pl.BlockSpec(memory_space=pltpu.SMEM),
        pl.BlockSpec((None, SCAN_PB, CHUNK, PA_W), lambda b, d, c: (pair_off + b, 0, chunk_idx(d, c), 0)),
    ]
    args = [lg, pa4]
    if use_rope:
        in_specs += [pl.BlockSpec((CHUNK, HG_W), lambda b, d, c: (chunk_idx(d, c), 0))] * 2
        args += list(rope)
    in_specs += [pl.BlockSpec((2, HG_W), lambda b, d, c: (0, 0)),
                 pl.BlockSpec((1, HG_W), lambda b, d, c: (0, 0))]
    args += [lgam, gn]
    state_spec = pl.BlockSpec((SCAN_PB, None, 2, HG_W, HG_W), lambda b, d, c: (b, d, 0, 0, 0))
    if has_init:
        in_specs.append(state_spec)
        args.append(st0)
    out_specs = [pl.BlockSpec((None, SCAN_PB, CHUNK, 2 * HG_W),
                              lambda b, d, c: (b, 0, jnp.where(d == 0, n - 1, n - 1 - c), 0))]
    out_shape = [jax.ShapeDtypeStruct((bsz // SCAN_PB, SCAN_PB, t, 2 * HG_W), F32)]
    if want_final:
        fin_spec = pl.BlockSpec((SCAN_PB, None, N_HEADS, HEAD_DIM, HEAD_DIM), lambda b, d, c: (b, d, 0, 0, 0))
        out_specs += [fin_spec, fin_spec]
        out_shape += [jax.ShapeDtypeStruct((bsz, 2, N_HEADS, HEAD_DIM, HEAD_DIM), F32)] * 2
    res = pl.pallas_call(
        functools.partial(_scan_kernel, n=n, use_rope=use_rope, has_init=has_init, want_final=want_final),
        grid=(bsz // SCAN_PB, 2, n),
        in_specs=in_specs,
        out_specs=out_specs,
        out_shape=out_shape,
        scratch_shapes=[pltpu.VMEM((SCAN_PB, 2, HG_W, HG_W), F32), pltpu.VMEM((n, SCAN_PB, CHUNK, 2 * HG_W), F32)],
        compiler_params=pltpu.CompilerParams(vmem_limit_bytes=VMEM_LIMIT),
        name="ret_gla_scan",
    )(*args)
    y = res[0].reshape(bsz * t, 2 * HG_W)
    return (y, res[1], res[2]) if want_final else (y, None, None)


S5_GB = LANES // S5_CH
S5_NEXP = 2 * S5_L + 1


def _s5_table_kernel(pwr_ref, pwi_ref, fz_ref, btr_ref, bti_ref, cr_ref, ci_ref, m_ref, sin_ref, sout_ref):
    def per_token(t, rows):
        return jnp.concatenate([jnp.broadcast_to(t[i:i + 1, :], (S5_CH, S5_STATE)) for i in rows], axis=0)

    def per_channel(t):
        return jnp.concatenate([t] * S5_L, axis=0)

    pos = range(S5_L)
    seqs = ([j for j in pos], [S5_L - 1 - j for j in pos], [2 * S5_L - 1 - j for j in pos],
            [S5_L + j for j in pos], [S5_L + j + 1 for j in pos], [2 * S5_L - j for j in pos])

    def powers(d, k):
        return per_token(pwr_ref[d], seqs[k]), per_token(pwi_ref[d], seqs[k])

    cr, ci = per_channel(cr_ref[...]), per_channel(ci_ref[...])
    tok_in = lax.broadcasted_iota(jnp.int32, (S5_ROW, S5_ROW), 0) // S5_CH
    tok_out = lax.broadcasted_iota(jnp.int32, (S5_ROW, S5_ROW), 1) // S5_CH
    nt_dims = (((1,), (1,)), ((), ()))
    m = jnp.zeros((S5_ROW, S5_ROW), F32)
    for d in range(2):
        fr, fi = fz_ref[d, 0:1, :], fz_ref[d, 1:2, :]
        bbr = per_channel(fr * btr_ref[...] - fi * bti_ref[...])
        bbi = per_channel(fr * bti_ref[...] + fi * btr_ref[...])
        k_out, k_in, k_sin, k_sout = (0, 1, 2, 4) if d == 0 else (1, 0, 3, 5)
        er, ei = powers(d, k_out)
        cq = jnp.concatenate([cr * er - ci * ei, -(cr * ei + ci * er)], axis=-1)
        er, ei = powers(d, k_in)
        bk = jnp.concatenate([bbr * er - bbi * ei, bbr * ei + bbi * er], axis=-1)
        full = lax.dot_general(bk, cq, nt_dims, precision=HIGHEST, preferred_element_type=F32)
        m = m + jnp.where((tok_out >= tok_in) if d == 0 else (tok_in >= tok_out), full, 0.0)
        er, ei = powers(d, k_sin)
        sin_ref[d] = jnp.concatenate([bbr * er - bbi * ei, bbr * ei + bbi * er], axis=-1).astype(BF16)
        er, ei = powers(d, k_sout)
        sout_ref[d] = jnp.concatenate([cr * er - ci * ei, -(cr * ei + ci * er)], axis=-1).astype(BF16)
    m_ref[...] = m.astype(BF16)


def _s5_power_table(a_re, a_im, log_dt):
    expo = np.concatenate([np.arange(S5_L, dtype=np.float32) - (S5_L - 1) / 2.0, np.arange(S5_L + 1, dtype=np.float32)])
    expo = jnp.asarray(expo)[:, None]
    dt = jnp.exp(log_dt)[..., None, None]
    ar, ai = a_re[..., None, :] * dt, a_im[..., None, :] * dt
    mag = jnp.exp(expo * ar)
    return mag * jnp.cos(expo * ai), mag * jnp.sin(expo * ai)


def _s5_tables(a_re, a_im, log_dt, b_re, b_im, c_re, c_im):
    depth = a_re.shape[0]
    pwr, pwi = _s5_power_table(a_re, a_im, log_dt)
    dt = jnp.exp(log_dt)[..., None]
    mag = jnp.exp(a_re * dt)
    lb_re, lb_im = mag * jnp.cos(a_im * dt), mag * jnp.sin(a_im * dt)
    den = a_re * a_re + a_im * a_im
    n_re = lb_re - 1.0
    fz = jnp.stack([(n_re * a_re + lb_im * a_im) / den, (lb_im * a_re - n_re * a_im) / den], axis=3)
    mag_l = jnp.exp(S5_L * (a_re * dt))
    pl_re, pl_im = mag_l * jnp.cos(S5_L * (a_im * dt)), mag_l * jnp.sin(S5_L * (a_im * dt))
    lam = jnp.stack([jnp.concatenate([pl_re[:, 0], pl_re[:, 0]], -1), jnp.concatenate([-pl_im[:, 0], pl_im[:, 0]], -1),
                     jnp.concatenate([pl_re[:, 1], pl_re[:, 1]], -1), jnp.concatenate([-pl_im[:, 1], pl_im[:, 1]], -1)],
                    axis=2)
    per_lg = lambda *shape: pl.BlockSpec((None, None) + shape, lambda l, g: (l, g) + (0,) * len(shape))
    per_dir = lambda *shape: pl.BlockSpec((None, 2, None) + shape, lambda l, g: (l, 0, g) + (0,) * len(shape))
    m, s_in, s_out = pl.pallas_call(
        _s5_table_kernel,
        grid=(depth, S5_GROUPS),
        in_specs=[per_dir(S5_NEXP, S5_STATE), per_dir(S5_NEXP, S5_STATE), per_dir(2, S5_STATE),
                  per_lg(S5_CH, S5_STATE), per_lg(S5_CH, S5_STATE), per_lg(S5_CH, S5_STATE), per_lg(S5_CH, S5_STATE)],
        out_specs=[per_lg(S5_ROW, S5_ROW), per_lg(2, S5_ROW, 2 * S5_STATE), per_lg(2, S5_ROW, 2 * S5_STATE)],
        out_shape=[jax.ShapeDtypeStruct((depth, S5_GROUPS, S5_ROW, S5_ROW), BF16),
                   jax.ShapeDtypeStruct((depth, S5_GROUPS, 2, S5_ROW, 2 * S5_STATE), BF16),
                   jax.ShapeDtypeStruct((depth, S5_GROUPS, 2, S5_ROW, 2 * S5_STATE), BF16)],
        compiler_params=pltpu.CompilerParams(vmem_limit_bytes=VMEM_LIMIT),
        name="s5_tables",
    )(pwr, pwi, fz, jnp.swapaxes(b_re, -1, -2), jnp.swapaxes(b_im, -1, -2), c_re, c_im)
    return m, s_in, s_out, lam


def _s5_kernel(*refs, n, bsz, has_init):
    refs = list(refs)
    su_ref = refs.pop(0)
    m_ref = refs.pop(0)
    sin_ref = refs.pop(0)
    sout_ref = refs.pop(0)
    lam_ref = refs.pop(0)
    dv_ref = refs.pop(0)
    if has_init:
        h0_ref = refs.pop(0)
    y_ref = refs.pop(0)
    fin_ref = refs.pop(0)
    sfs, sbs, hpf, hnb = refs

    r = n * bsz
    gran = lax.broadcasted_iota(jnp.int32, (1, LANES), 1) // S5_CH
    halves_per_row = S5_ROW // LANES
    per_half = S5_L // halves_per_row

    def tok_rows(j):
        return pl.ds(j, r, stride=S5_L)

    def granule_transpose(arrs):
        arrs = list(arrs)
        bit = S5_GB // 2
        while bit:
            upper = (gran & bit) != 0
            for i in range(S5_GB):
                if i & bit:
                    continue
                lo, hi = arrs[i], arrs[i | bit]
                arrs[i] = jnp.where(upper, pltpu.roll(hi, bit * S5_CH, 1), lo)
                arrs[i | bit] = jnp.where(upper, hi, pltpu.roll(lo, LANES - bit * S5_CH, 1))
            bit //= 2
        return arrs

    halves = [granule_transpose([su_ref[tok_rows(hh * per_half + i), :] for i in range(per_half)])
              for hh in range(halves_per_row)]
    u = jnp.stack([jnp.concatenate([halves[hh][gl] for hh in range(halves_per_row)], axis=-1)
                   for gl in range(S5_GB)]).astype(BF16)

    bm_dims = (((2,), (1,)), ((0,), (0,)))
    bm_nt_dims = (((2,), (2,)), ((0,), (0,)))
    y = lax.dot_general(u, m_ref[...], bm_dims, preferred_element_type=F32)
    sfs[...] = lax.dot_general(u, sin_ref[:, 0], bm_dims, preferred_element_type=F32)
    sbs[...] = lax.dot_general(u, sin_ref[:, 1], bm_dims, preferred_element_type=F32)

    a_f, s_f = lam_ref[:, 0:1, :], lam_ref[:, 1:2, :]
    a_b, s_b = lam_ref[:, 2:3, :], lam_ref[:, 3:4, :]
    if has_init:
        hf = h0_ref[:, 0]
        hb = h0_ref[:, 1]
    else:
        hf = jnp.zeros((S5_GB, bsz, 2 * S5_STATE), F32)
        hb = jnp.zeros((S5_GB, bsz, 2 * S5_STATE), F32)
    swap = lambda a: pltpu.roll(a, S5_STATE, 2)
    hf_sw, hb_sw = swap(hf), swap(hb)
    for c in range(n):
        rows = pl.ds(c, bsz, stride=n)
        hpf[:, rows, :] = hf
        s_in = sfs[:, rows, :]
        hf, hf_sw = hf * a_f + hf_sw * s_f + s_in, hf_sw * a_f - hf * s_f + swap(s_in)
        rows = pl.ds(n - 1 - c, bsz, stride=n)
        hnb[:, rows, :] = hb
        s_in = sbs[:, rows, :]
        hb, hb_sw = hb * a_b + hb_sw * s_b + s_in, hb_sw * a_b - hb * s_b + swap(s_in)
    fin_ref[:, 0] = hf
    fin_ref[:, 1] = hb
    y = y + lax.dot_general(hpf[...].astype(BF16), sout_ref[:, 0], bm_nt_dims, preferred_element_type=F32)
    y = y + lax.dot_general(hnb[...].astype(BF16), sout_ref[:, 1], bm_nt_dims, preferred_element_type=F32)

    for hh in range(halves_per_row):
        by_token = granule_transpose([y[gl][:, hh * LANES:(hh + 1) * LANES] for gl in range(S5_GB)])
        for i in range(per_half):
            rows = tok_rows(hh * per_half + i)
            y_ref[rows, :] = by_token[i] + su_ref[rows, :] * dv_ref[...]


def _s5_call(su, row_block, tabs, layer, dvec, bsz, t, h0):
    m, s_in, s_out, lam = tabs
    n = t // S5_L
    r = n * bsz
    rows = bsz * t
    has_init = h0 is not None
    gspec = lambda *shape: pl.BlockSpec((S5_GB,) + shape, lambda g: (g,) + (0,) * len(shape))
    lspec = lambda *shape: pl.BlockSpec((None, S5_GB) + shape, lambda g: (layer, g) + (0,) * len(shape))
    in_specs = [pl.BlockSpec((rows, LANES), lambda g: (row_block, g)),
                lspec(S5_ROW, S5_ROW), lspec(2, S5_ROW, 2 * S5_STATE), lspec(2, S5_ROW, 2 * S5_STATE),
                lspec(4, 2 * S5_STATE), pl.BlockSpec((1, LANES), lambda g: (0, g))]
    args = [su, m, s_in, s_out, lam, dvec]
    if has_init:
        in_specs.append(gspec(2, bsz, 2 * S5_STATE))
        args.append(h0)
    return pl.pallas_call(
        functools.partial(_s5_kernel, n=n, bsz=bsz, has_init=has_init),
        grid=(S5_GROUPS // S5_GB,),
        in_specs=in_specs,
        out_specs=[pl.BlockSpec((rows, LANES), lambda g: (0, g)), gspec(2, bsz, 2 * S5_STATE)],
        out_shape=[jax.ShapeDtypeStruct((rows, S5_W), F32),
                   jax.ShapeDtypeStruct((S5_GROUPS, 2, bsz, 2 * S5_STATE), F32)],
        scratch_shapes=[pltpu.VMEM((S5_GB, r, 2 * S5_STATE), F32)] * 4,
        compiler_params=pltpu.CompilerParams(vmem_limit_bytes=VMEM_LIMIT),
        name="s5_scan",
    )(*args)


def _outproj_kernel(yrg_c_ref, yrg_l_ref, ys_c_ref, ys_l_ref, xc_ref, xl_ref, mod_ref, n2_ref, wo_ref, gw_ref, gb_ref,
                    rw_ref, rb_ref, x1_ref, xs_ref, pos_ref, wt_ref, meta_ref, *, n_ctx_tiles):
    is_ctx = pl.program_id(0) < n_ctx_tiles
    x_in = jnp.where(is_ctx, xc_ref[...], xl_ref[...])
    ys = jnp.where(is_ctx, ys_c_ref[...], ys_l_ref[...])
    yrg = jnp.where(is_ctx, yrg_c_ref[...], yrg_l_ref[...])
    s = 0.5 * ys * (1.0 + jnp.tanh(np.sqrt(2.0 / np.pi).astype(np.float32) * (ys + 0.044715 * (ys * ys * ys))))
    s = s * _sigmoid(jnp.dot(s.astype(BF16), gw_ref[...], preferred_element_type=F32) + gb_ref[...])
    m = (jnp.dot(yrg.astype(BF16), wo_ref[0:2 * HG_W, :], preferred_element_type=F32)
         + jnp.dot(s.astype(BF16), wo_ref[2 * HG_W:, :], preferred_element_type=F32))
    x1 = x_in + mod_ref[2:3, :] * m
    x1_ref[...] = x1
    h = x1 * lax.rsqrt(jnp.mean(x1 * x1, axis=-1, keepdims=True) + EPS) * n2_ref[...]
    h = h * (1.0 + mod_ref[4:5, :]) + mod_ref[3:4, :]

    h_hi = h.astype(BF16)
    h_lo = (h - h_hi.astype(F32)).astype(BF16)
    pp = jnp.dot(jnp.concatenate([h_hi, h_lo], axis=0), rw_ref[...], preferred_element_type=F32)
    logits = ((pp[:TM, :LANES] + pp[TM:, :LANES]) + (pp[:TM, LANES:] + pp[TM:, LANES:])) + rb_ref[...]
    lane = lax.broadcasted_iota(jnp.int32, (TM, LANES), 1).astype(F32)
    cur = logits
    vals, idxs = [], []
    for _ in range(TOP_K):
        mx = jnp.max(cur, axis=-1, keepdims=True)
        am = jnp.min(jnp.where(cur == mx, lane, float(LANES)), axis=-1, keepdims=True)
        vals.append(mx)
        idxs.append(am)
        cur = jnp.where(lane == am, -jnp.inf, cur)
    es = [jnp.exp(v - vals[0]) for v in vals]
    den = es[0] + es[1] + es[2] + es[3]

    hit = [(lane == am) for am in idxs]
    assign = (hit[0] | hit[1] | hit[2] | hit[3]).astype(BF16)
    ti = lax.broadcasted_iota(jnp.int32, (TM, TM), 0)
    tj = lax.broadcasted_iota(jnp.int32, (TM, TM), 1)
    rank = jnp.dot((tj < ti).astype(BF16), assign, preferred_element_type=F32)
    cnt = jnp.sum(assign.astype(F32), axis=0, keepdims=True)
    cnt_al = jnp.floor((cnt + (SEG_ALIGN - 1)) * (1.0 / SEG_ALIGN)) * SEG_ALIGN
    ei = lax.broadcasted_iota(jnp.int32, (LANES, LANES), 0)
    ej = lax.broadcasted_iota(jnp.int32, (LANES, LANES), 1)
    seg = jnp.dot(jnp.broadcast_to(cnt_al, (SUBLANES, LANES)).astype(BF16), (ei < ej).astype(BF16),
                  preferred_element_type=F32)[0:1, :]
    base = seg + rank
    rlane = lax.broadcasted_iota(jnp.int32, (TM, ROWS_T), 1).astype(F32)
    klane = lax.broadcasted_iota(jnp.int32, (TM, LANES), 1)
    onehot = jnp.zeros((TM, ROWS_T), F32)
    pos_out = jnp.zeros((TM, LANES), F32)
    wt_out = jnp.zeros((TM, LANES), F32)
    for kk in range(TOP_K):
        pk = jnp.sum(jnp.where(hit[kk], base, 0.0), axis=-1, keepdims=True)
        onehot = onehot + (rlane == pk).astype(F32)
        pos_out = jnp.where(klane == kk, pk, pos_out)
        wt_out = jnp.where(klane == kk, es[kk] / den, wt_out)
    xs_ref[0] = _rows_to_chunks(lax.dot_general(onehot.astype(BF16), h.astype(BF16), (((0,), (0,)), ((), ())),
                                                preferred_element_type=F32))
    pos_ref[...] = pos_out.astype(jnp.int32)
    wt_ref[...] = wt_out
    mrow = lax.broadcasted_iota(jnp.int32, (SUBLANES, LANES), 0)
    meta = jnp.where(mrow == 0, jnp.broadcast_to(cnt_al, (SUBLANES, LANES)),
                     jnp.where(mrow == 1, jnp.broadcast_to(seg, (SUBLANES, LANES)), 0.0))
    meta_ref[0] = meta.astype(jnp.int32)


def _outproj_call(yrg_c, yrg_l, ys_c, ys_l, x_c, x_l, mod_l, n2, wo, gw, gb, rw, rb, n_ctx_tiles, tiles_per_lat):
    n = x_c.shape[0] + x_l.shape[0]
    nt = n // TM
    gid = functools.partial(_group_id, n_ctx_tiles=n_ctx_tiles, tiles_per_lat=tiles_per_lat)
    const = lambda *shape: pl.BlockSpec(shape, lambda i: (0,) * len(shape))
    ctx_map, lat_map = _path_maps(n_ctx_tiles)
    return pl.pallas_call(
        functools.partial(_outproj_kernel, n_ctx_tiles=n_ctx_tiles),
        grid=(nt,),
        in_specs=[
            pl.BlockSpec((TM, 2 * HG_W), ctx_map),
            pl.BlockSpec((TM, 2 * HG_W), lat_map),
            pl.BlockSpec((TM, S5_W), ctx_map),
            pl.BlockSpec((TM, S5_W), lat_map),
            pl.BlockSpec((TM, D_MODEL), ctx_map),
            pl.BlockSpec((TM, D_MODEL), lat_map),
            pl.BlockSpec((None, 6, D_MODEL), lambda i: (gid(i), 0, 0)),
            const(1, D_MODEL), const(D_MODEL, D_MODEL), const(S5_W, S5_W), const(1, S5_W),
            const(D_MODEL, 2 * LANES), const(1, LANES),
        ],
        out_specs=[
            pl.BlockSpec((TM, D_MODEL), lambda i: (i, 0)),
            pl.BlockSpec((1, CHUNKS_T, 2 * SEG_ALIGN, D_MODEL // 2), lambda i: (i, 0, 0, 0)),
            pl.BlockSpec((TM, LANES), lambda i: (i, 0)),
            pl.BlockSpec((TM, LANES), lambda i: (i, 0)),
            pl.BlockSpec((1, SUBLANES, LANES), lambda i: (i, 0, 0)),
        ],
        out_shape=[
            jax.ShapeDtypeStruct((n, D_MODEL), F32),
            jax.ShapeDtypeStruct((nt, CHUNKS_T, 2 * SEG_ALIGN, D_MODEL // 2), BF16),
            jax.ShapeDtypeStruct((n, LANES), jnp.int32),
            jax.ShapeDtypeStruct((n, LANES), F32),
            jax.ShapeDtypeStruct((nt, SUBLANES, LANES), jnp.int32),
        ],
        compiler_params=pltpu.CompilerParams(vmem_limit_bytes=VMEM_LIMIT),
        name="outproj_router_dispatch",
    )(yrg_c, yrg_l, ys_c, ys_l, x_c, x_l, mod_l, n2, wo, gw, gb, rw, rb)


SCHED_COLS = 6


def _schedule_kernel(meta_ref, blk_ref, cst_ref, nch_ref, srcb_ref, *, nt, nblk_pad):
    exact = functools.partial(jnp.dot, precision=HIGHEST, preferred_element_type=F32)
    nch = meta_ref[:, 0, :].astype(F32) * (1.0 / SEG_ALIGN)
    seg = meta_ref[:, 1, :].astype(F32) * (1.0 / SEG_ALIGN)
    ti = lax.broadcasted_iota(jnp.int32, (nt, nt), 0)
    tj = lax.broadcasted_iota(jnp.int32, (nt, nt), 1)
    cs_end = exact((tj <= ti).astype(F32), nch)
    cs_start = cs_end - nch
    tot = cs_end[nt - 1:nt, :]
    nb = jnp.floor((tot + (BLK_CHUNKS - 1)) * (1.0 / BLK_CHUNKS))
    ei = lax.broadcasted_iota(jnp.int32, (LANES, LANES), 0)
    ej = lax.broadcasted_iota(jnp.int32, (LANES, LANES), 1)
    blk_end = exact(jnp.broadcast_to(nb, (SUBLANES, LANES)), (ei <= ej).astype(F32))[0:1, :]
    blk_start = blk_end - nb
    n_active = blk_end[:, LANES - 1:LANES]
    j = lax.broadcasted_iota(jnp.int32, (nblk_pad, 1), 0).astype(F32)
    lane = lax.broadcasted_iota(jnp.int32, (nblk_pad, LANES), 1).astype(F32)
    row_sum = lambda a: jnp.sum(a, axis=-1, keepdims=True)
    be = row_sum((blk_end <= jnp.minimum(j, n_active - 1.0)).astype(F32))
    be = jnp.minimum(be, float(N_EXPERTS - 1))
    own = (lane == be).astype(F32)
    bs = (j - row_sum(own * blk_start)) * BLK_CHUNKS
    nvalid = jnp.where(j < n_active, jnp.clip(row_sum(own * tot) - bs, 0.0, float(BLK_CHUNKS)), 0.0)
    nt_dims = (((1,), (1,)), ((), ()))
    own_end = lax.dot_general(own, cs_end, nt_dims, precision=HIGHEST, preferred_element_type=F32)
    own_start = lax.dot_general(own, cs_start, nt_dims, precision=HIGHEST, preferred_element_type=F32)
    kfirst = row_sum((own_end <= bs).astype(F32))
    klast = row_sum((own_start < bs + nvalid).astype(F32))
    later = (ej > ei) & (jnp.broadcast_to(tot, (LANES, LANES)) > 0.0)
    nxt_e = jnp.min(jnp.where(later, ej.astype(F32), float(LANES)), axis=-1, keepdims=True)
    nxt = exact(own, jnp.broadcast_to(nxt_e, (LANES, LANES)))[:, 0:1]
    nxt = jnp.where(nxt >= float(N_EXPERTS), -1.0, nxt)
    cols = (be, nvalid, bs, kfirst, klast, nxt)
    out = jnp.zeros((nblk_pad, LANES), F32)
    for k_col, val in enumerate(cols):
        out = jnp.where(lane == float(k_col), val, out)
    blk_ref[...] = out.astype(jnp.int32)
    tile = lax.broadcasted_iota(jnp.int32, (nt, LANES), 0).astype(F32)
    cst_ref[...] = cs_start.astype(jnp.int32)
    nch_ref[...] = nch.astype(jnp.int32)
    srcb_ref[...] = (tile * CHUNKS_T + seg).astype(jnp.int32)


def _expert_schedule(meta, nblk):
    nt = meta.shape[0]
    nblk_pad = -(-nblk // SUBLANES) * SUBLANES
    whole = lambda *shape: pl.BlockSpec(shape, lambda i: (0,) * len(shape))
    blk, cst, nch, srcb = pl.pallas_call(
        functools.partial(_schedule_kernel, nt=nt, nblk_pad=nblk_pad),
        grid=(1,),
        in_specs=[whole(nt, SUBLANES, LANES)],
        out_specs=[whole(nblk_pad, LANES), whole(nt, LANES), whole(nt, LANES), whole(nt, LANES)],
        out_shape=[jax.ShapeDtypeStruct((nblk_pad, LANES), jnp.int32)] + [jax.ShapeDtypeStruct((nt, LANES), jnp.int32)] * 3,
        name="expert_schedule",
    )(meta)
    return tuple(blk[:nblk, k] for k in range(SCHED_COLS)) + (cst.reshape(-1), nch.reshape(-1), srcb.reshape(-1))


W_CAST_ROWS = 128


def _ffn_kernel(be_ref, nv_ref, bs_ref, kf_ref, kl_ref, nxt_ref, cst_ref, nch_ref, srcb_ref,
                xs_hbm, w1_hbm, b1_ref, w2_hbm, b2_ref, ys_hbm,
                xbuf, ybuf, w1s, w2s, w1b, w2b, gsem, ssem, wsem, *, nblk, layer):
    j = pl.program_id(0)
    slot = j % 2

    def for_pieces(blk, fn):
        e = be_ref[blk]
        lo_b = bs_ref[blk]
        hi_b = lo_b + nv_ref[blk]

        def body(i, carry):
            k = i * LANES + e
            ps = cst_ref[k]
            lo = jnp.maximum(ps, lo_b)
            n = jnp.minimum(ps + nch_ref[k], hi_b) - lo

            @pl.when(n > 0)
            def _():
                fn(srcb_ref[k] + (lo - ps), lo - lo_b, n)
            return carry

        lax.fori_loop(kf_ref[blk], kl_ref[blk], body, 0)

    def start_gather(blk, s):
        for_pieces(blk, lambda src, dst, n: pltpu.make_async_copy(
            xs_hbm.at[pl.ds(src, n)], xbuf.at[s, pl.ds(dst, n)], gsem.at[s]).start())

    def start_scatter(blk, s):
        for_pieces(blk, lambda src, dst, n: pltpu.make_async_copy(
            ybuf.at[s, pl.ds(dst, n)], ys_hbm.at[pl.ds(src, n)], ssem.at[s]).start())

    def wait_gather(blk, s):
        @pl.when(nv_ref[blk] > 0)
        def _():
            pltpu.make_async_copy(xs_hbm.at[pl.ds(0, nv_ref[blk])], xbuf.at[s, pl.ds(0, nv_ref[blk])],
                                  gsem.at[s]).wait()

    def wait_scatter(blk, s):
        @pl.when(nv_ref[blk] > 0)
        def _():
            pltpu.make_async_copy(ybuf.at[s, pl.ds(0, nv_ref[blk])], ys_hbm.at[pl.ds(0, nv_ref[blk])],
                                  ssem.at[s]).wait()

    def weight_copies(e):
        return (pltpu.make_async_copy(w1_hbm.at[layer, e], w1s, wsem.at[0]),
                pltpu.make_async_copy(w2_hbm.at[layer, e], w2s, wsem.at[1]))

    @pl.when(j == 0)
    def _():
        for cp in weight_copies(be_ref[0]):
            cp.start()
        xbuf[...] = jnp.zeros_like(xbuf)
        start_gather(0, 0)

    @pl.when(j + 1 < nblk)
    def _():
        start_gather(j + 1, 1 - slot)

    @pl.when((j == 0) | (be_ref[j] != be_ref[jnp.maximum(j - 1, 0)]))
    def _():
        for cp in weight_copies(be_ref[j]):
            cp.wait()

        def cast_rows(r, carry):
            rows = pl.ds(pl.multiple_of(r * W_CAST_ROWS, W_CAST_ROWS), W_CAST_ROWS)
            w1b[rows, :] = w1s[rows, :].astype(BF16)
            w2b[rows, :] = w2s[rows, :].astype(BF16)
            return carry
        lax.fori_loop(0, D_MODEL // W_CAST_ROWS, cast_rows, 0)

        @pl.when(nxt_ref[j] >= 0)
        def _():
            for cp in weight_copies(nxt_ref[j]):
                cp.start()

    wait_gather(j, slot)

    @pl.when(j >= 2)
    def _():
        wait_scatter(j - 2, slot)

    def compute_rows(first_chunk, nchunks):
        cs = pl.ds(pl.multiple_of(first_chunk, TAIL_CHUNKS), nchunks)
        x = _chunks_to_rows(xbuf[slot, cs])
        gu = jnp.dot(x, w1b[...], preferred_element_type=F32) + b1_ref[...]
        gate = jnp.minimum(gu[:, :D_FF], SWIGLU_LIMIT)
        up = jnp.clip(gu[:, D_FF:], -SWIGLU_LIMIT, SWIGLU_LIMIT)
        act = (up + 1.0) * gate * _sigmoid(SWIGLU_ALPHA * gate)
        y = jnp.dot(act.astype(BF16), w2b[...], preferred_element_type=F32) + b2_ref[...]
        ybuf[slot, cs] = _rows_to_chunks(y)

    nv = nv_ref[j]
    n_full = (nv + (SUB_CHUNKS - TAIL_CHUNKS) - 1) // SUB_CHUNKS
    lax.fori_loop(0, n_full, lambda sb, carry: (compute_rows(sb * SUB_CHUNKS, SUB_CHUNKS), carry)[1], 0)

    @pl.when(nv > n_full * SUB_CHUNKS)
    def _():
        compute_rows(n_full * SUB_CHUNKS, TAIL_CHUNKS)

    start_scatter(j, slot)

    @pl.when(j == nblk - 1)
    def _():
        wait_scatter(j, slot)
        if nblk >= 2:
            wait_scatter(j - 1, 1 - slot)


def _ffn_call(sched, xs_chunks, w1, b1, w2, b2, layer):
    assert D_FF == D_MODEL
    nblk = sched[0].shape[0]
    bmap = lambda j, be, *_: (layer, be[j], 0, 0)
    grid_spec = pltpu.PrefetchScalarGridSpec(
        num_scalar_prefetch=len(sched),
        grid=(nblk,),
        in_specs=[
            pl.BlockSpec(memory_space=pl.ANY),
            pl.BlockSpec(memory_space=pl.ANY),
            pl.BlockSpec((None, None, 1, 2 * D_FF), bmap),
            pl.BlockSpec(memory_space=pl.ANY),
            pl.BlockSpec((None, None, 1, D_MODEL), bmap),
        ],
        out_specs=pl.BlockSpec(memory_space=pl.ANY),
        scratch_shapes=[
            pltpu.VMEM((2, BLK_CHUNKS, 2 * SEG_ALIGN, D_MODEL // 2), BF16),
            pltpu.VMEM((2, BLK_CHUNKS, 2 * SEG_ALIGN, D_MODEL // 2), BF16),
            pltpu.VMEM((D_MODEL, 2 * D_FF), F32),
            pltpu.VMEM((D_FF, D_MODEL), F32),
            pltpu.VMEM((D_MODEL, 2 * D_FF), BF16),
            pltpu.VMEM((D_FF, D_MODEL), BF16),
            pltpu.SemaphoreType.DMA((2,)),
            pltpu.SemaphoreType.DMA((2,)),
            pltpu.SemaphoreType.DMA((2,)),
        ],
    )
    return pl.pallas_call(
        functools.partial(_ffn_kernel, nblk=nblk, layer=layer),
        grid_spec=grid_spec,
        out_shape=jax.ShapeDtypeStruct(xs_chunks.shape, BF16),
        input_output_aliases={len(sched): 0},
        compiler_params=pltpu.CompilerParams(vmem_limit_bytes=VMEM_LIMIT, dimension_semantics=("arbitrary",)),
        name="expert_ffn",
    )(*sched, xs_chunks, w1, b1, w2, b2)


def _combine_kernel(ys_ref, pos_ref, wt_ref, x1_ref, mod_ref, fg_ref, *out_refs, final, n_ctx_tiles):
    ys = _chunks_to_rows(ys_ref[0])
    rlane = lax.broadcasted_iota(jnp.int32, (TM, ROWS_T), 1)
    w = jnp.zeros((TM, ROWS_T), F32)
    for kk in range(TOP_K):
        w = w + jnp.where(rlane == pos_ref[:, kk:kk + 1], wt_ref[:, kk:kk + 1], 0.0)
    moe = jnp.dot(w.astype(BF16), ys, preferred_element_type=F32)
    y = x1_ref[...] + mod_ref[5:6, :] * moe
    if final:
        y = y * lax.rsqrt(jnp.mean(y * y, axis=-1, keepdims=True) + EPS) * fg_ref[...]
    is_ctx = pl.program_id(0) < n_ctx_tiles

    @pl.when(is_ctx)
    def _():
        out_refs[0][...] = y

    @pl.when(jnp.logical_not(is_ctx))
    def _():
        out_refs[1][...] = y


def _combine_call(ys, pos, wts, x1, mod_l, fg, n_ctx_tiles, tiles_per_lat, final):
    n = x1.shape[0]
    nt = n // TM
    gid = functools.partial(_group_id, n_ctx_tiles=n_ctx_tiles, tiles_per_lat=tiles_per_lat)
    tok = pl.BlockSpec((TM, D_MODEL), lambda i: (i, 0))
    ctx_map, lat_map = _path_maps(n_ctx_tiles)
    out_specs = [pl.BlockSpec((TM, D_MODEL), ctx_map), pl.BlockSpec((TM, D_MODEL), lat_map)]
    out_shape = [jax.ShapeDtypeStruct((n_ctx_tiles * TM, D_MODEL), F32),
                 jax.ShapeDtypeStruct((n - n_ctx_tiles * TM, D_MODEL), F32)]
    res = pl.pallas_call(
        functools.partial(_combine_kernel, final=final, n_ctx_tiles=n_ctx_tiles),
        grid=(nt,),
        in_specs=[
            pl.BlockSpec((1, CHUNKS_T, 2 * SEG_ALIGN, D_MODEL // 2), lambda i: (i, 0, 0, 0)),
            pl.BlockSpec((TM, LANES), lambda i: (i, 0)),
            pl.BlockSpec((TM, LANES), lambda i: (i, 0)),
            tok,
            pl.BlockSpec((None, 6, D_MODEL), lambda i: (gid(i), 0, 0)),
            pl.BlockSpec((1, D_MODEL), lambda i: (0, 0)),
        ],
        out_specs=out_specs,
        out_shape=out_shape,
        compiler_params=pltpu.CompilerParams(vmem_limit_bytes=VMEM_LIMIT),
        name="moe_combine",
    )(ys, pos, wts, x1, mod_l, fg)
    return res


def _rope_tables(t):
    pos = np.arange(t)
    nf = HEAD_DIM // 4
    inv = jnp.asarray(ROPE_BASE, F32) ** (-jnp.arange(nf, dtype=F32) / nf)
    ang = jnp.concatenate([jnp.asarray(pos // GRID_W, F32)[:, None] * inv,
                           jnp.asarray(pos % GRID_W, F32)[:, None] * inv], axis=-1)
    cos, sin = jnp.cos(ang), jnp.sin(ang)
    return (jnp.tile(jnp.concatenate([cos, cos], -1), (1, N_HEADS)),
            jnp.tile(jnp.concatenate([-sin, sin], -1), (1, N_HEADS)))


def _block_diag_t(s):
    eye = jnp.eye(N_HEADS, dtype=s.dtype)
    out = jnp.einsum('...hkv,hg->...hvgk', s, eye)
    return out.reshape(s.shape[:-3] + (HG_W, HG_W))


def kernel(x_prompt, x_sample, state_ret, state_gla, state_s5_re, state_s5_im, c, c_ctx, norm1_g, norm2_g, final_g, ada_w, ada_b, w_in, w_out, ret_decay, gla_w_lr, gla_b_lr, gla_norm_g, s5_a_re, s5_a_im, s5_log_dt, s5_b_re, s5_b_im, s5_c_re, s5_c_im, s5_d, s5_glu_w, s5_glu_b, router_w, router_b, moe_w1, moe_b1, moe_w2, moe_b2):
    bc, tc, d = x_prompt.shape
    bl, tl, _ = x_sample.shape
    depth = w_in.shape[0]
    nc, nl = bc * tc, bl * tl
    n = nc + nl
    assert d == D_MODEL and nc % TM == 0 and tl % TM == 0 and tc % CHUNK == 0 and tl % CHUNK == 0
    assert 1 + bl <= SUBLANES and nc % nl == 0
    n_ctx_tiles, tiles_per_lat = nc // TM, tl // TM
    nt = n // TM
    nblk = nt * CHUNKS_T // BLK_CHUNKS + N_EXPERTS

    x_c, x_l = x_prompt.reshape(nc, d), x_sample.reshape(nl, d)
    cond8 = jnp.concatenate([c_ctx[None, :], c, jnp.zeros((SUBLANES - 1 - bl, d), F32)], axis=0)
    mod = _ada_call(cond8, ada_w, ada_b).reshape(depth, SUBLANES, 6, d)
    rope = _rope_tables(tl)
    fg = final_g.reshape(1, d)
    s5_tabs = _s5_tables(s5_a_re, s5_a_im, s5_log_dt, s5_b_re, s5_b_im, s5_c_re, s5_c_im)

    ret_states, gla_states, re_states, im_states = [], [], [], []
    for l in range(depth):
        wl = w_in[l]
        w_cat = jnp.concatenate(
            [wl[:, :8 * HG_W], wl[:, 8 * HG_W + GLA_RANK:],
             jnp.pad(wl[:, 8 * HG_W:8 * HG_W + GLA_RANK], ((0, 0), (0, LANES - GLA_RANK)))], axis=1).astype(BF16)
        wlr = jnp.pad(gla_w_lr[l], ((0, 0), (0, LANES - GLA_RANK), (0, 0)))
        blr = gla_b_lr[l].reshape(2, 1, HG_W)
        pa, su = _inproj_call(x_c, x_l, mod[l], norm1_g[l].reshape(1, d), w_cat, wlr, blr, tl)

        log_gamma = jnp.log1p(-jnp.exp(ret_decay[l]))
        lgam = jnp.repeat(log_gamma, HEAD_DIM, axis=1)
        gn = jnp.tile(gla_norm_g[l], N_HEADS).reshape(1, HG_W)
        y_c, fin_ret, fin_gla = _scan_call(pa, 0, bc, tc, log_gamma, lgam, gn, None, None, True)
        st0 = jnp.stack([_block_diag_t(state_ret[:, l]), _block_diag_t(state_gla[:, l])], axis=2)
        y_l, _, _ = _scan_call(pa, nc, bl, tl, log_gamma, lgam, gn, rope, st0, False)
        ret_states.append(fin_ret)
        gla_states.append(fin_gla)

        dvec = s5_d[l].reshape(1, S5_W)
        ys_c, fin_c = _s5_call(su, 0, s5_tabs, l, dvec, bc, tc, None)
        h0 = jnp.concatenate([state_s5_re[:, l], state_s5_im[:, l]], axis=-1).transpose(2, 1, 0, 3)
        ys_l, _ = _s5_call(su, nc // nl, s5_tabs, l, dvec, bl, tl, h0)
        re_states.append(fin_c[..., :S5_STATE].transpose(2, 1, 0, 3))
        im_states.append(fin_c[..., S5_STATE:].transpose(2, 1, 0, 3))

        rw = jnp.pad(router_w[l], ((0, 0), (0, LANES - N_EXPERTS)))
        rw_hi = rw.astype(BF16)
        rw = jnp.concatenate([rw_hi, (rw - rw_hi.astype(F32)).astype(BF16)], axis=1)
        rb = jnp.concatenate([router_b[l], jnp.full((LANES - N_EXPERTS,), NEG_BIG, F32)]).reshape(1, LANES)
        x1, xs, pos, wts, meta = _outproj_call(
            y_c, y_l, ys_c, ys_l, x_c, x_l, mod[l], norm2_g[l].reshape(1, d), w_out[l].astype(BF16), s5_glu_w[l].astype(BF16),
            s5_glu_b[l].reshape(1, S5_W), rw, rb, n_ctx_tiles, tiles_per_lat)

        sched = _expert_schedule(meta, nblk)
        ys = _ffn_call(sched, xs.reshape(nt * CHUNKS_T, 2 * SEG_ALIGN, d // 2),
                       moe_w1, moe_b1.reshape(depth, N_EXPERTS, 1, 2 * D_FF),
                       moe_w2, moe_b2.reshape(depth, N_EXPERTS, 1, d), l)
        res = _combine_call(ys.reshape(nt, CHUNKS_T, 2 * SEG_ALIGN, d // 2), pos, wts, x1, mod[l], fg,
                            n_ctx_tiles, tiles_per_lat, l == depth - 1)
        x_c, x_l = res
    y_prompt = x_c.reshape(bc, tc, d)
    y_sample = x_l.reshape(bl, tl, d)
    return (y_prompt, y_sample, jnp.stack(ret_states, axis=1), jnp.stack(gla_states, axis=1),
            jnp.stack(re_states, axis=1), jnp.stack(im_states, axis=1))
```

```python
import functools

import numpy as np
import jax
import jax.numpy as jnp
from jax import lax
from jax.experimental import pallas as pl
from jax.experimental.pallas import tpu as pltpu

F32 = jnp.float32
BF16 = jnp.bfloat16
HIGHEST = lax.Precision.HIGHEST

D_MODEL = 1024
GRID_W = 64
CHUNK = 128
HEAD_DIM = 64
N_HEADS = 4
HG_W = N_HEADS * HEAD_DIM
S5_W = 512
S5_CH = 16
S5_GROUPS = 32
S5_STATE = 64
S5_L = 16
S5_ROW = S5_L * S5_CH
GLA_RANK = 16
GLA_TAU = 16.0
N_EXPERTS = 32
TOP_K = 4
D_FF = 1024
SWIGLU_LIMIT = 7.0
SWIGLU_ALPHA = 1.702
ROPE_BASE = 10000.0
EPS = 1e-6

LANES = 128
SUBLANES = 8
TM = 256
TM_IN = 512
SEG_ALIGN = SUBLANES
ROWS_T = -(-(TOP_K * TM + N_EXPERTS * (SEG_ALIGN - 1)) // LANES) * LANES
CHUNKS_T = ROWS_T // SEG_ALIGN
BLK_CHUNKS = 128
SUB_CHUNKS = 64
TAIL_CHUNKS = 32
NEG_BIG = -1e30
VMEM_LIMIT = 56 * 1024 * 1024

PA_W = 8 * HG_W + 2 * HG_W
W_CAT = 8 * HG_W + S5_W + LANES


def _sigmoid(x):
    return 1.0 / (1.0 + jnp.exp(-x))


def _rows_to_chunks(x):
    x3 = x.reshape(x.shape[0] // SEG_ALIGN, SEG_ALIGN, D_MODEL)
    return jnp.concatenate([x3[:, :, :D_MODEL // 2], x3[:, :, D_MODEL // 2:]], axis=1).astype(BF16)


def _chunks_to_rows(c):
    c3 = c.astype(F32)
    rows = jnp.concatenate([c3[:, :SEG_ALIGN, :], c3[:, SEG_ALIGN:, :]], axis=2)
    return rows.reshape(c.shape[0] * SEG_ALIGN, D_MODEL).astype(BF16)


def _group_id(i, n_ctx_tiles, tiles_per_lat):
    return jnp.where(i < n_ctx_tiles, 0, 1 + (i - n_ctx_tiles) // tiles_per_lat)


def _ada_kernel(c_ref, w_ref, b_ref, o_ref):
    c = c_ref[...]
    s = c * _sigmoid(c)
    o_ref[0] = jnp.dot(s, w_ref[0], precision=HIGHEST, preferred_element_type=F32) + b_ref[0]


def _ada_call(cond8, ada_w, ada_b):
    depth, d, n6 = ada_w.shape
    tn = 1024
    return pl.pallas_call(
        _ada_kernel,
        grid=(depth, n6 // tn),
        in_specs=[
            pl.BlockSpec((SUBLANES, d), lambda l, j: (0, 0)),
            pl.BlockSpec((1, d, tn), lambda l, j: (l, 0, j)),
            pl.BlockSpec((1, 1, tn), lambda l, j: (l, 0, j)),
        ],
        out_specs=pl.BlockSpec((1, SUBLANES, tn), lambda l, j: (l, 0, j)),
        out_shape=jax.ShapeDtypeStruct((depth, SUBLANES, n6), F32),
        compiler_params=pltpu.CompilerParams(vmem_limit_bytes=VMEM_LIMIT),
        name="ada_mod",
    )(cond8, ada_w, ada_b.reshape(depth, 1, n6))


def _inproj_kernel(xc_ref, xl_ref, mod_ref, n1_ref, w_ref, wlr_ref, blr_ref, pa_ref, su_ref, *, n_ctx_tiles):
    x = jnp.where(pl.program_id(0) < n_ctx_tiles, xc_ref[...], xl_ref[...])
    h = x * lax.rsqrt(jnp.mean(x * x, axis=-1, keepdims=True) + EPS) * n1_ref[...]
    h = h * (1.0 + mod_ref[1:2, :]) + mod_ref[0:1, :]
    r = jnp.dot(h.astype(BF16), w_ref[...], preferred_element_type=F32)
    pa_ref[:, : 8 * HG_W] = r[:, : 8 * HG_W]
    su_ref[...] = r[:, 8 * HG_W: 8 * HG_W + S5_W]
    glr = r[:, 8 * HG_W + S5_W:]
    for d in range(2):
        z = jnp.dot(glr, wlr_ref[d], precision=HIGHEST, preferred_element_type=F32) + blr_ref[d]
        log_sig = jnp.minimum(z, 0.0) - jnp.log(1.0 + jnp.exp(-jnp.abs(z)))
        pa_ref[:, (8 + d) * HG_W: (9 + d) * HG_W] = log_sig * (1.0 / GLA_TAU)


def _path_maps(n_ctx_tiles):
    return (lambda i: (jnp.minimum(i, n_ctx_tiles - 1), 0)), (lambda i: (jnp.maximum(i - n_ctx_tiles, 0), 0))


def _inproj_call(x_c, x_l, mod_l, n1, w_cat, wlr, blr, lat_seq):
    n = x_c.shape[0] + x_l.shape[0]
    assert x_c.shape[0] % TM_IN == 0 and lat_seq % TM_IN == 0
    n_ctx_tiles, tiles_per_lat = x_c.shape[0] // TM_IN, lat_seq // TM_IN
    gid = functools.partial(_group_id, n_ctx_tiles=n_ctx_tiles, tiles_per_lat=tiles_per_lat)
    ctx_map, lat_map = _path_maps(n_ctx_tiles)
    return pl.pallas_call(
        functools.partial(_inproj_kernel, n_ctx_tiles=n_ctx_tiles),
        grid=(n // TM_IN,),
        in_specs=[
            pl.BlockSpec((TM_IN, D_MODEL), ctx_map),
            pl.BlockSpec((TM_IN, D_MODEL), lat_map),
            pl.BlockSpec((None, 6, D_MODEL), lambda i: (gid(i), 0, 0)),
            pl.BlockSpec((1, D_MODEL), lambda i: (0, 0)),
            pl.BlockSpec((D_MODEL, W_CAT), lambda i: (0, 0)),
            pl.BlockSpec((2, LANES, HG_W), lambda i: (0, 0, 0)),
            pl.BlockSpec((2, 1, HG_W), lambda i: (0, 0, 0)),
        ],
        out_specs=[
            pl.BlockSpec((TM_IN, PA_W), lambda i: (i, 0)),
            pl.BlockSpec((TM_IN, S5_W), lambda i: (i, 0)),
        ],
        out_shape=[
            jax.ShapeDtypeStruct((n, PA_W), F32),
            jax.ShapeDtypeStruct((n, S5_W), F32),
        ],
        compiler_params=pltpu.CompilerParams(vmem_limit_bytes=VMEM_LIMIT),
        name="norm1_inproj",
    )(x_c, x_l, mod_l, n1, w_cat, wlr, blr)


SCAN_PB_MAX = 4


def _scan_kernel(*refs, n, pb, use_rope, has_init, want_final):
    refs = list(refs)
    lg_ref = refs.pop(0)
    pa_ref = refs.pop(0)
    if use_rope:
        cos_ref = refs.pop(0)
        sin_ref = refs.pop(0)
    lgam_ref = refs.pop(0)
    gn_ref = refs.pop(0)
    if has_init:
        st0_ref = refs.pop(0)
    y_ref = refs.pop(0)
    if want_final:
        fin_refs = [refs.pop(0), refs.pop(0)]
    st_scr = refs.pop(0)
    of_scr = refs.pop(0)

    d = pl.program_id(1)
    c = pl.program_id(2)
    cc = jnp.where(d == 0, c, n - 1 - c)

    @pl.when(c == 0)
    def _():
        if has_init:
            st_scr[...] = st0_ref[...]
        else:
            st_scr[...] = jnp.zeros_like(st_scr)

    ii = lax.broadcasted_iota(jnp.int32, (CHUNK, CHUNK), 0)
    jj = lax.broadcasted_iota(jnp.int32, (CHUNK, CHUNK), 1)
    rel = (1 - 2 * d) * (ii - jj)
    mask = rel >= 0
    i4 = lax.broadcasted_iota(jnp.int32, (CHUNK, N_HEADS * CHUNK), 0)
    j4 = lax.broadcasted_iota(jnp.int32, (CHUNK, N_HEADS * CHUNK), 1) % CHUNK
    mask4 = (1 - 2 * d) * (i4 - j4) >= 0
    relf = rel.astype(F32)
    row = lax.broadcasted_iota(jnp.int32, (CHUNK, 1), 0)
    p1 = jnp.where(d == 0, row + 1, CHUNK - row).astype(F32)
    lane = lax.broadcasted_iota(jnp.int32, (1, HG_W), 1)
    head_mask = [(lane // HEAD_DIM == h).astype(F32) for h in range(N_HEADS)]
    bi = lax.broadcasted_iota(jnp.int32, (HG_W, HG_W), 0) // HEAD_DIM
    bj = lax.broadcasted_iota(jnp.int32, (HG_W, HG_W), 1) // HEAD_DIM
    block_diag = bi == bj
    tri = mask.astype(BF16)
    lgl = lgam_ref[pl.ds(d, 1), :]
    ret_q_decay = jnp.exp(p1 * lgl)
    ret_k_decay = jnp.exp((CHUNK - p1) * lgl)
    ret_chunk_decay = jnp.exp(CHUNK * lgl)
    ret_weight = jnp.concatenate([jnp.where(mask, jnp.exp(relf * lg_ref[d, h]), 0.0) for h in range(N_HEADS)], axis=1)

    nt_dims = (((1,), (1,)), ((), ()))
    tn_dims = (((0,), (0,)), ((), ()))

    def split3(x):
        hi = x.astype(BF16)
        r1 = x - hi.astype(F32)
        mid = r1.astype(BF16)
        return hi, mid, (r1 - mid.astype(F32)).astype(BF16)

    def group(q, k, v, q_dec, k_upd, chunk_decay, score_weight, bb, g_idx):
        vb = v.astype(BF16)
        ks = jnp.concatenate([(k * head_mask[h]).astype(BF16) for h in range(N_HEADS)], axis=0)
        vs = jnp.concatenate([(v * head_mask[h]).astype(BF16) for h in range(N_HEADS)], axis=0)
        s = lax.dot_general(q.astype(BF16), ks, nt_dims, preferred_element_type=F32)
        s = jnp.where(mask4, s, 0.0) if score_weight is None else s * score_weight
        o = jnp.dot(s.astype(BF16), vs, preferred_element_type=F32)
        st = st_scr[bb, g_idx]
        o = o + lax.dot_general(q_dec.astype(BF16), st.astype(BF16), nt_dims, preferred_element_type=F32)
        upd = lax.dot_general(vb, k_upd.astype(BF16), tn_dims, preferred_element_type=F32)
        st_scr[bb, g_idx] = st * chunk_decay + jnp.where(block_diag, upd, 0.0)
        return o

    def rope(x):
        partner = jnp.where((lane % HEAD_DIM) < HEAD_DIM // 2,
                            pltpu.roll(x, HG_W - HEAD_DIM // 2, 1), pltpu.roll(x, HEAD_DIM // 2, 1))
        return x * cos_ref[...] + partner * sin_ref[...]

    def one_row(bb):
        q = pa_ref[bb, :, 0:HG_W]
        k = pa_ref[bb, :, HG_W:2 * HG_W] * (HEAD_DIM ** -0.5)
        v = pa_ref[bb, :, 2 * HG_W:3 * HG_W]
        if use_rope:
            q = rope(q)
            k = rope(k)
        o_ret = group(q, k, v, q * ret_q_decay, k * ret_k_decay, ret_chunk_decay, ret_weight, bb, 0)

        q = pa_ref[bb, :, 4 * HG_W:5 * HG_W] * (HEAD_DIM ** -0.5)
        k = pa_ref[bb, :, 5 * HG_W:6 * HG_W]
        v = pa_ref[bb, :, 6 * HG_W:7 * HG_W]
        la = jnp.where(d == 0, pa_ref[bb, :, 8 * HG_W:9 * HG_W], pa_ref[bb, :, 9 * HG_W:10 * HG_W])
        b = sum(jnp.dot(tri, part, preferred_element_type=F32) for part in split3(la))
        b_last = jnp.sum(la, axis=0, keepdims=True)
        q_dec = q * jnp.exp(b)
        o_gla = group(q_dec, k * jnp.exp(-b), v, q_dec, k * jnp.exp(b_last - b), jnp.exp(b_last), None, bb, 1)
        return o_ret, o_gla

    outs = [one_row(bb) for bb in range(pb)]

    @pl.when(d == 0)
    def _():
        for bb, (o_ret, o_gla) in enumerate(outs):
            of_scr[c, bb, :, 0:HG_W] = o_ret
            of_scr[c, bb, :, HG_W:2 * HG_W] = o_gla

    @pl.when(d == 1)
    def _():
        mean_mat = jnp.where(block_diag, 1.0 / HEAD_DIM, 0.0).astype(BF16)

        def head_rms(o):
            parts = jnp.concatenate(split3(o * o), axis=0)
            ms3 = jnp.dot(parts, mean_mat, preferred_element_type=F32)
            ms = ms3[0:CHUNK] + ms3[CHUNK:2 * CHUNK] + ms3[2 * CHUNK:3 * CHUNK]
            return o * lax.rsqrt(ms + EPS)

        for bb, (o_ret, o_gla) in enumerate(outs):
            o1 = o_ret + of_scr[cc, bb, :, 0:HG_W]
            g1 = pa_ref[bb, :, 3 * HG_W:4 * HG_W]
            y_ref[bb, :, 0:HG_W] = head_rms(o1) * (g1 * _sigmoid(g1))
            o2 = o_gla + of_scr[cc, bb, :, HG_W:2 * HG_W]
            g2 = pa_ref[bb, :, 7 * HG_W:8 * HG_W]
            y_ref[bb, :, HG_W:2 * HG_W] = head_rms(o2) * gn_ref[...] * (g2 * _sigmoid(g2))

    if want_final:
        @pl.when(c == n - 1)
        def _():
            for bb in range(pb):
                for g_idx in range(2):
                    st_t = st_scr[bb, g_idx].T
                    for h in range(N_HEADS):
                        hs = slice(h * HEAD_DIM, (h + 1) * HEAD_DIM)
                        fin_refs[g_idx][bb, h] = st_t[hs, hs]


def _scan_call(pa, first_row, bsz, t, lg, lgam, gn, rope, st0, want_final):
    n = t // CHUNK
    use_rope = rope is not None
    has_init = st0 is not None
    pb = min(bsz, SCAN_PB_MAX)
    pair_rows = pb * t
    assert bsz % pb == 0 and first_row % pair_rows == 0 and pa.shape[0] % pair_rows == 0
    pair_off = first_row // pair_rows
    pa4 = pa.reshape(pa.shape[0] // pair_rows, pb, t, PA_W)

    def chunk_idx(d, c):
        return jnp.where(d == 0, c, n - 1 - c)

    in_specs = [
        pl.BlockSpec(memory_space=pltpu.SMEM),
        pl.BlockSpec((None, pb, CHUNK, PA_W), lambda b, d, c: (pair_off + b, 0, chunk_idx(d, c), 0)),
    ]
    args = [lg, pa4]
    if use_rope:
        in_specs += [pl.BlockSpec((CHUNK, HG_W), lambda b, d, c: (chunk_idx(d, c), 0))] * 2
        args += list(rope)
    in_specs += [pl.BlockSpec((2, HG_W), lambda b, d, c: (0, 0)),
                 pl.BlockSpec((1, HG_W), lambda b, d, c: (0, 0))]
    args += [lgam, gn]
    state_spec = pl.BlockSpec((pb, None, 2, HG_W, HG_W), lambda b, d, c: (b, d, 0, 0, 0))
    if has_init:
        in_specs.append(state_spec)
        args.append(st0)
    out_specs = [pl.BlockSpec((None, pb, CHUNK, 2 * HG_W),
                              lambda b, d, c: (b, 0, jnp.where(d == 0, n - 1, n - 1 - c), 0))]
    out_shape = [jax.ShapeDtypeStruct((bsz // pb, pb, t, 2 * HG_W), F32)]
    if want_final:
        fin_spec = pl.BlockSpec((pb, None, N_HEADS, HEAD_DIM, HEAD_DIM), lambda b, d, c: (b, d, 0, 0, 0))
        out_specs += [fin_spec, fin_spec]
        out_shape += [jax.ShapeDtypeStruct((bsz, 2, N_HEADS, HEAD_DIM, HEAD_DIM), F32)] * 2
    res = pl.pallas_call(
        functools.partial(_scan_kernel, n=n, pb=pb, use_rope=use_rope, has_init=has_init, want_final=want_final),
        grid=(bsz // pb, 2, n),
        in_specs=in_specs,
        out_specs=out_specs,
        out_shape=out_shape,
        scratch_shapes=[pltpu.VMEM((pb, 2, HG_W, HG_W), F32), pltpu.VMEM((n, pb, CHUNK, 2 * HG_W), F32)],
        compiler_params=pltpu.CompilerParams(vmem_limit_bytes=VMEM_LIMIT),
        name="ret_gla_scan",
    )(*args)
    y = res[0].reshape(bsz * t, 2 * HG_W)
    return (y, res[1], res[2]) if want_final else (y, None, None)


S5_GB = LANES // S5_CH
S5_NEXP = 2 * S5_L + 1


def _s5_table_kernel(pwr_ref, pwi_ref, fz_ref, btr_ref, bti_ref, cr_ref, ci_ref, m_ref, sin_ref, sout_ref):
    def per_token(t, rows):
        return jnp.concatenate([jnp.broadcast_to(t[i:i + 1, :], (S5_CH, S5_STATE)) for i in rows], axis=0)

    def per_channel(t):
        return jnp.concatenate([t] * S5_L, axis=0)

    pos = range(S5_L)
    seqs = ([j for j in pos], [S5_L - 1 - j for j in pos], [2 * S5_L - 1 - j for j in pos],
            [S5_L + j for j in pos], [S5_L + j + 1 for j in pos], [2 * S5_L - j for j in pos])

    def powers(d, k):
        return per_token(pwr_ref[d], seqs[k]), per_token(pwi_ref[d], seqs[k])

    cr, ci = per_channel(cr_ref[...]), per_channel(ci_ref[...])
    tok_in = lax.broadcasted_iota(jnp.int32, (S5_ROW, S5_ROW), 0) // S5_CH
    tok_out = lax.broadcasted_iota(jnp.int32, (S5_ROW, S5_ROW), 1) // S5_CH
    nt_dims = (((1,), (1,)), ((), ()))
    m = jnp.zeros((S5_ROW, S5_ROW), F32)
    for d in range(2):
        fr, fi = fz_ref[d, 0:1, :], fz_ref[d, 1:2, :]
        bbr = per_channel(fr * btr_ref[...] - fi * bti_ref[...])
        bbi = per_channel(fr * bti_ref[...] + fi * btr_ref[...])
        k_out, k_in, k_sin, k_sout = (0, 1, 2, 4) if d == 0 else (1, 0, 3, 5)
        er, ei = powers(d, k_out)
        cq = jnp.concatenate([cr * er - ci * ei, -(cr * ei + ci * er)], axis=-1)
        er, ei = powers(d, k_in)
        bk = jnp.concatenate([bbr * er - bbi * ei, bbr * ei + bbi * er], axis=-1)
        full = lax.dot_general(bk, cq, nt_dims, precision=HIGHEST, preferred_element_type=F32)
        m = m + jnp.where((tok_out >= tok_in) if d == 0 else (tok_in >= tok_out), full, 0.0)
        er, ei = powers(d, k_sin)
        sin_ref[d] = jnp.concatenate([bbr * er - bbi * ei, bbr * ei + bbi * er], axis=-1).astype(BF16)
        er, ei = powers(d, k_sout)
        sout_ref[d] = jnp.concatenate([cr * er - ci * ei, -(cr * ei + ci * er)], axis=-1).astype(BF16)
    m_ref[...] = m.astype(BF16)


def _s5_power_table(a_re, a_im, log_dt):
    expo = np.concatenate([np.arange(S5_L, dtype=np.float32) - (S5_L - 1) / 2.0, np.arange(S5_L + 1, dtype=np.float32)])
    expo = jnp.asarray(expo)[:, None]
    dt = jnp.exp(log_dt)[..., None, None]
    ar, ai = a_re[..., None, :] * dt, a_im[..., None, :] * dt
    mag = jnp.exp(expo * ar)
    return mag * jnp.cos(expo * ai), mag * jnp.sin(expo * ai)


def _s5_tables(a_re, a_im, log_dt, b_re, b_im, c_re, c_im):
    depth = a_re.shape[0]
    pwr, pwi = _s5_power_table(a_re, a_im, log_dt)
    dt = jnp.exp(log_dt)[..., None]
    mag = jnp.exp(a_re * dt)
    lb_re, lb_im = mag * jnp.cos(a_im * dt), mag * jnp.sin(a_im * dt)
    den = a_re * a_re + a_im * a_im
    n_re = lb_re - 1.0
    fz = jnp.stack([(n_re * a_re + lb_im * a_im) / den, (lb_im * a_re - n_re * a_im) / den], axis=3)
    mag_l = jnp.exp(S5_L * (a_re * dt))
    pl_re, pl_im = mag_l * jnp.cos(S5_L * (a_im * dt)), mag_l * jnp.sin(S5_L * (a_im * dt))
    lam = jnp.stack([jnp.concatenate([pl_re[:, 0], pl_re[:, 0]], -1), jnp.concatenate([-pl_im[:, 0], pl_im[:, 0]], -1),
                     jnp.concatenate([pl_re[:, 1], pl_re[:, 1]], -1), jnp.concatenate([-pl_im[:, 1], pl_im[:, 1]], -1)],
                    axis=2)
    per_lg = lambda *shape: pl.BlockSpec((None, None) + shape, lambda l, g: (l, g) + (0,) * len(shape))
    per_dir = lambda *shape: pl.BlockSpec((None, 2, None) + shape, lambda l, g: (l, 0, g) + (0,) * len(shape))
    m, s_in, s_out = pl.pallas_call(
        _s5_table_kernel,
        grid=(depth, S5_GROUPS),
        in_specs=[per_dir(S5_NEXP, S5_STATE), per_dir(S5_NEXP, S5_STATE), per_dir(2, S5_STATE),
                  per_lg(S5_CH, S5_STATE), per_lg(S5_CH, S5_STATE), per_lg(S5_CH, S5_STATE), per_lg(S5_CH, S5_STATE)],
        out_specs=[per_lg(S5_ROW, S5_ROW), per_lg(2, S5_ROW, 2 * S5_STATE), per_lg(2, S5_ROW, 2 * S5_STATE)],
        out_shape=[jax.ShapeDtypeStruct((depth, S5_GROUPS, S5_ROW, S5_ROW), BF16),
                   jax.ShapeDtypeStruct((depth, S5_GROUPS, 2, S5_ROW, 2 * S5_STATE), BF16),
                   jax.ShapeDtypeStruct((depth, S5_GROUPS, 2, S5_ROW, 2 * S5_STATE), BF16)],
        compiler_params=pltpu.CompilerParams(vmem_limit_bytes=VMEM_LIMIT),
        name="s5_tables",
    )(pwr, pwi, fz, jnp.swapaxes(b_re, -1, -2), jnp.swapaxes(b_im, -1, -2), c_re, c_im)
    return m, s_in, s_out, lam


def _s5_kernel(*refs, n, bsz, has_init):
    refs = list(refs)
    su_ref = refs.pop(0)
    m_ref = refs.pop(0)
    sin_ref = refs.pop(0)
    sout_ref = refs.pop(0)
    lam_ref = refs.pop(0)
    dv_ref = refs.pop(0)
    if has_init:
        h0_ref = refs.pop(0)
    y_ref = refs.pop(0)
    fin_ref = refs.pop(0)
    sfs, sbs, hpf, hnb = refs

    r = n * bsz
    gran = lax.broadcasted_iota(jnp.int32, (1, LANES), 1) // S5_CH
    halves_per_row = S5_ROW // LANES
    per_half = S5_L // halves_per_row

    def tok_rows(j):
        return pl.ds(j, r, stride=S5_L)

    def granule_transpose(arrs):
        arrs = list(arrs)
        bit = S5_GB // 2
        while bit:
            upper = (gran & bit) != 0
            for i in range(S5_GB):
                if i & bit:
                    continue
                lo, hi = arrs[i], arrs[i | bit]
                arrs[i] = jnp.where(upper, pltpu.roll(hi, bit * S5_CH, 1), lo)
                arrs[i | bit] = jnp.where(upper, hi, pltpu.roll(lo, LANES - bit * S5_CH, 1))
            bit //= 2
        return arrs

    halves = [granule_transpose([su_ref[tok_rows(hh * per_half + i), :] for i in range(per_half)])
              for hh in range(halves_per_row)]
    u = jnp.stack([jnp.concatenate([halves[hh][gl] for hh in range(halves_per_row)], axis=-1)
                   for gl in range(S5_GB)]).astype(BF16)

    bm_dims = (((2,), (1,)), ((0,), (0,)))
    bm_nt_dims = (((2,), (2,)), ((0,), (0,)))
    y = lax.dot_general(u, m_ref[...], bm_dims, preferred_element_type=F32)
    sfs[...] = lax.dot_general(u, sin_ref[:, 0], bm_dims, preferred_element_type=F32)
    sbs[...] = lax.dot_general(u, sin_ref[:, 1], bm_dims, preferred_element_type=F32)

    a_f, s_f = lam_ref[:, 0:1, :], lam_ref[:, 1:2, :]
    a_b, s_b = lam_ref[:, 2:3, :], lam_ref[:, 3:4, :]
    if has_init:
        hf = h0_ref[:, 0]
        hb = h0_ref[:, 1]
    else:
        hf = jnp.zeros((S5_GB, bsz, 2 * S5_STATE), F32)
        hb = jnp.zeros((S5_GB, bsz, 2 * S5_STATE), F32)
    swap = lambda a: pltpu.roll(a, S5_STATE, 2)
    hf_sw, hb_sw = swap(hf), swap(hb)
    for c in range(n):
        rows = pl.ds(c, bsz, stride=n)
        hpf[:, rows, :] = hf
        s_in = sfs[:, rows, :]
        hf, hf_sw = hf * a_f + hf_sw * s_f + s_in, hf_sw * a_f - hf * s_f + swap(s_in)
        rows = pl.ds(n - 1 - c, bsz, stride=n)
        hnb[:, rows, :] = hb
        s_in = sbs[:, rows, :]
        hb, hb_sw = hb * a_b + hb_sw * s_b + s_in, hb_sw * a_b - hb * s_b + swap(s_in)
    fin_ref[:, 0] = hf
    fin_ref[:, 1] = hb
    y = y + lax.dot_general(hpf[...].astype(BF16), sout_ref[:, 0], bm_nt_dims, preferred_element_type=F32)
    y = y + lax.dot_general(hnb[...].astype(BF16), sout_ref[:, 1], bm_nt_dims, preferred_element_type=F32)

    for hh in range(halves_per_row):
        by_token = granule_transpose([y[gl][:, hh * LANES:(hh + 1) * LANES] for gl in range(S5_GB)])
        for i in range(per_half):
            rows = tok_rows(hh * per_half + i)
            y_ref[rows, :] = by_token[i] + su_ref[rows, :] * dv_ref[...]


def _s5_call(su, row_block, tabs, layer, dvec, bsz, t, h0):
    m, s_in, s_out, lam = tabs
    n = t // S5_L
    r = n * bsz
    rows = bsz * t
    has_init = h0 is not None
    gspec = lambda *shape: pl.BlockSpec((S5_GB,) + shape, lambda g: (g,) + (0,) * len(shape))
    lspec = lambda *shape: pl.BlockSpec((None, S5_GB) + shape, lambda g: (layer, g) + (0,) * len(shape))
    in_specs = [pl.BlockSpec((rows, LANES), lambda g: (row_block, g)),
                lspec(S5_ROW, S5_ROW), lspec(2, S5_ROW, 2 * S5_STATE), lspec(2, S5_ROW, 2 * S5_STATE),
                lspec(4, 2 * S5_STATE), pl.BlockSpec((1, LANES), lambda g: (0, g))]
    args = [su, m, s_in, s_out, lam, dvec]
    if has_init:
        in_specs.append(gspec(2, bsz, 2 * S5_STATE))
        args.append(h0)
    return pl.pallas_call(
        functools.partial(_s5_kernel, n=n, bsz=bsz, has_init=has_init),
        grid=(S5_GROUPS // S5_GB,),
        in_specs=in_specs,
        out_specs=[pl.BlockSpec((rows, LANES), lambda g: (0, g)), gspec(2, bsz, 2 * S5_STATE)],
        out_shape=[jax.ShapeDtypeStruct((rows, S5_W), F32),
                   jax.ShapeDtypeStruct((S5_GROUPS, 2, bsz, 2 * S5_STATE), F32)],
        scratch_shapes=[pltpu.VMEM((S5_GB, r, 2 * S5_STATE), F32)] * 4,
        compiler_params=pltpu.CompilerParams(vmem_limit_bytes=VMEM_LIMIT),
        name="s5_scan",
    )(*args)


def _outproj_kernel(yrg_c_ref, yrg_l_ref, ys_c_ref, ys_l_ref, xc_ref, xl_ref, mod_ref, n2_ref, wo_ref, gw_ref, gb_ref,
                    rw_ref, rb_ref, x1_ref, xs_ref, pos_ref, wt_ref, meta_ref, *, n_ctx_tiles):
    is_ctx = pl.program_id(0) < n_ctx_tiles
    x_in = jnp.where(is_ctx, xc_ref[...], xl_ref[...])
    ys = jnp.where(is_ctx, ys_c_ref[...], ys_l_ref[...])
    yrg = jnp.where(is_ctx, yrg_c_ref[...], yrg_l_ref[...])
    s = 0.5 * ys * (1.0 + jnp.tanh(np.sqrt(2.0 / np.pi).astype(np.float32) * (ys + 0.044715 * (ys * ys * ys))))
    s = s * _sigmoid(jnp.dot(s.astype(BF16), gw_ref[...], preferred_element_type=F32) + gb_ref[...])
    m = (jnp.dot(yrg.astype(BF16), wo_ref[0:2 * HG_W, :], preferred_element_type=F32)
         + jnp.dot(s.astype(BF16), wo_ref[2 * HG_W:, :], preferred_element_type=F32))
    x1 = x_in + mod_ref[2:3, :] * m
    x1_ref[...] = x1
    h = x1 * lax.rsqrt(jnp.mean(x1 * x1, axis=-1, keepdims=True) + EPS) * n2_ref[...]
    h = h * (1.0 + mod_ref[4:5, :]) + mod_ref[3:4, :]

    h_hi = h.astype(BF16)
    h_lo = (h - h_hi.astype(F32)).astype(BF16)
    pp = jnp.dot(jnp.concatenate([h_hi, h_lo], axis=0), rw_ref[...], preferred_element_type=F32)
    logits = ((pp[:TM, :LANES] + pp[TM:, :LANES]) + (pp[:TM, LANES:] + pp[TM:, LANES:])) + rb_ref[...]
    lane = lax.broadcasted_iota(jnp.int32, (TM, LANES), 1).astype(F32)
    cur = logits
    vals, idxs = [], []
    for _ in range(TOP_K):
        mx = jnp.max(cur, axis=-1, keepdims=True)
        am = jnp.min(jnp.where(cur == mx, lane, float(LANES)), axis=-1, keepdims=True)
        vals.append(mx)
        idxs.append(am)
        cur = jnp.where(lane == am, -jnp.inf, cur)
    es = [jnp.exp(v - vals[0]) for v in vals]
    den = es[0] + es[1] + es[2] + es[3]

    hit = [(lane == am) for am in idxs]
    assign = (hit[0] | hit[1] | hit[2] | hit[3]).astype(BF16)
    ti = lax.broadcasted_iota(jnp.int32, (TM, TM), 0)
    tj = lax.broadcasted_iota(jnp.int32, (TM, TM), 1)
    rank = jnp.dot((tj < ti).astype(BF16), assign, preferred_element_type=F32)
    cnt = jnp.sum(assign.astype(F32), axis=0, keepdims=True)
    cnt_al = jnp.floor((cnt + (SEG_ALIGN - 1)) * (1.0 / SEG_ALIGN)) * SEG_ALIGN
    ei = lax.broadcasted_iota(jnp.int32, (LANES, LANES), 0)
    ej = lax.broadcasted_iota(jnp.int32, (LANES, LANES), 1)
    seg = jnp.dot(jnp.broadcast_to(cnt_al, (SUBLANES, LANES)).astype(BF16), (ei < ej).astype(BF16),
                  preferred_element_type=F32)[0:1, :]
    base = seg + rank
    rlane = lax.broadcasted_iota(jnp.int32, (TM, ROWS_T), 1).astype(F32)
    klane = lax.broadcasted_iota(jnp.int32, (TM, LANES), 1)
    onehot = jnp.zeros((TM, ROWS_T), F32)
    pos_out = jnp.zeros((TM, LANES), F32)
    wt_out = jnp.zeros((TM, LANES), F32)
    for kk in range(TOP_K):
        pk = jnp.sum(jnp.where(hit[kk], base, 0.0), axis=-1, keepdims=True)
        onehot = onehot + (rlane == pk).astype(F32)
        pos_out = jnp.where(klane == kk, pk, pos_out)
        wt_out = jnp.where(klane == kk, es[kk] / den, wt_out)
    xs_ref[0] = _rows_to_chunks(lax.dot_general(onehot.astype(BF16), h.astype(BF16), (((0,), (0,)), ((), ())),
                                                preferred_element_type=F32))
    pos_ref[...] = pos_out.astype(jnp.int32)
    wt_ref[...] = wt_out
    mrow = lax.broadcasted_iota(jnp.int32, (SUBLANES, LANES), 0)
    meta = jnp.where(mrow == 0, jnp.broadcast_to(cnt_al, (SUBLANES, LANES)),
                     jnp.where(mrow == 1, jnp.broadcast_to(seg, (SUBLANES, LANES)), 0.0))
    meta_ref[0] = meta.astype(jnp.int32)


def _outproj_call(yrg_c, yrg_l, ys_c, ys_l, x_c, x_l, mod_l, n2, wo, gw, gb, rw, rb, n_ctx_tiles, tiles_per_lat):
    n = x_c.shape[0] + x_l.shape[0]
    nt = n // TM
    gid = functools.partial(_group_id, n_ctx_tiles=n_ctx_tiles, tiles_per_lat=tiles_per_lat)
    const = lambda *shape: pl.BlockSpec(shape, lambda i: (0,) * len(shape))
    ctx_map, lat_map = _path_maps(n_ctx_tiles)
    return pl.pallas_call(
        functools.partial(_outproj_kernel, n_ctx_tiles=n_ctx_tiles),
        grid=(nt,),
        in_specs=[
            pl.BlockSpec((TM, 2 * HG_W), ctx_map),
            pl.BlockSpec((TM, 2 * HG_W), lat_map),
            pl.BlockSpec((TM, S5_W), ctx_map),
            pl.BlockSpec((TM, S5_W), lat_map),
            pl.BlockSpec((TM, D_MODEL), ctx_map),
            pl.BlockSpec((TM, D_MODEL), lat_map),
            pl.BlockSpec((None, 6, D_MODEL), lambda i: (gid(i), 0, 0)),
            const(1, D_MODEL), const(D_MODEL, D_MODEL), const(S5_W, S5_W), const(1, S5_W),
            const(D_MODEL, 2 * LANES), const(1, LANES),
        ],
        out_specs=[
            pl.BlockSpec((TM, D_MODEL), lambda i: (i, 0)),
            pl.BlockSpec((1, CHUNKS_T, 2 * SEG_ALIGN, D_MODEL // 2), lambda i: (i, 0, 0, 0)),
            pl.BlockSpec((TM, LANES), lambda i: (i, 0)),
            pl.BlockSpec((TM, LANES), lambda i: (i, 0)),
            pl.BlockSpec((1, SUBLANES, LANES), lambda i: (i, 0, 0)),
        ],
        out_shape=[
            jax.ShapeDtypeStruct((n, D_MODEL), F32),
            jax.ShapeDtypeStruct((nt, CHUNKS_T, 2 * SEG_ALIGN, D_MODEL // 2), BF16),
            jax.ShapeDtypeStruct((n, LANES), jnp.int32),
            jax.ShapeDtypeStruct((n, LANES), F32),
            jax.ShapeDtypeStruct((nt, SUBLANES, LANES), jnp.int32),
        ],
        compiler_params=pltpu.CompilerParams(vmem_limit_bytes=VMEM_LIMIT),
        name="outproj_router_dispatch",
    )(yrg_c, yrg_l, ys_c, ys_l, x_c, x_l, mod_l, n2, wo, gw, gb, rw, rb)


SCHED_COLS = 6


def _schedule_kernel(meta_ref, blk_ref, cst_ref, nch_ref, srcb_ref, *, nt, nblk_pad):
    exact = functools.partial(jnp.dot, precision=HIGHEST, preferred_element_type=F32)
    nch = meta_ref[:, 0, :].astype(F32) * (1.0 / SEG_ALIGN)
    seg = meta_ref[:, 1, :].astype(F32) * (1.0 / SEG_ALIGN)
    ti = lax.broadcasted_iota(jnp.int32, (nt, nt), 0)
    tj = lax.broadcasted_iota(jnp.int32, (nt, nt), 1)
    cs_end = exact((tj <= ti).astype(F32), nch)
    cs_start = cs_end - nch
    tot = cs_end[nt - 1:nt, :]
    nb = jnp.floor((tot + (BLK_CHUNKS - 1)) * (1.0 / BLK_CHUNKS))
    ei = lax.broadcasted_iota(jnp.int32, (LANES, LANES), 0)
    ej = lax.broadcasted_iota(jnp.int32, (LANES, LANES), 1)
    blk_end = exact(jnp.broadcast_to(nb, (SUBLANES, LANES)), (ei <= ej).astype(F32))[0:1, :]
    blk_start = blk_end - nb
    n_active = blk_end[:, LANES - 1:LANES]
    j = lax.broadcasted_iota(jnp.int32, (nblk_pad, 1), 0).astype(F32)
    lane = lax.broadcasted_iota(jnp.int32, (nblk_pad, LANES), 1).astype(F32)
    row_sum = lambda a: jnp.sum(a, axis=-1, keepdims=True)
    be = row_sum((blk_end <= jnp.minimum(j, n_active - 1.0)).astype(F32))
    be = jnp.minimum(be, float(N_EXPERTS - 1))
    own = (lane == be).astype(F32)
    bs = (j - row_sum(own * blk_start)) * BLK_CHUNKS
    nvalid = jnp.where(j < n_active, jnp.clip(row_sum(own * tot) - bs, 0.0, float(BLK_CHUNKS)), 0.0)
    nt_dims = (((1,), (1,)), ((), ()))
    own_end = lax.dot_general(own, cs_end, nt_dims, precision=HIGHEST, preferred_element_type=F32)
    own_start = lax.dot_general(own, cs_start, nt_dims, precision=HIGHEST, preferred_element_type=F32)
    kfirst = row_sum((own_end <= bs).astype(F32))
    klast = row_sum((own_start < bs + nvalid).astype(F32))
    later = (ej > ei) & (jnp.broadcast_to(tot, (LANES, LANES)) > 0.0)
    nxt_e = jnp.min(jnp.where(later, ej.astype(F32), float(LANES)), axis=-1, keepdims=True)
    nxt = exact(own, jnp.broadcast_to(nxt_e, (LANES, LANES)))[:, 0:1]
    nxt = jnp.where(nxt >= float(N_EXPERTS), -1.0, nxt)
    cols = (be, nvalid, bs, kfirst, klast, nxt)
    out = jnp.zeros((nblk_pad, LANES), F32)
    for k_col, val in enumerate(cols):
        out = jnp.where(lane == float(k_col), val, out)
    blk_ref[...] = out.astype(jnp.int32)
    tile = lax.broadcasted_iota(jnp.int32, (nt, LANES), 0).astype(F32)
    cst_ref[...] = cs_start.astype(jnp.int32)
    nch_ref[...] = nch.astype(jnp.int32)
    srcb_ref[...] = (tile * CHUNKS_T + seg).astype(jnp.int32)


def _expert_schedule(meta, nblk):
    nt = meta.shape[0]
    nblk_pad = -(-nblk // SUBLANES) * SUBLANES
    whole = lambda *shape: pl.BlockSpec(shape, lambda i: (0,) * len(shape))
    blk, cst, nch, srcb = pl.pallas_call(
        functools.partial(_schedule_kernel, nt=nt, nblk_pad=nblk_pad),
        grid=(1,),
        in_specs=[whole(nt, SUBLANES, LANES)],
        out_specs=[whole(nblk_pad, LANES), whole(nt, LANES), whole(nt, LANES), whole(nt, LANES)],
        out_shape=[jax.ShapeDtypeStruct((nblk_pad, LANES), jnp.int32)] + [jax.ShapeDtypeStruct((nt, LANES), jnp.int32)] * 3,
        name="expert_schedule",
    )(meta)
    return tuple(blk[:nblk, k] for k in range(SCHED_COLS)) + (cst.reshape(-1), nch.reshape(-1), srcb.reshape(-1))


W_CAST_ROWS = 128


def _ffn_kernel(be_ref, nv_ref, bs_ref, kf_ref, kl_ref, nxt_ref, cst_ref, nch_ref, srcb_ref,
                xs_hbm, w1_hbm, b1_ref, w2_hbm, b2_ref, ys_hbm,
                xbuf, ybuf, w1s, w2s, w1b, w2b, gsem, ssem, wsem, *, nblk, layer):
    j = pl.program_id(0)
    slot = j % 2

    def for_pieces(blk, fn):
        e = be_ref[blk]
        lo_b = bs_ref[blk]
        hi_b = lo_b + nv_ref[blk]

        def body(i, carry):
            k = i * LANES + e
            ps = cst_ref[k]
            lo = jnp.maximum(ps, lo_b)
            n = jnp.minimum(ps + nch_ref[k], hi_b) - lo

            @pl.when(n > 0)
            def _():
                fn(srcb_ref[k] + (lo - ps), lo - lo_b, n)
            return carry

        lax.fori_loop(kf_ref[blk], kl_ref[blk], body, 0)

    def start_gather(blk, s):
        for_pieces(blk, lambda src, dst, n: pltpu.make_async_copy(
            xs_hbm.at[pl.ds(src, n)], xbuf.at[s, pl.ds(dst, n)], gsem.at[s]).start())

    def start_scatter(blk, s):
        for_pieces(blk, lambda src, dst, n: pltpu.make_async_copy(
            ybuf.at[s, pl.ds(dst, n)], ys_hbm.at[pl.ds(src, n)], ssem.at[s]).start())

    def wait_gather(blk, s):
        @pl.when(nv_ref[blk] > 0)
        def _():
            pltpu.make_async_copy(xs_hbm.at[pl.ds(0, nv_ref[blk])], xbuf.at[s, pl.ds(0, nv_ref[blk])],
                                  gsem.at[s]).wait()

    def wait_scatter(blk, s):
        @pl.when(nv_ref[blk] > 0)
        def _():
            pltpu.make_async_copy(ybuf.at[s, pl.ds(0, nv_ref[blk])], ys_hbm.at[pl.ds(0, nv_ref[blk])],
                                  ssem.at[s]).wait()

    def weight_copies(e):
        return (pltpu.make_async_copy(w1_hbm.at[layer, e], w1s, wsem.at[0]),
                pltpu.make_async_copy(w2_hbm.at[layer, e], w2s, wsem.at[1]))

    @pl.when(j == 0)
    def _():
        for cp in weight_copies(be_ref[0]):
            cp.start()
        xbuf[...] = jnp.zeros_like(xbuf)
        start_gather(0, 0)

    @pl.when(j + 1 < nblk)
    def _():
        start_gather(j + 1, 1 - slot)

    @pl.when((j == 0) | (be_ref[j] != be_ref[jnp.maximum(j - 1, 0)]))
    def _():
        for cp in weight_copies(be_ref[j]):
            cp.wait()

        def cast_rows(r, carry):
            rows = pl.ds(pl.multiple_of(r * W_CAST_ROWS, W_CAST_ROWS), W_CAST_ROWS)
            w1b[rows, :] = w1s[rows, :].astype(BF16)
            w2b[rows, :] = w2s[rows, :].astype(BF16)
            return carry
        lax.fori_loop(0, D_MODEL // W_CAST_ROWS, cast_rows, 0)

        @pl.when(nxt_ref[j] >= 0)
        def _():
            for cp in weight_copies(nxt_ref[j]):
                cp.start()

    wait_gather(j, slot)

    @pl.when(j >= 2)
    def _():
        wait_scatter(j - 2, slot)

    def compute_rows(first_chunk, nchunks):
        cs = pl.ds(pl.multiple_of(first_chunk, TAIL_CHUNKS), nchunks)
        x = _chunks_to_rows(xbuf[slot, cs])
        gu = jnp.dot(x, w1b[...], preferred_element_type=F32) + b1_ref[...]
        gate = jnp.minimum(gu[:, :D_FF], SWIGLU_LIMIT)
        up = jnp.clip(gu[:, D_FF:], -SWIGLU_LIMIT, SWIGLU_LIMIT)
        act = (up + 1.0) * gate * _sigmoid(SWIGLU_ALPHA * gate)
        y = jnp.dot(act.astype(BF16), w2b[...], preferred_element_type=F32) + b2_ref[...]
        ybuf[slot, cs] = _rows_to_chunks(y)

    nv = nv_ref[j]
    n_full = (nv + (SUB_CHUNKS - TAIL_CHUNKS) - 1) // SUB_CHUNKS
    lax.fori_loop(0, n_full, lambda sb, carry: (compute_rows(sb * SUB_CHUNKS, SUB_CHUNKS), carry)[1], 0)

    @pl.when(nv > n_full * SUB_CHUNKS)
    def _():
        compute_rows(n_full * SUB_CHUNKS, TAIL_CHUNKS)

    start_scatter(j, slot)

    @pl.when(j == nblk - 1)
    def _():
        wait_scatter(j, slot)
        if nblk >= 2:
            wait_scatter(j - 1, 1 - slot)


def _ffn_call(sched, xs_chunks, w1, b1, w2, b2, layer):
    assert D_FF == D_MODEL
    nblk = sched[0].shape[0]
    bmap = lambda j, be, *_: (layer, be[j], 0, 0)
    grid_spec = pltpu.PrefetchScalarGridSpec(
        num_scalar_prefetch=len(sched),
        grid=(nblk,),
        in_specs=[
            pl.BlockSpec(memory_space=pl.ANY),
            pl.BlockSpec(memory_space=pl.ANY),
            pl.BlockSpec((None, None, 1, 2 * D_FF), bmap),
            pl.BlockSpec(memory_space=pl.ANY),
            pl.BlockSpec((None, None, 1, D_MODEL), bmap),
        ],
        out_specs=pl.BlockSpec(memory_space=pl.ANY),
        scratch_shapes=[
            pltpu.VMEM((2, BLK_CHUNKS, 2 * SEG_ALIGN, D_MODEL // 2), BF16),
            pltpu.VMEM((2, BLK_CHUNKS, 2 * SEG_ALIGN, D_MODEL // 2), BF16),
            pltpu.VMEM((D_MODEL, 2 * D_FF), F32),
            pltpu.VMEM((D_FF, D_MODEL), F32),
            pltpu.VMEM((D_MODEL, 2 * D_FF), BF16),
            pltpu.VMEM((D_FF, D_MODEL), BF16),
            pltpu.SemaphoreType.DMA((2,)),
            pltpu.SemaphoreType.DMA((2,)),
            pltpu.SemaphoreType.DMA((2,)),
        ],
    )
    return pl.pallas_call(
        functools.partial(_ffn_kernel, nblk=nblk, layer=layer),
        grid_spec=grid_spec,
        out_shape=jax.ShapeDtypeStruct(xs_chunks.shape, BF16),
        input_output_aliases={len(sched): 0},
        compiler_params=pltpu.CompilerParams(vmem_limit_bytes=VMEM_LIMIT, dimension_semantics=("arbitrary",)),
        name="expert_ffn",
    )(*sched, xs_chunks, w1, b1, w2, b2)


def _combine_kernel(ys_ref, pos_ref, wt_ref, x1_ref, mod_ref, fg_ref, *out_refs, final, n_ctx_tiles):
    ys = _chunks_to_rows(ys_ref[0])
    rlane = lax.broadcasted_iota(jnp.int32, (TM, ROWS_T), 1)
    w = jnp.zeros((TM, ROWS_T), F32)
    for kk in range(TOP_K):
        w = w + jnp.where(rlane == pos_ref[:, kk:kk + 1], wt_ref[:, kk:kk + 1], 0.0)
    moe = jnp.dot(w.astype(BF16), ys, preferred_element_type=F32)
    y = x1_ref[...] + mod_ref[5:6, :] * moe
    if final:
        y = y * lax.rsqrt(jnp.mean(y * y, axis=-1, keepdims=True) + EPS) * fg_ref[...]
    is_ctx = pl.program_id(0) < n_ctx_tiles

    @pl.when(is_ctx)
    def _():
        out_refs[0][...] = y

    @pl.when(jnp.logical_not(is_ctx))
    def _():
        out_refs[1][...] = y


def _combine_call(ys, pos, wts, x1, mod_l, fg, n_ctx_tiles, tiles_per_lat, final):
    n = x1.shape[0]
    nt = n // TM
    gid = functools.partial(_group_id, n_ctx_tiles=n_ctx_tiles, tiles_per_lat=tiles_per_lat)
    tok = pl.BlockSpec((TM, D_MODEL), lambda i: (i, 0))
    ctx_map, lat_map = _path_maps(n_ctx_tiles)
    out_specs = [pl.BlockSpec((TM, D_MODEL), ctx_map), pl.BlockSpec((TM, D_MODEL), lat_map)]
    out_shape = [jax.ShapeDtypeStruct((n_ctx_tiles * TM, D_MODEL), F32),
                 jax.ShapeDtypeStruct((n - n_ctx_tiles * TM, D_MODEL), F32)]
    res = pl.pallas_call(
        functools.partial(_combine_kernel, final=final, n_ctx_tiles=n_ctx_tiles),
        grid=(nt,),
        in_specs=[
            pl.BlockSpec((1, CHUNKS_T, 2 * SEG_ALIGN, D_MODEL // 2), lambda i: (i, 0, 0, 0)),
            pl.BlockSpec((TM, LANES), lambda i: (i, 0)),
            pl.BlockSpec((TM, LANES), lambda i: (i, 0)),
            tok,
            pl.BlockSpec((None, 6, D_MODEL), lambda i: (gid(i), 0, 0)),
            pl.BlockSpec((1, D_MODEL), lambda i: (0, 0)),
        ],
        out_specs=out_specs,
        out_shape=out_shape,
        compiler_params=pltpu.CompilerParams(vmem_limit_bytes=VMEM_LIMIT),
        name="moe_combine",
    )(ys, pos, wts, x1, mod_l, fg)
    return res


def _rope_tables(t):
    pos = np.arange(t)
    nf = HEAD_DIM // 4
    inv = jnp.asarray(ROPE_BASE, F32) ** (-jnp.arange(nf, dtype=F32) / nf)
    ang = jnp.concatenate([jnp.asarray(pos // GRID_W, F32)[:, None] * inv,
                           jnp.asarray(pos % GRID_W, F32)[:, None] * inv], axis=-1)
    cos, sin = jnp.cos(ang), jnp.sin(ang)
    return (jnp.tile(jnp.concatenate([cos, cos], -1), (1, N_HEADS)),
            jnp.tile(jnp.concatenate([-sin, sin], -1), (1, N_HEADS)))


def _block_diag_t(s):
    eye = jnp.eye(N_HEADS, dtype=s.dtype)
    out = jnp.einsum('...hkv,hg->...hvgk', s, eye)
    return out.reshape(s.shape[:-3] + (HG_W, HG_W))


def kernel(x_prompt, x_sample, state_ret, state_gla, state_s5_re, state_s5_im, c, c_ctx, norm1_g, norm2_g, final_g, ada_w, ada_b, w_in, w_out, ret_decay, gla_w_lr, gla_b_lr, gla_norm_g, s5_a_re, s5_a_im, s5_log_dt, s5_b_re, s5_b_im, s5_c_re, s5_c_im, s5_d, s5_glu_w, s5_glu_b, router_w, router_b, moe_w1, moe_b1, moe_w2, moe_b2):
    bc, tc, d = x_prompt.shape
    bl, tl, _ = x_sample.shape
    depth = w_in.shape[0]
    nc, nl = bc * tc, bl * tl
    n = nc + nl
    assert d == D_MODEL and nc % TM == 0 and tl % TM == 0 and tc % CHUNK == 0 and tl % CHUNK == 0
    assert 1 + bl <= SUBLANES and nc % nl == 0
    n_ctx_tiles, tiles_per_lat = nc // TM, tl // TM
    nt = n // TM
    nblk = nt * CHUNKS_T // BLK_CHUNKS + N_EXPERTS

    x_c, x_l = x_prompt.reshape(nc, d), x_sample.reshape(nl, d)
    cond8 = jnp.concatenate([c_ctx[None, :], c, jnp.zeros((SUBLANES - 1 - bl, d), F32)], axis=0)
    mod = _ada_call(cond8, ada_w, ada_b).reshape(depth, SUBLANES, 6, d)
    rope = _rope_tables(tl)
    fg = final_g.reshape(1, d)
    s5_tabs = _s5_tables(s5_a_re, s5_a_im, s5_log_dt, s5_b_re, s5_b_im, s5_c_re, s5_c_im)

    ret_states, gla_states, re_states, im_states = [], [], [], []
    for l in range(depth):
        wl = w_in[l]
        w_cat = jnp.concatenate(
            [wl[:, :8 * HG_W], wl[:, 8 * HG_W + GLA_RANK:],
             jnp.pad(wl[:, 8 * HG_W:8 * HG_W + GLA_RANK], ((0, 0), (0, LANES - GLA_RANK)))], axis=1).astype(BF16)
        wlr = jnp.pad(gla_w_lr[l], ((0, 0), (0, LANES - GLA_RANK), (0, 0)))
        blr = gla_b_lr[l].reshape(2, 1, HG_W)
        pa, su = _inproj_call(x_c, x_l, mod[l], norm1_g[l].reshape(1, d), w_cat, wlr, blr, tl)

        log_gamma = jnp.log1p(-jnp.exp(ret_decay[l]))
        lgam = jnp.repeat(log_gamma, HEAD_DIM, axis=1)
        gn = jnp.tile(gla_norm_g[l], N_HEADS).reshape(1, HG_W)
        y_c, fin_ret, fin_gla = _scan_call(pa, 0, bc, tc, log_gamma, lgam, gn, None, None, True)
        st0 = jnp.stack([_block_diag_t(state_ret[:, l]), _block_diag_t(state_gla[:, l])], axis=2)
        y_l, _, _ = _scan_call(pa, nc, bl, tl, log_gamma, lgam, gn, rope, st0, False)
        ret_states.append(fin_ret)
        gla_states.append(fin_gla)

        dvec = s5_d[l].reshape(1, S5_W)
        ys_c, fin_c = _s5_call(su, 0, s5_tabs, l, dvec, bc, tc, None)
        h0 = jnp.concatenate([state_s5_re[:, l], state_s5_im[:, l]], axis=-1).transpose(2, 1, 0, 3)
        ys_l, _ = _s5_call(su, nc // nl, s5_tabs, l, dvec, bl, tl, h0)
        re_states.append(fin_c[..., :S5_STATE].transpose(2, 1, 0, 3))
        im_states.append(fin_c[..., S5_STATE:].transpose(2, 1, 0, 3))

        rw = jnp.pad(router_w[l], ((0, 0), (0, LANES - N_EXPERTS)))
        rw_hi = rw.astype(BF16)
        rw = jnp.concatenate([rw_hi, (rw - rw_hi.astype(F32)).astype(BF16)], axis=1)
        rb = jnp.concatenate([router_b[l], jnp.full((LANES - N_EXPERTS,), NEG_BIG, F32)]).reshape(1, LANES)
        x1, xs, pos, wts, meta = _outproj_call(
            y_c, y_l, ys_c, ys_l, x_c, x_l, mod[l], norm2_g[l].reshape(1, d), w_out[l].astype(BF16), s5_glu_w[l].astype(BF16),
            s5_glu_b[l].reshape(1, S5_W), rw, rb, n_ctx_tiles, tiles_per_lat)

        sched = _expert_schedule(meta, nblk)
        ys = _ffn_call(sched, xs.reshape(nt * CHUNKS_T, 2 * SEG_ALIGN, d // 2),
                       moe_w1, moe_b1.reshape(depth, N_EXPERTS, 1, 2 * D_FF),
                       moe_w2, moe_b2.reshape(depth, N_EXPERTS, 1, d), l)
        res = _combine_call(ys.reshape(nt, CHUNKS_T, 2 * SEG_ALIGN, d // 2), pos, wts, x1, mod[l], fg,
                            n_ctx_tiles, tiles_per_lat, l == depth - 1)
        x_c, x_l = res
    y_prompt = x_c.reshape(bc, tc, d)
    y_sample = x_l.reshape(bl, tl, d)
    return (y_prompt, y_sample, jnp.stack(ret_states, axis=1), jnp.stack(gla_states, axis=1),
            jnp.stack(re_states, axis=1), jnp.stack(im_states, axis=1))
```

```python
import functools

import numpy as np
import jax
import jax.numpy as jnp
from jax import lax
from jax.experimental import pallas as pl
from jax.experimental.pallas import tpu as pltpu

F32 = jnp.float32
BF16 = jnp.bfloat16
HIGHEST = lax.Precision.HIGHEST

D_MODEL = 1024
GRID_W = 64
CHUNK = 128
HEAD_DIM = 64
N_HEADS = 4
HG_W = N_HEADS * HEAD_DIM
S5_W = 512
S5_CH = 16
S5_GROUPS = 32
S5_STATE = 64
S5_L = 16
S5_ROW = S5_L * S5_CH
GLA_RANK = 16
GLA_TAU = 16.0
N_EXPERTS = 32
TOP_K = 4
D_FF = 1024
SWIGLU_LIMIT = 7.0
SWIGLU_ALPHA = 1.702
ROPE_BASE = 10000.0
EPS = 1e-6

LANES = 128
SUBLANES = 8
TM = 256
TM_IN = 512
SEG_ALIGN = SUBLANES
ROWS_T = -(-(TOP_K * TM + N_EXPERTS * (SEG_ALIGN - 1)) // LANES) * LANES
CHUNKS_T = ROWS_T // SEG_ALIGN
BLK_CHUNKS = 128
SUB_CHUNKS = 64
TAIL_CHUNKS = 32
NEG_BIG = -1e30
VMEM_LIMIT = 56 * 1024 * 1024

PA_W = 8 * HG_W + 2 * HG_W
W_CAT = 8 * HG_W + S5_W + LANES


def _sigmoid(x):
    return 1.0 / (1.0 + jnp.exp(-x))


def _rows_to_chunks(x):
    x3 = x.reshape(x.shape[0] // SEG_ALIGN, SEG_ALIGN, D_MODEL)
    return jnp.concatenate([x3[:, :, :D_MODEL // 2], x3[:, :, D_MODEL // 2:]], axis=1).astype(BF16)


def _chunks_to_rows(c):
    c3 = c.astype(F32)
    rows = jnp.concatenate([c3[:, :SEG_ALIGN, :], c3[:, SEG_ALIGN:, :]], axis=2)
    return rows.reshape(c.shape[0] * SEG_ALIGN, D_MODEL).astype(BF16)


def _group_id(i, n_ctx_tiles, tiles_per_lat):
    return jnp.where(i < n_ctx_tiles, 0, 1 + (i - n_ctx_tiles) // tiles_per_lat)


def _ada_kernel(c_ref, w_ref, b_ref, o_ref):
    c = c_ref[...]
    s = c * _sigmoid(c)
    o_ref[0] = jnp.dot(s, w_ref[0], precision=HIGHEST, preferred_element_type=F32) + b_ref[0]


def _ada_call(cond8, ada_w, ada_b):
    depth, d, n6 = ada_w.shape
    tn = 1024
    return pl.pallas_call(
        _ada_kernel,
        grid=(depth, n6 // tn),
        in_specs=[
            pl.BlockSpec((SUBLANES, d), lambda l, j: (0, 0)),
            pl.BlockSpec((1, d, tn), lambda l, j: (l, 0, j)),
            pl.BlockSpec((1, 1, tn), lambda l, j: (l, 0, j)),
        ],
        out_specs=pl.BlockSpec((1, SUBLANES, tn), lambda l, j: (l, 0, j)),
        out_shape=jax.ShapeDtypeStruct((depth, SUBLANES, n6), F32),
        compiler_params=pltpu.CompilerParams(vmem_limit_bytes=VMEM_LIMIT),
        name="ada_mod",
    )(cond8, ada_w, ada_b.reshape(depth, 1, n6))


def _inproj_kernel(xc_ref, xl_ref, mod_ref, n1_ref, w_ref, wlr_ref, blr_ref, pa_ref, su_ref, *, n_ctx_tiles):
    x = jnp.where(pl.program_id(0) < n_ctx_tiles, xc_ref[...], xl_ref[...])
    h = x * lax.rsqrt(jnp.mean(x * x, axis=-1, keepdims=True) + EPS) * n1_ref[...]
    h = h * (1.0 + mod_ref[1:2, :]) + mod_ref[0:1, :]
    r = jnp.dot(h.astype(BF16), w_ref[...], preferred_element_type=F32)
    pa_ref[:, : 8 * HG_W] = r[:, : 8 * HG_W]
    su_ref[...] = r[:, 8 * HG_W: 8 * HG_W + S5_W]
    glr = r[:, 8 * HG_W + S5_W:]
    for d in range(2):
        z = jnp.dot(glr, wlr_ref[d], precision=HIGHEST, preferred_element_type=F32) + blr_ref[d]
        log_sig = jnp.minimum(z, 0.0) - jnp.log(1.0 + jnp.exp(-jnp.abs(z)))
        pa_ref[:, (8 + d) * HG_W: (9 + d) * HG_W] = log_sig * (1.0 / GLA_TAU)


def _path_maps(n_ctx_tiles):
    return (lambda i: (jnp.minimum(i, n_ctx_tiles - 1), 0)), (lambda i: (jnp.maximum(i - n_ctx_tiles, 0), 0))


def _inproj_call(x_c, x_l, mod_l, n1, w_cat, wlr, blr, lat_seq):
    n = x_c.shape[0] + x_l.shape[0]
    assert x_c.shape[0] % TM_IN == 0 and lat_seq % TM_IN == 0
    n_ctx_tiles, tiles_per_lat = x_c.shape[0] // TM_IN, lat_seq // TM_IN
    gid = functools.partial(_group_id, n_ctx_tiles=n_ctx_tiles, tiles_per_lat=tiles_per_lat)
    ctx_map, lat_map = _path_maps(n_ctx_tiles)
    return pl.pallas_call(
        functools.partial(_inproj_kernel, n_ctx_tiles=n_ctx_tiles),
        grid=(n // TM_IN,),
        in_specs=[
            pl.BlockSpec((TM_IN, D_MODEL), ctx_map),
            pl.BlockSpec((TM_IN, D_MODEL), lat_map),
            pl.BlockSpec((None, 6, D_MODEL), lambda i: (gid(i), 0, 0)),
            pl.BlockSpec((1, D_MODEL), lambda i: (0, 0)),
            pl.BlockSpec((D_MODEL, W_CAT), lambda i: (0, 0)),
            pl.BlockSpec((2, LANES, HG_W), lambda i: (0, 0, 0)),
            pl.BlockSpec((2, 1, HG_W), lambda i: (0, 0, 0)),
        ],
        out_specs=[
            pl.BlockSpec((TM_IN, PA_W), lambda i: (i, 0)),
            pl.BlockSpec((TM_IN, S5_W), lambda i: (i, 0)),
        ],
        out_shape=[
            jax.ShapeDtypeStruct((n, PA_W), F32),
            jax.ShapeDtypeStruct((n, S5_W), F32),
        ],
        compiler_params=pltpu.CompilerParams(vmem_limit_bytes=VMEM_LIMIT),
        name="norm1_inproj",
    )(x_c, x_l, mod_l, n1, w_cat, wlr, blr)


SCAN_PB_MAX = 8


def _scan_kernel(*refs, n, pb, use_rope, has_init, want_final):
    refs = list(refs)
    lg_ref = refs.pop(0)
    pa_ref = refs.pop(0)
    if use_rope:
        cos_ref = refs.pop(0)
        sin_ref = refs.pop(0)
    lgam_ref = refs.pop(0)
    gn_ref = refs.pop(0)
    if has_init:
        st0_ref = refs.pop(0)
    y_ref = refs.pop(0)
    if want_final:
        fin_refs = [refs.pop(0), refs.pop(0)]
    st_scr = refs.pop(0)
    of_scr = refs.pop(0)

    d = pl.program_id(1)
    c = pl.program_id(2)
    cc = jnp.where(d == 0, c, n - 1 - c)

    @pl.when(c == 0)
    def _():
        if has_init:
            st_scr[...] = st0_ref[...]
        else:
            st_scr[...] = jnp.zeros_like(st_scr)

    ii = lax.broadcasted_iota(jnp.int32, (CHUNK, CHUNK), 0)
    jj = lax.broadcasted_iota(jnp.int32, (CHUNK, CHUNK), 1)
    rel = (1 - 2 * d) * (ii - jj)
    mask = rel >= 0
    i4 = lax.broadcasted_iota(jnp.int32, (CHUNK, N_HEADS * CHUNK), 0)
    j4 = lax.broadcasted_iota(jnp.int32, (CHUNK, N_HEADS * CHUNK), 1) % CHUNK
    mask4 = (1 - 2 * d) * (i4 - j4) >= 0
    relf = rel.astype(F32)
    row = lax.broadcasted_iota(jnp.int32, (CHUNK, 1), 0)
    p1 = jnp.where(d == 0, row + 1, CHUNK - row).astype(F32)
    lane = lax.broadcasted_iota(jnp.int32, (1, HG_W), 1)
    head_mask = [(lane // HEAD_DIM == h).astype(F32) for h in range(N_HEADS)]
    bi = lax.broadcasted_iota(jnp.int32, (HG_W, HG_W), 0) // HEAD_DIM
    bj = lax.broadcasted_iota(jnp.int32, (HG_W, HG_W), 1) // HEAD_DIM
    block_diag = bi == bj
    tri = mask.astype(BF16)
    lgl = lgam_ref[pl.ds(d, 1), :]
    ret_q_decay = jnp.exp(p1 * lgl)
    ret_k_decay = jnp.exp((CHUNK - p1) * lgl)
    ret_chunk_decay = jnp.exp(CHUNK * lgl)
    ret_weight = jnp.concatenate([jnp.where(mask, jnp.exp(relf * lg_ref[d, h]), 0.0) for h in range(N_HEADS)], axis=1)

    nt_dims = (((1,), (1,)), ((), ()))
    tn_dims = (((0,), (0,)), ((), ()))

    def split3(x):
        hi = x.astype(BF16)
        r1 = x - hi.astype(F32)
        mid = r1.astype(BF16)
        return hi, mid, (r1 - mid.astype(F32)).astype(BF16)

    def group(q, k, v, q_dec, k_upd, chunk_decay, score_weight, bb, g_idx):
        vb = v.astype(BF16)
        ks = jnp.concatenate([(k * head_mask[h]).astype(BF16) for h in range(N_HEADS)], axis=0)
        vs = jnp.concatenate([(v * head_mask[h]).astype(BF16) for h in range(N_HEADS)], axis=0)
        s = lax.dot_general(q.astype(BF16), ks, nt_dims, preferred_element_type=F32)
        s = jnp.where(mask4, s, 0.0) if score_weight is None else s * score_weight
        o = jnp.dot(s.astype(BF16), vs, preferred_element_type=F32)
        st = st_scr[bb, g_idx]
        o = o + lax.dot_general(q_dec.astype(BF16), st.astype(BF16), nt_dims, preferred_element_type=F32)
        upd = lax.dot_general(vb, k_upd.astype(BF16), tn_dims, preferred_element_type=F32)
        st_scr[bb, g_idx] = st * chunk_decay + jnp.where(block_diag, upd, 0.0)
        return o

    def rope(x):
        partner = jnp.where((lane % HEAD_DIM) < HEAD_DIM // 2,
                            pltpu.roll(x, HG_W - HEAD_DIM // 2, 1), pltpu.roll(x, HEAD_DIM // 2, 1))
        return x * cos_ref[...] + partner * sin_ref[...]

    def one_row(bb):
        q = pa_ref[bb, :, 0:HG_W]
        k = pa_ref[bb, :, HG_W:2 * HG_W] * (HEAD_DIM ** -0.5)
        v = pa_ref[bb, :, 2 * HG_W:3 * HG_W]
        if use_rope:
            q = rope(q)
            k = rope(k)
        o_ret = group(q, k, v, q * ret_q_decay, k * ret_k_decay, ret_chunk_decay, ret_weight, bb, 0)

        q = pa_ref[bb, :, 4 * HG_W:5 * HG_W] * (HEAD_DIM ** -0.5)
        k = pa_ref[bb, :, 5 * HG_W:6 * HG_W]
        v = pa_ref[bb, :, 6 * HG_W:7 * HG_W]
        la = jnp.where(d == 0, pa_ref[bb, :, 8 * HG_W:9 * HG_W], pa_ref[bb, :, 9 * HG_W:10 * HG_W])
        b = sum(jnp.dot(tri, part, preferred_element_type=F32) for part in split3(la))
        b_last = jnp.sum(la, axis=0, keepdims=True)
        q_dec = q * jnp.exp(b)
        o_gla = group(q_dec, k * jnp.exp(-b), v, q_dec, k * jnp.exp(b_last - b), jnp.exp(b_last), None, bb, 1)
        return o_ret, o_gla

    outs = [one_row(bb) for bb in range(pb)]

    @pl.when(d == 0)
    def _():
        for bb, (o_ret, o_gla) in enumerate(outs):
            of_scr[c, bb, :, 0:HG_W] = o_ret
            of_scr[c, bb, :, HG_W:2 * HG_W] = o_gla

    @pl.when(d == 1)
    def _():
        mean_mat = jnp.where(block_diag, 1.0 / HEAD_DIM, 0.0).astype(BF16)

        def head_rms(o):
            parts = jnp.concatenate(split3(o * o), axis=0)
            ms3 = jnp.dot(parts, mean_mat, preferred_element_type=F32)
            ms = ms3[0:CHUNK] + ms3[CHUNK:2 * CHUNK] + ms3[2 * CHUNK:3 * CHUNK]
            return o * lax.rsqrt(ms + EPS)

        for bb, (o_ret, o_gla) in enumerate(outs):
            o1 = o_ret + of_scr[cc, bb, :, 0:HG_W]
            g1 = pa_ref[bb, :, 3 * HG_W:4 * HG_W]
            y_ref[bb, :, 0:HG_W] = head_rms(o1) * (g1 * _sigmoid(g1))
            o2 = o_gla + of_scr[cc, bb, :, HG_W:2 * HG_W]
            g2 = pa_ref[bb, :, 7 * HG_W:8 * HG_W]
            y_ref[bb, :, HG_W:2 * HG_W] = head_rms(o2) * gn_ref[...] * (g2 * _sigmoid(g2))

    if want_final:
        @pl.when(c == n - 1)
        def _():
            for bb in range(pb):
                for g_idx in range(2):
                    st_t = st_scr[bb, g_idx].T
                    for h in range(N_HEADS):
                        hs = slice(h * HEAD_DIM, (h + 1) * HEAD_DIM)
                        fin_refs[g_idx][bb, h] = st_t[hs, hs]


def _scan_call(pa, first_row, bsz, t, lg, lgam, gn, rope, st0, want_final):
    n = t // CHUNK
    use_rope = rope is not None
    has_init = st0 is not None
    pb = min(bsz, SCAN_PB_MAX)
    pair_rows = pb * t
    assert bsz % pb == 0 and first_row % pair_rows == 0 and pa.shape[0] % pair_rows == 0
    pair_off = first_row // pair_rows
    pa4 = pa.reshape(pa.shape[0] // pair_rows, pb, t, PA_W)

    def chunk_idx(d, c):
        return jnp.where(d == 0, c, n - 1 - c)

    in_specs = [
        pl.BlockSpec(memory_space=pltpu.SMEM),
        pl.BlockSpec((None, pb, CHUNK, PA_W), lambda b, d, c: (pair_off + b, 0, chunk_idx(d, c), 0)),
    ]
    args = [lg, pa4]
    if use_rope:
        in_specs += [pl.BlockSpec((CHUNK, HG_W), lambda b, d, c: (chunk_idx(d, c), 0))] * 2
        args += list(rope)
    in_specs += [pl.BlockSpec((2, HG_W), lambda b, d, c: (0, 0)),
                 pl.BlockSpec((1, HG_W), lambda b, d, c: (0, 0))]
    args += [lgam, gn]
    state_spec = pl.BlockSpec((pb, None, 2, HG_W, HG_W), lambda b, d, c: (b, d, 0, 0, 0))
    if has_init:
        in_specs.append(state_spec)
        args.append(st0)
    out_specs = [pl.BlockSpec((None, pb, CHUNK, 2 * HG_W),
                              lambda b, d, c: (b, 0, jnp.where(d == 0, n - 1, n - 1 - c), 0))]
    out_shape = [jax.ShapeDtypeStruct((bsz // pb, pb, t, 2 * HG_W), F32)]
    if want_final:
        fin_spec = pl.BlockSpec((pb, None, N_HEADS, HEAD_DIM, HEAD_DIM), lambda b, d, c: (b, d, 0, 0, 0))
        out_specs += [fin_spec, fin_spec]
        out_shape += [jax.ShapeDtypeStruct((bsz, 2, N_HEADS, HEAD_DIM, HEAD_DIM), F32)] * 2
    res = pl.pallas_call(
        functools.partial(_scan_kernel, n=n, pb=pb, use_rope=use_rope, has_init=has_init, want_final=want_final),
        grid=(bsz // pb, 2, n),
        in_specs=in_specs,
        out_specs=out_specs,
        out_shape=out_shape,
        scratch_shapes=[pltpu.VMEM((pb, 2, HG_W, HG_W), F32), pltpu.VMEM((n, pb, CHUNK, 2 * HG_W), F32)],
        compiler_params=pltpu.CompilerParams(vmem_limit_bytes=VMEM_LIMIT),
        name="ret_gla_scan",
    )(*args)
    y = res[0].reshape(bsz * t, 2 * HG_W)
    return (y, res[1], res[2]) if want_final else (y, None, None)


S5_GB = LANES // S5_CH
S5_NEXP = 2 * S5_L + 1


def _s5_table_kernel(pwr_ref, pwi_ref, fz_ref, btr_ref, bti_ref, cr_ref, ci_ref, m_ref, sin_ref, sout_ref):
    def per_token(t, rows):
        return jnp.concatenate([jnp.broadcast_to(t[i:i + 1, :], (S5_CH, S5_STATE)) for i in rows], axis=0)

    def per_channel(t):
        return jnp.concatenate([t] * S5_L, axis=0)

    pos = range(S5_L)
    seqs = ([j for j in pos], [S5_L - 1 - j for j in pos], [2 * S5_L - 1 - j for j in pos],
            [S5_L + j for j in pos], [S5_L + j + 1 for j in pos], [2 * S5_L - j for j in pos])

    def powers(d, k):
        return per_token(pwr_ref[d], seqs[k]), per_token(pwi_ref[d], seqs[k])

    cr, ci = per_channel(cr_ref[...]), per_channel(ci_ref[...])
    tok_in = lax.broadcasted_iota(jnp.int32, (S5_ROW, S5_ROW), 0) // S5_CH
    tok_out = lax.broadcasted_iota(jnp.int32, (S5_ROW, S5_ROW), 1) // S5_CH
    nt_dims = (((1,), (1,)), ((), ()))
    m = jnp.zeros((S5_ROW, S5_ROW), F32)
    for d in range(2):
        fr, fi = fz_ref[d, 0:1, :], fz_ref[d, 1:2, :]
        bbr = per_channel(fr * btr_ref[...] - fi * bti_ref[...])
        bbi = per_channel(fr * bti_ref[...] + fi * btr_ref[...])
        k_out, k_in, k_sin, k_sout = (0, 1, 2, 4) if d == 0 else (1, 0, 3, 5)
        er, ei = powers(d, k_out)
        cq = jnp.concatenate([cr * er - ci * ei, -(cr * ei + ci * er)], axis=-1)
        er, ei = powers(d, k_in)
        bk = jnp.concatenate([bbr * er - bbi * ei, bbr * ei + bbi * er], axis=-1)
        full = lax.dot_general(bk, cq, nt_dims, precision=HIGHEST, preferred_element_type=F32)
        m = m + jnp.where((tok_out >= tok_in) if d == 0 else (tok_in >= tok_out), full, 0.0)
        er, ei = powers(d, k_sin)
        sin_ref[d] = jnp.concatenate([bbr * er - bbi * ei, bbr * ei + bbi * er], axis=-1).astype(BF16)
        er, ei = powers(d, k_sout)
        sout_ref[d] = jnp.concatenate([cr * er - ci * ei, -(cr * ei + ci * er)], axis=-1).astype(BF16)
    m_ref[...] = m.astype(BF16)


def _s5_power_table(a_re, a_im, log_dt):
    expo = np.concatenate([np.arange(S5_L, dtype=np.float32) - (S5_L - 1) / 2.0, np.arange(S5_L + 1, dtype=np.float32)])
    expo = jnp.asarray(expo)[:, None]
    dt = jnp.exp(log_dt)[..., None, None]
    ar, ai = a_re[..., None, :] * dt, a_im[..., None, :] * dt
    mag = jnp.exp(expo * ar)
    return mag * jnp.cos(expo * ai), mag * jnp.sin(expo * ai)


def _s5_tables(a_re, a_im, log_dt, b_re, b_im, c_re, c_im):
    depth = a_re.shape[0]
    pwr, pwi = _s5_power_table(a_re, a_im, log_dt)
    dt = jnp.exp(log_dt)[..., None]
    mag = jnp.exp(a_re * dt)
    lb_re, lb_im = mag * jnp.cos(a_im * dt), mag * jnp.sin(a_im * dt)
    den = a_re * a_re + a_im * a_im
    n_re = lb_re - 1.0
    fz = jnp.stack([(n_re * a_re + lb_im * a_im) / den, (lb_im * a_re - n_re * a_im) / den], axis=3)
    mag_l = jnp.exp(S5_L * (a_re * dt))
    pl_re, pl_im = mag_l * jnp.cos(S5_L * (a_im * dt)), mag_l * jnp.sin(S5_L * (a_im * dt))
    lam = jnp.stack([jnp.concatenate([pl_re[:, 0], pl_re[:, 0]], -1), jnp.concatenate([-pl_im[:, 0], pl_im[:, 0]], -1),
                     jnp.concatenate([pl_re[:, 1], pl_re[:, 1]], -1), jnp.concatenate([-pl_im[:, 1], pl_im[:, 1]], -1)],
                    axis=2)
    per_lg = lambda *shape: pl.BlockSpec((None, None) + shape, lambda l, g: (l, g) + (0,) * len(shape))
    per_dir = lambda *shape: pl.BlockSpec((None, 2, None) + shape, lambda l, g: (l, 0, g) + (0,) * len(shape))
    m, s_in, s_out = pl.pallas_call(
        _s5_table_kernel,
        grid=(depth, S5_GROUPS),
        in_specs=[per_dir(S5_NEXP, S5_STATE), per_dir(S5_NEXP, S5_STATE), per_dir(2, S5_STATE),
                  per_lg(S5_CH, S5_STATE), per_lg(S5_CH, S5_STATE), per_lg(S5_CH, S5_STATE), per_lg(S5_CH, S5_STATE)],
        out_specs=[per_lg(S5_ROW, S5_ROW), per_lg(2, S5_ROW, 2 * S5_STATE), per_lg(2, S5_ROW, 2 * S5_STATE)],
        out_shape=[jax.ShapeDtypeStruct((depth, S5_GROUPS, S5_ROW, S5_ROW), BF16),
                   jax.ShapeDtypeStruct((depth, S5_GROUPS, 2, S5_ROW, 2 * S5_STATE), BF16),
                   jax.ShapeDtypeStruct((depth, S5_GROUPS, 2, S5_ROW, 2 * S5_STATE), BF16)],
        compiler_params=pltpu.CompilerParams(vmem_limit_bytes=VMEM_LIMIT),
        name="s5_tables",
    )(pwr, pwi, fz, jnp.swapaxes(b_re, -1, -2), jnp.swapaxes(b_im, -1, -2), c_re, c_im)
    return m, s_in, s_out, lam


def _s5_kernel(*refs, n, bsz, has_init):
    refs = list(refs)
    su_ref = refs.pop(0)
    m_ref = refs.pop(0)
    sin_ref = refs.pop(0)
    sout_ref = refs.pop(0)
    lam_ref = refs.pop(0)
    dv_ref = refs.pop(0)
    if has_init:
        h0_ref = refs.pop(0)
    y_ref = refs.pop(0)
    fin_ref = refs.pop(0)
    sfs, sbs, hpf, hnb = refs

    r = n * bsz
    gran = lax.broadcasted_iota(jnp.int32, (1, LANES), 1) // S5_CH
    halves_per_row = S5_ROW // LANES
    per_half = S5_L // halves_per_row

    def tok_rows(j):
        return pl.ds(j, r, stride=S5_L)

    def granule_transpose(arrs):
        arrs = list(arrs)
        bit = S5_GB // 2
        while bit:
            upper = (gran & bit) != 0
            for i in range(S5_GB):
                if i & bit:
                    continue
                lo, hi = arrs[i], arrs[i | bit]
                arrs[i] = jnp.where(upper, pltpu.roll(hi, bit * S5_CH, 1), lo)
                arrs[i | bit] = jnp.where(upper, hi, pltpu.roll(lo, LANES - bit * S5_CH, 1))
            bit //= 2
        return arrs

    halves = [granule_transpose([su_ref[tok_rows(hh * per_half + i), :] for i in range(per_half)])
              for hh in range(halves_per_row)]
    u = jnp.stack([jnp.concatenate([halves[hh][gl] for hh in range(halves_per_row)], axis=-1)
                   for gl in range(S5_GB)]).astype(BF16)

    bm_dims = (((2,), (1,)), ((0,), (0,)))
    bm_nt_dims = (((2,), (2,)), ((0,), (0,)))
    y = lax.dot_general(u, m_ref[...], bm_dims, preferred_element_type=F32)
    sfs[...] = lax.dot_general(u, sin_ref[:, 0], bm_dims, preferred_element_type=F32)
    sbs[...] = lax.dot_general(u, sin_ref[:, 1], bm_dims, preferred_element_type=F32)

    a_f, s_f = lam_ref[:, 0:1, :], lam_ref[:, 1:2, :]
    a_b, s_b = lam_ref[:, 2:3, :], lam_ref[:, 3:4, :]
    if has_init:
        hf = h0_ref[:, 0]
        hb = h0_ref[:, 1]
    else:
        hf = jnp.zeros((S5_GB, bsz, 2 * S5_STATE), F32)
        hb = jnp.zeros((S5_GB, bsz, 2 * S5_STATE), F32)
    swap = lambda a: pltpu.roll(a, S5_STATE, 2)
    hf_sw, hb_sw = swap(hf), swap(hb)
    for c in range(n):
        rows = pl.ds(c, bsz, stride=n)
        hpf[:, rows, :] = hf
        s_in = sfs[:, rows, :]
        hf, hf_sw = hf * a_f + hf_sw * s_f + s_in, hf_sw * a_f - hf * s_f + swap(s_in)
        rows = pl.ds(n - 1 - c, bsz, stride=n)
        hnb[:, rows, :] = hb
        s_in = sbs[:, rows, :]
        hb, hb_sw = hb * a_b + hb_sw * s_b + s_in, hb_sw * a_b - hb * s_b + swap(s_in)
    fin_ref[:, 0] = hf
    fin_ref[:, 1] = hb
    y = y + lax.dot_general(hpf[...].astype(BF16), sout_ref[:, 0], bm_nt_dims, preferred_element_type=F32)
    y = y + lax.dot_general(hnb[...].astype(BF16), sout_ref[:, 1], bm_nt_dims, preferred_element_type=F32)

    for hh in range(halves_per_row):
        by_token = granule_transpose([y[gl][:, hh * LANES:(hh + 1) * LANES] for gl in range(S5_GB)])
        for i in range(per_half):
            rows = tok_rows(hh * per_half + i)
            y_ref[rows, :] = by_token[i] + su_ref[rows, :] * dv_ref[...]


def _s5_call(su, row_block, tabs, layer, dvec, bsz, t, h0):
    m, s_in, s_out, lam = tabs
    n = t // S5_L
    r = n * bsz
    rows = bsz * t
    has_init = h0 is not None
    gspec = lambda *shape: pl.BlockSpec((S5_GB,) + shape, lambda g: (g,) + (0,) * len(shape))
    lspec = lambda *shape: pl.BlockSpec((None, S5_GB) + shape, lambda g: (layer, g) + (0,) * len(shape))
    in_specs = [pl.BlockSpec((rows, LANES), lambda g: (row_block, g)),
                lspec(S5_ROW, S5_ROW), lspec(2, S5_ROW, 2 * S5_STATE), lspec(2, S5_ROW, 2 * S5_STATE),
                lspec(4, 2 * S5_STATE), pl.BlockSpec((1, LANES), lambda g: (0, g))]
    args = [su, m, s_in, s_out, lam, dvec]
    if has_init:
        in_specs.append(gspec(2, bsz, 2 * S5_STATE))
        args.append(h0)
    return pl.pallas_call(
        functools.partial(_s5_kernel, n=n, bsz=bsz, has_init=has_init),
        grid=(S5_GROUPS // S5_GB,),
        in_specs=in_specs,
        out_specs=[pl.BlockSpec((rows, LANES), lambda g: (0, g)), gspec(2, bsz, 2 * S5_STATE)],
        out_shape=[jax.ShapeDtypeStruct((rows, S5_W), F32),
                   jax.ShapeDtypeStruct((S5_GROUPS, 2, bsz, 2 * S5_STATE), F32)],
        scratch_shapes=[pltpu.VMEM((S5_GB, r, 2 * S5_STATE), F32)] * 4,
        compiler_params=pltpu.CompilerParams(vmem_limit_bytes=VMEM_LIMIT),
        name="s5_scan",
    )(*args)


def _outproj_kernel(yrg_c_ref, yrg_l_ref, ys_c_ref, ys_l_ref, xc_ref, xl_ref, mod_ref, n2_ref, wo_ref, gw_ref, gb_ref,
                    rw_ref, rb_ref, x1_ref, xs_ref, pos_ref, wt_ref, meta_ref, *, n_ctx_tiles):
    is_ctx = pl.program_id(0) < n_ctx_tiles
    x_in = jnp.where(is_ctx, xc_ref[...], xl_ref[...])
    ys = jnp.where(is_ctx, ys_c_ref[...], ys_l_ref[...])
    yrg = jnp.where(is_ctx, yrg_c_ref[...], yrg_l_ref[...])
    s = 0.5 * ys * (1.0 + jnp.tanh(np.sqrt(2.0 / np.pi).astype(np.float32) * (ys + 0.044715 * (ys * ys * ys))))
    s = s * _sigmoid(jnp.dot(s.astype(BF16), gw_ref[...], preferred_element_type=F32) + gb_ref[...])
    m = (jnp.dot(yrg.astype(BF16), wo_ref[0:2 * HG_W, :], preferred_element_type=F32)
         + jnp.dot(s.astype(BF16), wo_ref[2 * HG_W:, :], preferred_element_type=F32))
    x1 = x_in + mod_ref[2:3, :] * m
    x1_ref[...] = x1
    h = x1 * lax.rsqrt(jnp.mean(x1 * x1, axis=-1, keepdims=True) + EPS) * n2_ref[...]
    h = h * (1.0 + mod_ref[4:5, :]) + mod_ref[3:4, :]

    h_hi = h.astype(BF16)
    h_lo = (h - h_hi.astype(F32)).astype(BF16)
    pp = jnp.dot(jnp.concatenate([h_hi, h_lo], axis=0), rw_ref[...], preferred_element_type=F32)
    logits = ((pp[:TM, :LANES] + pp[TM:, :LANES]) + (pp[:TM, LANES:] + pp[TM:, LANES:])) + rb_ref[...]
    lane = lax.broadcasted_iota(jnp.int32, (TM, LANES), 1).astype(F32)
    cur = logits
    vals, idxs = [], []
    for _ in range(TOP_K):
        mx = jnp.max(cur, axis=-1, keepdims=True)
        am = jnp.min(jnp.where(cur == mx, lane, float(LANES)), axis=-1, keepdims=True)
        vals.append(mx)
        idxs.append(am)
        cur = jnp.where(lane == am, -jnp.inf, cur)
    es = [jnp.exp(v - vals[0]) for v in vals]
    den = es[0] + es[1] + es[2] + es[3]

    hit = [(lane == am) for am in idxs]
    assign = (hit[0] | hit[1] | hit[2] | hit[3]).astype(BF16)
    ti = lax.broadcasted_iota(jnp.int32, (TM, TM), 0)
    tj = lax.broadcasted_iota(jnp.int32, (TM, TM), 1)
    rank = jnp.dot((tj < ti).astype(BF16), assign, preferred_element_type=F32)
    cnt = jnp.sum(assign.astype(F32), axis=0, keepdims=True)
    cnt_al = jnp.floor((cnt + (SEG_ALIGN - 1)) * (1.0 / SEG_ALIGN)) * SEG_ALIGN
    ei = lax.broadcasted_iota(jnp.int32, (LANES, LANES), 0)
    ej = lax.broadcasted_iota(jnp.int32, (LANES, LANES), 1)
    seg = jnp.dot(jnp.broadcast_to(cnt_al, (SUBLANES, LANES)).astype(BF16), (ei < ej).astype(BF16),
                  preferred_element_type=F32)[0:1, :]
    base = seg + rank
    rlane = lax.broadcasted_iota(jnp.int32, (TM, ROWS_T), 1).astype(F32)
    klane = lax.broadcasted_iota(jnp.int32, (TM, LANES), 1)
    onehot = jnp.zeros((TM, ROWS_T), F32)
    pos_out = jnp.zeros((TM, LANES), F32)
    wt_out = jnp.zeros((TM, LANES), F32)
    for kk in range(TOP_K):
        pk = jnp.sum(jnp.where(hit[kk], base, 0.0), axis=-1, keepdims=True)
        onehot = onehot + (rlane == pk).astype(F32)
        pos_out = jnp.where(klane == kk, pk, pos_out)
        wt_out = jnp.where(klane == kk, es[kk] / den, wt_out)
    xs_ref[0] = _rows_to_chunks(lax.dot_general(onehot.astype(BF16), h.astype(BF16), (((0,), (0,)), ((), ())),
                                                preferred_element_type=F32))
    pos_ref[...] = pos_out.astype(jnp.int32)
    wt_ref[...] = wt_out
    mrow = lax.broadcasted_iota(jnp.int32, (SUBLANES, LANES), 0)
    meta = jnp.where(mrow == 0, jnp.broadcast_to(cnt_al, (SUBLANES, LANES)),
                     jnp.where(mrow == 1, jnp.broadcast_to(seg, (SUBLANES, LANES)), 0.0))
    meta_ref[0] = meta.astype(jnp.int32)


def _outproj_call(yrg_c, yrg_l, ys_c, ys_l, x_c, x_l, mod_l, n2, wo, gw, gb, rw, rb, n_ctx_tiles, tiles_per_lat):
    n = x_c.shape[0] + x_l.shape[0]
    nt = n // TM
    gid = functools.partial(_group_id, n_ctx_tiles=n_ctx_tiles, tiles_per_lat=tiles_per_lat)
    const = lambda *shape: pl.BlockSpec(shape, lambda i: (0,) * len(shape))
    ctx_map, lat_map = _path_maps(n_ctx_tiles)
    return pl.pallas_call(
        functools.partial(_outproj_kernel, n_ctx_tiles=n_ctx_tiles),
        grid=(nt,),
        in_specs=[
            pl.BlockSpec((TM, 2 * HG_W), ctx_map),
            pl.BlockSpec((TM, 2 * HG_W), lat_map),
            pl.BlockSpec((TM, S5_W), ctx_map),
            pl.BlockSpec((TM, S5_W), lat_map),
            pl.BlockSpec((TM, D_MODEL), ctx_map),
            pl.BlockSpec((TM, D_MODEL), lat_map),
            pl.BlockSpec((None, 6, D_MODEL), lambda i: (gid(i), 0, 0)),
            const(1, D_MODEL), const(D_MODEL, D_MODEL), const(S5_W, S5_W), const(1, S5_W),
            const(D_MODEL, 2 * LANES), const(1, LANES),
        ],
        out_specs=[
            pl.BlockSpec((TM, D_MODEL), lambda i: (i, 0)),
            pl.BlockSpec((1, CHUNKS_T, 2 * SEG_ALIGN, D_MODEL // 2), lambda i: (i, 0, 0, 0)),
            pl.BlockSpec((TM, LANES), lambda i: (i, 0)),
            pl.BlockSpec((TM, LANES), lambda i: (i, 0)),
            pl.BlockSpec((1, SUBLANES, LANES), lambda i: (i, 0, 0)),
        ],
        out_shape=[
            jax.ShapeDtypeStruct((n, D_MODEL), F32),
            jax.ShapeDtypeStruct((nt, CHUNKS_T, 2 * SEG_ALIGN, D_MODEL // 2), BF16),
            jax.ShapeDtypeStruct((n, LANES), jnp.int32),
            jax.ShapeDtypeStruct((n, LANES), F32),
            jax.ShapeDtypeStruct((nt, SUBLANES, LANES), jnp.int32),
        ],
        compiler_params=pltpu.CompilerParams(vmem_limit_bytes=VMEM_LIMIT),
        name="outproj_router_dispatch",
    )(yrg_c, yrg_l, ys_c, ys_l, x_c, x_l, mod_l, n2, wo, gw, gb, rw, rb)


SCHED_COLS = 6


def _schedule_kernel(meta_ref, blk_ref, cst_ref, nch_ref, srcb_ref, *, nt, nblk_pad):
    exact = functools.partial(jnp.dot, precision=HIGHEST, preferred_element_type=F32)
    nch = meta_ref[:, 0, :].astype(F32) * (1.0 / SEG_ALIGN)
    seg = meta_ref[:, 1, :].astype(F32) * (1.0 / SEG_ALIGN)
    ti = lax.broadcasted_iota(jnp.int32, (nt, nt), 0)
    tj = lax.broadcasted_iota(jnp.int32, (nt, nt), 1)
    cs_end = exact((tj <= ti).astype(F32), nch)
    cs_start = cs_end - nch
    tot = cs_end[nt - 1:nt, :]
    nb = jnp.floor((tot + (BLK_CHUNKS - 1)) * (1.0 / BLK_CHUNKS))
    ei = lax.broadcasted_iota(jnp.int32, (LANES, LANES), 0)
    ej = lax.broadcasted_iota(jnp.int32, (LANES, LANES), 1)
    blk_end = exact(jnp.broadcast_to(nb, (SUBLANES, LANES)), (ei <= ej).astype(F32))[0:1, :]
    blk_start = blk_end - nb
    n_active = blk_end[:, LANES - 1:LANES]
    j = lax.broadcasted_iota(jnp.int32, (nblk_pad, 1), 0).astype(F32)
    lane = lax.broadcasted_iota(jnp.int32, (nblk_pad, LANES), 1).astype(F32)
    row_sum = lambda a: jnp.sum(a, axis=-1, keepdims=True)
    be = row_sum((blk_end <= jnp.minimum(j, n_active - 1.0)).astype(F32))
    be = jnp.minimum(be, float(N_EXPERTS - 1))
    own = (lane == be).astype(F32)
    bs = (j - row_sum(own * blk_start)) * BLK_CHUNKS
    nvalid = jnp.where(j < n_active, jnp.clip(row_sum(own * tot) - bs, 0.0, float(BLK_CHUNKS)), 0.0)
    nt_dims = (((1,), (1,)), ((), ()))
    own_end = lax.dot_general(own, cs_end, nt_dims, precision=HIGHEST, preferred_element_type=F32)
    own_start = lax.dot_general(own, cs_start, nt_dims, precision=HIGHEST, preferred_element_type=F32)
    kfirst = row_sum((own_end <= bs).astype(F32))
    klast = row_sum((own_start < bs + nvalid).astype(F32))
    later = (ej > ei) & (jnp.broadcast_to(tot, (LANES, LANES)) > 0.0)
    nxt_e = jnp.min(jnp.where(later, ej.astype(F32), float(LANES)), axis=-1, keepdims=True)
    nxt = exact(own, jnp.broadcast_to(nxt_e, (LANES, LANES)))[:, 0:1]
    nxt = jnp.where(nxt >= float(N_EXPERTS), -1.0, nxt)
    cols = (be, nvalid, bs, kfirst, klast, nxt)
    out = jnp.zeros((nblk_pad, LANES), F32)
    for k_col, val in enumerate(cols):
        out = jnp.where(lane == float(k_col), val, out)
    blk_ref[...] = out.astype(jnp.int32)
    tile = lax.broadcasted_iota(jnp.int32, (nt, LANES), 0).astype(F32)
    cst_ref[...] = cs_start.astype(jnp.int32)
    nch_ref[...] = nch.astype(jnp.int32)
    srcb_ref[...] = (tile * CHUNKS_T + seg).astype(jnp.int32)


def _expert_schedule(meta, nblk):
    nt = meta.shape[0]
    nblk_pad = -(-nblk // SUBLANES) * SUBLANES
    whole = lambda *shape: pl.BlockSpec(shape, lambda i: (0,) * len(shape))
    blk, cst, nch, srcb = pl.pallas_call(
        functools.partial(_schedule_kernel, nt=nt, nblk_pad=nblk_pad),
        grid=(1,),
        in_specs=[whole(nt, SUBLANES, LANES)],
        out_specs=[whole(nblk_pad, LANES), whole(nt, LANES), whole(nt, LANES), whole(nt, LANES)],
        out_shape=[jax.ShapeDtypeStruct((nblk_pad, LANES), jnp.int32)] + [jax.ShapeDtypeStruct((nt, LANES), jnp.int32)] * 3,
        name="expert_schedule",
    )(meta)
    return tuple(blk[:nblk, k] for k in range(SCHED_COLS)) + (cst.reshape(-1), nch.reshape(-1), srcb.reshape(-1))


W_CAST_ROWS = 128


def _ffn_kernel(be_ref, nv_ref, bs_ref, kf_ref, kl_ref, nxt_ref, cst_ref, nch_ref, srcb_ref,
                xs_hbm, w1_hbm, b1_ref, w2_hbm, b2_ref, ys_hbm,
                xbuf, ybuf, w1s, w2s, w1b, w2b, gsem, ssem, wsem, *, nblk, layer):
    j = pl.program_id(0)
    slot = j % 2

    def for_pieces(blk, fn):
        e = be_ref[blk]
        lo_b = bs_ref[blk]
        hi_b = lo_b + nv_ref[blk]

        def body(i, carry):
            k = i * LANES + e
            ps = cst_ref[k]
            lo = jnp.maximum(ps, lo_b)
            n = jnp.minimum(ps + nch_ref[k], hi_b) - lo

            @pl.when(n > 0)
            def _():
                fn(srcb_ref[k] + (lo - ps), lo - lo_b, n)
            return carry

        lax.fori_loop(kf_ref[blk], kl_ref[blk], body, 0)

    def start_gather(blk, s):
        for_pieces(blk, lambda src, dst, n: pltpu.make_async_copy(
            xs_hbm.at[pl.ds(src, n)], xbuf.at[s, pl.ds(dst, n)], gsem.at[s]).start())

    def start_scatter(blk, s):
        for_pieces(blk, lambda src, dst, n: pltpu.make_async_copy(
            ybuf.at[s, pl.ds(dst, n)], ys_hbm.at[pl.ds(src, n)], ssem.at[s]).start())

    def wait_gather(blk, s):
        @pl.when(nv_ref[blk] > 0)
        def _():
            pltpu.make_async_copy(xs_hbm.at[pl.ds(0, nv_ref[blk])], xbuf.at[s, pl.ds(0, nv_ref[blk])],
                                  gsem.at[s]).wait()

    def wait_scatter(blk, s):
        @pl.when(nv_ref[blk] > 0)
        def _():
            pltpu.make_async_copy(ybuf.at[s, pl.ds(0, nv_ref[blk])], ys_hbm.at[pl.ds(0, nv_ref[blk])],
                                  ssem.at[s]).wait()

    def weight_copies(e):
        return (pltpu.make_async_copy(w1_hbm.at[layer, e], w1s, wsem.at[0]),
                pltpu.make_async_copy(w2_hbm.at[layer, e], w2s, wsem.at[1]))

    @pl.when(j == 0)
    def _():
        for cp in weight_copies(be_ref[0]):
            cp.start()
        xbuf[...] = jnp.zeros_like(xbuf)
        start_gather(0, 0)

    @pl.when(j + 1 < nblk)
    def _():
        start_gather(j + 1, 1 - slot)

    @pl.when((j == 0) | (be_ref[j] != be_ref[jnp.maximum(j - 1, 0)]))
    def _():
        for cp in weight_copies(be_ref[j]):
            cp.wait()

        def cast_rows(r, carry):
            rows = pl.ds(pl.multiple_of(r * W_CAST_ROWS, W_CAST_ROWS), W_CAST_ROWS)
            w1b[rows, :] = w1s[rows, :].astype(BF16)
            w2b[rows, :] = w2s[rows, :].astype(BF16)
            return carry
        lax.fori_loop(0, D_MODEL // W_CAST_ROWS, cast_rows, 0)

        @pl.when(nxt_ref[j] >= 0)
        def _():
            for cp in weight_copies(nxt_ref[j]):
                cp.start()

    wait_gather(j, slot)

    @pl.when(j >= 2)
    def _():
        wait_scatter(j - 2, slot)

    def compute_rows(first_chunk, nchunks):
        cs = pl.ds(pl.multiple_of(first_chunk, TAIL_CHUNKS), nchunks)
        x = _chunks_to_rows(xbuf[slot, cs])
        gu = jnp.dot(x, w1b[...], preferred_element_type=F32) + b1_ref[...]
        gate = jnp.minimum(gu[:, :D_FF], SWIGLU_LIMIT)
        up = jnp.clip(gu[:, D_FF:], -SWIGLU_LIMIT, SWIGLU_LIMIT)
        act = (up + 1.0) * gate * _sigmoid(SWIGLU_ALPHA * gate)
        y = jnp.dot(act.astype(BF16), w2b[...], preferred_element_type=F32) + b2_ref[...]
        ybuf[slot, cs] = _rows_to_chunks(y)

    nv = nv_ref[j]
    n_full = (nv + (SUB_CHUNKS - TAIL_CHUNKS) - 1) // SUB_CHUNKS
    lax.fori_loop(0, n_full, lambda sb, carry: (compute_rows(sb * SUB_CHUNKS, SUB_CHUNKS), carry)[1], 0)

    @pl.when(nv > n_full * SUB_CHUNKS)
    def _():
        compute_rows(n_full * SUB_CHUNKS, TAIL_CHUNKS)

    start_scatter(j, slot)

    @pl.when(j == nblk - 1)
    def _():
        wait_scatter(j, slot)
        if nblk >= 2:
            wait_scatter(j - 1, 1 - slot)


def _ffn_call(sched, xs_chunks, w1, b1, w2, b2, layer):
    assert D_FF == D_MODEL
    nblk = sched[0].shape[0]
    bmap = lambda j, be, *_: (layer, be[j], 0, 0)
    grid_spec = pltpu.PrefetchScalarGridSpec(
        num_scalar_prefetch=len(sched),
        grid=(nblk,),
        in_specs=[
            pl.BlockSpec(memory_space=pl.ANY),
            pl.BlockSpec(memory_space=pl.ANY),
            pl.BlockSpec((None, None, 1, 2 * D_FF), bmap),
            pl.BlockSpec(memory_space=pl.ANY),
            pl.BlockSpec((None, None, 1, D_MODEL), bmap),
        ],
        out_specs=pl.BlockSpec(memory_space=pl.ANY),
        scratch_shapes=[
            pltpu.VMEM((2, BLK_CHUNKS, 2 * SEG_ALIGN, D_MODEL // 2), BF16),
            pltpu.VMEM((2, BLK_CHUNKS, 2 * SEG_ALIGN, D_MODEL // 2), BF16),
            pltpu.VMEM((D_MODEL, 2 * D_FF), F32),
            pltpu.VMEM((D_FF, D_MODEL), F32),
            pltpu.VMEM((D_MODEL, 2 * D_FF), BF16),
            pltpu.VMEM((D_FF, D_MODEL), BF16),
            pltpu.SemaphoreType.DMA((2,)),
            pltpu.SemaphoreType.DMA((2,)),
            pltpu.SemaphoreType.DMA((2,)),
        ],
    )
    return pl.pallas_call(
        functools.partial(_ffn_kernel, nblk=nblk, layer=layer),
        grid_spec=grid_spec,
        out_shape=jax.ShapeDtypeStruct(xs_chunks.shape, BF16),
        input_output_aliases={len(sched): 0},
        compiler_params=pltpu.CompilerParams(vmem_limit_bytes=VMEM_LIMIT, dimension_semantics=("arbitrary",)),
        name="expert_ffn",
    )(*sched, xs_chunks, w1, b1, w2, b2)


def _combine_kernel(ys_ref, pos_ref, wt_ref, x1_ref, mod_ref, fg_ref, *out_refs, final, n_ctx_tiles):
    ys = _chunks_to_rows(ys_ref[0])
    rlane = lax.broadcasted_iota(jnp.int32, (TM, ROWS_T), 1)
    w = jnp.zeros((TM, ROWS_T), F32)
    for kk in range(TOP_K):
        w = w + jnp.where(rlane == pos_ref[:, kk:kk + 1], wt_ref[:, kk:kk + 1], 0.0)
    moe = jnp.dot(w.astype(BF16), ys, preferred_element_type=F32)
    y = x1_ref[...] + mod_ref[5:6, :] * moe
    if final:
        y = y * lax.rsqrt(jnp.mean(y * y, axis=-1, keepdims=True) + EPS) * fg_ref[...]
    is_ctx = pl.program_id(0) < n_ctx_tiles

    @pl.when(is_ctx)
    def _():
        out_refs[0][...] = y

    @pl.when(jnp.logical_not(is_ctx))
    def _():
        out_refs[1][...] = y


def _combine_call(ys, pos, wts, x1, mod_l, fg, n_ctx_tiles, tiles_per_lat, final):
    n = x1.shape[0]
    nt = n // TM
    gid = functools.partial(_group_id, n_ctx_tiles=n_ctx_tiles, tiles_per_lat=tiles_per_lat)
    tok = pl.BlockSpec((TM, D_MODEL), lambda i: (i, 0))
    ctx_map, lat_map = _path_maps(n_ctx_tiles)
    out_specs = [pl.BlockSpec((TM, D_MODEL), ctx_map), pl.BlockSpec((TM, D_MODEL), lat_map)]
    out_shape = [jax.ShapeDtypeStruct((n_ctx_tiles * TM, D_MODEL), F32),
                 jax.ShapeDtypeStruct((n - n_ctx_tiles * TM, D_MODEL), F32)]
    res = pl.pallas_call(
        functools.partial(_combine_kernel, final=final, n_ctx_tiles=n_ctx_tiles),
        grid=(nt,),
        in_specs=[
            pl.BlockSpec((1, CHUNKS_T, 2 * SEG_ALIGN, D_MODEL // 2), lambda i: (i, 0, 0, 0)),
            pl.BlockSpec((TM, LANES), lambda i: (i, 0)),
            pl.BlockSpec((TM, LANES), lambda i: (i, 0)),
            tok,
            pl.BlockSpec((None, 6, D_MODEL), lambda i: (gid(i), 0, 0)),
            pl.BlockSpec((1, D_MODEL), lambda i: (0, 0)),
        ],
        out_specs=out_specs,
        out_shape=out_shape,
        compiler_params=pltpu.CompilerParams(vmem_limit_bytes=VMEM_LIMIT),
        name="moe_combine",
    )(ys, pos, wts, x1, mod_l, fg)
    return res


def _rope_tables(t):
    pos = np.arange(t)
    nf = HEAD_DIM // 4
    inv = jnp.asarray(ROPE_BASE, F32) ** (-jnp.arange(nf, dtype=F32) / nf)
    ang = jnp.concatenate([jnp.asarray(pos // GRID_W, F32)[:, None] * inv,
                           jnp.asarray(pos % GRID_W, F32)[:, None] * inv], axis=-1)
    cos, sin = jnp.cos(ang), jnp.sin(ang)
    return (jnp.tile(jnp.concatenate([cos, cos], -1), (1, N_HEADS)),
            jnp.tile(jnp.concatenate([-sin, sin], -1), (1, N_HEADS)))


def _block_diag_t(s):
    eye = jnp.eye(N_HEADS, dtype=s.dtype)
    out = jnp.einsum('...hkv,hg->...hvgk', s, eye)
    return out.reshape(s.shape[:-3] + (HG_W, HG_W))


def kernel(x_prompt, x_sample, state_ret, state_gla, state_s5_re, state_s5_im, c, c_ctx, norm1_g, norm2_g, final_g, ada_w, ada_b, w_in, w_out, ret_decay, gla_w_lr, gla_b_lr, gla_norm_g, s5_a_re, s5_a_im, s5_log_dt, s5_b_re, s5_b_im, s5_c_re, s5_c_im, s5_d, s5_glu_w, s5_glu_b, router_w, router_b, moe_w1, moe_b1, moe_w2, moe_b2):
    bc, tc, d = x_prompt.shape
    bl, tl, _ = x_sample.shape
    depth = w_in.shape[0]
    nc, nl = bc * tc, bl * tl
    n = nc + nl
    assert d == D_MODEL and nc % TM == 0 and tl % TM == 0 and tc % CHUNK == 0 and tl % CHUNK == 0
    assert 1 + bl <= SUBLANES and nc % nl == 0
    n_ctx_tiles, tiles_per_lat = nc // TM, tl // TM
    nt = n // TM
    nblk = nt * CHUNKS_T // BLK_CHUNKS + N_EXPERTS

    x_c, x_l = x_prompt.reshape(nc, d), x_sample.reshape(nl, d)
    cond8 = jnp.concatenate([c_ctx[None, :], c, jnp.zeros((SUBLANES - 1 - bl, d), F32)], axis=0)
    mod = _ada_call(cond8, ada_w, ada_b).reshape(depth, SUBLANES, 6, d)
    rope = _rope_tables(tl)
    fg = final_g.reshape(1, d)
    s5_tabs = _s5_tables(s5_a_re, s5_a_im, s5_log_dt, s5_b_re, s5_b_im, s5_c_re, s5_c_im)

    ret_states, gla_states, re_states, im_states = [], [], [], []
    for l in range(depth):
        wl = w_in[l]
        w_cat = jnp.concatenate(
            [wl[:, :8 * HG_W], wl[:, 8 * HG_W + GLA_RANK:],
             jnp.pad(wl[:, 8 * HG_W:8 * HG_W + GLA_RANK], ((0, 0), (0, LANES - GLA_RANK)))], axis=1).astype(BF16)
        wlr = jnp.pad(gla_w_lr[l], ((0, 0), (0, LANES - GLA_RANK), (0, 0)))
        blr = gla_b_lr[l].reshape(2, 1, HG_W)
        pa, su = _inproj_call(x_c, x_l, mod[l], norm1_g[l].reshape(1, d), w_cat, wlr, blr, tl)

        log_gamma = jnp.log1p(-jnp.exp(ret_decay[l]))
        lgam = jnp.repeat(log_gamma, HEAD_DIM, axis=1)
        gn = jnp.tile(gla_norm_g[l], N_HEADS).reshape(1, HG_W)
        y_c, fin_ret, fin_gla = _scan_call(pa, 0, bc, tc, log_gamma, lgam, gn, None, None, True)
        st0 = jnp.stack([_block_diag_t(state_ret[:, l]), _block_diag_t(state_gla[:, l])], axis=2)
        y_l, _, _ = _scan_call(pa, nc, bl, tl, log_gamma, lgam, gn, rope, st0, False)
        ret_states.append(fin_ret)
        gla_states.append(fin_gla)

        dvec = s5_d[l].reshape(1, S5_W)
        ys_c, fin_c = _s5_call(su, 0, s5_tabs, l, dvec, bc, tc, None)
        h0 = jnp.concatenate([state_s5_re[:, l], state_s5_im[:, l]], axis=-1).transpose(2, 1, 0, 3)
        ys_l, _ = _s5_call(su, nc // nl, s5_tabs, l, dvec, bl, tl, h0)
        re_states.append(fin_c[..., :S5_STATE].transpose(2, 1, 0, 3))
        im_states.append(fin_c[..., S5_STATE:].transpose(2, 1, 0, 3))

        rw = jnp.pad(router_w[l], ((0, 0), (0, LANES - N_EXPERTS)))
        rw_hi = rw.astype(BF16)
        rw = jnp.concatenate([rw_hi, (rw - rw_hi.astype(F32)).astype(BF16)], axis=1)
        rb = jnp.concatenate([router_b[l], jnp.full((LANES - N_EXPERTS,), NEG_BIG, F32)]).reshape(1, LANES)
        x1, xs, pos, wts, meta = _outproj_call(
            y_c, y_l, ys_c, ys_l, x_c, x_l, mod[l], norm2_g[l].reshape(1, d), w_out[l].astype(BF16), s5_glu_w[l].astype(BF16),
            s5_glu_b[l].reshape(1, S5_W), rw, rb, n_ctx_tiles, tiles_per_lat)

        sched = _expert_schedule(meta, nblk)
        ys = _ffn_call(sched, xs.reshape(nt * CHUNKS_T, 2 * SEG_ALIGN, d // 2),
                       moe_w1, moe_b1.reshape(depth, N_EXPERTS, 1, 2 * D_FF),
                       moe_w2, moe_b2.reshape(depth, N_EXPERTS, 1, d), l)
        res = _combine_call(ys.reshape(nt, CHUNKS_T, 2 * SEG_ALIGN, d // 2), pos, wts, x1, mod[l], fg,
                            n_ctx_tiles, tiles_per_lat, l == depth - 1)
        x_c, x_l = res
    y_prompt = x_c.reshape(bc, tc, d)
    y_sample = x_l.reshape(bl, tl, d)
    return (y_prompt, y_sample, jnp.stack(ret_states, axis=1), jnp.stack(gla_states, axis=1),
            jnp.stack(re_states, axis=1), jnp.stack(im_states, axis=1))
```

```python
import functools

import numpy as np
import jax
import jax.numpy as jnp
from jax import lax
from jax.experimental import pallas as pl
from jax.experimental.pallas import tpu as pltpu

F32 = jnp.float32
BF16 = jnp.bfloat16
HIGHEST = lax.Precision.HIGHEST

D_MODEL = 1024
GRID_W = 64
CHUNK = 128
HEAD_DIM = 64
N_HEADS = 4
HG_W = N_HEADS * HEAD_DIM
S5_W = 512
S5_CH = 16
S5_GROUPS = 32
S5_STATE = 64
S5_L = 16
S5_ROW = S5_L * S5_CH
GLA_RANK = 16
GLA_TAU = 16.0
N_EXPERTS = 32
TOP_K = 4
D_FF = 1024
SWIGLU_LIMIT = 7.0
SWIGLU_ALPHA = 1.702
ROPE_BASE = 10000.0
EPS = 1e-6

LANES = 128
SUBLANES = 8
TM = 256
TM_IN = 512
SEG_ALIGN = SUBLANES
ROWS_T = -(-(TOP_K * TM + N_EXPERTS * (SEG_ALIGN - 1)) // LANES) * LANES
CHUNKS_T = ROWS_T // SEG_ALIGN
BLK_CHUNKS = 128
SUB_CHUNKS = 64
TAIL_CHUNKS = 32
NEG_BIG = -1e30
VMEM_LIMIT = 56 * 1024 * 1024

PA_W = 8 * HG_W + 2 * HG_W
W_CAT = 8 * HG_W + S5_W + LANES


def _sigmoid(x):
    return 1.0 / (1.0 + jnp.exp(-x))


def _rows_to_chunks(x):
    x3 = x.reshape(x.shape[0] // SEG_ALIGN, SEG_ALIGN, D_MODEL)
    return jnp.concatenate([x3[:, :, :D_MODEL // 2], x3[:, :, D_MODEL // 2:]], axis=1).astype(BF16)


def _chunks_to_rows(c):
    c3 = c.astype(F32)
    rows = jnp.concatenate([c3[:, :SEG_ALIGN, :], c3[:, SEG_ALIGN:, :]], axis=2)
    return rows.reshape(c.shape[0] * SEG_ALIGN, D_MODEL).astype(BF16)


def _group_id(i, n_ctx_tiles, tiles_per_lat):
    return jnp.where(i < n_ctx_tiles, 0, 1 + (i - n_ctx_tiles) // tiles_per_lat)


def _ada_kernel(c_ref, w_ref, b_ref, o_ref):
    c = c_ref[...]
    s = c * _sigmoid(c)
    o_ref[0] = jnp.dot(s, w_ref[0], precision=HIGHEST, preferred_element_type=F32) + b_ref[0]


def _ada_call(cond8, ada_w, ada_b):
    depth, d, n6 = ada_w.shape
    tn = 2048
    return pl.pallas_call(
        _ada_kernel,
        grid=(depth, n6 // tn),
        in_specs=[
            pl.BlockSpec((SUBLANES, d), lambda l, j: (0, 0)),
            pl.BlockSpec((1, d, tn), lambda l, j: (l, 0, j)),
            pl.BlockSpec((1, 1, tn), lambda l, j: (l, 0, j)),
        ],
        out_specs=pl.BlockSpec((1, SUBLANES, tn), lambda l, j: (l, 0, j)),
        out_shape=jax.ShapeDtypeStruct((depth, SUBLANES, n6), F32),
        compiler_params=pltpu.CompilerParams(vmem_limit_bytes=VMEM_LIMIT),
        name="ada_mod",
    )(cond8, ada_w, ada_b.reshape(depth, 1, n6))


def _inproj_kernel(xc_ref, xl_ref, mod_ref, n1_ref, w_ref, wlr_ref, blr_ref, pa_ref, su_ref, *, n_ctx_tiles):
    x = jnp.where(pl.program_id(0) < n_ctx_tiles, xc_ref[...], xl_ref[...])
    h = x * lax.rsqrt(jnp.mean(x * x, axis=-1, keepdims=True) + EPS) * n1_ref[...]
    h = h * (1.0 + mod_ref[1:2, :]) + mod_ref[0:1, :]
    r = jnp.dot(h.astype(BF16), w_ref[...], preferred_element_type=F32)
    pa_ref[:, : 8 * HG_W] = r[:, : 8 * HG_W]
    su_ref[...] = r[:, 8 * HG_W: 8 * HG_W + S5_W]
    glr = r[:, 8 * HG_W + S5_W:]
    for d in range(2):
        z = jnp.dot(glr, wlr_ref[d], precision=HIGHEST, preferred_element_type=F32) + blr_ref[d]
        log_sig = jnp.minimum(z, 0.0) - jnp.log(1.0 + jnp.exp(-jnp.abs(z)))
        pa_ref[:, (8 + d) * HG_W: (9 + d) * HG_W] = log_sig * (1.0 / GLA_TAU)


def _path_maps(n_ctx_tiles):
    return (lambda i: (jnp.minimum(i, n_ctx_tiles - 1), 0)), (lambda i: (jnp.maximum(i - n_ctx_tiles, 0), 0))


def _inproj_call(x_c, x_l, mod_l, n1, w_cat, wlr, blr, lat_seq):
    n = x_c.shape[0] + x_l.shape[0]
    assert x_c.shape[0] % TM_IN == 0 and lat_seq % TM_IN == 0
    n_ctx_tiles, tiles_per_lat = x_c.shape[0] // TM_IN, lat_seq // TM_IN
    gid = functools.partial(_group_id, n_ctx_tiles=n_ctx_tiles, tiles_per_lat=tiles_per_lat)
    ctx_map, lat_map = _path_maps(n_ctx_tiles)
    return pl.pallas_call(
        functools.partial(_inproj_kernel, n_ctx_tiles=n_ctx_tiles),
        grid=(n // TM_IN,),
        in_specs=[
            pl.BlockSpec((TM_IN, D_MODEL), ctx_map),
            pl.BlockSpec((TM_IN, D_MODEL), lat_map),
            pl.BlockSpec((None, 6, D_MODEL), lambda i: (gid(i), 0, 0)),
            pl.BlockSpec((1, D_MODEL), lambda i: (0, 0)),
            pl.BlockSpec((D_MODEL, W_CAT), lambda i: (0, 0)),
            pl.BlockSpec((2, LANES, HG_W), lambda i: (0, 0, 0)),
            pl.BlockSpec((2, 1, HG_W), lambda i: (0, 0, 0)),
        ],
        out_specs=[
            pl.BlockSpec((TM_IN, PA_W), lambda i: (i, 0)),
            pl.BlockSpec((TM_IN, S5_W), lambda i: (i, 0)),
        ],
        out_shape=[
            jax.ShapeDtypeStruct((n, PA_W), F32),
            jax.ShapeDtypeStruct((n, S5_W), F32),
        ],
        compiler_params=pltpu.CompilerParams(vmem_limit_bytes=VMEM_LIMIT),
        name="norm1_inproj",
    )(x_c, x_l, mod_l, n1, w_cat, wlr, blr)


SCAN_PB_MAX = 8


def _scan_kernel(*refs, n, pb, use_rope, has_init, want_final):
    refs = list(refs)
    lg_ref = refs.pop(0)
    pa_ref = refs.pop(0)
    if use_rope:
        cos_ref = refs.pop(0)
        sin_ref = refs.pop(0)
    lgam_ref = refs.pop(0)
    gn_ref = refs.pop(0)
    if has_init:
        st0_ref = refs.pop(0)
    y_ref = refs.pop(0)
    if want_final:
        fin_refs = [refs.pop(0), refs.pop(0)]
    st_scr = refs.pop(0)
    of_scr = refs.pop(0)

    d = pl.program_id(1)
    c = pl.program_id(2)
    cc = jnp.where(d == 0, c, n - 1 - c)

    @pl.when(c == 0)
    def _():
        if has_init:
            st_scr[...] = st0_ref[...]
        else:
            st_scr[...] = jnp.zeros_like(st_scr)

    ii = lax.broadcasted_iota(jnp.int32, (CHUNK, CHUNK), 0)
    jj = lax.broadcasted_iota(jnp.int32, (CHUNK, CHUNK), 1)
    rel = (1 - 2 * d) * (ii - jj)
    mask = rel >= 0
    i4 = lax.broadcasted_iota(jnp.int32, (CHUNK, N_HEADS * CHUNK), 0)
    j4 = lax.broadcasted_iota(jnp.int32, (CHUNK, N_HEADS * CHUNK), 1) % CHUNK
    mask4 = (1 - 2 * d) * (i4 - j4) >= 0
    relf = rel.astype(F32)
    row = lax.broadcasted_iota(jnp.int32, (CHUNK, 1), 0)
    p1 = jnp.where(d == 0, row + 1, CHUNK - row).astype(F32)
    lane = lax.broadcasted_iota(jnp.int32, (1, HG_W), 1)
    head_mask = [(lane // HEAD_DIM == h).astype(BF16) for h in range(N_HEADS)]
    bi = lax.broadcasted_iota(jnp.int32, (HG_W, HG_W), 0) // HEAD_DIM
    bj = lax.broadcasted_iota(jnp.int32, (HG_W, HG_W), 1) // HEAD_DIM
    block_diag = bi == bj
    tri = mask.astype(BF16)
    lgl = lgam_ref[pl.ds(d, 1), :]
    ret_q_decay = jnp.exp(p1 * lgl)
    ret_k_decay = jnp.exp((CHUNK - p1) * lgl)
    ret_chunk_decay = jnp.exp(CHUNK * lgl)
    ret_weight = jnp.concatenate([jnp.where(mask, jnp.exp(relf * lg_ref[d, h]), 0.0) for h in range(N_HEADS)], axis=1)

    nt_dims = (((1,), (1,)), ((), ()))
    tn_dims = (((0,), (0,)), ((), ()))

    def split3(x):
        hi = x.astype(BF16)
        r1 = x - hi.astype(F32)
        mid = r1.astype(BF16)
        return hi, mid, (r1 - mid.astype(F32)).astype(BF16)

    def group(q, k, v, q_dec, k_upd, chunk_decay, score_weight, bb, g_idx):
        kb = k.astype(BF16)
        vb = v.astype(BF16)
        ks = jnp.concatenate([kb * head_mask[h] for h in range(N_HEADS)], axis=0)
        vs = jnp.concatenate([vb * head_mask[h] for h in range(N_HEADS)], axis=0)
        s = lax.dot_general(q.astype(BF16), ks, nt_dims, preferred_element_type=F32)
        s = jnp.where(mask4, s, 0.0) if score_weight is None else s * score_weight
        o = jnp.dot(s.astype(BF16), vs, preferred_element_type=F32)
        st = st_scr[bb, g_idx]
        o = o + lax.dot_general(q_dec.astype(BF16), st.astype(BF16), nt_dims, preferred_element_type=F32)
        upd = lax.dot_general(vb, k_upd.astype(BF16), tn_dims, preferred_element_type=F32)
        st_scr[bb, g_idx] = st * chunk_decay + jnp.where(block_diag, upd, 0.0)
        return o

    def rope(x):
        partner = jnp.where((lane % HEAD_DIM) < HEAD_DIM // 2,
                            pltpu.roll(x, HG_W - HEAD_DIM // 2, 1), pltpu.roll(x, HEAD_DIM // 2, 1))
        return x * cos_ref[...] + partner * sin_ref[...]

    def one_row(bb):
        q = pa_ref[bb, :, 0:HG_W]
        k = pa_ref[bb, :, HG_W:2 * HG_W] * (HEAD_DIM ** -0.5)
        v = pa_ref[bb, :, 2 * HG_W:3 * HG_W]
        if use_rope:
            q = rope(q)
            k = rope(k)
        o_ret = group(q, k, v, q * ret_q_decay, k * ret_k_decay, ret_chunk_decay, ret_weight, bb, 0)

        q = pa_ref[bb, :, 4 * HG_W:5 * HG_W] * (HEAD_DIM ** -0.5)
        k = pa_ref[bb, :, 5 * HG_W:6 * HG_W]
        v = pa_ref[bb, :, 6 * HG_W:7 * HG_W]
        la = jnp.where(d == 0, pa_ref[bb, :, 8 * HG_W:9 * HG_W], pa_ref[bb, :, 9 * HG_W:10 * HG_W])
        b = sum(jnp.dot(tri, part, preferred_element_type=F32) for part in split3(la))
        b_last = jnp.sum(la, axis=0, keepdims=True)
        q_dec = q * jnp.exp(b)
        o_gla = group(q_dec, k * jnp.exp(-b), v, q_dec, k * jnp.exp(b_last - b), jnp.exp(b_last), None, bb, 1)
        return o_ret, o_gla

    outs = [one_row(bb) for bb in range(pb)]

    @pl.when(d == 0)
    def _():
        for bb, (o_ret, o_gla) in enumerate(outs):
            of_scr[c, bb, :, 0:HG_W] = o_ret
            of_scr[c, bb, :, HG_W:2 * HG_W] = o_gla

    @pl.when(d == 1)
    def _():
        mean_mat = jnp.where(block_diag, 1.0 / HEAD_DIM, 0.0).astype(BF16)

        def head_rms(o):
            parts = jnp.concatenate(split3(o * o), axis=0)
            ms3 = jnp.dot(parts, mean_mat, preferred_element_type=F32)
            ms = ms3[0:CHUNK] + ms3[CHUNK:2 * CHUNK] + ms3[2 * CHUNK:3 * CHUNK]
            return o * lax.rsqrt(ms + EPS)

        for bb, (o_ret, o_gla) in enumerate(outs):
            o1 = o_ret + of_scr[cc, bb, :, 0:HG_W]
            g1 = pa_ref[bb, :, 3 * HG_W:4 * HG_W]
            y_ref[bb, :, 0:HG_W] = head_rms(o1) * (g1 * _sigmoid(g1))
            o2 = o_gla + of_scr[cc, bb, :, HG_W:2 * HG_W]
            g2 = pa_ref[bb, :, 7 * HG_W:8 * HG_W]
            y_ref[bb, :, HG_W:2 * HG_W] = head_rms(o2) * gn_ref[...] * (g2 * _sigmoid(g2))

    if want_final:
        @pl.when(c == n - 1)
        def _():
            for bb in range(pb):
                for g_idx in range(2):
                    st_t = st_scr[bb, g_idx].T
                    for h in range(N_HEADS):
                        hs = slice(h * HEAD_DIM, (h + 1) * HEAD_DIM)
                        fin_refs[g_idx][bb, h] = st_t[hs, hs]


def _scan_call(pa, first_row, bsz, t, lg, lgam, gn, rope, st0, want_final):
    n = t // CHUNK
    use_rope = rope is not None
    has_init = st0 is not None
    pb = min(bsz, SCAN_PB_MAX)
    pair_rows = pb * t
    assert bsz % pb == 0 and first_row % pair_rows == 0 and pa.shape[0] % pair_rows == 0
    pair_off = first_row // pair_rows
    pa4 = pa.reshape(pa.shape[0] // pair_rows, pb, t, PA_W)

    def chunk_idx(d, c):
        return jnp.where(d == 0, c, n - 1 - c)

    in_specs = [
        pl.BlockSpec(memory_space=pltpu.SMEM),
        pl.BlockSpec((None, pb, CHUNK, PA_W), lambda b, d, c: (pair_off + b, 0, chunk_idx(d, c), 0)),
    ]
    args = [lg, pa4]
    if use_rope:
        in_specs += [pl.BlockSpec((CHUNK, HG_W), lambda b, d, c: (chunk_idx(d, c), 0))] * 2
        args += list(rope)
    in_specs += [pl.BlockSpec((2, HG_W), lambda b, d, c: (0, 0)),
                 pl.BlockSpec((1, HG_W), lambda b, d, c: (0, 0))]
    args += [lgam, gn]
    state_spec = pl.BlockSpec((pb, None, 2, HG_W, HG_W), lambda b, d, c: (b, d, 0, 0, 0))
    if has_init:
        in_specs.append(state_spec)
        args.append(st0)
    out_specs = [pl.BlockSpec((None, pb, CHUNK, 2 * HG_W),
                              lambda b, d, c: (b, 0, jnp.where(d == 0, n - 1, n - 1 - c), 0))]
    out_shape = [jax.ShapeDtypeStruct((bsz // pb, pb, t, 2 * HG_W), F32)]
    if want_final:
        fin_spec = pl.BlockSpec((pb, None, N_HEADS, HEAD_DIM, HEAD_DIM), lambda b, d, c: (b, d, 0, 0, 0))
        out_specs += [fin_spec, fin_spec]
        out_shape += [jax.ShapeDtypeStruct((bsz, 2, N_HEADS, HEAD_DIM, HEAD_DIM), F32)] * 2
    res = pl.pallas_call(
        functools.partial(_scan_kernel, n=n, pb=pb, use_rope=use_rope, has_init=has_init, want_final=want_final),
        grid=(bsz // pb, 2, n),
        in_specs=in_specs,
        out_specs=out_specs,
        out_shape=out_shape,
        scratch_shapes=[pltpu.VMEM((pb, 2, HG_W, HG_W), F32), pltpu.VMEM((n, pb, CHUNK, 2 * HG_W), F32)],
        compiler_params=pltpu.CompilerParams(vmem_limit_bytes=VMEM_LIMIT),
        name="ret_gla_scan",
    )(*args)
    y = res[0].reshape(bsz * t, 2 * HG_W)
    return (y, res[1], res[2]) if want_final else (y, None, None)


S5_GB = LANES // S5_CH
S5_NEXP = 2 * S5_L + 1


def _s5_table_kernel(pwr_ref, pwi_ref, fz_ref, btr_ref, bti_ref, cr_ref, ci_ref, m_ref, sin_ref, sout_ref):
    def per_token(t, rows):
        return jnp.concatenate([jnp.broadcast_to(t[i:i + 1, :], (S5_CH, S5_STATE)) for i in rows], axis=0)

    def per_channel(t):
        return jnp.concatenate([t] * S5_L, axis=0)

    pos = range(S5_L)
    seqs = ([j for j in pos], [S5_L - 1 - j for j in pos], [2 * S5_L - 1 - j for j in pos],
            [S5_L + j for j in pos], [S5_L + j + 1 for j in pos], [2 * S5_L - j for j in pos])

    def powers(d, k):
        return per_token(pwr_ref[d], seqs[k]), per_token(pwi_ref[d], seqs[k])

    cr, ci = per_channel(cr_ref[...]), per_channel(ci_ref[...])
    tok_in = lax.broadcasted_iota(jnp.int32, (S5_ROW, S5_ROW), 0) // S5_CH
    tok_out = lax.broadcasted_iota(jnp.int32, (S5_ROW, S5_ROW), 1) // S5_CH
    nt_dims = (((1,), (1,)), ((), ()))
    m = jnp.zeros((S5_ROW, S5_ROW), F32)
    for d in range(2):
        fr, fi = fz_ref[d, 0:1, :], fz_ref[d, 1:2, :]
        bbr = per_channel(fr * btr_ref[...] - fi * bti_ref[...])
        bbi = per_channel(fr * bti_ref[...] + fi * btr_ref[...])
        k_out, k_in, k_sin, k_sout = (0, 1, 2, 4) if d == 0 else (1, 0, 3, 5)
        er, ei = powers(d, k_out)
        cq = jnp.concatenate([cr * er - ci * ei, -(cr * ei + ci * er)], axis=-1)
        er, ei = powers(d, k_in)
        bk = jnp.concatenate([bbr * er - bbi * ei, bbr * ei + bbi * er], axis=-1)
        full = lax.dot_general(bk, cq, nt_dims, precision=HIGHEST, preferred_element_type=F32)
        m = m + jnp.where((tok_out >= tok_in) if d == 0 else (tok_in >= tok_out), full, 0.0)
        er, ei = powers(d, k_sin)
        sin_ref[d] = jnp.concatenate([bbr * er - bbi * ei, bbr * ei + bbi * er], axis=-1).astype(BF16)
        er, ei = powers(d, k_sout)
        sout_ref[d] = jnp.concatenate([cr * er - ci * ei, -(cr * ei + ci * er)], axis=-1).astype(BF16)
    m_ref[...] = m.astype(BF16)


def _s5_power_table(a_re, a_im, log_dt):
    expo = np.concatenate([np.arange(S5_L, dtype=np.float32) - (S5_L - 1) / 2.0, np.arange(S5_L + 1, dtype=np.float32)])
    expo = jnp.asarray(expo)[:, None]
    dt = jnp.exp(log_dt)[..., None, None]
    ar, ai = a_re[..., None, :] * dt, a_im[..., None, :] * dt
    mag = jnp.exp(expo * ar)
    return mag * jnp.cos(expo * ai), mag * jnp.sin(expo * ai)


def _s5_tables(a_re, a_im, log_dt, b_re, b_im, c_re, c_im):
    depth = a_re.shape[0]
    pwr, pwi = _s5_power_table(a_re, a_im, log_dt)
    dt = jnp.exp(log_dt)[..., None]
    mag = jnp.exp(a_re * dt)
    lb_re, lb_im = mag * jnp.cos(a_im * dt), mag * jnp.sin(a_im * dt)
    den = a_re * a_re + a_im * a_im
    n_re = lb_re - 1.0
    fz = jnp.stack([(n_re * a_re + lb_im * a_im) / den, (lb_im * a_re - n_re * a_im) / den], axis=3)
    mag_l = jnp.exp(S5_L * (a_re * dt))
    pl_re, pl_im = mag_l * jnp.cos(S5_L * (a_im * dt)), mag_l * jnp.sin(S5_L * (a_im * dt))
    lam = jnp.stack([jnp.concatenate([pl_re[:, 0], pl_re[:, 0]], -1), jnp.concatenate([-pl_im[:, 0], pl_im[:, 0]], -1),
                     jnp.concatenate([pl_re[:, 1], pl_re[:, 1]], -1), jnp.concatenate([-pl_im[:, 1], pl_im[:, 1]], -1)],
                    axis=2)
    per_lg = lambda *shape: pl.BlockSpec((None, None) + shape, lambda l, g: (l, g) + (0,) * len(shape))
    per_dir = lambda *shape: pl.BlockSpec((None, 2, None) + shape, lambda l, g: (l, 0, g) + (0,) * len(shape))
    m, s_in, s_out = pl.pallas_call(
        _s5_table_kernel,
        grid=(depth, S5_GROUPS),
        in_specs=[per_dir(S5_NEXP, S5_STATE), per_dir(S5_NEXP, S5_STATE), per_dir(2, S5_STATE),
                  per_lg(S5_CH, S5_STATE), per_lg(S5_CH, S5_STATE), per_lg(S5_CH, S5_STATE), per_lg(S5_CH, S5_STATE)],
        out_specs=[per_lg(S5_ROW, S5_ROW), per_lg(2, S5_ROW, 2 * S5_STATE), per_lg(2, S5_ROW, 2 * S5_STATE)],
        out_shape=[jax.ShapeDtypeStruct((depth, S5_GROUPS, S5_ROW, S5_ROW), BF16),
                   jax.ShapeDtypeStruct((depth, S5_GROUPS, 2, S5_ROW, 2 * S5_STATE), BF16),
                   jax.ShapeDtypeStruct((depth, S5_GROUPS, 2, S5_ROW, 2 * S5_STATE), BF16)],
        compiler_params=pltpu.CompilerParams(vmem_limit_bytes=VMEM_LIMIT),
        name="s5_tables",
    )(pwr, pwi, fz, jnp.swapaxes(b_re, -1, -2), jnp.swapaxes(b_im, -1, -2), c_re, c_im)
    return m, s_in, s_out, lam


def _s5_kernel(*refs, n, bsz, has_init):
    refs = list(refs)
    su_ref = refs.pop(0)
    m_ref = refs.pop(0)
    sin_ref = refs.pop(0)
    sout_ref = refs.pop(0)
    lam_ref = refs.pop(0)
    dv_ref = refs.pop(0)
    if has_init:
        h0_ref = refs.pop(0)
    y_ref = refs.pop(0)
    fin_ref = refs.pop(0)
    sfs, sbs, hpf, hnb = refs

    r = n * bsz
    gran = lax.broadcasted_iota(jnp.int32, (1, LANES), 1) // S5_CH
    halves_per_row = S5_ROW // LANES
    per_half = S5_L // halves_per_row

    def tok_rows(j):
        return pl.ds(j, r, stride=S5_L)

    def granule_transpose(arrs):
        arrs = list(arrs)
        bit = S5_GB // 2
        while bit:
            upper = (gran & bit) != 0
            for i in range(S5_GB):
                if i & bit:
                    continue
                lo, hi = arrs[i], arrs[i | bit]
                arrs[i] = jnp.where(upper, pltpu.roll(hi, bit * S5_CH, 1), lo)
                arrs[i | bit] = jnp.where(upper, hi, pltpu.roll(lo, LANES - bit * S5_CH, 1))
            bit //= 2
        return arrs

    halves = [granule_transpose([su_ref[tok_rows(hh * per_half + i), :] for i in range(per_half)])
              for hh in range(halves_per_row)]
    u = jnp.stack([jnp.concatenate([halves[hh][gl] for hh in range(halves_per_row)], axis=-1)
                   for gl in range(S5_GB)]).astype(BF16)

    bm_dims = (((2,), (1,)), ((0,), (0,)))
    bm_nt_dims = (((2,), (2,)), ((0,), (0,)))
    y = lax.dot_general(u, m_ref[...], bm_dims, preferred_element_type=F32)
    sfs[...] = lax.dot_general(u, sin_ref[:, 0], bm_dims, preferred_element_type=F32)
    sbs[...] = lax.dot_general(u, sin_ref[:, 1], bm_dims, preferred_element_type=F32)

    a_f, s_f = lam_ref[:, 0:1, :], lam_ref[:, 1:2, :]
    a_b, s_b = lam_ref[:, 2:3, :], lam_ref[:, 3:4, :]
    if has_init:
        hf = h0_ref[:, 0]
        hb = h0_ref[:, 1]
    else:
        hf = jnp.zeros((S5_GB, bsz, 2 * S5_STATE), F32)
        hb = jnp.zeros((S5_GB, bsz, 2 * S5_STATE), F32)
    swap = lambda a: pltpu.roll(a, S5_STATE, 2)
    hf_sw, hb_sw = swap(hf), swap(hb)
    for c in range(n):
        rows = pl.ds(c, bsz, stride=n)
        hpf[:, rows, :] = hf
        s_in = sfs[:, rows, :]
        hf, hf_sw = hf * a_f + hf_sw * s_f + s_in, hf_sw * a_f - hf * s_f + swap(s_in)
        rows = pl.ds(n - 1 - c, bsz, stride=n)
        hnb[:, rows, :] = hb
        s_in = sbs[:, rows, :]
        hb, hb_sw = hb * a_b + hb_sw * s_b + s_in, hb_sw * a_b - hb * s_b + swap(s_in)
    fin_ref[:, 0] = hf
    fin_ref[:, 1] = hb
    y = y + lax.dot_general(hpf[...].astype(BF16), sout_ref[:, 0], bm_nt_dims, preferred_element_type=F32)
    y = y + lax.dot_general(hnb[...].astype(BF16), sout_ref[:, 1], bm_nt_dims, preferred_element_type=F32)

    for hh in range(halves_per_row):
        by_token = granule_transpose([y[gl][:, hh * LANES:(hh + 1) * LANES] for gl in range(S5_GB)])
        for i in range(per_half):
            rows = tok_rows(hh * per_half + i)
            y_ref[rows, :] = by_token[i] + su_ref[rows, :] * dv_ref[...]


def _s5_call(su, row_block, tabs, layer, dvec, bsz, t, h0):
    m, s_in, s_out, lam = tabs
    n = t // S5_L
    r = n * bsz
    rows = bsz * t
    has_init = h0 is not None
    gspec = lambda *shape: pl.BlockSpec((S5_GB,) + shape, lambda g: (g,) + (0,) * len(shape))
    lspec = lambda *shape: pl.BlockSpec((None, S5_GB) + shape, lambda g: (layer, g) + (0,) * len(shape))
    in_specs = [pl.BlockSpec((rows, LANES), lambda g: (row_block, g)),
                lspec(S5_ROW, S5_ROW), lspec(2, S5_ROW, 2 * S5_STATE), lspec(2, S5_ROW, 2 * S5_STATE),
                lspec(4, 2 * S5_STATE), pl.BlockSpec((1, LANES), lambda g: (0, g))]
    args = [su, m, s_in, s_out, lam, dvec]
    if has_init:
        in_specs.append(gspec(2, bsz, 2 * S5_STATE))
        args.append(h0)
    return pl.pallas_call(
        functools.partial(_s5_kernel, n=n, bsz=bsz, has_init=has_init),
        grid=(S5_GROUPS // S5_GB,),
        in_specs=in_specs,
        out_specs=[pl.BlockSpec((rows, LANES), lambda g: (0, g)), gspec(2, bsz, 2 * S5_STATE)],
        out_shape=[jax.ShapeDtypeStruct((rows, S5_W), F32),
                   jax.ShapeDtypeStruct((S5_GROUPS, 2, bsz, 2 * S5_STATE), F32)],
        scratch_shapes=[pltpu.VMEM((S5_GB, r, 2 * S5_STATE), F32)] * 4,
        compiler_params=pltpu.CompilerParams(vmem_limit_bytes=VMEM_LIMIT),
        name="s5_scan",
    )(*args)


def _outproj_kernel(yrg_c_ref, yrg_l_ref, ys_c_ref, ys_l_ref, xc_ref, xl_ref, mod_ref, n2_ref, wo_ref, gw_ref, gb_ref,
                    rw_ref, rb_ref, x1_ref, xs_ref, pos_ref, wt_ref, meta_ref, *, n_ctx_tiles):
    is_ctx = pl.program_id(0) < n_ctx_tiles
    x_in = jnp.where(is_ctx, xc_ref[...], xl_ref[...])
    ys = jnp.where(is_ctx, ys_c_ref[...], ys_l_ref[...])
    yrg = jnp.where(is_ctx, yrg_c_ref[...], yrg_l_ref[...])
    s = 0.5 * ys * (1.0 + jnp.tanh(np.sqrt(2.0 / np.pi).astype(np.float32) * (ys + 0.044715 * (ys * ys * ys))))
    s = s * _sigmoid(jnp.dot(s.astype(BF16), gw_ref[...], preferred_element_type=F32) + gb_ref[...])
    m = (jnp.dot(yrg.astype(BF16), wo_ref[0:2 * HG_W, :], preferred_element_type=F32)
         + jnp.dot(s.astype(BF16), wo_ref[2 * HG_W:, :], preferred_element_type=F32))
    x1 = x_in + mod_ref[2:3, :] * m
    x1_ref[...] = x1
    h = x1 * lax.rsqrt(jnp.mean(x1 * x1, axis=-1, keepdims=True) + EPS) * n2_ref[...]
    h = h * (1.0 + mod_ref[4:5, :]) + mod_ref[3:4, :]

    h_hi = h.astype(BF16)
    h_lo = (h - h_hi.astype(F32)).astype(BF16)
    pp = jnp.dot(jnp.concatenate([h_hi, h_lo], axis=0), rw_ref[...], preferred_element_type=F32)
    logits = ((pp[:TM, :LANES] + pp[TM:, :LANES]) + (pp[:TM, LANES:] + pp[TM:, LANES:])) + rb_ref[...]
    lane = lax.broadcasted_iota(jnp.int32, (TM, LANES), 1).astype(F32)
    cur = logits
    vals, idxs = [], []
    for _ in range(TOP_K):
        mx = jnp.max(cur, axis=-1, keepdims=True)
        am = jnp.min(jnp.where(cur == mx, lane, float(LANES)), axis=-1, keepdims=True)
        vals.append(mx)
        idxs.append(am)
        cur = jnp.where(lane == am, -jnp.inf, cur)
    es = [jnp.exp(v - vals[0]) for v in vals]
    den = es[0] + es[1] + es[2] + es[3]

    hit = [(lane == am) for am in idxs]
    assign = (hit[0] | hit[1] | hit[2] | hit[3]).astype(BF16)
    ti = lax.broadcasted_iota(jnp.int32, (TM, TM), 0)
    tj = lax.broadcasted_iota(jnp.int32, (TM, TM), 1)
    rank = jnp.dot((tj < ti).astype(BF16), assign, preferred_element_type=F32)
    cnt = jnp.sum(assign.astype(F32), axis=0, keepdims=True)
    cnt_al = jnp.floor((cnt + (SEG_ALIGN - 1)) * (1.0 / SEG_ALIGN)) * SEG_ALIGN
    ei = lax.broadcasted_iota(jnp.int32, (LANES, LANES), 0)
    ej = lax.broadcasted_iota(jnp.int32, (LANES, LANES), 1)
    seg = jnp.dot(jnp.broadcast_to(cnt_al, (SUBLANES, LANES)).astype(BF16), (ei < ej).astype(BF16),
                  preferred_element_type=F32)[0:1, :]
    base = seg + rank
    rlane = lax.broadcasted_iota(jnp.int32, (TM, ROWS_T), 1).astype(F32)
    klane = lax.broadcasted_iota(jnp.int32, (TM, LANES), 1)
    onehot = jnp.zeros((TM, ROWS_T), F32)
    pos_out = jnp.zeros((TM, LANES), F32)
    wt_out = jnp.zeros((TM, LANES), F32)
    for kk in range(TOP_K):
        pk = jnp.sum(jnp.where(hit[kk], base, 0.0), axis=-1, keepdims=True)
        onehot = onehot + (rlane == pk).astype(F32)
        pos_out = jnp.where(klane == kk, pk, pos_out)
        wt_out = jnp.where(klane == kk, es[kk] / den, wt_out)
    xs_ref[0] = _rows_to_chunks(lax.dot_general(onehot.astype(BF16), h.astype(BF16), (((0,), (0,)), ((), ())),
                                                preferred_element_type=F32))
    pos_ref[...] = pos_out.astype(jnp.int32)
    wt_ref[...] = wt_out
    mrow = lax.broadcasted_iota(jnp.int32, (SUBLANES, LANES), 0)
    meta = jnp.where(mrow == 0, jnp.broadcast_to(cnt_al, (SUBLANES, LANES)),
                     jnp.where(mrow == 1, jnp.broadcast_to(seg, (SUBLANES, LANES)), 0.0))
    meta_ref[0] = meta.astype(jnp.int32)


def _outproj_call(yrg_c, yrg_l, ys_c, ys_l, x_c, x_l, mod_l, n2, wo, gw, gb, rw, rb, n_ctx_tiles, tiles_per_lat):
    n = x_c.shape[0] + x_l.shape[0]
    nt = n // TM
    gid = functools.partial(_group_id, n_ctx_tiles=n_ctx_tiles, tiles_per_lat=tiles_per_lat)
    const = lambda *shape: pl.BlockSpec(shape, lambda i: (0,) * len(shape))
    ctx_map, lat_map = _path_maps(n_ctx_tiles)
    return pl.pallas_call(
        functools.partial(_outproj_kernel, n_ctx_tiles=n_ctx_tiles),
        grid=(nt,),
        in_specs=[
            pl.BlockSpec((TM, 2 * HG_W), ctx_map),
            pl.BlockSpec((TM, 2 * HG_W), lat_map),
            pl.BlockSpec((TM, S5_W), ctx_map),
            pl.BlockSpec((TM, S5_W), lat_map),
            pl.BlockSpec((TM, D_MODEL), ctx_map),
            pl.BlockSpec((TM, D_MODEL), lat_map),
            pl.BlockSpec((None, 6, D_MODEL), lambda i: (gid(i), 0, 0)),
            const(1, D_MODEL), const(D_MODEL, D_MODEL), const(S5_W, S5_W), const(1, S5_W),
            const(D_MODEL, 2 * LANES), const(1, LANES),
        ],
        out_specs=[
            pl.BlockSpec((TM, D_MODEL), lambda i: (i, 0)),
            pl.BlockSpec((1, CHUNKS_T, 2 * SEG_ALIGN, D_MODEL // 2), lambda i: (i, 0, 0, 0)),
            pl.BlockSpec((TM, LANES), lambda i: (i, 0)),
            pl.BlockSpec((TM, LANES), lambda i: (i, 0)),
            pl.BlockSpec((1, SUBLANES, LANES), lambda i: (i, 0, 0)),
        ],
        out_shape=[
            jax.ShapeDtypeStruct((n, D_MODEL), F32),
            jax.ShapeDtypeStruct((nt, CHUNKS_T, 2 * SEG_ALIGN, D_MODEL // 2), BF16),
            jax.ShapeDtypeStruct((n, LANES), jnp.int32),
            jax.ShapeDtypeStruct((n, LANES), F32),
            jax.ShapeDtypeStruct((nt, SUBLANES, LANES), jnp.int32),
        ],
        compiler_params=pltpu.CompilerParams(vmem_limit_bytes=VMEM_LIMIT),
        name="outproj_router_dispatch",
    )(yrg_c, yrg_l, ys_c, ys_l, x_c, x_l, mod_l, n2, wo, gw, gb, rw, rb)


SCHED_COLS = 6


def _schedule_kernel(meta_ref, blk_ref, cst_ref, nch_ref, srcb_ref, *, nt, nblk_pad):
    exact = functools.partial(jnp.dot, precision=HIGHEST, preferred_element_type=F32)
    nch = meta_ref[:, 0, :].astype(F32) * (1.0 / SEG_ALIGN)
    seg = meta_ref[:, 1, :].astype(F32) * (1.0 / SEG_ALIGN)
    ti = lax.broadcasted_iota(jnp.int32, (nt, nt), 0)
    tj = lax.broadcasted_iota(jnp.int32, (nt, nt), 1)
    cs_end = exact((tj <= ti).astype(F32), nch)
    cs_start = cs_end - nch
    tot = cs_end[nt - 1:nt, :]
    nb = jnp.floor((tot + (BLK_CHUNKS - 1)) * (1.0 / BLK_CHUNKS))
    ei = lax.broadcasted_iota(jnp.int32, (LANES, LANES), 0)
    ej = lax.broadcasted_iota(jnp.int32, (LANES, LANES), 1)
    blk_end = exact(jnp.broadcast_to(nb, (SUBLANES, LANES)), (ei <= ej).astype(F32))[0:1, :]
    blk_start = blk_end - nb
    n_active = blk_end[:, LANES - 1:LANES]
    j = lax.broadcasted_iota(jnp.int32, (nblk_pad, 1), 0).astype(F32)
    lane = lax.broadcasted_iota(jnp.int32, (nblk_pad, LANES), 1).astype(F32)
    row_sum = lambda a: jnp.sum(a, axis=-1, keepdims=True)
    be = row_sum((blk_end <= jnp.minimum(j, n_active - 1.0)).astype(F32))
    be = jnp.minimum(be, float(N_EXPERTS - 1))
    own = (lane == be).astype(F32)
    bs = (j - row_sum(own * blk_start)) * BLK_CHUNKS
    nvalid = jnp.where(j < n_active, jnp.clip(row_sum(own * tot) - bs, 0.0, float(BLK_CHUNKS)), 0.0)
    nt_dims = (((1,), (1,)), ((), ()))
    own_end = lax.dot_general(own, cs_end, nt_dims, precision=HIGHEST, preferred_element_type=F32)
    own_start = lax.dot_general(own, cs_start, nt_dims, precision=HIGHEST, preferred_element_type=F32)
    kfirst = row_sum((own_end <= bs).astype(F32))
    klast = row_sum((own_start < bs + nvalid).astype(F32))
    later = (ej > ei) & (jnp.broadcast_to(tot, (LANES, LANES)) > 0.0)
    nxt_e = jnp.min(jnp.where(later, ej.astype(F32), float(LANES)), axis=-1, keepdims=True)
    nxt = exact(own, jnp.broadcast_to(nxt_e, (LANES, LANES)))[:, 0:1]
    nxt = jnp.where(nxt >= float(N_EXPERTS), -1.0, nxt)
    cols = (be, nvalid, bs, kfirst, klast, nxt)
    out = jnp.zeros((nblk_pad, LANES), F32)
    for k_col, val in enumerate(cols):
        out = jnp.where(lane == float(k_col), val, out)
    blk_ref[...] = out.astype(jnp.int32)
    tile = lax.broadcasted_iota(jnp.int32, (nt, LANES), 0).astype(F32)
    cst_ref[...] = cs_start.astype(jnp.int32)
    nch_ref[...] = nch.astype(jnp.int32)
    srcb_ref[...] = (tile * CHUNKS_T + seg).astype(jnp.int32)


def _expert_schedule(meta, nblk):
    nt = meta.shape[0]
    nblk_pad = -(-nblk // SUBLANES) * SUBLANES
    whole = lambda *shape: pl.BlockSpec(shape, lambda i: (0,) * len(shape))
    blk, cst, nch, srcb = pl.pallas_call(
        functools.partial(_schedule_kernel, nt=nt, nblk_pad=nblk_pad),
        grid=(1,),
        in_specs=[whole(nt, SUBLANES, LANES)],
        out_specs=[whole(nblk_pad, LANES), whole(nt, LANES), whole(nt, LANES), whole(nt, LANES)],
        out_shape=[jax.ShapeDtypeStruct((nblk_pad, LANES), jnp.int32)] + [jax.ShapeDtypeStruct((nt, LANES), jnp.int32)] * 3,
        name="expert_schedule",
    )(meta)
    return tuple(blk[:nblk, k] for k in range(SCHED_COLS)) + (cst.reshape(-1), nch.reshape(-1), srcb.reshape(-1))


W_CAST_ROWS = 128


def _ffn_kernel(be_ref, nv_ref, bs_ref, kf_ref, kl_ref, nxt_ref, cst_ref, nch_ref, srcb_ref,
                xs_hbm, w1_hbm, b1_ref, w2_hbm, b2_ref, ys_hbm,
                xbuf, ybuf, w1s, w2s, w1b, w2b, gsem, ssem, wsem, *, nblk, layer):
    j = pl.program_id(0)
    slot = j % 2

    def for_pieces(blk, fn):
        e = be_ref[blk]
        lo_b = bs_ref[blk]
        hi_b = lo_b + nv_ref[blk]

        def body(i, carry):
            k = i * LANES + e
            ps = cst_ref[k]
            lo = jnp.maximum(ps, lo_b)
            n = jnp.minimum(ps + nch_ref[k], hi_b) - lo

            @pl.when(n > 0)
            def _():
                fn(srcb_ref[k] + (lo - ps), lo - lo_b, n)
            return carry

        lax.fori_loop(kf_ref[blk], kl_ref[blk], body, 0)

    def start_gather(blk, s):
        for_pieces(blk, lambda src, dst, n: pltpu.make_async_copy(
            xs_hbm.at[pl.ds(src, n)], xbuf.at[s, pl.ds(dst, n)], gsem.at[s]).start())

    def start_scatter(blk, s):
        for_pieces(blk, lambda src, dst, n: pltpu.make_async_copy(
            ybuf.at[s, pl.ds(dst, n)], ys_hbm.at[pl.ds(src, n)], ssem.at[s]).start())

    def wait_gather(blk, s):
        @pl.when(nv_ref[blk] > 0)
        def _():
            pltpu.make_async_copy(xs_hbm.at[pl.ds(0, nv_ref[blk])], xbuf.at[s, pl.ds(0, nv_ref[blk])],
                                  gsem.at[s]).wait()

    def wait_scatter(blk, s):
        @pl.when(nv_ref[blk] > 0)
        def _():
            pltpu.make_async_copy(ybuf.at[s, pl.ds(0, nv_ref[blk])], ys_hbm.at[pl.ds(0, nv_ref[blk])],
                                  ssem.at[s]).wait()

    def weight_copies(e):
        return (pltpu.make_async_copy(w1_hbm.at[layer, e], w1s, wsem.at[0]),
                pltpu.make_async_copy(w2_hbm.at[layer, e], w2s, wsem.at[1]))

    @pl.when(j == 0)
    def _():
        for cp in weight_copies(be_ref[0]):
            cp.start()
        xbuf[...] = jnp.zeros_like(xbuf)
        start_gather(0, 0)

    @pl.when(j + 1 < nblk)
    def _():
        start_gather(j + 1, 1 - slot)

    @pl.when((j == 0) | (be_ref[j] != be_ref[jnp.maximum(j - 1, 0)]))
    def _():
        for cp in weight_copies(be_ref[j]):
            cp.wait()

        def cast_rows(r, carry):
            rows = pl.ds(pl.multiple_of(r * W_CAST_ROWS, W_CAST_ROWS), W_CAST_ROWS)
            w1b[rows, :] = w1s[rows, :].astype(BF16)
            w2b[rows, :] = w2s[rows, :].astype(BF16)
            return carry
        lax.fori_loop(0, D_MODEL // W_CAST_ROWS, cast_rows, 0)

        @pl.when(nxt_ref[j] >= 0)
        def _():
            for cp in weight_copies(nxt_ref[j]):
                cp.start()

    wait_gather(j, slot)

    @pl.when(j >= 2)
    def _():
        wait_scatter(j - 2, slot)

    def compute_rows(first_chunk, nchunks):
        cs = pl.ds(pl.multiple_of(first_chunk, TAIL_CHUNKS), nchunks)
        x = _chunks_to_rows(xbuf[slot, cs])
        gu = jnp.dot(x, w1b[...], preferred_element_type=F32) + b1_ref[...]
        gate = jnp.minimum(gu[:, :D_FF], SWIGLU_LIMIT)
        up = jnp.clip(gu[:, D_FF:], -SWIGLU_LIMIT, SWIGLU_LIMIT)
        act = (up + 1.0) * gate * _sigmoid(SWIGLU_ALPHA * gate)
        y = jnp.dot(act.astype(BF16), w2b[...], preferred_element_type=F32) + b2_ref[...]
        ybuf[slot, cs] = _rows_to_chunks(y)

    nv = nv_ref[j]
    n_full = (nv + (SUB_CHUNKS - TAIL_CHUNKS) - 1) // SUB_CHUNKS
    lax.fori_loop(0, n_full, lambda sb, carry: (compute_rows(sb * SUB_CHUNKS, SUB_CHUNKS), carry)[1], 0)

    @pl.when(nv > n_full * SUB_CHUNKS)
    def _():
        compute_rows(n_full * SUB_CHUNKS, TAIL_CHUNKS)

    start_scatter(j, slot)

    @pl.when(j == nblk - 1)
    def _():
        wait_scatter(j, slot)
        if nblk >= 2:
            wait_scatter(j - 1, 1 - slot)


def _ffn_call(sched, xs_chunks, w1, b1, w2, b2, layer):
    assert D_FF == D_MODEL
    nblk = sched[0].shape[0]
    bmap = lambda j, be, *_: (layer, be[j], 0, 0)
    grid_spec = pltpu.PrefetchScalarGridSpec(
        num_scalar_prefetch=len(sched),
        grid=(nblk,),
        in_specs=[
            pl.BlockSpec(memory_space=pl.ANY),
            pl.BlockSpec(memory_space=pl.ANY),
            pl.BlockSpec((None, None, 1, 2 * D_FF), bmap),
            pl.BlockSpec(memory_space=pl.ANY),
            pl.BlockSpec((None, None, 1, D_MODEL), bmap),
        ],
        out_specs=pl.BlockSpec(memory_space=pl.ANY),
        scratch_shapes=[
            pltpu.VMEM((2, BLK_CHUNKS, 2 * SEG_ALIGN, D_MODEL // 2), BF16),
            pltpu.VMEM((2, BLK_CHUNKS, 2 * SEG_ALIGN, D_MODEL // 2), BF16),
            pltpu.VMEM((D_MODEL, 2 * D_FF), F32),
            pltpu.VMEM((D_FF, D_MODEL), F32),
            pltpu.VMEM((D_MODEL, 2 * D_FF), BF16),
            pltpu.VMEM((D_FF, D_MODEL), BF16),
            pltpu.SemaphoreType.DMA((2,)),
            pltpu.SemaphoreType.DMA((2,)),
            pltpu.SemaphoreType.DMA((2,)),
        ],
    )
    return pl.pallas_call(
        functools.partial(_ffn_kernel, nblk=nblk, layer=layer),
        grid_spec=grid_spec,
        out_shape=jax.ShapeDtypeStruct(xs_chunks.shape, BF16),
        input_output_aliases={len(sched): 0},
        compiler_params=pltpu.CompilerParams(vmem_limit_bytes=VMEM_LIMIT, dimension_semantics=("arbitrary",)),
        name="expert_ffn",
    )(*sched, xs_chunks, w1, b1, w2, b2)


def _combine_kernel(ys_ref, pos_ref, wt_ref, x1_ref, mod_ref, fg_ref, *out_refs, final, n_ctx_tiles):
    ys = _chunks_to_rows(ys_ref[0])
    rlane = lax.broadcasted_iota(jnp.int32, (TM, ROWS_T), 1)
    w = jnp.zeros((TM, ROWS_T), F32)
    for kk in range(TOP_K):
        w = w + jnp.where(rlane == pos_ref[:, kk:kk + 1], wt_ref[:, kk:kk + 1], 0.0)
    moe = jnp.dot(w.astype(BF16), ys, preferred_element_type=F32)
    y = x1_ref[...] + mod_ref[5:6, :] * moe
    if final:
        y = y * lax.rsqrt(jnp.mean(y * y, axis=-1, keepdims=True) + EPS) * fg_ref[...]
    is_ctx = pl.program_id(0) < n_ctx_tiles

    @pl.when(is_ctx)
    def _():
        out_refs[0][...] = y

    @pl.when(jnp.logical_not(is_ctx))
    def _():
        out_refs[1][...] = y


def _combine_call(ys, pos, wts, x1, mod_l, fg, n_ctx_tiles, tiles_per_lat, final):
    n = x1.shape[0]
    nt = n // TM
    gid = functools.partial(_group_id, n_ctx_tiles=n_ctx_tiles, tiles_per_lat=tiles_per_lat)
    tok = pl.BlockSpec((TM, D_MODEL), lambda i: (i, 0))
    ctx_map, lat_map = _path_maps(n_ctx_tiles)
    out_specs = [pl.BlockSpec((TM, D_MODEL), ctx_map), pl.BlockSpec((TM, D_MODEL), lat_map)]
    out_shape = [jax.ShapeDtypeStruct((n_ctx_tiles * TM, D_MODEL), F32),
                 jax.ShapeDtypeStruct((n - n_ctx_tiles * TM, D_MODEL), F32)]
    res = pl.pallas_call(
        functools.partial(_combine_kernel, final=final, n_ctx_tiles=n_ctx_tiles),
        grid=(nt,),
        in_specs=[
            pl.BlockSpec((1, CHUNKS_T, 2 * SEG_ALIGN, D_MODEL // 2), lambda i: (i, 0, 0, 0)),
            pl.BlockSpec((TM, LANES), lambda i: (i, 0)),
            pl.BlockSpec((TM, LANES), lambda i: (i, 0)),
            tok,
            pl.BlockSpec((None, 6, D_MODEL), lambda i: (gid(i), 0, 0)),
            pl.BlockSpec((1, D_MODEL), lambda i: (0, 0)),
        ],
        out_specs=out_specs,
        out_shape=out_shape,
        compiler_params=pltpu.CompilerParams(vmem_limit_bytes=VMEM_LIMIT),
        name="moe_combine",
    )(ys, pos, wts, x1, mod_l, fg)
    return res


def _rope_tables(t):
    pos = np.arange(t)
    nf = HEAD_DIM // 4
    inv = jnp.asarray(ROPE_BASE, F32) ** (-jnp.arange(nf, dtype=F32) / nf)
    ang = jnp.concatenate([jnp.asarray(pos // GRID_W, F32)[:, None] * inv,
                           jnp.asarray(pos % GRID_W, F32)[:, None] * inv], axis=-1)
    cos, sin = jnp.cos(ang), jnp.sin(ang)
    return (jnp.tile(jnp.concatenate([cos, cos], -1), (1, N_HEADS)),
            jnp.tile(jnp.concatenate([-sin, sin], -1), (1, N_HEADS)))


def _block_diag_t(s):
    eye = jnp.eye(N_HEADS, dtype=s.dtype)
    out = jnp.einsum('...hkv,hg->...hvgk', s, eye)
    return out.reshape(s.shape[:-3] + (HG_W, HG_W))


def kernel(x_prompt, x_sample, state_ret, state_gla, state_s5_re, state_s5_im, c, c_ctx, norm1_g, norm2_g, final_g, ada_w, ada_b, w_in, w_out, ret_decay, gla_w_lr, gla_b_lr, gla_norm_g, s5_a_re, s5_a_im, s5_log_dt, s5_b_re, s5_b_im, s5_c_re, s5_c_im, s5_d, s5_glu_w, s5_glu_b, router_w, router_b, moe_w1, moe_b1, moe_w2, moe_b2):
    bc, tc, d = x_prompt.shape
    bl, tl, _ = x_sample.shape
    depth = w_in.shape[0]
    nc, nl = bc * tc, bl * tl
    n = nc + nl
    assert d == D_MODEL and nc % TM == 0 and tl % TM == 0 and tc % CHUNK == 0 and tl % CHUNK == 0
    assert 1 + bl <= SUBLANES and nc % nl == 0
    n_ctx_tiles, tiles_per_lat = nc // TM, tl // TM
    nt = n // TM
    nblk = nt * CHUNKS_T // BLK_CHUNKS + N_EXPERTS

    x_c, x_l = x_prompt.reshape(nc, d), x_sample.reshape(nl, d)
    cond8 = jnp.concatenate([c_ctx[None, :], c, jnp.zeros((SUBLANES - 1 - bl, d), F32)], axis=0)
    mod = _ada_call(cond8, ada_w, ada_b).reshape(depth, SUBLANES, 6, d)
    rope = _rope_tables(tl)
    fg = final_g.reshape(1, d)
    s5_tabs = _s5_tables(s5_a_re, s5_a_im, s5_log_dt, s5_b_re, s5_b_im, s5_c_re, s5_c_im)

    ret_states, gla_states, re_states, im_states = [], [], [], []
    for l in range(depth):
        wl = w_in[l]
        w_cat = jnp.concatenate(
            [wl[:, :8 * HG_W], wl[:, 8 * HG_W + GLA_RANK:],
             jnp.pad(wl[:, 8 * HG_W:8 * HG_W + GLA_RANK], ((0, 0), (0, LANES - GLA_RANK)))], axis=1).astype(BF16)
        wlr = jnp.pad(gla_w_lr[l], ((0, 0), (0, LANES - GLA_RANK), (0, 0)))
        blr = gla_b_lr[l].reshape(2, 1, HG_W)
        pa, su = _inproj_call(x_c, x_l, mod[l], norm1_g[l].reshape(1, d), w_cat, wlr, blr, tl)

        log_gamma = jnp.log1p(-jnp.exp(ret_decay[l]))
        lgam = jnp.repeat(log_gamma, HEAD_DIM, axis=1)
        gn = jnp.tile(gla_norm_g[l], N_HEADS).reshape(1, HG_W)
        y_c, fin_ret, fin_gla = _scan_call(pa, 0, bc, tc, log_gamma, lgam, gn, None, None, True)
        st0 = jnp.stack([_block_diag_t(state_ret[:, l]), _block_diag_t(state_gla[:, l])], axis=2)
        y_l, _, _ = _scan_call(pa, nc, bl, tl, log_gamma, lgam, gn, rope, st0, False)
        ret_states.append(fin_ret)
        gla_states.append(fin_gla)

        dvec = s5_d[l].reshape(1, S5_W)
        ys_c, fin_c = _s5_call(su, 0, s5_tabs, l, dvec, bc, tc, None)
        h0 = jnp.concatenate([state_s5_re[:, l], state_s5_im[:, l]], axis=-1).transpose(2, 1, 0, 3)
        ys_l, _ = _s5_call(su, nc // nl, s5_tabs, l, dvec, bl, tl, h0)
        re_states.append(fin_c[..., :S5_STATE].transpose(2, 1, 0, 3))
        im_states.append(fin_c[..., S5_STATE:].transpose(2, 1, 0, 3))

        rw = jnp.pad(router_w[l], ((0, 0), (0, LANES - N_EXPERTS)))
        rw_hi = rw.astype(BF16)
        rw = jnp.concatenate([rw_hi, (rw - rw_hi.astype(F32)).astype(BF16)], axis=1)
        rb = jnp.concatenate([router_b[l], jnp.full((LANES - N_EXPERTS,), NEG_BIG, F32)]).reshape(1, LANES)
        x1, xs, pos, wts, meta = _outproj_call(
            y_c, y_l, ys_c, ys_l, x_c, x_l, mod[l], norm2_g[l].reshape(1, d), w_out[l].astype(BF16), s5_glu_w[l].astype(BF16),
            s5_glu_b[l].reshape(1, S5_W), rw, rb, n_ctx_tiles, tiles_per_lat)

        sched = _expert_schedule(meta, nblk)
        ys = _ffn_call(sched, xs.reshape(nt * CHUNKS_T, 2 * SEG_ALIGN, d // 2),
                       moe_w1, moe_b1.reshape(depth, N_EXPERTS, 1, 2 * D_FF),
                       moe_w2, moe_b2.reshape(depth, N_EXPERTS, 1, d), l)
        res = _combine_call(ys.reshape(nt, CHUNKS_T, 2 * SEG_ALIGN, d // 2), pos, wts, x1, mod[l], fg,
                            n_ctx_tiles, tiles_per_lat, l == depth - 1)
        x_c, x_l = res
    y_prompt = x_c.reshape(bc, tc, d)
    y_sample = x_l.reshape(bl, tl, d)
    return (y_prompt, y_sample, jnp.stack(ret_states, axis=1), jnp.stack(gla_states, axis=1),
            jnp.stack(re_states, axis=1), jnp.stack(im_states, axis=1))
```
